```python
import math
import jax, jax.numpy as jnp
from jax import lax
import numpy as np

D_MODEL = 1024
BATCH = 32
SEQ = 256
DEPTH = 2
DEC_BATCH = 8
DEC_SEQ = 2048
PAST_LEN = 512

GRID_W = 64
CHUNK = 128
Q_BLOCK = 128
SSD_HEADS = 8
SSD_HEAD_DIM = 64
SSD_INNER = SSD_HEADS * SSD_HEAD_DIM
SSD_GROUPS = 2
SSD_STATE = 64
SSD_CONV = 3
SSD_XBC = SSD_INNER + 2 * SSD_GROUPS * SSD_STATE
MLA_HEADS = 8
MLA_NOPE = 64
MLA_ROPE = 32
MLA_V = 64
MLA_Q_RANK = 384
MLA_KV_RANK = 256
MLA_OUT = MLA_HEADS * MLA_V
SIZES0 = (SSD_INNER, SSD_XBC, 2 * SSD_HEADS, MLA_Q_RANK, MLA_KV_RANK, MLA_ROPE)
IN0 = sum(SIZES0)
MIX0 = SSD_INNER + MLA_OUT
RET_HEADS = 8
RET_QK = 128
RET_V = 256
SIZES1 = (RET_HEADS * RET_QK, RET_HEADS * RET_QK, RET_HEADS * RET_V, RET_HEADS * RET_V)
IN1 = sum(SIZES1)
MIX1 = RET_HEADS * RET_V
D_FF = 2816
N_EXPERTS = 8
TOP_K = 2
D_FF_EXPERT = 3584
N_EVEN = (DEPTH + 1) // 2
N_ODD = DEPTH // 2
ALPHA = (2 * DEPTH) ** 0.25
BETA = (8 * DEPTH) ** -0.25
LN_EPS = 1e-5
RMS_EPS = 1e-6
ROPE_BASE = 10000.0

kernel_name = 'hybrid_ssd_mla_retention_diffusion_step'

F32 = jnp.float32


def _split_points(sizes):
    pts, acc = [], 0
    for s in sizes[:-1]:
        acc += s
        pts.append(acc)
    return pts


def layer_norm(x, g, b):
    xf = x.astype(F32)
    mu = jnp.mean(xf, -1, keepdims=True)
    var = jnp.mean(jnp.square(xf - mu), -1, keepdims=True)
    return ((xf - mu) * lax.rsqrt(var + LN_EPS) * g + b).astype(x.dtype)


def rms_norm(x, g):
    xf = x.astype(F32)
    return (xf * lax.rsqrt(jnp.mean(xf * xf, -1, keepdims=True) + RMS_EPS) * g).astype(x.dtype)


def adaln_vectors(cond, w, b):
    m = jax.nn.silu(cond) @ w + b
    return jnp.split(m[:, None, :], 6, axis=-1)


def modulate(x, shift, scale):
    return x * (1 + scale) + shift


def axial_rope(n_tok, dim):
    rows = n_tok // GRID_W
    row = jnp.repeat(jnp.arange(rows), GRID_W).astype(F32)
    col = jnp.tile(jnp.arange(GRID_W), rows).astype(F32)
    axis_dim = dim // 2
    inv = 1.0 / (ROPE_BASE ** (jnp.arange(0, axis_dim, 2, dtype=F32) / axis_dim))
    ang = jnp.concatenate([row[:, None] * inv, col[:, None] * inv], -1)
    return jnp.cos(ang), jnp.sin(ang)


def apply_rope(x, cos, sin):
    x1, x2 = jnp.split(x, 2, -1)
    cs, sn = cos[:, None, :], sin[:, None, :]
    return jnp.concatenate([x1 * cs - x2 * sn, x1 * sn + x2 * cs], -1).astype(x.dtype)


def dwconv_centred(x, w, b):
    k = w.shape[0]
    y = lax.conv_general_dilated(x, w[:, None, :].astype(x.dtype), window_strides=(1,),
                                 padding=[(k // 2, k - 1 - k // 2)],
                                 dimension_numbers=('NWC', 'WIO', 'NWC'),
                                 feature_group_count=x.shape[-1])
    return y + b


def chunked_decay_scan(cm, bm, xv, log_a, h0):
    b, l, h, n = cm.shape
    p = xv.shape[-1]
    nc = l // CHUNK
    chunks = lambda t: t.reshape((b, nc, CHUNK) + t.shape[2:])
    cm, bm, xv = chunks(cm), chunks(bm), chunks(xv)
    acs = jnp.cumsum(chunks(log_a.astype(F32)), axis=2)
    idx = jnp.arange(CHUNK)
    lower = (idx[:, None] >= idx[None, :])[None, None, :, :, None]
    seg = acs[:, :, :, None, :] - acs[:, :, None, :, :]
    decay_in = jnp.exp(jnp.where(lower, seg, -jnp.inf)).astype(xv.dtype)
    scores = jnp.einsum('bcihn,bcjhn->bcijh', cm, bm) * decay_in
    y_diag = jnp.einsum('bcijh,bcjhp->bcihp', scores, xv)
    decay_to_end = jnp.exp(acs[:, :, -1:, :] - acs).astype(xv.dtype)
    chunk_states = jnp.einsum('bcjhn,bcjh,bcjhp->bchpn', bm, decay_to_end, xv)
    chunk_decay = jnp.exp(acs[:, :, -1, :])

    def step(state, inp):
        s_c, d_c = inp
        return state * d_c[:, :, None, None] + s_c, state

    h_final, h_prev = lax.scan(step, h0.astype(F32),
                               (jnp.moveaxis(chunk_states, 1, 0).astype(F32),
                                jnp.moveaxis(chunk_decay, 1, 0)))
    h_prev = jnp.moveaxis(h_prev, 0, 1).astype(xv.dtype)
    y_off = jnp.einsum('bcihn,bchpn,bcih->bcihp', cm, h_prev, jnp.exp(acs).astype(xv.dtype))
    y = (y_diag + y_off).reshape(b, l, h, p)
    return y, h_final.astype(h0.dtype)


def bidir_scan(cm, bm, x_f, x_b, la_f, la_b, h0_f, h0_b):
    y_f, s_f = chunked_decay_scan(cm, bm, x_f, la_f, h0_f)
    flip = lambda t: jnp.flip(t, axis=1)
    y_b, s_b = chunked_decay_scan(flip(cm), flip(bm), flip(x_b), flip(la_b), h0_b)
    return y_f + flip(y_b), s_f, s_b


def ssd_mixer(z, xbc, dt_raw, h0_f, h0_b, conv_w, conv_b, a_log_f, a_log_b,
              dt_bias_f, dt_bias_b, d_skip, norm_g):
    b, l, _ = z.shape
    xbc = jax.nn.silu(dwconv_centred(xbc, conv_w, conv_b))
    xs, bm, cm = jnp.split(xbc, [SSD_INNER, SSD_INNER + SSD_GROUPS * SSD_STATE], -1)
    xs = xs.reshape(b, l, SSD_HEADS, SSD_HEAD_DIM)
    rep = SSD_HEADS // SSD_GROUPS
    bm = jnp.repeat(bm.reshape(b, l, SSD_GROUPS, SSD_STATE), rep, axis=2)
    cm = jnp.repeat(cm.reshape(b, l, SSD_GROUPS, SSD_STATE), rep, axis=2)
    dt_f = jax.nn.softplus(dt_raw[..., :SSD_HEADS].astype(F32) + dt_bias_f)
    dt_b = jax.nn.softplus(dt_raw[..., SSD_HEADS:].astype(F32) + dt_bias_b)
    la_f = -dt_f * jnp.exp(a_log_f.astype(F32))
    la_b = -dt_b * jnp.exp(a_log_b.astype(F32))
    x_f = xs * dt_f[..., None].astype(xs.dtype)
    x_b = xs * dt_b[..., None].astype(xs.dtype)
    y, s_f, s_b = bidir_scan(cm, bm, x_f, x_b, la_f, la_b, h0_f, h0_b)
    y = y + d_skip[:, None] * xs
    y = rms_norm(y.reshape(b, l, SSD_INNER) * jax.nn.silu(z), norm_g)
    return y, s_f, s_b


def mla_keys_values(ckv, k_rope, w_kv_up):
    b, l, _ = ckv.shape
    kv = (ckv @ w_kv_up).reshape(b, l, MLA_HEADS, MLA_NOPE + MLA_V)
    k = jnp.concatenate([kv[..., :MLA_NOPE],
                         jnp.broadcast_to(k_rope[:, :, None, :], (b, l, MLA_HEADS, MLA_ROPE))], -1)
    return k, kv[..., MLA_NOPE:]


def block_attention(q, k, v):
    b, sq, h, d = q.shape
    nb = sq // Q_BLOCK
    scale = d ** -0.5
    qb = jnp.moveaxis(q.reshape(b, nb, Q_BLOCK, h, d), 1, 0)

    def one_block(q_blk):
        s = jnp.einsum('bqhd,bkhd->bhqk', q_blk, k).astype(F32) * scale
        pr = jax.nn.softmax(s, axis=-1).astype(v.dtype)
        return jnp.einsum('bhqk,bkhd->bqhd', pr, v)

    o = lax.map(one_block, qb)
    return jnp.moveaxis(o, 0, 1).reshape(b, sq, h, v.shape[-1])


def even_mixer(h, h0_f, h0_b, ctx_kv, rope, w_in, conv_w, conv_b, a_log_f, a_log_b,
               dt_bias_f, dt_bias_b, d_skip, ssd_norm_g, q_norm_g, w_q_up, kv_norm_g,
               w_kv_up, w_out):
    b, l, _ = h.shape
    z, xbc, dt_raw, q_lat, kv_lat, k_rope = jnp.split(h @ w_in, _split_points(SIZES0), -1)
    y_ssd, s_f, s_b = ssd_mixer(z, xbc, dt_raw, h0_f, h0_b, conv_w, conv_b, a_log_f, a_log_b,
                                dt_bias_f, dt_bias_b, d_skip, ssd_norm_g)
    q = (rms_norm(q_lat, q_norm_g) @ w_q_up).reshape(b, l, MLA_HEADS, MLA_NOPE + MLA_ROPE)
    q_nope, q_rope = q[..., :MLA_NOPE], q[..., MLA_NOPE:]
    ckv = rms_norm(kv_lat, kv_norm_g)
    if rope is None:
        k_rope_pos = k_rope
    else:
        cos, sin = rope
        q_rope = apply_rope(q_rope, cos, sin)
        k_rope_pos = apply_rope(k_rope[:, :, None, :], cos, sin)[:, :, 0, :]
    k, v = mla_keys_values(ckv, k_rope_pos, w_kv_up)
    if ctx_kv is not None:
        k_c, v_c = mla_keys_values(ctx_kv[0], ctx_kv[1], w_kv_up)
        k = jnp.concatenate([k, k_c], axis=1)
        v = jnp.concatenate([v, v_c], axis=1)
    o = block_attention(jnp.concatenate([q_nope, q_rope], -1), k, v).reshape(b, l, MLA_OUT)
    out = jnp.concatenate([y_ssd, o], -1) @ w_out
    return out, ckv, k_rope, s_f, s_b


def retention_mixer(h, h0_f, h0_b, rope, w_in, decay_f, decay_b, w_out):
    b, l, _ = h.shape
    q, k, v, g = jnp.split(h @ w_in, _split_points(SIZES1), -1)
    q = q.reshape(b, l, RET_HEADS, RET_QK)
    k = k.reshape(b, l, RET_HEADS, RET_QK) * (RET_QK ** -0.5)
    v = v.reshape(b, l, RET_HEADS, RET_V)
    if rope is not None:
        cos, sin = rope
        q = apply_rope(q, cos, sin)
        k = apply_rope(k, cos, sin)
    la_f = jnp.broadcast_to(-jnp.exp(decay_f.astype(F32))[None, None, :], (b, l, RET_HEADS))
    la_b = jnp.broadcast_to(-jnp.exp(decay_b.astype(F32))[None, None, :], (b, l, RET_HEADS))
    y, s_f, s_b = bidir_scan(q, k, v, v, la_f, la_b, h0_f, h0_b)
    yf = y.astype(F32)
    mu = jnp.mean(yf, -1, keepdims=True)
    var = jnp.mean(jnp.square(yf - mu), -1, keepdims=True)
    y = ((yf - mu) * lax.rsqrt(var + LN_EPS)).astype(h.dtype).reshape(b, l, MIX1)
    return (y * jax.nn.silu(g)) @ w_out, s_f, s_b


def swiglu(x, w_gate, w_up, w_down):
    return (jax.nn.silu(x @ w_gate) * (x @ w_up)) @ w_down


def moe_swiglu(x, router, w_gate, w_up, w_down):
    b, l, d = x.shape
    t = x.reshape(b * l, d)
    logits = (t @ router).astype(F32)
    top_v, top_i = lax.top_k(logits, TOP_K)
    gates = jax.nn.softmax(top_v, axis=-1)
    dense_gate = jnp.sum(jax.nn.one_hot(top_i, N_EXPERTS, dtype=F32) * gates[..., None], axis=1)
    dense_gate = dense_gate.astype(t.dtype)
    out = jnp.zeros_like(t)
    for e in range(N_EXPERTS):
        out = out + dense_gate[:, e:e + 1] * swiglu(t, w_gate[e], w_up[e], w_down[e])
    return out.reshape(b, l, d)


def setup_inputs(seed: int = 0) -> dict:
    key = jax.random.key(seed)
    keys = iter(jax.random.split(key, 64))

    def nrm(shape, scale):
        return jax.random.normal(next(keys), shape, F32) * scale

    def gain(shape):
        return 1.0 + nrm(shape, 0.02)

    E, O = N_EVEN, N_ODD
    inp = {}
    inp['x_prompt'] = nrm((BATCH, SEQ, D_MODEL), 1.0)
    inp['x_sample'] = nrm((DEC_BATCH, DEC_SEQ, D_MODEL), 1.0)
    inp['cache_mla_ckv'] = nrm((DEC_BATCH, E, PAST_LEN, MLA_KV_RANK), 1.0)
    inp['cache_mla_krope'] = nrm((DEC_BATCH, E, PAST_LEN, MLA_ROPE), 1.0)
    inp['state_ssd_f'] = nrm((DEC_BATCH, E, SSD_HEADS, SSD_HEAD_DIM, SSD_STATE), 0.5)
    inp['state_ssd_b'] = nrm((DEC_BATCH, E, SSD_HEADS, SSD_HEAD_DIM, SSD_STATE), 0.5)
    inp['state_ret_f'] = nrm((DEC_BATCH, O, RET_HEADS, RET_V, RET_QK), 1.0)
    inp['state_ret_b'] = nrm((DEC_BATCH, O, RET_HEADS, RET_V, RET_QK), 1.0)
    inp['c'] = nrm((DEC_BATCH, D_MODEL), 1.0)
    inp['c_ctx'] = nrm((D_MODEL,), 1.0)
    inp['ada_w'] = nrm((DEPTH, D_MODEL, 6 * D_MODEL), 0.5 * D_MODEL ** -0.5)
    inp['ada_b'] = nrm((DEPTH, 6 * D_MODEL), 0.02)
    inp['ln1_g'] = gain((DEPTH, D_MODEL))
    inp['ln1_b'] = nrm((DEPTH, D_MODEL), 0.02)
    inp['ln2_g'] = gain((DEPTH, D_MODEL))
    inp['ln2_b'] = nrm((DEPTH, D_MODEL), 0.02)
    inp['w_in0'] = nrm((E, D_MODEL, IN0), D_MODEL ** -0.5)
    inp['ssd_conv_w'] = nrm((E, SSD_CONV, SSD_XBC), SSD_CONV ** -0.5)
    inp['ssd_conv_b'] = nrm((E, SSD_XBC), 0.02)
    inp['ssd_a_log_f'] = jnp.log(jax.random.uniform(next(keys), (E, SSD_HEADS), F32, 1.0, 16.0))
    inp['ssd_a_log_b'] = jnp.log(jax.random.uniform(next(keys), (E, SSD_HEADS), F32, 1.0, 16.0))
    lo, hi = math.log(1e-3), math.log(1e-1)
    dt_f = jnp.exp(jax.random.uniform(next(keys), (E, SSD_HEADS), F32, lo, hi))
    dt_b = jnp.exp(jax.random.uniform(next(keys), (E, SSD_HEADS), F32, lo, hi))
    inp['ssd_dt_bias_f'] = dt_f + jnp.log(-jnp.expm1(-dt_f))
    inp['ssd_dt_bias_b'] = dt_b + jnp.log(-jnp.expm1(-dt_b))
    inp['ssd_d'] = gain((E, SSD_HEADS))
    inp['ssd_norm_g'] = gain((E, SSD_INNER))
    inp['mla_q_norm_g'] = gain((E, MLA_Q_RANK))
    inp['mla_w_q_up'] = nrm((E, MLA_Q_RANK, MLA_HEADS * (MLA_NOPE + MLA_ROPE)), MLA_Q_RANK ** -0.5)
    inp['mla_kv_norm_g'] = gain((E, MLA_KV_RANK))
    inp['mla_w_kv_up'] = nrm((E, MLA_KV_RANK, MLA_HEADS * (MLA_NOPE + MLA_V)), MLA_KV_RANK ** -0.5)
    inp['w_out0'] = nrm((E, MIX0, D_MODEL), BETA * MIX0 ** -0.5)
    inp['ffn_w_gate'] = nrm((E, D_MODEL, D_FF), D_MODEL ** -0.5)
    inp['ffn_w_up'] = nrm((E, D_MODEL, D_FF), D_MODEL ** -0.5)
    inp['ffn_w_down'] = nrm((E, D_FF, D_MODEL), BETA * D_FF ** -0.5)
    inp['w_in1'] = nrm((O, D_MODEL, IN1), D_MODEL ** -0.5)
    base = jnp.log(-jnp.log1p(-(2.0 ** (-5.0 - jnp.arange(RET_HEADS, dtype=F32)))))
    inp['ret_decay_f'] = base[None, :] + nrm((O, RET_HEADS), 0.05)
    inp['ret_decay_b'] = base[None, :] + nrm((O, RET_HEADS), 0.05)
    inp['w_out1'] = nrm((O, MIX1, D_MODEL), BETA * MIX1 ** -0.5)
    inp['moe_router'] = nrm((O, D_MODEL, N_EXPERTS), D_MODEL ** -0.5)
    inp['moe_w_gate'] = nrm((O, N_EXPERTS, D_MODEL, D_FF_EXPERT), D_MODEL ** -0.5)
    inp['moe_w_up'] = nrm((O, N_EXPERTS, D_MODEL, D_FF_EXPERT), D_MODEL ** -0.5)
    inp['moe_w_down'] = nrm((O, N_EXPERTS, D_FF_EXPERT, D_MODEL), BETA * D_FF_EXPERT ** -0.5)
    return inp


def reference(x_prompt, x_sample, cache_mla_ckv, cache_mla_krope, state_ssd_f, state_ssd_b,
              state_ret_f, state_ret_b, c, c_ctx, ada_w, ada_b, ln1_g, ln1_b, ln2_g, ln2_b,
              w_in0, ssd_conv_w, ssd_conv_b, ssd_a_log_f, ssd_a_log_b, ssd_dt_bias_f,
              ssd_dt_bias_b, ssd_d, ssd_norm_g, mla_q_norm_g, mla_w_q_up, mla_kv_norm_g,
              mla_w_kv_up, w_out0, ffn_w_gate, ffn_w_up, ffn_w_down, w_in1, ret_decay_f,
              ret_decay_b, w_out1, moe_router, moe_w_gate, moe_w_up, moe_w_down):
    bp = x_prompt.shape[0]
    n_lat = x_sample.shape[1]
    rope_mla = axial_rope(n_lat, MLA_ROPE)
    rope_ret = axial_rope(n_lat, RET_QK)
    xc, xl = x_prompt, x_sample
    ckv_list, krope_list, ssdf_list, ssdb_list, retf_list, retb_list = [], [], [], [], [], []
    for i in range(DEPTH):
        j = i // 2
        mc = adaln_vectors(c_ctx[None, :], ada_w[i], ada_b[i])
        ml = adaln_vectors(c, ada_w[i], ada_b[i])
        hc = modulate(xc, mc[0], mc[1])
        hl = modulate(xl, ml[0], ml[1])
        if i % 2 == 0:
            even_p = (w_in0[j], ssd_conv_w[j], ssd_conv_b[j], ssd_a_log_f[j], ssd_a_log_b[j],
                      ssd_dt_bias_f[j], ssd_dt_bias_b[j], ssd_d[j], ssd_norm_g[j],
                      mla_q_norm_g[j], mla_w_q_up[j], mla_kv_norm_g[j], mla_w_kv_up[j], w_out0[j])
            zero_state = jnp.zeros((bp, SSD_HEADS, SSD_HEAD_DIM, SSD_STATE), xc.dtype)
            oc, ckv, krope, sf, sb = even_mixer(hc, zero_state, zero_state, None, None, *even_p)
            ol, _, _, _, _ = even_mixer(hl, state_ssd_f[:, j], state_ssd_b[:, j],
                                        (cache_mla_ckv[:, j], cache_mla_krope[:, j]),
                                        rope_mla, *even_p)
            ckv_list.append(ckv)
            krope_list.append(krope)
            ssdf_list.append(sf)
            ssdb_list.append(sb)
        else:
            zero_state = jnp.zeros((bp, RET_HEADS, RET_V, RET_QK), xc.dtype)
            oc, rf, rb = retention_mixer(hc, zero_state, zero_state, None, w_in1[j],
                                         ret_decay_f[j], ret_decay_b[j], w_out1[j])
            ol, _, _ = retention_mixer(hl, state_ret_f[:, j], state_ret_b[:, j], rope_ret,
                                       w_in1[j], ret_decay_f[j], ret_decay_b[j], w_out1[j])
            retf_list.append(rf)
            retb_list.append(rb)
        xc = layer_norm(ALPHA * xc + mc[2] * oc, ln1_g[i], ln1_b[i])
        xl = layer_norm(ALPHA * xl + ml[2] * ol, ln1_g[i], ln1_b[i])
        hc = modulate(xc, mc[3], mc[4])
        hl = modulate(xl, ml[3], ml[4])
        if i % 2 == 0:
            fc = swiglu(hc, ffn_w_gate[j], ffn_w_up[j], ffn_w_down[j])
            fl = swiglu(hl, ffn_w_gate[j], ffn_w_up[j], ffn_w_down[j])
        else:
            fc = moe_swiglu(hc, moe_router[j], moe_w_gate[j], moe_w_up[j], moe_w_down[j])
            fl = moe_swiglu(hl, moe_router[j], moe_w_gate[j], moe_w_up[j], moe_w_down[j])
        xc = layer_norm(ALPHA * xc + mc[5] * fc, ln2_g[i], ln2_b[i])
        xl = layer_norm(ALPHA * xl + ml[5] * fl, ln2_g[i], ln2_b[i])
    y_prompt, y_sample = xc, xl
    new_mla_ckv = jnp.stack(ckv_list, axis=1)
    new_mla_krope = jnp.stack(krope_list, axis=1)
    new_ssd_f = jnp.stack(ssdf_list, axis=1)
    new_ssd_b = jnp.stack(ssdb_list, axis=1)
    new_ret_f = jnp.stack(retf_list, axis=1)
    new_ret_b = jnp.stack(retb_list, axis=1)
    return (y_prompt, y_sample, new_mla_ckv, new_mla_krope, new_ssd_f, new_ssd_b, new_ret_f, new_ret_b)
```

```python
import functools
import math

import jax
import jax.numpy as jnp
from jax import lax
from jax.experimental import pallas as pl
from jax.experimental.pallas import tpu as pltpu

F32 = jnp.float32
BF16 = jnp.bfloat16
I32 = jnp.int32

D_MODEL = 1024
DEPTH = 2
GRID_W = 64
CHUNK = 128
SSD_HEADS = 8
SSD_HEAD_DIM = 64
SSD_INNER = SSD_HEADS * SSD_HEAD_DIM
SSD_GROUPS = 2
SSD_STATE = 64
SSD_XBC = SSD_INNER + 2 * SSD_GROUPS * SSD_STATE
MLA_HEADS = 8
MLA_NOPE = 64
MLA_ROPE = 32
MLA_V = 64
MLA_Q_RANK = 384
MLA_KV_RANK = 256
MLA_OUT = MLA_HEADS * MLA_V
RET_HEADS = 8
RET_QK = 128
RET_V = 256
MIX1 = RET_HEADS * RET_V
D_FF = 2816
N_EXPERTS = 8
D_FF_EXPERT = 3584
ALPHA = (2 * DEPTH) ** 0.25
LN_EPS = 1e-5
RMS_EPS = 1e-6
ROPE_BASE = 10000.0

LANES = 128
VMEM_LIMIT = 56 * 1024 * 1024
NEG_BIG = -1e30

MOE_SLOT_TILE = 512
MOE_TOK_CHUNK = 512
MOE_FF_CHUNK = 512

_NT = (((1,), (1,)), ((), ()))
_TN = (((0,), (0,)), ((), ()))


def _cparams(*sem):
    return pltpu.CompilerParams(dimension_semantics=sem, vmem_limit_bytes=VMEM_LIMIT)


def _resident(shape):
    nd = len(shape)
    return pl.BlockSpec(shape, lambda *_: (0,) * nd, pipeline_mode=pl.Buffered(1))


def _silu(x):
    return x * jax.nn.sigmoid(x)


def _dot(a, b):
    return jnp.dot(a, b, preferred_element_type=F32)


def _cond_index(row, n_ctx, lat_len):
    return jnp.where(row < n_ctx, 0, 1 + (row - n_ctx) // lat_len)


def _layer_norm(y, g, b):
    mu = jnp.mean(y, -1, keepdims=True)
    d = y - mu
    var = jnp.mean(d * d, -1, keepdims=True)
    return d * lax.rsqrt(var + LN_EPS) * g + b


def _ada_kernel(c_ref, w_ref, b_ref, o_ref):
    s = _silu(c_ref[...])
    o_ref[...] = jnp.dot(s, w_ref[...], precision=lax.Precision.HIGHEST,
                         preferred_element_type=F32) + b_ref[...]


def _ada_vectors(cond, ada_w, ada_b):
    r = cond.shape[0]
    tn = 1024
    out = pl.pallas_call(
        _ada_kernel,
        grid=(DEPTH, 6 * D_MODEL // tn),
        in_specs=[pl.BlockSpec((r, D_MODEL), lambda l, j: (0, 0)),
                  pl.BlockSpec((None, D_MODEL, tn), lambda l, j: (l, 0, j)),
                  pl.BlockSpec((None, 1, tn), lambda l, j: (l, 0, j))],
        out_specs=pl.BlockSpec((None, r, tn), lambda l, j: (l, 0, j)),
        out_shape=jax.ShapeDtypeStruct((DEPTH, r, 6 * D_MODEL), F32),
        compiler_params=_cparams("parallel", "parallel"),
        name="ada",
    )(cond, ada_w, ada_b.reshape(DEPTH, 1, 6 * D_MODEL))
    out = out.reshape(DEPTH, r, 6, D_MODEL)
    return jnp.pad(out, ((0, 0), (0, 0), (0, 2), (0, 0)))


def _inproj_kernel(x_ref, mod_ref, w_ref, *o_refs, segs, shift_row):
    x = x_ref[...]
    h = x * (1.0 + mod_ref[shift_row + 1:shift_row + 2, :]) + mod_ref[shift_row:shift_row + 1, :]
    h = h.astype(BF16)
    for o_ref, (a, b) in zip(o_refs, segs):
        o_ref[...] = _dot(h, w_ref[:, a:b]).astype(o_ref.dtype)


def _inproj(x, mods, w, seg_widths, seg_dtypes, *, shift_row, tm, n_ctx, lat_len, name):
    nt = x.shape[0]
    segs, acc = [], 0
    for wd in seg_widths:
        segs.append((acc, acc + wd))
        acc += wd
    assert acc == w.shape[1]
    cmap = lambda i: (_cond_index(i * tm, n_ctx, lat_len), 0, 0)
    return pl.pallas_call(
        functools.partial(_inproj_kernel, segs=tuple(segs), shift_row=shift_row),
        grid=(nt // tm,),
        in_specs=[pl.BlockSpec((tm, D_MODEL), lambda i: (i, 0)),
                  pl.BlockSpec((None, 8, D_MODEL), cmap),
                  _resident(w.shape)],
        out_specs=[pl.BlockSpec((tm, wd), lambda i: (i, 0)) for wd in seg_widths],
        out_shape=[jax.ShapeDtypeStruct((nt, wd), dt) for wd, dt in zip(seg_widths, seg_dtypes)],
        compiler_params=_cparams("parallel"),
        name=name,
    )(x, mods, w)


def _cumsum_rows(tril_bf, x):
    hi = x.astype(BF16)
    r = x - hi.astype(F32)
    mid = r.astype(BF16)
    lo = (r - mid.astype(F32)).astype(BF16)
    return _dot(tril_bf, hi) + _dot(tril_bf, mid) + _dot(tril_bf, lo)


def _ssd_kernel(z_ref, xbc_ref, dt_ref, sf0_ref, sb0_ref, cw_ref, cb_ref, dtb_ref, alog_ref,
                dsk_ref, ng_ref, y_ref, sf_ref, sb_ref,
                yacc, xs_s, cm_s, bmt_s, xb_s, erb_s, xf_s, *, seq_len):
    nc = seq_len // CHUNK
    hd, ns = SSD_HEAD_DIM, SSD_STATE
    gw = (SSD_HEADS // SSD_GROUPS) * hd
    ri = lax.broadcasted_iota(I32, (CHUNK, CHUNK), 0)
    ci = lax.broadcasted_iota(I32, (CHUNK, CHUNK), 1)
    lower = ri >= ci
    upper = ri <= ci
    tril_bf = jnp.where(lower, 1.0, 0.0).astype(BF16)
    rowid = lax.broadcasted_iota(I32, (CHUNK, 1), 0)
    lane = lax.broadcasted_iota(I32, (CHUNK, LANES), 1)

    sf_ref[...] = sf0_ref[...]
    sb_ref[...] = sb0_ref[...]

    def fwd(c, carry):
        r0 = pl.multiple_of(c * CHUNK, CHUNK)
        rows = pl.ds(r0, CHUNK)
        cur = xbc_ref[rows, :]
        pstart = pl.multiple_of(jnp.maximum(r0 - 8, 0), 8)
        nstart = pl.multiple_of(jnp.minimum(r0 + CHUNK, seq_len - 8), 8)
        prev_row = xbc_ref[pl.ds(pstart, 8), :][7:8, :] * (c > 0).astype(F32)
        next_row = xbc_ref[pl.ds(nstart, 8), :][0:1, :] * (c < nc - 1).astype(F32)
        sh_prev = jnp.where(rowid == 0, prev_row, pltpu.roll(cur, 1, axis=0))
        sh_next = jnp.where(rowid == CHUNK - 1, next_row, pltpu.roll(cur, CHUNK - 1, axis=0))
        conv = cw_ref[0:1, :] * sh_prev + cw_ref[1:2, :] * cur + cw_ref[2:3, :] * sh_next + cb_ref[...]
        u = _silu(conv)
        xs = u[:, :SSD_INNER]
        bm = u[:, SSD_INNER:SSD_INNER + LANES]
        cm = u[:, SSD_INNER + LANES:]

        xr = dt_ref[rows, :] + dtb_ref[...]
        dt = jnp.maximum(xr, 0.0) + jnp.log1p(jnp.exp(-jnp.abs(xr)))
        la = -dt * jnp.exp(alog_ref[...])
        facs = _cumsum_rows(tril_bf, la)
        racs = facs[CHUNK - 1:CHUNK, :] - facs + la
        packed = jnp.where(lane < SSD_HEADS, facs,
                           jnp.where(lane < 2 * SSD_HEADS, racs, pltpu.roll(dt, 2 * SSD_HEADS, axis=1)))
        packed_t = packed.T
        e_f = jnp.exp(facs)
        e_r = jnp.exp(racs)
        w_f = dt * jnp.exp(facs[CHUNK - 1:CHUNK, :] - facs)
        w_b = dt * jnp.exp(racs[0:1, :] - racs)

        cm_bf = cm.astype(BF16)
        bm_bf = bm.astype(BF16)
        bmt_bf = bm.T.astype(BF16)
        xs_bf = xs.astype(BF16)
        for g in range(SSD_GROUPS):
            gl = slice(g * ns, (g + 1) * ns)
            s_g = lax.dot_general(cm_bf[:, gl], bm_bf[:, gl], _NT, preferred_element_type=F32)
            yoff = _dot(cm_bf[:, gl], sf_ref[:, g * gw:(g + 1) * gw].astype(BF16))
            for hh in range(SSD_HEADS // SSD_GROUPS):
                h = g * (SSD_HEADS // SSD_GROUPS) + hh
                hb = SSD_HEADS + h
                hs = slice(h * hd, (h + 1) * hd)
                seg_f = facs[:, h:h + 1] - packed_t[h:h + 1, :]
                seg_b = racs[:, hb:hb + 1] - packed_t[hb:hb + 1, :]
                d_f = jnp.exp(jnp.where(lower, seg_f, NEG_BIG))
                d_b = jnp.exp(jnp.where(upper, seg_b, NEG_BIG))
                dt_f_row = packed_t[2 * SSD_HEADS + h:2 * SSD_HEADS + h + 1, :]
                dt_b_row = packed_t[2 * SSD_HEADS + hb:2 * SSD_HEADS + hb + 1, :]
                m = (s_g * (d_f * dt_f_row + d_b * dt_b_row)).astype(BF16)
                y_h = _dot(m, xs_bf[:, hs]) + e_f[:, h:h + 1] * yoff[:, hh * hd:(hh + 1) * hd]
                yacc[rows, hs] = y_h
                xf_s[:, hs] = (xs[:, hs] * w_f[:, h:h + 1]).astype(BF16)
                xb_s[rows, hs] = (xs[:, hs] * w_b[:, hb:hb + 1]).astype(BF16)
        for g in range(SSD_GROUPS):
            new = _dot(bmt_bf[g * ns:(g + 1) * ns, :], xf_s[:, g * gw:(g + 1) * gw])
            for hh in range(SSD_HEADS // SSD_GROUPS):
                h = g * (SSD_HEADS // SSD_GROUPS) + hh
                hs = slice(h * hd, (h + 1) * hd)
                sf_ref[:, hs] = e_f[CHUNK - 1:CHUNK, h:h + 1] * sf_ref[:, hs] + new[:, hh * hd:(hh + 1) * hd]
        xs_s[rows, :] = xs
        cm_s[rows, :] = cm_bf
        bmt_s[c] = bmt_bf
        erb_s[rows, :] = e_r
        return carry

    lax.fori_loop(0, nc, fwd, 0)

    def bwd(i, carry):
        c = nc - 1 - i
        r0 = pl.multiple_of(c * CHUNK, CHUNK)
        rows = pl.ds(r0, CHUNK)
        cm_bf = cm_s[rows, :]
        bmt_bf = bmt_s[c]
        e_r = erb_s[rows, :]
        for g in range(SSD_GROUPS):
            gl = slice(g * ns, (g + 1) * ns)
            yoff = _dot(cm_bf[:, gl], sb_ref[:, g * gw:(g + 1) * gw].astype(BF16))
            new = _dot(bmt_bf[g * ns:(g + 1) * ns, :], xb_s[rows, g * gw:(g + 1) * gw])
            for hh in range(SSD_HEADS // SSD_GROUPS):
                h = g * (SSD_HEADS // SSD_GROUPS) + hh
                hb = SSD_HEADS + h
                hs = slice(h * hd, (h + 1) * hd)
                yacc[rows, hs] += e_r[:, hb:hb + 1] * yoff[:, hh * hd:(hh + 1) * hd]
                sb_ref[:, hs] = e_r[0:1, hb:hb + 1] * sb_ref[:, hs] + new[:, hh * hd:(hh + 1) * hd]
        yv = yacc[rows, :] + dsk_ref[...] * xs_s[rows, :]
        gz = yv * _silu(z_ref[rows, :])
        ms = jnp.mean(gz * gz, -1, keepdims=True)
        y_ref[rows, :] = (gz * lax.rsqrt(ms + RMS_EPS) * ng_ref[...]).astype(y_ref.dtype)
        return carry

    lax.fori_loop(0, nc, bwd, 0)


def _ssd(z, xbc, dt, sf0, sb0, params, *, row0, n_seq, seq_len, name):
    cw, cb, dtb, alog, dsk, ng = params
    nc = seq_len // CHUNK
    blk0 = row0 // seq_len
    rmap = lambda b: (b + blk0, 0)
    smap = lambda b: (b, 0, 0)
    const = lambda b: (0, 0)
    st = jax.ShapeDtypeStruct((n_seq, SSD_STATE, SSD_INNER), F32)
    return pl.pallas_call(
        functools.partial(_ssd_kernel, seq_len=seq_len),
        grid=(n_seq,),
        in_specs=[pl.BlockSpec((seq_len, SSD_INNER), rmap),
                  pl.BlockSpec((seq_len, SSD_XBC), rmap),
                  pl.BlockSpec((seq_len, LANES), rmap),
                  pl.BlockSpec((None, SSD_STATE, SSD_INNER), smap),
                  pl.BlockSpec((None, SSD_STATE, SSD_INNER), smap),
                  pl.BlockSpec(cw.shape, const), pl.BlockSpec(cb.shape, const),
                  pl.BlockSpec(dtb.shape, const), pl.BlockSpec(alog.shape, const),
                  pl.BlockSpec(dsk.shape, const), pl.BlockSpec(ng.shape, const)],
        out_specs=[pl.BlockSpec((seq_len, SSD_INNER), lambda b: (b, 0)),
                   pl.BlockSpec((None, SSD_STATE, SSD_INNER), smap),
                   pl.BlockSpec((None, SSD_STATE, SSD_INNER), smap)],
        out_shape=[jax.ShapeDtypeStruct((n_seq * seq_len, SSD_INNER), BF16), st, st],
        scratch_shapes=[pltpu.VMEM((seq_len, SSD_INNER), F32),
                        pltpu.VMEM((seq_len, SSD_INNER), F32),
                        pltpu.VMEM((seq_len, LANES), BF16),
                        pltpu.VMEM((nc, LANES, CHUNK), BF16),
                        pltpu.VMEM((seq_len, SSD_INNER), BF16),
                        pltpu.VMEM((seq_len, LANES), F32),
                        pltpu.VMEM((CHUNK, SSD_INNER), BF16)],
        compiler_params=_cparams("parallel"),
        name=name,
    )(z, xbc, dt, sf0, sb0, cw, cb, dtb, alog, dsk, ng)


def _rms(x, g):
    return x * lax.rsqrt(jnp.mean(x * x, -1, keepdims=True) + RMS_EPS) * g


def _mla_prep_kernel(*refs, do_q, do_norm, do_rope):
    it = iter(refs)
    qlat_ref = next(it) if do_q else None
    kv_ref = next(it)
    kr_ref = next(it)
    cos_ref = next(it) if do_rope else None
    sin_ref = next(it) if do_rope else None
    if do_q:
        gq_ref, wq_ref = next(it), next(it)
        wqr_ref = next(it) if do_rope else None
    gkv_ref = next(it) if do_norm else None
    wk_ref, wv_ref = next(it), next(it)
    q_out = next(it) if do_q else None
    k_out, v_out = next(it), next(it)
    ckv_out = next(it) if do_norm else None

    if do_rope:
        cs, sn = cos_ref[...], sin_ref[...]
    if do_q:
        qn = _rms(qlat_ref[...], gq_ref[...]).astype(BF16)
        qa = _dot(qn, wq_ref[...])
        if do_rope:
            qb = _dot(qn, wqr_ref[...])
        scale = (MLA_NOPE + MLA_ROPE) ** -0.5
        for h in range(MLA_HEADS):
            hs = slice(h * LANES, (h + 1) * LANES)
            qh = qa[:, hs] * cs + qb[:, hs] * sn if do_rope else qa[:, hs]
            q_out[h] = (qh * scale).astype(BF16)
    ckv = kv_ref[...]
    if do_norm:
        ckv = _rms(ckv, gkv_ref[...])
        ckv_out[...] = ckv
    ckv_bf = ckv.astype(BF16)
    kr = kr_ref[...]
    krp = kr[:, :LANES] * cs + kr[:, LANES:] * sn if do_rope else kr[:, :LANES]
    kn = _dot(ckv_bf, wk_ref[...])
    for h in range(MLA_HEADS):
        k_out[h] = (kn[:, h * LANES:(h + 1) * LANES] + krp).astype(BF16)
    v_out[...] = _dot(ckv_bf, wv_ref[...]).astype(BF16)


def _mla_prep(qlat, kv, kr, tables, weights, *, row0, n_rows, do_q, do_norm, tm, name):
    do_rope = tables is not None
    gq, wq, wqr, gkv, wk, wv = weights
    b0 = row0 // tm
    rmap = lambda i: (i + b0, 0)
    omap = lambda i: (i, 0)
    hmap = lambda i: (0, i, 0)
    ins, specs = [], []

    def add(a, spec):
        ins.append(a)
        specs.append(spec)

    if do_q:
        add(qlat, pl.BlockSpec((tm, MLA_Q_RANK), rmap))
    add(kv, pl.BlockSpec((tm, MLA_KV_RANK), rmap))
    add(kr, pl.BlockSpec((tm, 2 * LANES), rmap))
    if do_rope:
        lat_tiles = tables[0].shape[0] // tm
        tmap = lambda i: (i % lat_tiles, 0)
        add(tables[0], pl.BlockSpec((tm, LANES), tmap))
        add(tables[1], pl.BlockSpec((tm, LANES), tmap))
    if do_q:
        add(gq, _resident(gq.shape))
        add(wq, _resident(wq.shape))
        if do_rope:
            add(wqr, _resident(wqr.shape))
    if do_norm:
        add(gkv, _resident(gkv.shape))
    add(wk, _resident(wk.shape))
    add(wv, _resident(wv.shape))
    out_shape, out_specs = [], []
    if do_q:
        out_shape.append(jax.ShapeDtypeStruct((MLA_HEADS, n_rows, LANES), BF16))
        out_specs.append(pl.BlockSpec((MLA_HEADS, tm, LANES), hmap))
    out_shape.append(jax.ShapeDtypeStruct((MLA_HEADS, n_rows, LANES), BF16))
    out_specs.append(pl.BlockSpec((MLA_HEADS, tm, LANES), hmap))
    out_shape.append(jax.ShapeDtypeStruct((n_rows, MLA_OUT), BF16))
    out_specs.append(pl.BlockSpec((tm, MLA_OUT), omap))
    if do_norm:
        out_shape.append(jax.ShapeDtypeStruct((n_rows, MLA_KV_RANK), F32))
        out_specs.append(pl.BlockSpec((tm, MLA_KV_RANK), omap))
    return pl.pallas_call(
        functools.partial(_mla_prep_kernel, do_q=do_q, do_norm=do_norm, do_rope=do_rope),
        grid=(n_rows // tm,),
        in_specs=specs, out_specs=out_specs, out_shape=out_shape,
        compiler_params=_cparams("parallel"),
        name=name,
    )(*ins)


def _attn_kernel(*refs, seq_len, cache_len, kblk):
    if cache_len:
        q_ref, k_ref, v_ref, kc_ref, vc_ref, o_ref, s_scr = refs
    else:
        q_ref, k_ref, v_ref, o_ref, s_scr = refs
    tq = q_ref.shape[1]
    blocks = [(k_ref, v_ref, i * kblk) for i in range(seq_len // kblk)]
    if cache_len:
        blocks += [(kc_ref, vc_ref, i * kblk) for i in range(cache_len // kblk)]
    outs = []
    for hh in range(2):
        q = q_ref[hh]
        m = jnp.full((tq, 1), NEG_BIG, F32)
        for j, (kr, _, off) in enumerate(blocks):
            s = lax.dot_general(q, kr[hh, off:off + kblk, :], _NT, preferred_element_type=F32)
            s_scr[:, j * kblk:(j + 1) * kblk] = s
            m = jnp.maximum(m, jnp.max(s, -1, keepdims=True))
        l = jnp.zeros((tq, 1), F32)
        acc = jnp.zeros((tq, LANES), F32)
        for j, (_, vr, off) in enumerate(blocks):
            p = jnp.exp(s_scr[:, j * kblk:(j + 1) * kblk] - m)
            l = l + jnp.sum(p, -1, keepdims=True)
            acc = acc + _dot(p.astype(BF16), vr[off:off + kblk, :])
        outs.append(acc / l)
    lane = lax.broadcasted_iota(I32, (tq, LANES), 1)
    o_ref[...] = jnp.where(lane < MLA_V, outs[0], outs[1]).astype(o_ref.dtype)


def _attention(q, k, v, kc, vc, *, n_seq, seq_len, cache_len, tq, name):
    nq = seq_len // tq
    kblk = min(512, seq_len)
    n = n_seq * seq_len
    ins = [q, k, v]
    specs = [pl.BlockSpec((2, tq, LANES), lambda b, hp, qi: (hp, b * nq + qi, 0)),
             pl.BlockSpec((2, seq_len, LANES), lambda b, hp, qi: (hp, b, 0)),
             pl.BlockSpec((seq_len, LANES), lambda b, hp, qi: (b, hp))]
    if cache_len:
        ins += [kc, vc]
        specs += [pl.BlockSpec((2, cache_len, LANES), lambda b, hp, qi: (hp, b, 0)),
                  pl.BlockSpec((cache_len, LANES), lambda b, hp, qi: (b, hp))]
    return pl.pallas_call(
        functools.partial(_attn_kernel, seq_len=seq_len, cache_len=cache_len, kblk=kblk),
        grid=(n_seq, MLA_HEADS // 2, nq),
        in_specs=specs,
        out_specs=pl.BlockSpec((tq, LANES), lambda b, hp, qi: (b * nq + qi, hp)),
        out_shape=jax.ShapeDtypeStruct((n, MLA_OUT), BF16),
        scratch_shapes=[pltpu.VMEM((tq, seq_len + cache_len), F32)],
        compiler_params=_cparams("parallel", "parallel", "arbitrary"),
        name=name,
    )(*ins)


def _outproj_kernel(*refs, n_in, gate_row):
    a_refs, w_refs = refs[:n_in], refs[n_in:2 * n_in]
    x_ref, mod_ref, g_ref, b_ref, o_ref = refs[2 * n_in:]
    acc = _dot(a_refs[0][...], w_refs[0][...])
    for a_ref, w_ref in zip(a_refs[1:], w_refs[1:]):
        acc = acc + _dot(a_ref[...], w_ref[...])
    y = ALPHA * x_ref[...] + mod_ref[gate_row:gate_row + 1, :] * acc
    o_ref[...] = _layer_norm(y, g_ref[...], b_ref[...])


def _outproj(acts, ws, x, mods, g, b, *, gate_row, tm, n_ctx, lat_len, name):
    nt = x.shape[0]
    n_in = len(acts)
    cmap = lambda i: (_cond_index(i * tm, n_ctx, lat_len), 0, 0)
    specs = [pl.BlockSpec((tm, a.shape[1]), lambda i: (i, 0)) for a in acts]
    specs += [_resident(w.shape) for w in ws]
    specs += [pl.BlockSpec((tm, D_MODEL), lambda i: (i, 0)),
              pl.BlockSpec((None, 8, D_MODEL), cmap),
              _resident(g.shape), _resident(b.shape)]
    return pl.pallas_call(
        functools.partial(_outproj_kernel, n_in=n_in, gate_row=gate_row),
        grid=(nt // tm,),
        in_specs=specs,
        out_specs=pl.BlockSpec((tm, D_MODEL), lambda i: (i, 0)),
        out_shape=jax.ShapeDtypeStruct((nt, D_MODEL), F32),
        compiler_params=_cparams("parallel"),
        name=name,
    )(*acts, *ws, x, mods, g, b)


def _ffn_kernel(x_ref, mod_ref, wg_ref, wu_ref, wd_ref, g_ref, b_ref, o_ref, *, ff_chunks):
    x = x_ref[...]
    h = (x * (1.0 + mod_ref[4:5, :]) + mod_ref[3:4, :]).astype(BF16)
    acc = None
    for a, b in ff_chunks:
        gt = _dot(h, wg_ref[:, a:b])
        up = _dot(h, wu_ref[:, a:b])
        act = (_silu(gt) * up).astype(BF16)
        part = _dot(act, wd_ref[a:b, :])
        acc = part if acc is None else acc + part
    y = ALPHA * x + mod_ref[5:6, :] * acc
    o_ref[...] = _layer_norm(y, g_ref[...], b_ref[...])


def _ffn(x, mods, wg, wu, wd, g, b, *, tm, n_ctx, lat_len):
    nt = x.shape[0]
    ff = wg.shape[1]
    chunks, a = [], 0
    while a < ff:
        chunks.append((a, min(a + 512, ff)))
        a += 512
    cmap = lambda i: (_cond_index(i * tm, n_ctx, lat_len), 0, 0)
    return pl.pallas_call(
        functools.partial(_ffn_kernel, ff_chunks=tuple(chunks)),
        grid=(nt // tm,),
        in_specs=[pl.BlockSpec((tm, D_MODEL), lambda i: (i, 0)),
                  pl.BlockSpec((None, 8, D_MODEL), cmap),
                  _resident(wg.shape), _resident(wu.shape), _resident(wd.shape),
                  _resident(g.shape), _resident(b.shape)],
        out_specs=pl.BlockSpec((tm, D_MODEL), lambda i: (i, 0)),
        out_shape=jax.ShapeDtypeStruct((nt, D_MODEL), F32),
        compiler_params=_cparams("parallel"),
        name="ffn",
    )(x, mods, wg, wu, wd, g, b)


def _ret_kernel(*refs, seq_len, do_rope):
    if do_rope:
        (q_ref, k_ref, v_ref, g_ref, cos_ref, sin_ref, dec_ref, sf0_ref, sb0_ref,
         y_ref, sf_ref, sb_ref, yacc, q_s, kb_s) = refs
    else:
        (q_ref, k_ref, v_ref, g_ref, dec_ref, sf0_ref, sb0_ref,
         y_ref, sf_ref, sb_ref, yacc, q_s, kb_s) = refs
    nc = seq_len // CHUNK
    la_f = -jnp.exp(dec_ref[0:1, :])
    la_b = -jnp.exp(dec_ref[1:2, :])
    ri = lax.broadcasted_iota(I32, (CHUNK, CHUNK), 0)
    ci = lax.broadcasted_iota(I32, (CHUNK, CHUNK), 1)
    dij = (ri - ci).astype(F32)
    d_comb = (jnp.exp(jnp.where(ri >= ci, dij * la_f[:, :CHUNK], NEG_BIG)) +
              jnp.exp(jnp.where(ri <= ci, -dij * la_b[:, :CHUNK], NEG_BIG)))
    pos_k = lax.broadcasted_iota(I32, (CHUNK, RET_QK), 0).astype(F32)
    pos_v = lax.broadcasted_iota(I32, (CHUNK, RET_V), 0).astype(F32)
    eoff_f = jnp.exp((pos_v + 1.0) * la_f)
    eoff_b = jnp.exp((CHUNK - pos_v) * la_b)
    wst_f = jnp.exp((CHUNK - 1.0 - pos_k) * la_f[:, :RET_QK])
    wst_b = jnp.exp(pos_k * la_b[:, :RET_QK])
    cdec_f = jnp.exp(CHUNK * la_f)
    cdec_b = jnp.exp(CHUNK * la_b)

    sf_ref[...] = sf0_ref[...]
    sb_ref[...] = sb0_ref[...]

    def fwd(c, carry):
        rows = pl.ds(pl.multiple_of(c * CHUNK, CHUNK), CHUNK)
        q = q_ref[rows, :]
        k = k_ref[rows, :] * (RET_QK ** -0.5)
        if do_rope:
            cs, sn = cos_ref[rows, :], sin_ref[rows, :]
            q = q * cs + pltpu.roll(q, RET_QK // 2, axis=1) * sn
            k = k * cs + pltpu.roll(k, RET_QK // 2, axis=1) * sn
        q_bf = q.astype(BF16)
        v = v_ref[rows, :]
        s = lax.dot_general(q_bf, k.astype(BF16), _NT, preferred_element_type=F32)
        y = _dot((s * d_comb).astype(BF16), v)
        y = y + _dot(q_bf, sf_ref[...].astype(BF16)) * eoff_f
        yacc[rows, :] = y
        upd = lax.dot_general((k * wst_f).astype(BF16), v, _TN, preferred_element_type=F32)
        sf_ref[...] = cdec_f * sf_ref[...] + upd
        q_s[rows, :] = q_bf
        kb_s[rows, :] = (k * wst_b).astype(BF16)
        return carry

    lax.fori_loop(0, nc, fwd, 0)

    def bwd(i, carry):
        rows = pl.ds(pl.multiple_of((nc - 1 - i) * CHUNK, CHUNK), CHUNK)
        v = v_ref[rows, :]
        y = yacc[rows, :] + _dot(q_s[rows, :], sb_ref[...].astype(BF16)) * eoff_b
        upd = lax.dot_general(kb_s[rows, :], v, _TN, preferred_element_type=F32)
        sb_ref[...] = cdec_b * sb_ref[...] + upd
        mu = jnp.mean(y, -1, keepdims=True)
        d = y - mu
        var = jnp.mean(d * d, -1, keepdims=True)
        yn = d * lax.rsqrt(var + LN_EPS)
        y_ref[rows, :] = (yn * _silu(g_ref[rows, :])).astype(y_ref.dtype)
        return carry

    lax.fori_loop(0, nc, bwd, 0)


def _retention(q, k, v, g, tables, dec, sf0, sb0, *, row0, n_seq, seq_len, name):
    do_rope = tables is not None
    blk0 = row0 // seq_len
    qmap = lambda b, h: (b + blk0, h)
    smap = lambda b, h: (b, h, 0, 0)
    ins = [q, k, v, g]
    specs = [pl.BlockSpec((seq_len, RET_QK), qmap), pl.BlockSpec((seq_len, RET_QK), qmap),
             pl.BlockSpec((seq_len, RET_V), qmap), pl.BlockSpec((seq_len, RET_V), qmap)]
    if do_rope:
        ins += list(tables)
        specs += [pl.BlockSpec((seq_len, RET_QK), lambda b, h: (0, 0))] * 2
    ins += [dec, sf0, sb0]
    specs += [pl.BlockSpec((None, 8, RET_V), lambda b, h: (h, 0, 0)),
              pl.BlockSpec((None, None, RET_QK, RET_V), smap),
              pl.BlockSpec((None, None, RET_QK, RET_V), smap)]
    st = jax.ShapeDtypeStruct((n_seq, RET_HEADS, RET_QK, RET_V), F32)
    return pl.pallas_call(
        functools.partial(_ret_kernel, seq_len=seq_len, do_rope=do_rope),
        grid=(n_seq, RET_HEADS),
        in_specs=specs,
        out_specs=[pl.BlockSpec((seq_len, RET_V), lambda b, h: (b, h)),
                   pl.BlockSpec((None, None, RET_QK, RET_V), smap),
                   pl.BlockSpec((None, None, RET_QK, RET_V), smap)],
        out_shape=[jax.ShapeDtypeStruct((n_seq * seq_len, MIX1), BF16), st, st],
        scratch_shapes=[pltpu.VMEM((seq_len, RET_V), F32),
                        pltpu.VMEM((seq_len, RET_QK), BF16),
                        pltpu.VMEM((seq_len, RET_QK), BF16)],
        compiler_params=_cparams("parallel", "parallel"),
        name=name,
    )(*ins)


def _router_kernel(x_ref, mod_ref, w_ref, hb_ref, route_ref, before_ref, total_ref, carry):
    i = pl.program_id(0)
    tm = x_ref.shape[0]

    @pl.when(i == 0)
    def _():
        carry[...] = jnp.zeros_like(carry)

    x = x_ref[...]
    h = x * (1.0 + mod_ref[4:5, :]) + mod_ref[3:4, :]
    h_hi = h.astype(BF16)
    hb_ref[...] = h_hi
    h_lo = (h - h_hi.astype(F32)).astype(BF16)
    w = w_ref[...]
    w_hi = w.astype(BF16)
    w_lo = (w - w_hi.astype(F32)).astype(BF16)
    logits = _dot(h_hi, w_hi) + (_dot(h_hi, w_lo) + _dot(h_lo, w_hi))
    lane = lax.broadcasted_iota(I32, (tm, LANES), 1)
    logits = jnp.where(lane < N_EXPERTS, logits, NEG_BIG)
    m1 = jnp.max(logits, -1, keepdims=True)
    i1 = jnp.min(jnp.where(logits == m1, lane, LANES), -1, keepdims=True)
    rest = jnp.where(lane == i1, NEG_BIG, logits)
    m2 = jnp.max(rest, -1, keepdims=True)
    i2 = jnp.min(jnp.where(rest == m2, lane, LANES), -1, keepdims=True)
    e = jnp.exp(m2 - m1)
    g1 = 1.0 / (1.0 + e)
    g2 = e / (1.0 + e)
    sel1 = lane == i1
    sel2 = lane == i2
    onehot = jnp.where(sel1 | sel2, 1.0, 0.0)
    ri = lax.broadcasted_iota(I32, (tm, tm), 0)
    ci = lax.broadcasted_iota(I32, (tm, tm), 1)
    strict = jnp.where(ri > ci, 1.0, 0.0).astype(BF16)
    before = carry[0:1, :]
    prefix = _dot(strict, onehot.astype(BF16)) + before
    rank1 = jnp.sum(jnp.where(sel1, prefix, 0.0), -1, keepdims=True)
    rank2 = jnp.sum(jnp.where(sel2, prefix, 0.0), -1, keepdims=True)
    route = jnp.where(lane == 0, i1.astype(F32),
            jnp.where(lane == 1, i2.astype(F32),
            jnp.where(lane == 2, g1,
            jnp.where(lane == 3, g2,
            jnp.where(lane == 4, rank1,
            jnp.where(lane == 5, rank2, 0.0))))))
    route_ref[...] = route
    before_ref[...] = jnp.broadcast_to(before, before_ref.shape)
    after = before + jnp.sum(onehot, 0, keepdims=True)
    carry[...] = jnp.broadcast_to(after, carry.shape)
    total_ref[...] = jnp.broadcast_to(after, total_ref.shape)


def _router(x, mods, w_pad, *, n_ctx, lat_len):
    nt = x.shape[0]
    tm = MOE_TOK_CHUNK
    nchunk = nt // tm
    cmap = lambda i: (_cond_index(i * tm, n_ctx, lat_len), 0, 0)
    return pl.pallas_call(
        _router_kernel,
        grid=(nchunk,),
        in_specs=[pl.BlockSpec((tm, D_MODEL), lambda i: (i, 0)),
                  pl.BlockSpec((None, 8, D_MODEL), cmap),
                  _resident(w_pad.shape)],
        out_specs=[pl.BlockSpec((tm, D_MODEL), lambda i: (i, 0)),
                   pl.BlockSpec((tm, LANES), lambda i: (i, 0)),
                   pl.BlockSpec((None, 8, LANES), lambda i: (i, 0, 0)),
                   pl.BlockSpec((8, LANES), lambda i: (0, 0))],
        out_shape=[jax.ShapeDtypeStruct((nt, D_MODEL), BF16),
                   jax.ShapeDtypeStruct((nt, LANES), F32),
                   jax.ShapeDtypeStruct((nchunk, 8, LANES), F32),
                   jax.ShapeDtypeStruct((8, LANES), F32)],
        scratch_shapes=[pltpu.VMEM((8, LANES), F32)],
        compiler_params=_cparams("arbitrary"),
        name="router",
    )(x, mods, w_pad)


def _dispatch_kernel(pt_ref, pc_ref, pf_ref, tok_ref, hb_ref, o_ref):
    p = pl.program_id(0)
    flag = pf_ref[p]
    base = pc_ref[p] * MOE_TOK_CHUNK
    bs = tok_ref.shape[0]

    def gathered():
        col = lax.broadcasted_iota(I32, (bs, MOE_TOK_CHUNK), 1) + base
        onehot = jnp.where(tok_ref[...] == col, 1.0, 0.0).astype(BF16)
        return _dot(onehot, hb_ref[...]).astype(o_ref.dtype)

    @pl.when(flag == 3)
    def _():
        o_ref[...] = gathered()

    @pl.when(flag == 1)
    def _():
        o_ref[...] += gathered()


def _dispatch(pt, pc, pf, tok_sorted, hb, *, n_slots):
    bs = MOE_SLOT_TILE
    grid_spec = pltpu.PrefetchScalarGridSpec(
        num_scalar_prefetch=3,
        grid=(pt.shape[0],),
        in_specs=[pl.BlockSpec((bs, 1), lambda p, pt, pc, pf: (pt[p], 0)),
                  pl.BlockSpec((MOE_TOK_CHUNK, D_MODEL), lambda p, pt, pc, pf: (pc[p], 0))],
        out_specs=pl.BlockSpec((bs, D_MODEL), lambda p, pt, pc, pf: (pt[p], 0)),
    )
    return pl.pallas_call(
        _dispatch_kernel,
        grid_spec=grid_spec,
        out_shape=jax.ShapeDtypeStruct((n_slots, D_MODEL), BF16),
        compiler_params=_cparams("arbitrary"),
        name="moe_dispatch",
    )(pt, pc, pf, tok_sorted, hb)


def _expert_kernel(te_ref, nu_ref, x_ref, gate_ref, wg_ref, wu_ref, wd_ref, o_ref, acc):
    t = pl.program_id(0)
    f = pl.program_id(1)
    nf = pl.num_programs(1)

    @pl.when(t < nu_ref[0])
    def _():
        x = x_ref[...]
        gt = _dot(x, wg_ref[...])
        up = _dot(x, wu_ref[...])
        part = _dot((_silu(gt) * up).astype(BF16), wd_ref[...])

        @pl.when(f == 0)
        def _():
            acc[...] = part

        @pl.when(f > 0)
        def _():
            acc[...] += part

        @pl.when(f == nf - 1)
        def _():
            o_ref[...] = (gate_ref[...] * acc[...]).astype(o_ref.dtype)


def _experts(te, nu, xs, gate_sorted, wg, wu, wd):
    bs = MOE_SLOT_TILE
    n_slots = xs.shape[0]
    nf = D_FF_EXPERT // MOE_FF_CHUNK

    def tt(t, nu):
        return jnp.minimum(t, nu[0] - 1)

    def ff(t, f, nu):
        return jnp.where(t < nu[0], f, nf - 1)

    grid_spec = pltpu.PrefetchScalarGridSpec(
        num_scalar_prefetch=2,
        grid=(n_slots // bs, nf),
        in_specs=[pl.BlockSpec((bs, D_MODEL), lambda t, f, te, nu: (tt(t, nu), 0)),
                  pl.BlockSpec((bs, 1), lambda t, f, te, nu: (tt(t, nu), 0)),
                  pl.BlockSpec((None, D_MODEL, MOE_FF_CHUNK), lambda t, f, te, nu: (te[tt(t, nu)], 0, ff(t, f, nu))),
                  pl.BlockSpec((None, D_MODEL, MOE_FF_CHUNK), lambda t, f, te, nu: (te[tt(t, nu)], 0, ff(t, f, nu))),
                  pl.BlockSpec((None, MOE_FF_CHUNK, D_MODEL), lambda t, f, te, nu: (te[tt(t, nu)], ff(t, f, nu), 0))],
        out_specs=pl.BlockSpec((bs, D_MODEL), lambda t, f, te, nu: (tt(t, nu), 0)),
        scratch_shapes=[pltpu.VMEM((bs, D_MODEL), F32)],
    )
    return pl.pallas_call(
        _expert_kernel,
        grid_spec=grid_spec,
        out_shape=jax.ShapeDtypeStruct((n_slots, D_MODEL), BF16),
        compiler_params=_cparams("arbitrary", "arbitrary"),
        name="moe_experts",
    )(te, nu, xs, gate_sorted, wg, wu, wd)


def _combine_kernel(qc_ref, qt_ref, qf_ref, pos_ref, ys_ref, x_ref, mod_ref, g_ref, b_ref, o_ref, acc):
    p = pl.program_id(0)
    flag = qf_ref[p]
    base = qt_ref[p] * MOE_SLOT_TILE
    tm = pos_ref.shape[0]

    @pl.when((flag & 1) == 1)
    def _():
        col = lax.broadcasted_iota(I32, (tm, MOE_SLOT_TILE), 1) + base
        pos = pos_ref[...]
        hit = (pos[:, 0:1] == col) | (pos[:, 1:2] == col)
        part = _dot(jnp.where(hit, 1.0, 0.0).astype(BF16), ys_ref[...])

        @pl.when((flag & 2) == 2)
        def _():
            acc[...] = part

        @pl.when((flag & 2) == 0)
        def _():
            acc[...] += part

        @pl.when((flag & 4) == 4)
        def _():
            y = ALPHA * x_ref[...] + mod_ref[5:6, :] * acc[...]
            o_ref[...] = _layer_norm(y, g_ref[...], b_ref[...])


def _combine(qc, qt, qf, pos, ys, x, mods, g, b, *, n_ctx, lat_len):
    nt = x.shape[0]
    tm = MOE_TOK_CHUNK
    cmap = lambda p, qc, qt, qf: (_cond_index(qc[p] * tm, n_ctx, lat_len), 0, 0)
    grid_spec = pltpu.PrefetchScalarGridSpec(
        num_scalar_prefetch=3,
        grid=(qc.shape[0],),
        in_specs=[pl.BlockSpec((tm, 2), lambda p, qc, qt, qf: (qc[p], 0)),
                  pl.BlockSpec((MOE_SLOT_TILE, D_MODEL), lambda p, qc, qt, qf: (qt[p], 0)),
                  pl.BlockSpec((tm, D_MODEL), lambda p, qc, qt, qf: (qc[p], 0)),
                  pl.BlockSpec((None, 8, D_MODEL), cmap),
                  pl.BlockSpec((1, D_MODEL), lambda p, qc, qt, qf: (0, 0)),
                  pl.BlockSpec((1, D_MODEL), lambda p, qc, qt, qf: (0, 0))],
        out_specs=pl.BlockSpec((tm, D_MODEL), lambda p, qc, qt, qf: (qc[p], 0)),
        scratch_shapes=[pltpu.VMEM((tm, D_MODEL), F32)],
    )
    return pl.pallas_call(
        _combine_kernel,
        grid_spec=grid_spec,
        out_shape=jax.ShapeDtypeStruct((nt, D_MODEL), F32),
        compiler_params=_cparams("arbitrary"),
        name="moe_combine",
    )(qc, qt, qf, pos, ys, x, mods, g, b)


def _pair_lists(n_cell, t_lo, order_major_expert, n_pairs_max):
    nchunk, ne = n_cell.shape
    cidx = jnp.broadcast_to(jnp.arange(nchunk, dtype=I32)[:, None], (nchunk, ne))
    if order_major_expert:
        n_flat, lo_flat, c_flat = n_cell.T.reshape(-1), t_lo.T.reshape(-1), cidx.T.reshape(-1)
    else:
        n_flat, lo_flat, c_flat = n_cell.reshape(-1), t_lo.reshape(-1), cidx.reshape(-1)
    ends = jnp.cumsum(n_flat)
    total = ends[-1]
    p = jnp.arange(n_pairs_max, dtype=I32)
    pv = jnp.minimum(p, total - 1)
    cell = jnp.searchsorted(ends, pv, side="right").astype(I32)
    start = ends[cell] - n_flat[cell]
    tile = lo_flat[cell] + (pv - start)
    chunk = c_flat[cell]
    valid = p < total
    key = tile if order_major_expert else chunk
    prev_key = jnp.concatenate([jnp.full((1,), -1, I32), key[:-1]])
    next_key = jnp.concatenate([key[1:], jnp.full((1,), -1, I32)])
    first = key != prev_key
    last = (key != next_key) | (p == total - 1)
    flags = jnp.where(valid, 1 + 2 * first.astype(I32) + 4 * last.astype(I32), 0)
    return chunk.astype(I32), tile.astype(I32), flags.astype(I32)


def _moe(x, mods, router_w, wg, wu, wd, ln_g, ln_b, *, n_ctx, lat_len):
    nt = x.shape[0]
    bs, tc = MOE_SLOT_TILE, MOE_TOK_CHUNK
    nchunk = nt // tc
    n_tiles = (2 * nt) // bs + N_EXPERTS
    n_slots = n_tiles * bs
    n_pairs = nchunk * N_EXPERTS + n_tiles

    w_pad = jnp.pad(router_w, ((0, 0), (0, LANES - N_EXPERTS)))
    hb, route, before, total = _router(x, mods, w_pad, n_ctx=n_ctx, lat_len=lat_len)

    idx = route[:, 0:2].astype(I32)
    gates = route[:, 2:4]
    ranks = route[:, 4:6].astype(I32)
    counts = total[0, :N_EXPERTS].astype(I32)
    padded = ((counts + bs - 1) // bs) * bs
    gend = jnp.cumsum(padded)
    gstart = gend - padded
    pos = gstart[idx] + ranks
    tok = jnp.broadcast_to(jnp.arange(nt, dtype=I32)[:, None], (nt, 2))
    tok_sorted = jnp.full((n_slots,), -1, I32).at[pos.reshape(-1)].set(tok.reshape(-1))
    gate_sorted = jnp.zeros((n_slots,), F32).at[pos.reshape(-1)].set(gates.reshape(-1))
    tile_expert = jnp.minimum(
        jnp.searchsorted(gend, jnp.arange(n_tiles, dtype=I32) * bs, side="right"), N_EXPERTS - 1).astype(I32)
    n_used = (gend[-1] // bs).astype(I32).reshape(1)

    cb = before[:, 0, :N_EXPERTS].astype(I32)
    cb_next = jnp.concatenate([cb[1:], counts[None, :]], 0)
    lo = gstart[None, :] + cb
    hi = gstart[None, :] + cb_next
    t_lo = lo // bs
    n_cell = jnp.where(hi > lo, (hi - 1) // bs - t_lo + 1, 0)
    d_chunk, d_tile, d_flag = _pair_lists(n_cell, t_lo, True, n_pairs)
    c_chunk, c_tile, c_flag = _pair_lists(n_cell, t_lo, False, n_pairs)

    xs = _dispatch(d_tile, d_chunk, d_flag & 3, tok_sorted.reshape(n_slots, 1), hb, n_slots=n_slots)
    ys = _experts(tile_expert, n_used, xs, gate_sorted.reshape(n_slots, 1), wg, wu, wd)
    return _combine(c_chunk, c_tile, c_flag, pos, ys, x, mods, ln_g, ln_b, n_ctx=n_ctx, lat_len=lat_len)


def _axial_angles(n_tok, dim):
    rows = n_tok // GRID_W
    row = jnp.repeat(jnp.arange(rows), GRID_W).astype(F32)
    col = jnp.tile(jnp.arange(GRID_W), rows).astype(F32)
    axis_dim = dim // 2
    inv = 1.0 / (ROPE_BASE ** (jnp.arange(0, axis_dim, 2, dtype=F32) / axis_dim))
    ang = jnp.concatenate([row[:, None] * inv, col[:, None] * inv], -1)
    return jnp.cos(ang), jnp.sin(ang)


def _mla_tables(n_lat):
    cos, sin = _axial_angles(n_lat, MLA_ROPE)
    one = jnp.ones((n_lat, MLA_NOPE), F32)
    zero = jnp.zeros((n_lat, MLA_NOPE), F32)
    pad1 = jnp.ones((n_lat, LANES - MLA_NOPE - MLA_ROPE), F32)
    pad0 = jnp.zeros((n_lat, LANES - MLA_NOPE - MLA_ROPE), F32)
    return (jnp.concatenate([one, cos, cos, pad1], -1), jnp.concatenate([zero, sin, sin, pad0], -1))


def _ret_tables(n_lat):
    cos, sin = _axial_angles(n_lat, RET_QK)
    return jnp.concatenate([cos, cos], -1), jnp.concatenate([-sin, sin], -1)


def _rot_cols(w):
    half = w.shape[1] // 2
    return jnp.concatenate([-w[:, half:], w[:, :half]], 1)


def _in0_weights(w_in0):
    z, xbc, dt, ql, kvl, kr = jnp.split(
        w_in0, [SSD_INNER, SSD_INNER + SSD_XBC, SSD_INNER + SSD_XBC + 2 * SSD_HEADS,
                SSD_INNER + SSD_XBC + 2 * SSD_HEADS + MLA_Q_RANK,
                SSD_INNER + SSD_XBC + 2 * SSD_HEADS + MLA_Q_RANK + MLA_KV_RANK], axis=1)
    dtp = jnp.pad(dt, ((0, 0), (0, LANES - 2 * SSD_HEADS)))
    lpad = ((0, 0), (MLA_NOPE, LANES - MLA_NOPE - MLA_ROPE))
    krp = jnp.concatenate([jnp.pad(kr, lpad), jnp.pad(_rot_cols(kr), lpad)], 1)
    return jnp.concatenate([z, xbc, dtp, ql, kvl, krp], 1).astype(BF16)


def _mla_weights(w_q_up, w_kv_up):
    d = w_q_up.shape[0]
    wq = w_q_up.reshape(d, MLA_HEADS, MLA_NOPE + MLA_ROPE)
    nope, rope = wq[..., :MLA_NOPE], wq[..., MLA_NOPE:]
    half = MLA_ROPE // 2
    rot = jnp.concatenate([-rope[..., half:], rope[..., :half]], -1)
    tail = jnp.zeros((d, MLA_HEADS, LANES - MLA_NOPE - MLA_ROPE), F32)
    wq_pad = jnp.concatenate([nope, rope, tail], -1).reshape(d, MLA_HEADS * LANES)
    wq_rot = jnp.concatenate([jnp.zeros_like(nope), rot, tail], -1).reshape(d, MLA_HEADS * LANES)
    r = w_kv_up.shape[0]
    wkv = w_kv_up.reshape(r, MLA_HEADS, MLA_NOPE + MLA_V)
    wk = jnp.concatenate([wkv[..., :MLA_NOPE], jnp.zeros((r, MLA_HEADS, LANES - MLA_NOPE), F32)], -1)
    wv = wkv[..., MLA_NOPE:]
    return (wq_pad.astype(BF16), wq_rot.astype(BF16),
            wk.reshape(r, MLA_HEADS * LANES).astype(BF16), wv.reshape(r, MLA_OUT).astype(BF16))


def _lane_row(v, width):
    return jnp.pad(v, (0, width - v.shape[0])).reshape(1, width)


def kernel(x_prompt, x_sample, cache_mla_ckv, cache_mla_krope, state_ssd_f, state_ssd_b, state_ret_f, state_ret_b, c, c_ctx, ada_w, ada_b, ln1_g, ln1_b, ln2_g, ln2_b, w_in0, ssd_conv_w, ssd_conv_b, ssd_a_log_f, ssd_a_log_b, ssd_dt_bias_f, ssd_dt_bias_b, ssd_d, ssd_norm_g, mla_q_norm_g, mla_w_q_up, mla_kv_norm_g, mla_w_kv_up, w_out0, ffn_w_gate, ffn_w_up, ffn_w_down, w_in1, ret_decay_f, ret_decay_b, w_out1, moe_router, moe_w_gate, moe_w_up, moe_w_down):
    bc, lc, _ = x_prompt.shape
    bl, ll, _ = x_sample.shape
    past = cache_mla_ckv.shape[2]
    n_ctx, n_lat = bc * lc, bl * ll
    geo = dict(n_ctx=n_ctx, lat_len=ll)

    x = jnp.concatenate([x_prompt.reshape(n_ctx, D_MODEL), x_sample.reshape(n_lat, D_MODEL)], 0)
    n_cond = 1 + bl
    cond = jnp.concatenate([c_ctx[None, :], c, jnp.zeros((-n_cond % 8, D_MODEL), F32)], 0)
    mods = _ada_vectors(cond, ada_w, ada_b)

    seg_w = (SSD_INNER, SSD_XBC, LANES, MLA_Q_RANK, MLA_KV_RANK, 2 * LANES)
    z, xbc, dt, qlat, kvlat, kr = _inproj(
        x, mods[0], _in0_weights(w_in0[0]), seg_w, (F32,) * 6,
        shift_row=0, tm=512, name="in0", **geo)

    cw = jnp.pad(ssd_conv_w[0], ((0, 8 - ssd_conv_w.shape[1]), (0, 0)))
    ssd_params = (cw, ssd_conv_b[0].reshape(1, SSD_XBC),
                  _lane_row(jnp.concatenate([ssd_dt_bias_f[0], ssd_dt_bias_b[0]]), LANES),
                  _lane_row(jnp.concatenate([ssd_a_log_f[0], ssd_a_log_b[0]]), LANES),
                  jnp.repeat(ssd_d[0], SSD_HEAD_DIM).reshape(1, SSD_INNER),
                  ssd_norm_g[0].reshape(1, SSD_INNER))

    def st_in(s):
        return jnp.transpose(s, (0, 3, 1, 2)).reshape(s.shape[0], SSD_STATE, SSD_INNER)

    def st_out(s):
        return jnp.transpose(s.reshape(s.shape[0], SSD_STATE, SSD_HEADS, SSD_HEAD_DIM), (0, 2, 3, 1))

    zero_ssd = jnp.zeros((bc, SSD_STATE, SSD_INNER), F32)
    y_ssd_c, ssd_f, ssd_b = _ssd(z, xbc, dt, zero_ssd, zero_ssd, ssd_params,
                                 row0=0, n_seq=bc, seq_len=lc, name="ssd_ctx")
    y_ssd_l, _, _ = _ssd(z, xbc, dt, st_in(state_ssd_f[:, 0]), st_in(state_ssd_b[:, 0]), ssd_params,
                         row0=n_ctx, n_seq=bl, seq_len=ll, name="ssd_lat")

    wq_pad, wq_rot, wk_pad, wv = _mla_weights(mla_w_q_up[0], mla_w_kv_up[0])
    mla_w = (mla_q_norm_g[0].reshape(1, MLA_Q_RANK), wq_pad, wq_rot,
             mla_kv_norm_g[0].reshape(1, MLA_KV_RANK), wk_pad, wv)
    q_c, k_c, v_c, ckv_c = _mla_prep(qlat, kvlat, kr, None, mla_w, row0=0, n_rows=n_ctx,
                                     do_q=True, do_norm=True, tm=256, name="mla_prep_ctx")
    q_l, k_l, v_l, _ = _mla_prep(qlat, kvlat, kr, _mla_tables(ll), mla_w, row0=n_ctx, n_rows=n_lat,
                                 do_q=True, do_norm=True, tm=256, name="mla_prep_lat")
    lpad = ((0, 0), (MLA_NOPE, 2 * LANES - MLA_NOPE - MLA_ROPE))
    k_p, v_p = _mla_prep(None, cache_mla_ckv[:, 0].reshape(bl * past, MLA_KV_RANK),
                         jnp.pad(cache_mla_krope[:, 0].reshape(bl * past, MLA_ROPE), lpad),
                         None, mla_w, row0=0, n_rows=bl * past,
                         do_q=False, do_norm=False, tm=256, name="mla_prep_cache")
    o_c = _attention(q_c, k_c, v_c, None, None, n_seq=bc, seq_len=lc, cache_len=0, tq=lc, name="attn_ctx")
    o_l = _attention(q_l, k_l, v_l, k_p, v_p, n_seq=bl, seq_len=ll, cache_len=past, tq=256, name="attn_lat")

    w_out0_bf = w_out0[0].astype(BF16)
    x = _outproj([jnp.concatenate([y_ssd_c, y_ssd_l], 0), jnp.concatenate([o_c, o_l], 0)],
                 [w_out0_bf[:SSD_INNER], w_out0_bf[SSD_INNER:]], x, mods[0],
                 ln1_g[0].reshape(1, D_MODEL), ln1_b[0].reshape(1, D_MODEL),
                 gate_row=2, tm=512, name="out0", **geo)
    x = _ffn(x, mods[0], ffn_w_gate[0].astype(BF16), ffn_w_up[0].astype(BF16), ffn_w_down[0].astype(BF16),
             ln2_g[0].reshape(1, D_MODEL), ln2_b[0].reshape(1, D_MODEL), tm=512, **geo)

    hq = RET_HEADS * RET_QK
    q1, k1, v1, g1 = _inproj(x, mods[1], w_in1[0].astype(BF16), (hq, hq, MIX1, MIX1), (F32, F32, BF16, F32),
                             shift_row=0, tm=256, name="in1", **geo)
    dec = jnp.stack([ret_decay_f[0], ret_decay_b[0]], 1)
    dec = jnp.broadcast_to(jnp.pad(dec, ((0, 0), (0, 6)))[:, :, None], (RET_HEADS, 8, RET_V))
    zero_ret = jnp.zeros((bc, RET_HEADS, RET_QK, RET_V), F32)
    tr = lambda s: jnp.swapaxes(s, -1, -2)
    y_ret_c, ret_f, ret_b = _retention(q1, k1, v1, g1, None, dec, zero_ret, zero_ret,
                                       row0=0, n_seq=bc, seq_len=lc, name="ret_ctx")
    y_ret_l, _, _ = _retention(q1, k1, v1, g1, _ret_tables(ll), dec, tr(state_ret_f[:, 0]), tr(state_ret_b[:, 0]),
                               row0=n_ctx, n_seq=bl, seq_len=ll, name="ret_lat")
    x = _outproj([jnp.concatenate([y_ret_c, y_ret_l], 0)], [w_out1[0].astype(BF16)], x, mods[1],
                 ln1_g[1].reshape(1, D_MODEL), ln1_b[1].reshape(1, D_MODEL),
                 gate_row=2, tm=512, name="out1", **geo)
    x = _moe(x, mods[1], moe_router[0], moe_w_gate[0].astype(BF16), moe_w_up[0].astype(BF16),
             moe_w_down[0].astype(BF16), ln2_g[1].reshape(1, D_MODEL), ln2_b[1].reshape(1, D_MODEL), **geo)

    y_prompt = x[:n_ctx].reshape(bc, lc, D_MODEL)
    y_sample = x[n_ctx:].reshape(bl, ll, D_MODEL)
    new_ckv = ckv_c.reshape(bc, 1, lc, MLA_KV_RANK)
    new_krope = kr[:n_ctx, MLA_NOPE:MLA_NOPE + MLA_ROPE].reshape(bc, 1, lc, MLA_ROPE)
    return (y_prompt, y_sample, new_ckv, new_krope,
            st_out(ssd_f)[:, None], st_out(ssd_b)[:, None], tr(ret_f)[:, None], tr(ret_b)[:, None])
```

```python
import functools
import math

import jax
import jax.numpy as jnp
from jax import lax
from jax.experimental import pallas as pl
from jax.experimental.pallas import tpu as pltpu

F32 = jnp.float32
BF16 = jnp.bfloat16
I32 = jnp.int32

D_MODEL = 1024
DEPTH = 2
GRID_W = 64
CHUNK = 128
SSD_HEADS = 8
SSD_HEAD_DIM = 64
SSD_INNER = SSD_HEADS * SSD_HEAD_DIM
SSD_GROUPS = 2
SSD_STATE = 64
SSD_XBC = SSD_INNER + 2 * SSD_GROUPS * SSD_STATE
MLA_HEADS = 8
MLA_NOPE = 64
MLA_ROPE = 32
MLA_V = 64
MLA_Q_RANK = 384
MLA_KV_RANK = 256
MLA_OUT = MLA_HEADS * MLA_V
RET_HEADS = 8
RET_QK = 128
RET_V = 256
MIX1 = RET_HEADS * RET_V
D_FF = 2816
N_EXPERTS = 8
D_FF_EXPERT = 3584
ALPHA = (2 * DEPTH) ** 0.25
LN_EPS = 1e-5
RMS_EPS = 1e-6
ROPE_BASE = 10000.0

LANES = 128
VMEM_LIMIT = 56 * 1024 * 1024
NEG_BIG = -1e30

MOE_SLOT_TILE = 512
MOE_TOK_CHUNK = 512
MOE_FF_CHUNK = 512
MOE_CELL_ALIGN = 16
MOE_BIG_PIECE = 64
MOE_LOCAL_ROWS = 2 * MOE_TOK_CHUNK + N_EXPERTS * MOE_CELL_ALIGN
RET_HEADS_PER_STEP = 2

_NT = (((1,), (1,)), ((), ()))
_TN = (((0,), (0,)), ((), ()))


def _cparams(*sem):
    return pltpu.CompilerParams(dimension_semantics=sem, vmem_limit_bytes=VMEM_LIMIT)


def _resident(shape):
    nd = len(shape)
    return pl.BlockSpec(shape, lambda *_: (0,) * nd, pipeline_mode=pl.Buffered(1))


def _silu(x):
    return x * jax.nn.sigmoid(x)


def _dot(a, b):
    return jnp.dot(a, b, preferred_element_type=F32)


def _cond_index(row, n_ctx, lat_len):
    return jnp.where(row < n_ctx, 0, 1 + (row - n_ctx) // lat_len)


def _layer_norm(y, g, b):
    mu = jnp.mean(y, -1, keepdims=True)
    d = y - mu
    var = jnp.mean(d * d, -1, keepdims=True)
    return d * lax.rsqrt(var + LN_EPS) * g + b


def _ada_kernel(c_ref, w_ref, b_ref, o_ref):
    s = _silu(c_ref[...])
    o_ref[...] = jnp.dot(s, w_ref[...], precision=lax.Precision.HIGHEST,
                         preferred_element_type=F32) + b_ref[...]


def _ada_vectors(cond, ada_w, ada_b):
    r = cond.shape[0]
    tn = 1024
    out = pl.pallas_call(
        _ada_kernel,
        grid=(DEPTH, 6 * D_MODEL // tn),
        in_specs=[pl.BlockSpec((r, D_MODEL), lambda l, j: (0, 0)),
                  pl.BlockSpec((None, D_MODEL, tn), lambda l, j: (l, 0, j)),
                  pl.BlockSpec((None, 1, tn), lambda l, j: (l, 0, j))],
        out_specs=pl.BlockSpec((None, r, tn), lambda l, j: (l, 0, j)),
        out_shape=jax.ShapeDtypeStruct((DEPTH, r, 6 * D_MODEL), F32),
        compiler_params=_cparams("parallel", "parallel"),
        name="ada",
    )(cond, ada_w, ada_b.reshape(DEPTH, 1, 6 * D_MODEL))
    out = out.reshape(DEPTH, r, 6, D_MODEL)
    return jnp.pad(out, ((0, 0), (0, 0), (0, 2), (0, 0)))


def _row_specs(parts, tm):
    cols = parts[0].shape[1]
    if len(parts) == 1:
        return [pl.BlockSpec((tm, cols), lambda i, *_: (i, 0))]
    nct = parts[0].shape[0] // tm
    return [pl.BlockSpec((tm, cols), lambda i, *_: (jnp.minimum(i, nct - 1), 0)),
            pl.BlockSpec((tm, cols), lambda i, *_: (jnp.maximum(i - nct, 0), 0))]


def _row_tile(refs, n_ctx_tiles):
    if len(refs) == 1:
        return refs[0][...]
    return jnp.where(pl.program_id(0) < n_ctx_tiles, refs[0][...], refs[1][...])


def _inproj_kernel(*refs, n_x, n_ctx_tiles, segs, shift_row):
    x_refs, (mod_ref, w_ref), o_refs = refs[:n_x], refs[n_x:n_x + 2], refs[n_x + 2:]
    x = _row_tile(x_refs, n_ctx_tiles)
    h = x * (1.0 + mod_ref[shift_row + 1:shift_row + 2, :]) + mod_ref[shift_row:shift_row + 1, :]
    h = h.astype(BF16)
    for o_ref, (a, b) in zip(o_refs, segs):
        o_ref[...] = _dot(h, w_ref[:, a:b]).astype(o_ref.dtype)


def _inproj(x_parts, mods, w, seg_widths, seg_dtypes, *, shift_row, tm, n_ctx, lat_len, name):
    nt = sum(p.shape[0] for p in x_parts)
    segs, acc = [], 0
    for wd in seg_widths:
        segs.append((acc, acc + wd))
        acc += wd
    assert acc == w.shape[1]
    cmap = lambda i: (_cond_index(i * tm, n_ctx, lat_len), 0, 0)
    return pl.pallas_call(
        functools.partial(_inproj_kernel, n_x=len(x_parts), n_ctx_tiles=n_ctx // tm,
                          segs=tuple(segs), shift_row=shift_row),
        grid=(nt // tm,),
        in_specs=_row_specs(x_parts, tm) + [pl.BlockSpec((None, 8, D_MODEL), cmap), _resident(w.shape)],
        out_specs=[pl.BlockSpec((tm, wd), lambda i: (i, 0)) for wd in seg_widths],
        out_shape=[jax.ShapeDtypeStruct((nt, wd), dt) for wd, dt in zip(seg_widths, seg_dtypes)],
        compiler_params=_cparams("parallel"),
        name=name,
    )(*x_parts, mods, w)


def _cumsum_rows(tril_bf, x):
    hi = x.astype(BF16)
    r = x - hi.astype(F32)
    mid = r.astype(BF16)
    lo = (r - mid.astype(F32)).astype(BF16)
    return _dot(tril_bf, hi) + _dot(tril_bf, mid) + _dot(tril_bf, lo)


def _ssd_kernel(z_ref, xbc_ref, dt_ref, sf0_ref, sb0_ref, cw_ref, cb_ref, dtb_ref, alog_ref,
                dsk_ref, ng_ref, y_ref, sf_ref, sb_ref,
                yacc, xs_s, cm_s, bmt_s, xb_s, erb_s, xf_s, *, seq_len):
    nc = seq_len // CHUNK
    hd, ns = SSD_HEAD_DIM, SSD_STATE
    gw = (SSD_HEADS // SSD_GROUPS) * hd
    ri = lax.broadcasted_iota(I32, (CHUNK, CHUNK), 0)
    ci = lax.broadcasted_iota(I32, (CHUNK, CHUNK), 1)
    lower = ri >= ci
    upper = ri <= ci
    tril_bf = jnp.where(lower, 1.0, 0.0).astype(BF16)
    rowid = lax.broadcasted_iota(I32, (CHUNK, 1), 0)
    lane = lax.broadcasted_iota(I32, (CHUNK, LANES), 1)

    sf_ref[...] = sf0_ref[...]
    sb_ref[...] = sb0_ref[...]

    def fwd(c, carry):
        r0 = pl.multiple_of(c * CHUNK, CHUNK)
        rows = pl.ds(r0, CHUNK)
        cur = xbc_ref[rows, :]
        pstart = pl.multiple_of(jnp.maximum(r0 - 8, 0), 8)
        nstart = pl.multiple_of(jnp.minimum(r0 + CHUNK, seq_len - 8), 8)
        prev_row = xbc_ref[pl.ds(pstart, 8), :][7:8, :] * jnp.where(c > 0, 1.0, 0.0)
        next_row = xbc_ref[pl.ds(nstart, 8), :][0:1, :] * jnp.where(c < nc - 1, 1.0, 0.0)
        sh_prev = jnp.where(rowid == 0, prev_row, pltpu.roll(cur, 1, axis=0))
        sh_next = jnp.where(rowid == CHUNK - 1, next_row, pltpu.roll(cur, CHUNK - 1, axis=0))
        conv = cw_ref[0:1, :] * sh_prev + cw_ref[1:2, :] * cur + cw_ref[2:3, :] * sh_next + cb_ref[...]
        u = _silu(conv)
        xs = u[:, :SSD_INNER]
        bm = u[:, SSD_INNER:SSD_INNER + LANES]
        cm = u[:, SSD_INNER + LANES:]

        xr = dt_ref[rows, :] + dtb_ref[...]
        dt = jnp.maximum(xr, 0.0) + jnp.log1p(jnp.exp(-jnp.abs(xr)))
        la = -dt * jnp.exp(alog_ref[...])
        facs = _cumsum_rows(tril_bf, la)
        racs = facs[CHUNK - 1:CHUNK, :] - facs + la
        packed = jnp.where(lane < SSD_HEADS, facs,
                           jnp.where(lane < 2 * SSD_HEADS, racs, pltpu.roll(dt, 2 * SSD_HEADS, axis=1)))
        packed_t = packed.T
        e_f = jnp.exp(facs)
        e_r = jnp.exp(racs)
        w_f = dt * jnp.exp(facs[CHUNK - 1:CHUNK, :] - facs)
        w_b = dt * jnp.exp(racs[0:1, :] - racs)

        cm_bf = cm.astype(BF16)
        bm_bf = bm.astype(BF16)
        bmt_bf = bm.T.astype(BF16)
        xs_bf = xs.astype(BF16)
        for g in range(SSD_GROUPS):
            gl = slice(g * ns, (g + 1) * ns)
            s_g = lax.dot_general(cm_bf[:, gl], bm_bf[:, gl], _NT, preferred_element_type=F32)
            yoff = _dot(cm_bf[:, gl], sf_ref[:, g * gw:(g + 1) * gw].astype(BF16))
            for hh in range(SSD_HEADS // SSD_GROUPS):
                h = g * (SSD_HEADS // SSD_GROUPS) + hh
                hb = SSD_HEADS + h
                hs = slice(h * hd, (h + 1) * hd)
                seg_f = facs[:, h:h + 1] - packed_t[h:h + 1, :]
                seg_b = racs[:, hb:hb + 1] - packed_t[hb:hb + 1, :]
                d_f = jnp.exp(jnp.where(lower, seg_f, NEG_BIG))
                d_b = jnp.exp(jnp.where(upper, seg_b, NEG_BIG))
                dt_f_row = packed_t[2 * SSD_HEADS + h:2 * SSD_HEADS + h + 1, :]
                dt_b_row = packed_t[2 * SSD_HEADS + hb:2 * SSD_HEADS + hb + 1, :]
                m = (s_g * (d_f * dt_f_row + d_b * dt_b_row)).astype(BF16)
                y_h = _dot(m, xs_bf[:, hs]) + e_f[:, h:h + 1] * yoff[:, hh * hd:(hh + 1) * hd]
                yacc[rows, hs] = y_h
                xf_s[:, hs] = (xs[:, hs] * w_f[:, h:h + 1]).astype(BF16)
                xb_s[rows, hs] = (xs[:, hs] * w_b[:, hb:hb + 1]).astype(BF16)
        for g in range(SSD_GROUPS):
            new = _dot(bmt_bf[g * ns:(g + 1) * ns, :], xf_s[:, g * gw:(g + 1) * gw])
            for hh in range(SSD_HEADS // SSD_GROUPS):
                h = g * (SSD_HEADS // SSD_GROUPS) + hh
                hs = slice(h * hd, (h + 1) * hd)
                sf_ref[:, hs] = e_f[CHUNK - 1:CHUNK, h:h + 1] * sf_ref[:, hs] + new[:, hh * hd:(hh + 1) * hd]
        xs_s[rows, :] = xs
        cm_s[rows, :] = cm_bf
        bmt_s[c] = bmt_bf
        erb_s[rows, :] = e_r
        return carry

    lax.fori_loop(0, nc, fwd, 0)

    def bwd(i, carry):
        c = nc - 1 - i
        r0 = pl.multiple_of(c * CHUNK, CHUNK)
        rows = pl.ds(r0, CHUNK)
        cm_bf = cm_s[rows, :]
        bmt_bf = bmt_s[c]
        e_r = erb_s[rows, :]
        for g in range(SSD_GROUPS):
            gl = slice(g * ns, (g + 1) * ns)
            yoff = _dot(cm_bf[:, gl], sb_ref[:, g * gw:(g + 1) * gw].astype(BF16))
            new = _dot(bmt_bf[g * ns:(g + 1) * ns, :], xb_s[rows, g * gw:(g + 1) * gw])
            for hh in range(SSD_HEADS // SSD_GROUPS):
                h = g * (SSD_HEADS // SSD_GROUPS) + hh
                hb = SSD_HEADS + h
                hs = slice(h * hd, (h + 1) * hd)
                yacc[rows, hs] += e_r[:, hb:hb + 1] * yoff[:, hh * hd:(hh + 1) * hd]
                sb_ref[:, hs] = e_r[0:1, hb:hb + 1] * sb_ref[:, hs] + new[:, hh * hd:(hh + 1) * hd]
        yv = yacc[rows, :] + dsk_ref[...] * xs_s[rows, :]
        gz = yv * _silu(z_ref[rows, :])
        ms = jnp.mean(gz * gz, -1, keepdims=True)
        y_ref[rows, :] = (gz * lax.rsqrt(ms + RMS_EPS) * ng_ref[...]).astype(y_ref.dtype)
        return carry

    lax.fori_loop(0, nc, bwd, 0)


def _ssd(z, xbc, dt, sf0, sb0, params, *, row0, n_seq, seq_len, name):
    cw, cb, dtb, alog, dsk, ng = params
    nc = seq_len // CHUNK
    blk0 = row0 // seq_len
    rmap = lambda b: (b + blk0, 0)
    smap = lambda b: (b, 0, 0)
    const = lambda b: (0, 0)
    st = jax.ShapeDtypeStruct((n_seq, SSD_STATE, SSD_INNER), F32)
    return pl.pallas_call(
        functools.partial(_ssd_kernel, seq_len=seq_len),
        grid=(n_seq,),
        in_specs=[pl.BlockSpec((seq_len, SSD_INNER), rmap),
                  pl.BlockSpec((seq_len, SSD_XBC), rmap),
                  pl.BlockSpec((seq_len, LANES), rmap),
                  pl.BlockSpec((None, SSD_STATE, SSD_INNER), smap),
                  pl.BlockSpec((None, SSD_STATE, SSD_INNER), smap),
                  pl.BlockSpec(cw.shape, const), pl.BlockSpec(cb.shape, const),
                  pl.BlockSpec(dtb.shape, const), pl.BlockSpec(alog.shape, const),
                  pl.BlockSpec(dsk.shape, const), pl.BlockSpec(ng.shape, const)],
        out_specs=[pl.BlockSpec((seq_len, SSD_INNER), lambda b: (b, 0)),
                   pl.BlockSpec((None, SSD_STATE, SSD_INNER), smap),
                   pl.BlockSpec((None, SSD_STATE, SSD_INNER), smap)],
        out_shape=[jax.ShapeDtypeStruct((n_seq * seq_len, SSD_INNER), BF16), st, st],
        scratch_shapes=[pltpu.VMEM((seq_len, SSD_INNER), F32),
                        pltpu.VMEM((seq_len, SSD_INNER), F32),
                        pltpu.VMEM((seq_len, LANES), BF16),
                        pltpu.VMEM((nc, LANES, CHUNK), BF16),
                        pltpu.VMEM((seq_len, SSD_INNER), BF16),
                        pltpu.VMEM((seq_len, LANES), F32),
                        pltpu.VMEM((CHUNK, SSD_INNER), BF16)],
        compiler_params=_cparams("parallel"),
        name=name,
    )(z, xbc, dt, sf0, sb0, cw, cb, dtb, alog, dsk, ng)


def _rms(x, g):
    return x * lax.rsqrt(jnp.mean(x * x, -1, keepdims=True) + RMS_EPS) * g


def _mla_prep_kernel(*refs, do_q, do_norm, do_rope):
    it = iter(refs)
    qlat_ref = next(it) if do_q else None
    kv_ref = next(it)
    kr_ref = next(it)
    cos_ref = next(it) if do_rope else None
    sin_ref = next(it) if do_rope else None
    if do_q:
        gq_ref, wq_ref = next(it), next(it)
        wqr_ref = next(it) if do_rope else None
    gkv_ref = next(it) if do_norm else None
    wk_ref, wv_ref = next(it), next(it)
    q_out = next(it) if do_q else None
    k_out, v_out = next(it), next(it)
    ckv_out = next(it) if do_norm else None

    if do_rope:
        cs, sn = cos_ref[...], sin_ref[...]
    if do_q:
        qn = _rms(qlat_ref[...], gq_ref[...]).astype(BF16)
        qa = _dot(qn, wq_ref[...])
        if do_rope:
            qb = _dot(qn, wqr_ref[...])
        scale = (MLA_NOPE + MLA_ROPE) ** -0.5
        for h in range(MLA_HEADS):
            hs = slice(h * LANES, (h + 1) * LANES)
            qh = qa[:, hs] * cs + qb[:, hs] * sn if do_rope else qa[:, hs]
            q_out[h] = (qh * scale).astype(BF16)
    ckv = kv_ref[...]
    if do_norm:
        ckv = _rms(ckv, gkv_ref[...])
        ckv_out[...] = ckv
    ckv_bf = ckv.astype(BF16)
    kr = kr_ref[...]
    krp = kr[:, :LANES] * cs + kr[:, LANES:] * sn if do_rope else kr[:, :LANES]
    kn = _dot(ckv_bf, wk_ref[...])
    for h in range(MLA_HEADS):
        k_out[h] = (kn[:, h * LANES:(h + 1) * LANES] + krp).astype(BF16)
    v_out[...] = _dot(ckv_bf, wv_ref[...]).astype(BF16)


def _mla_prep(qlat, kv, kr, tables, weights, *, row0, n_rows, do_q, do_norm, tm, name):
    do_rope = tables is not None
    gq, wq, wqr, gkv, wk, wv = weights
    b0 = row0 // tm
    rmap = lambda i: (i + b0, 0)
    omap = lambda i: (i, 0)
    hmap = lambda i: (0, i, 0)
    ins, specs = [], []

    def add(a, spec):
        ins.append(a)
        specs.append(spec)

    if do_q:
        add(qlat, pl.BlockSpec((tm, MLA_Q_RANK), rmap))
    add(kv, pl.BlockSpec((tm, MLA_KV_RANK), rmap))
    add(kr, pl.BlockSpec((tm, 2 * LANES), rmap))
    if do_rope:
        lat_tiles = tables[0].shape[0] // tm
        tmap = lambda i: (i % lat_tiles, 0)
        add(tables[0], pl.BlockSpec((tm, LANES), tmap))
        add(tables[1], pl.BlockSpec((tm, LANES), tmap))
    if do_q:
        add(gq, _resident(gq.shape))
        add(wq, _resident(wq.shape))
        if do_rope:
            add(wqr, _resident(wqr.shape))
    if do_norm:
        add(gkv, _resident(gkv.shape))
    add(wk, _resident(wk.shape))
    add(wv, _resident(wv.shape))
    out_shape, out_specs = [], []
    if do_q:
        out_shape.append(jax.ShapeDtypeStruct((MLA_HEADS, n_rows, LANES), BF16))
        out_specs.append(pl.BlockSpec((MLA_HEADS, tm, LANES), hmap))
    out_shape.append(jax.ShapeDtypeStruct((MLA_HEADS, n_rows, LANES), BF16))
    out_specs.append(pl.BlockSpec((MLA_HEADS, tm, LANES), hmap))
    out_shape.append(jax.ShapeDtypeStruct((n_rows, MLA_OUT), BF16))
    out_specs.append(pl.BlockSpec((tm, MLA_OUT), omap))
    if do_norm:
        out_shape.append(jax.ShapeDtypeStruct((n_rows, MLA_KV_RANK), F32))
        out_specs.append(pl.BlockSpec((tm, MLA_KV_RANK), omap))
    return pl.pallas_call(
        functools.partial(_mla_prep_kernel, do_q=do_q, do_norm=do_norm, do_rope=do_rope),
        grid=(n_rows // tm,),
        in_specs=specs, out_specs=out_specs, out_shape=out_shape,
        compiler_params=_cparams("parallel"),
        name=name,
    )(*ins)


def _attn_kernel(*refs, seq_len, cache_len, kblk):
    if cache_len:
        q_ref, k_ref, v_ref, kc_ref, vc_ref, o_ref, s_scr = refs
    else:
        q_ref, k_ref, v_ref, o_ref, s_scr = refs
    tq = q_ref.shape[1]
    blocks = [(k_ref, v_ref, i * kblk) for i in range(seq_len // kblk)]
    if cache_len:
        blocks += [(kc_ref, vc_ref, i * kblk) for i in range(cache_len // kblk)]
    outs = []
    for hh in range(2):
        q = q_ref[hh]
        m = jnp.full((tq, 1), NEG_BIG, F32)
        for j, (kr, _, off) in enumerate(blocks):
            s = lax.dot_general(q, kr[hh, off:off + kblk, :], _NT, preferred_element_type=F32)
            s_scr[:, j * kblk:(j + 1) * kblk] = s
            m = jnp.maximum(m, jnp.max(s, -1, keepdims=True))
        l = jnp.zeros((tq, 1), F32)
        acc = jnp.zeros((tq, LANES), F32)
        for j, (_, vr, off) in enumerate(blocks):
            p = jnp.exp(s_scr[:, j * kblk:(j + 1) * kblk] - m)
            l = l + jnp.sum(p, -1, keepdims=True)
            acc = acc + _dot(p.astype(BF16), vr[off:off + kblk, :])
        outs.append(acc / l)
    lane = lax.broadcasted_iota(I32, (tq, LANES), 1)
    o_ref[...] = jnp.where(lane < MLA_V, outs[0], outs[1]).astype(o_ref.dtype)


def _attention(q, k, v, kc, vc, *, n_seq, seq_len, cache_len, tq, name):
    nq = seq_len // tq
    kblk = min(512, seq_len)
    n = n_seq * seq_len
    ins = [q, k, v]
    specs = [pl.BlockSpec((2, tq, LANES), lambda b, hp, qi: (hp, b * nq + qi, 0)),
             pl.BlockSpec((2, seq_len, LANES), lambda b, hp, qi: (hp, b, 0)),
             pl.BlockSpec((seq_len, LANES), lambda b, hp, qi: (b, hp))]
    if cache_len:
        ins += [kc, vc]
        specs += [pl.BlockSpec((2, cache_len, LANES), lambda b, hp, qi: (hp, b, 0)),
                  pl.BlockSpec((cache_len, LANES), lambda b, hp, qi: (b, hp))]
    return pl.pallas_call(
        functools.partial(_attn_kernel, seq_len=seq_len, cache_len=cache_len, kblk=kblk),
        grid=(n_seq, MLA_HEADS // 2, nq),
        in_specs=specs,
        out_specs=pl.BlockSpec((tq, LANES), lambda b, hp, qi: (b * nq + qi, hp)),
        out_shape=jax.ShapeDtypeStruct((n, MLA_OUT), BF16),
        scratch_shapes=[pltpu.VMEM((tq, seq_len + cache_len), F32)],
        compiler_params=_cparams("parallel", "parallel", "arbitrary"),
        name=name,
    )(*ins)


def _outproj_kernel(*refs, n_parts, n_ctx_tiles, gate_row):
    it = iter(refs)
    acc = None
    for n in n_parts[:-1]:
        a = _row_tile([next(it) for _ in range(n)], n_ctx_tiles)
        part = _dot(a, next(it)[...])
        acc = part if acc is None else acc + part
    x = _row_tile([next(it) for _ in range(n_parts[-1])], n_ctx_tiles)
    mod_ref, g_ref, b_ref, o_ref = it
    y = ALPHA * x + mod_ref[gate_row:gate_row + 1, :] * acc
    o_ref[...] = _layer_norm(y, g_ref[...], b_ref[...])


def _outproj(acts, ws, x_parts, mods, g, b, *, gate_row, tm, n_ctx, lat_len, name):
    nt = sum(p.shape[0] for p in x_parts)
    cmap = lambda i: (_cond_index(i * tm, n_ctx, lat_len), 0, 0)
    ins, specs = [], []
    for parts, w in zip(acts, ws):
        ins += list(parts) + [w]
        specs += _row_specs(parts, tm) + [_resident(w.shape)]
    ins += list(x_parts) + [mods, g, b]
    specs += _row_specs(x_parts, tm) + [pl.BlockSpec((None, 8, D_MODEL), cmap),
                                        _resident(g.shape), _resident(b.shape)]
    n_parts = tuple(len(p) for p in acts) + (len(x_parts),)
    return pl.pallas_call(
        functools.partial(_outproj_kernel, n_parts=n_parts, n_ctx_tiles=n_ctx // tm, gate_row=gate_row),
        grid=(nt // tm,),
        in_specs=specs,
        out_specs=pl.BlockSpec((tm, D_MODEL), lambda i: (i, 0)),
        out_shape=jax.ShapeDtypeStruct((nt, D_MODEL), F32),
        compiler_params=_cparams("parallel"),
        name=name,
    )(*ins)


def _ffn_kernel(x_ref, mod_ref, wg_ref, wu_ref, wd_ref, g_ref, b_ref, o_ref, *, ff_chunks):
    x = x_ref[...]
    h = (x * (1.0 + mod_ref[4:5, :]) + mod_ref[3:4, :]).astype(BF16)
    acc = None
    for a, b in ff_chunks:
        gt = _dot(h, wg_ref[:, a:b])
        up = _dot(h, wu_ref[:, a:b])
        act = (_silu(gt) * up).astype(BF16)
        part = _dot(act, wd_ref[a:b, :])
        acc = part if acc is None else acc + part
    y = ALPHA * x + mod_ref[5:6, :] * acc
    o_ref[...] = _layer_norm(y, g_ref[...], b_ref[...])


def _ffn(x, mods, wg, wu, wd, g, b, *, tm, n_ctx, lat_len):
    nt = x.shape[0]
    ff = wg.shape[1]
    chunks, a = [], 0
    while a < ff:
        chunks.append((a, min(a + 512, ff)))
        a += 512
    cmap = lambda i: (_cond_index(i * tm, n_ctx, lat_len), 0, 0)
    return pl.pallas_call(
        functools.partial(_ffn_kernel, ff_chunks=tuple(chunks)),
        grid=(nt // tm,),
        in_specs=[pl.BlockSpec((tm, D_MODEL), lambda i: (i, 0)),
                  pl.BlockSpec((None, 8, D_MODEL), cmap),
                  _resident(wg.shape), _resident(wu.shape), _resident(wd.shape),
                  _resident(g.shape), _resident(b.shape)],
        out_specs=pl.BlockSpec((tm, D_MODEL), lambda i: (i, 0)),
        out_shape=jax.ShapeDtypeStruct((nt, D_MODEL), F32),
        compiler_params=_cparams("parallel"),
        name="ffn",
    )(x, mods, wg, wu, wd, g, b)


def _ret_kernel(*refs, seq_len, do_rope):
    if do_rope:
        (q_ref, k_ref, v_ref, g_ref, cos_ref, sin_ref, dec_ref, sf0_ref, sb0_ref,
         y_ref, sf_ref, sb_ref, yacc, q_s, kb_s) = refs
    else:
        (q_ref, k_ref, v_ref, g_ref, dec_ref, sf0_ref, sb0_ref,
         y_ref, sf_ref, sb_ref, yacc, q_s, kb_s) = refs
    nc = seq_len // CHUNK
    ri = lax.broadcasted_iota(I32, (CHUNK, CHUNK), 0)
    ci = lax.broadcasted_iota(I32, (CHUNK, CHUNK), 1)
    dij = (ri - ci).astype(F32)
    pos_k = lax.broadcasted_iota(I32, (CHUNK, RET_QK), 0).astype(F32)
    pos_v = lax.broadcasted_iota(I32, (CHUNK, RET_V), 0).astype(F32)
    consts = []
    for hh in range(RET_HEADS_PER_STEP):
        la_f = -jnp.exp(dec_ref[hh, 0:1, :])
        la_b = -jnp.exp(dec_ref[hh, 1:2, :])
        consts.append(dict(
            d_comb=(jnp.exp(jnp.where(ri >= ci, dij * la_f[:, :CHUNK], NEG_BIG)) +
                    jnp.exp(jnp.where(ri <= ci, -dij * la_b[:, :CHUNK], NEG_BIG))),
            eoff_f=jnp.exp((pos_v + 1.0) * la_f),
            eoff_b=jnp.exp((CHUNK - pos_v) * la_b),
            wst_f=jnp.exp((CHUNK - 1.0 - pos_k) * la_f[:, :RET_QK]),
            wst_b=jnp.exp(pos_k * la_b[:, :RET_QK]),
            cdec_f=jnp.exp(CHUNK * la_f),
            cdec_b=jnp.exp(CHUNK * la_b)))

    sf_ref[...] = sf0_ref[...]
    sb_ref[...] = sb0_ref[...]

    def fwd(c, carry):
        rows = pl.ds(pl.multiple_of(c * CHUNK, CHUNK), CHUNK)
        if do_rope:
            cs, sn = cos_ref[rows, :], sin_ref[rows, :]
        for hh, cst in enumerate(consts):
            qs = slice(hh * RET_QK, (hh + 1) * RET_QK)
            vs = slice(hh * RET_V, (hh + 1) * RET_V)
            q = q_ref[rows, qs]
            k = k_ref[rows, qs] * (RET_QK ** -0.5)
            if do_rope:
                q = q * cs + pltpu.roll(q, RET_QK // 2, axis=1) * sn
                k = k * cs + pltpu.roll(k, RET_QK // 2, axis=1) * sn
            q_bf = q.astype(BF16)
            v = v_ref[rows, vs]
            s = lax.dot_general(q_bf, k.astype(BF16), _NT, preferred_element_type=F32)
            y = _dot((s * cst["d_comb"]).astype(BF16), v)
            y = y + _dot(q_bf, sf_ref[hh].astype(BF16)) * cst["eoff_f"]
            yacc[rows, vs] = y
            upd = lax.dot_general((k * cst["wst_f"]).astype(BF16), v, _TN, preferred_element_type=F32)
            sf_ref[hh] = cst["cdec_f"] * sf_ref[hh] + upd
            q_s[rows, qs] = q_bf
            kb_s[rows, qs] = (k * cst["wst_b"]).astype(BF16)
        return carry

    lax.fori_loop(0, nc, fwd, 0, unroll=2)

    def bwd(i, carry):
        rows = pl.ds(pl.multiple_of((nc - 1 - i) * CHUNK, CHUNK), CHUNK)
        for hh, cst in enumerate(consts):
            qs = slice(hh * RET_QK, (hh + 1) * RET_QK)
            vs = slice(hh * RET_V, (hh + 1) * RET_V)
            v = v_ref[rows, vs]
            y = yacc[rows, vs] + _dot(q_s[rows, qs], sb_ref[hh].astype(BF16)) * cst["eoff_b"]
            upd = lax.dot_general(kb_s[rows, qs], v, _TN, preferred_element_type=F32)
            sb_ref[hh] = cst["cdec_b"] * sb_ref[hh] + upd
            mu = jnp.mean(y, -1, keepdims=True)
            d = y - mu
            var = jnp.mean(d * d, -1, keepdims=True)
            yn = d * lax.rsqrt(var + LN_EPS)
            y_ref[rows, vs] = (yn * _silu(g_ref[rows, vs])).astype(y_ref.dtype)
        return carry

    lax.fori_loop(0, nc, bwd, 0, unroll=2)


def _retention(q, k, v, g, tables, dec, sf0, sb0, *, row0, n_seq, seq_len, name):
    do_rope = tables is not None
    hps = RET_HEADS_PER_STEP
    blk0 = row0 // seq_len
    qmap = lambda b, h: (b + blk0, h)
    smap = lambda b, h: (b, h, 0, 0)
    ins = [q, k, v, g]
    specs = [pl.BlockSpec((seq_len, hps * RET_QK), qmap), pl.BlockSpec((seq_len, hps * RET_QK), qmap),
             pl.BlockSpec((seq_len, hps * RET_V), qmap), pl.BlockSpec((seq_len, hps * RET_V), qmap)]
    if do_rope:
        ins += list(tables)
        specs += [pl.BlockSpec((seq_len, RET_QK), lambda b, h: (0, 0))] * 2
    ins += [dec, sf0, sb0]
    specs += [pl.BlockSpec((hps, 8, RET_V), lambda b, h: (h, 0, 0)),
              pl.BlockSpec((None, hps, RET_QK, RET_V), smap),
              pl.BlockSpec((None, hps, RET_QK, RET_V), smap)]
    st = jax.ShapeDtypeStruct((n_seq, RET_HEADS, RET_QK, RET_V), F32)
    return pl.pallas_call(
        functools.partial(_ret_kernel, seq_len=seq_len, do_rope=do_rope),
        grid=(n_seq, RET_HEADS // hps),
        in_specs=specs,
        out_specs=[pl.BlockSpec((seq_len, hps * RET_V), lambda b, h: (b, h)),
                   pl.BlockSpec((None, hps, RET_QK, RET_V), smap),
                   pl.BlockSpec((None, hps, RET_QK, RET_V), smap)],
        out_shape=[jax.ShapeDtypeStruct((n_seq * seq_len, MIX1), BF16), st, st],
        scratch_shapes=[pltpu.VMEM((seq_len, hps * RET_V), F32),
                        pltpu.VMEM((seq_len, hps * RET_QK), BF16),
                        pltpu.VMEM((seq_len, hps * RET_QK), BF16)],
        compiler_params=_cparams("parallel", "parallel"),
        name=name,
    )(*ins)


def _router_kernel(x_ref, mod_ref, w_ref, hb_ref, tok_ref, lpt_ref, cap_ref):
    tm = x_ref.shape[0]
    x = x_ref[...]
    h = x * (1.0 + mod_ref[4:5, :]) + mod_ref[3:4, :]
    h_hi = h.astype(BF16)
    hb_ref[...] = h_hi
    h_lo = (h - h_hi.astype(F32)).astype(BF16)
    w = w_ref[...]
    w_hi = w.astype(BF16)
    w_lo = (w - w_hi.astype(F32)).astype(BF16)
    logits = _dot(h_hi, w_hi) + (_dot(h_hi, w_lo) + _dot(h_lo, w_hi))
    lane = lax.broadcasted_iota(I32, (tm, LANES), 1)
    logits = jnp.where(lane < N_EXPERTS, logits, NEG_BIG)
    m1 = jnp.max(logits, -1, keepdims=True)
    i1 = jnp.min(jnp.where(logits == m1, lane, LANES), -1, keepdims=True)
    rest = jnp.where(lane == i1, NEG_BIG, logits)
    m2 = jnp.max(rest, -1, keepdims=True)
    i2 = jnp.min(jnp.where(rest == m2, lane, LANES), -1, keepdims=True)
    e = jnp.exp(m2 - m1)
    g1 = 1.0 / (1.0 + e)
    g2 = e / (1.0 + e)
    sel1 = lane == i1
    sel2 = lane == i2
    onehot = jnp.where(sel1 | sel2, 1.0, 0.0)
    ri = lax.broadcasted_iota(I32, (tm, tm), 0)
    ci = lax.broadcasted_iota(I32, (tm, tm), 1)
    strict = jnp.where(ri > ci, 1.0, 0.0).astype(BF16)
    prefix = _dot(strict, onehot.astype(BF16))
    n_row = jnp.sum(onehot, 0, keepdims=True)
    cap_row = jnp.floor((n_row + (MOE_CELL_ALIGN - 1.0)) * (1.0 / MOE_CELL_ALIGN)) * MOE_CELL_ALIGN
    li = lax.broadcasted_iota(I32, (LANES, LANES), 0)
    lj = lax.broadcasted_iota(I32, (LANES, LANES), 1)
    lower_lanes = jnp.where(li < lj, 1.0, 0.0).astype(BF16)
    cap8 = jnp.broadcast_to(cap_row, (8, LANES))
    base_row = _dot(cap8.astype(BF16), lower_lanes)[0:1, :]
    local = prefix + base_row
    lpos1 = jnp.sum(jnp.where(sel1, local, 0.0), -1, keepdims=True)
    lpos2 = jnp.sum(jnp.where(sel2, local, 0.0), -1, keepdims=True)
    tok = jnp.where(lane == 0, g1,
          jnp.where(lane == 1, g2,
          jnp.where(lane == 2, lpos1,
          jnp.where(lane == 3, lpos2, 0.0))))
    tok_ref[...] = tok
    lpt_ref[...] = tok.T[0:8, :]
    cap_ref[...] = cap8


def _router(x, mods, w_pad, *, n_ctx, lat_len):
    nt = x.shape[0]
    tm = MOE_TOK_CHUNK
    nchunk = nt // tm
    cmap = lambda i: (_cond_index(i * tm, n_ctx, lat_len), 0, 0)
    return pl.pallas_call(
        _router_kernel,
        grid=(nchunk,),
        in_specs=[pl.BlockSpec((tm, D_MODEL), lambda i: (i, 0)),
                  pl.BlockSpec((None, 8, D_MODEL), cmap),
                  _resident(w_pad.shape)],
        out_specs=[pl.BlockSpec((tm, D_MODEL), lambda i: (i, 0)),
                   pl.BlockSpec((tm, LANES), lambda i: (i, 0)),
                   pl.BlockSpec((None, 8, tm), lambda i: (i, 0, 0)),
                   pl.BlockSpec((None, 8, LANES), lambda i: (i, 0, 0))],
        out_shape=[jax.ShapeDtypeStruct((nt, D_MODEL), BF16),
                   jax.ShapeDtypeStruct((nt, LANES), F32),
                   jax.ShapeDtypeStruct((nchunk, 8, tm), F32),
                   jax.ShapeDtypeStruct((nchunk, 8, LANES), F32)],
        compiler_params=_cparams("parallel"),
        name="router",
    )(x, mods, w_pad)


def _cell_copies(tabs, chunk, hbm_ref, buf_ref, slot, sem, *, to_local, wait):
    g_ref, a_ref, nbig_ref, nsmall_ref = tabs
    for e in range(N_EXPERTS):
        k = chunk * N_EXPERTS + e
        g0, a0, nbig, nsmall = g_ref[k], a_ref[k], nbig_ref[k], nsmall_ref[k]

        def piece(i, carry, rows, goff, aoff):
            g = pl.multiple_of(goff + i * rows, MOE_CELL_ALIGN)
            a = pl.multiple_of(aoff + i * rows, MOE_CELL_ALIGN)
            far = hbm_ref.at[pl.ds(g, rows), :]
            near = buf_ref.at[slot, pl.ds(a, rows), :]
            cp = (pltpu.make_async_copy(far, near, sem.at[slot]) if to_local
                  else pltpu.make_async_copy(near, far, sem.at[slot]))
            if wait:
                cp.wait()
            else:
                cp.start()
            return carry

        lax.fori_loop(0, nbig, functools.partial(piece, rows=MOE_BIG_PIECE, goff=g0, aoff=a0), 0)
        done = nbig * MOE_BIG_PIECE
        lax.fori_loop(0, nsmall, functools.partial(piece, rows=MOE_CELL_ALIGN, goff=g0 + done, aoff=a0 + done), 0)


def _dispatch_kernel(g_ref, a_ref, nbig_ref, nsmall_ref, hb_ref, lpt_ref, zeros_hbm, xs_hbm, ybuf, sem):
    del zeros_hbm
    c = pl.program_id(0)
    nc = pl.num_programs(0)
    slot = c % 2
    tabs = (g_ref, a_ref, nbig_ref, nsmall_ref)
    l1 = lpt_ref[2:3, :]
    l2 = lpt_ref[3:4, :]
    rb = MOE_LOCAL_ROWS // 3
    for r in range(3):
        rid = (lax.broadcasted_iota(I32, (rb, MOE_TOK_CHUNK), 0) + r * rb).astype(F32)
        onehot = jnp.where((l1 == rid) | (l2 == rid), 1.0, 0.0).astype(BF16)
        ybuf[slot, r * rb:(r + 1) * rb, :] = _dot(onehot, hb_ref[...]).astype(BF16)

    @pl.when(c > 0)
    def _():
        _cell_copies(tabs, c - 1, xs_hbm, ybuf, 1 - slot, sem, to_local=False, wait=True)

    _cell_copies(tabs, c, xs_hbm, ybuf, slot, sem, to_local=False, wait=False)

    @pl.when(c == nc - 1)
    def _():
        _cell_copies(tabs, c, xs_hbm, ybuf, slot, sem, to_local=False, wait=True)


def _dispatch(tabs, hb, lpt, *, n_slots):
    tm = MOE_TOK_CHUNK
    grid_spec = pltpu.PrefetchScalarGridSpec(
        num_scalar_prefetch=4,
        grid=(hb.shape[0] // tm,),
        in_specs=[pl.BlockSpec((tm, D_MODEL), lambda c, *_: (c, 0)),
                  pl.BlockSpec((None, 8, tm), lambda c, *_: (c, 0, 0)),
                  pl.BlockSpec(memory_space=pl.ANY)],
        out_specs=pl.BlockSpec(memory_space=pl.ANY),
        scratch_shapes=[pltpu.VMEM((2, MOE_LOCAL_ROWS, D_MODEL), BF16),
                        pltpu.SemaphoreType.DMA((2,))],
    )
    n_prefetch = len(tabs)
    return pl.pallas_call(
        _dispatch_kernel,
        grid_spec=grid_spec,
        out_shape=jax.ShapeDtypeStruct((n_slots, D_MODEL), BF16),
        input_output_aliases={n_prefetch + 2: 0},
        compiler_params=_cparams("arbitrary"),
        name="moe_dispatch",
    )(*tabs, hb, lpt, jnp.zeros((n_slots, D_MODEL), BF16))


def _expert_kernel(te_ref, nu_ref, x_ref, wg_ref, wu_ref, wd_ref, o_ref, acc):
    t = pl.program_id(0)
    f = pl.program_id(1)
    nf = pl.num_programs(1)

    @pl.when(t < nu_ref[0])
    def _():
        x = x_ref[...]
        gt = _dot(x, wg_ref[...])
        up = _dot(x, wu_ref[...])
        part = _dot((_silu(gt) * up).astype(BF16), wd_ref[...])

        @pl.when(f == 0)
        def _():
            acc[...] = part

        @pl.when(f > 0)
        def _():
            acc[...] += part

        @pl.when(f == nf - 1)
        def _():
            o_ref[...] = acc[...].astype(o_ref.dtype)

    @pl.when((t >= nu_ref[0]) & (f == nf - 1))
    def _():
        o_ref[...] = jnp.zeros_like(o_ref)


def _experts(te, nu, xs, wg, wu, wd):
    bs = MOE_SLOT_TILE
    n_slots = xs.shape[0]
    nf = D_FF_EXPERT // MOE_FF_CHUNK

    def tt(t, nu):
        return jnp.minimum(t, nu[0] - 1)

    def ff(t, f, nu):
        return jnp.where(t < nu[0], f, nf - 1)

    grid_spec = pltpu.PrefetchScalarGridSpec(
        num_scalar_prefetch=2,
        grid=(n_slots // bs, nf),
        in_specs=[pl.BlockSpec((bs, D_MODEL), lambda t, f, te, nu: (tt(t, nu), 0)),
                  pl.BlockSpec((None, D_MODEL, MOE_FF_CHUNK), lambda t, f, te, nu: (te[tt(t, nu)], 0, ff(t, f, nu))),
                  pl.BlockSpec((None, D_MODEL, MOE_FF_CHUNK), lambda t, f, te, nu: (te[tt(t, nu)], 0, ff(t, f, nu))),
                  pl.BlockSpec((None, MOE_FF_CHUNK, D_MODEL), lambda t, f, te, nu: (te[tt(t, nu)], ff(t, f, nu), 0))],
        out_specs=pl.BlockSpec((bs, D_MODEL), lambda t, f, te, nu: (t, 0)),
        scratch_shapes=[pltpu.VMEM((bs, D_MODEL), F32)],
    )
    return pl.pallas_call(
        _expert_kernel,
        grid_spec=grid_spec,
        out_shape=jax.ShapeDtypeStruct((n_slots, D_MODEL), BF16),
        compiler_params=_cparams("arbitrary", "arbitrary"),
        name="moe_experts",
    )(te, nu, xs, wg, wu, wd)


def _combine_kernel(g_ref, a_ref, nbig_ref, nsmall_ref, tok_ref, x_ref, mod_ref, lg_ref, lb_ref, ys_hbm,
                    oc_ref, ol_ref, ybuf, sem, *, n_ctx_tiles):
    c = pl.program_id(0)
    nc = pl.num_programs(0)
    slot = c % 2
    tabs = (g_ref, a_ref, nbig_ref, nsmall_ref)
    tm = tok_ref.shape[0]

    @pl.when(c == 0)
    def _():
        ybuf[...] = jnp.zeros_like(ybuf)
        _cell_copies(tabs, 0, ys_hbm, ybuf, 0, sem, to_local=True, wait=False)

    @pl.when(c + 1 < nc)
    def _():
        _cell_copies(tabs, c + 1, ys_hbm, ybuf, 1 - slot, sem, to_local=True, wait=False)

    _cell_copies(tabs, c, ys_hbm, ybuf, slot, sem, to_local=True, wait=True)

    tok = tok_ref[...]
    col = lax.broadcasted_iota(I32, (tm, MOE_LOCAL_ROWS), 1).astype(F32)
    rows = ybuf[slot]
    pick1 = jnp.where(tok[:, 2:3] == col, 1.0, 0.0).astype(BF16)
    pick2 = jnp.where(tok[:, 3:4] == col, 1.0, 0.0).astype(BF16)
    f = tok[:, 0:1] * _dot(pick1, rows) + tok[:, 1:2] * _dot(pick2, rows)
    y = ALPHA * x_ref[...] + mod_ref[5:6, :] * f
    res = _layer_norm(y, lg_ref[...], lb_ref[...])

    @pl.when(c < n_ctx_tiles)
    def _():
        oc_ref[...] = res

    @pl.when(c >= n_ctx_tiles)
    def _():
        ol_ref[...] = res


def _combine(tabs, tok, ys, x, mods, g, b, *, n_ctx, lat_len):
    nt = x.shape[0]
    tm = MOE_TOK_CHUNK
    nct = n_ctx // tm
    cmap = lambda c, *_: (_cond_index(c * tm, n_ctx, lat_len), 0, 0)
    grid_spec = pltpu.PrefetchScalarGridSpec(
        num_scalar_prefetch=4,
        grid=(nt // tm,),
        in_specs=[pl.BlockSpec((tm, LANES), lambda c, *_: (c, 0)),
                  pl.BlockSpec((tm, D_MODEL), lambda c, *_: (c, 0)),
                  pl.BlockSpec((None, 8, D_MODEL), cmap),
                  pl.BlockSpec((1, D_MODEL), lambda c, *_: (0, 0)),
                  pl.BlockSpec((1, D_MODEL), lambda c, *_: (0, 0)),
                  pl.BlockSpec(memory_space=pl.ANY)],
        out_specs=[pl.BlockSpec((tm, D_MODEL), lambda c, *_: (jnp.minimum(c, nct - 1), 0)),
                   pl.BlockSpec((tm, D_MODEL), lambda c, *_: (jnp.maximum(c - nct, 0), 0))],
        scratch_shapes=[pltpu.VMEM((2, MOE_LOCAL_ROWS, D_MODEL), BF16),
                        pltpu.SemaphoreType.DMA((2,))],
    )
    return pl.pallas_call(
        functools.partial(_combine_kernel, n_ctx_tiles=nct),
        grid_spec=grid_spec,
        out_shape=[jax.ShapeDtypeStruct((n_ctx, D_MODEL), F32),
                   jax.ShapeDtypeStruct((nt - n_ctx, D_MODEL), F32)],
        compiler_params=_cparams("arbitrary"),
        name="moe_combine",
    )(*tabs, tok, x, mods, g, b, ys)


def _moe(x, mods, router_w, wg, wu, wd, ln_g, ln_b, *, n_ctx, lat_len):
    nt = x.shape[0]
    bs, tc = MOE_SLOT_TILE, MOE_TOK_CHUNK
    nchunk = nt // tc
    max_rows = 2 * nt + nchunk * N_EXPERTS * (MOE_CELL_ALIGN - 1)
    n_tiles = -(-max_rows // bs) + N_EXPERTS
    n_slots = n_tiles * bs

    w_pad = jnp.pad(router_w, ((0, 0), (0, LANES - N_EXPERTS)))
    hb, tok, lpt, capt = _router(x, mods, w_pad, n_ctx=n_ctx, lat_len=lat_len)

    cap = capt[:, 0, :N_EXPERTS].astype(I32)
    a_loc = jnp.cumsum(cap, 1) - cap
    tot = jnp.sum(cap, 0)
    padded = ((tot + bs - 1) // bs) * bs
    gend = jnp.cumsum(padded)
    g_glob = (gend - padded)[None, :] + (jnp.cumsum(cap, 0) - cap)
    n_big = cap // MOE_BIG_PIECE
    n_small = (cap - n_big * MOE_BIG_PIECE) // MOE_CELL_ALIGN
    tabs = tuple(t.reshape(-1).astype(I32) for t in (g_glob, a_loc, n_big, n_small))
    tile_start = jnp.arange(n_tiles, dtype=I32) * bs
    tile_expert = jnp.minimum(jnp.sum((gend[None, :] <= tile_start[:, None]).astype(I32), 1), N_EXPERTS - 1)
    n_used = (gend[-1] // bs).astype(I32).reshape(1)

    xs = _dispatch(tabs, hb, lpt, n_slots=n_slots)
    ys = _experts(tile_expert, n_used, xs, wg, wu, wd)
    return _combine(tabs, tok, ys, x, mods, ln_g, ln_b, n_ctx=n_ctx, lat_len=lat_len)


def _axial_angles(n_tok, dim):
    rows = n_tok // GRID_W
    row = jnp.repeat(jnp.arange(rows), GRID_W).astype(F32)
    col = jnp.tile(jnp.arange(GRID_W), rows).astype(F32)
    axis_dim = dim // 2
    inv = 1.0 / (ROPE_BASE ** (jnp.arange(0, axis_dim, 2, dtype=F32) / axis_dim))
    ang = jnp.concatenate([row[:, None] * inv, col[:, None] * inv], -1)
    return jnp.cos(ang), jnp.sin(ang)


def _mla_tables(n_lat):
    cos, sin = _axial_angles(n_lat, MLA_ROPE)
    one = jnp.ones((n_lat, MLA_NOPE), F32)
    zero = jnp.zeros((n_lat, MLA_NOPE), F32)
    pad1 = jnp.ones((n_lat, LANES - MLA_NOPE - MLA_ROPE), F32)
    pad0 = jnp.zeros((n_lat, LANES - MLA_NOPE - MLA_ROPE), F32)
    return (jnp.concatenate([one, cos, cos, pad1], -1), jnp.concatenate([zero, sin, sin, pad0], -1))


def _ret_tables(n_lat):
    cos, sin = _axial_angles(n_lat, RET_QK)
    return jnp.concatenate([cos, cos], -1), jnp.concatenate([-sin, sin], -1)


def _rot_cols(w):
    half = w.shape[1] // 2
    return jnp.concatenate([-w[:, half:], w[:, :half]], 1)


def _in0_weights(w_in0):
    z, xbc, dt, ql, kvl, kr = jnp.split(
        w_in0, [SSD_INNER, SSD_INNER + SSD_XBC, SSD_INNER + SSD_XBC + 2 * SSD_HEADS,
                SSD_INNER + SSD_XBC + 2 * SSD_HEADS + MLA_Q_RANK,
                SSD_INNER + SSD_XBC + 2 * SSD_HEADS + MLA_Q_RANK + MLA_KV_RANK], axis=1)
    dtp = jnp.pad(dt, ((0, 0), (0, LANES - 2 * SSD_HEADS)))
    lpad = ((0, 0), (MLA_NOPE, LANES - MLA_NOPE - MLA_ROPE))
    krp = jnp.concatenate([jnp.pad(kr, lpad), jnp.pad(_rot_cols(kr), lpad)], 1)
    return jnp.concatenate([z, xbc, dtp, ql, kvl, krp], 1).astype(BF16)


def _mla_weights(w_q_up, w_kv_up):
    d = w_q_up.shape[0]
    wq = w_q_up.reshape(d, MLA_HEADS, MLA_NOPE + MLA_ROPE)
    nope, rope = wq[..., :MLA_NOPE], wq[..., MLA_NOPE:]
    half = MLA_ROPE // 2
    rot = jnp.concatenate([-rope[..., half:], rope[..., :half]], -1)
    tail = jnp.zeros((d, MLA_HEADS, LANES - MLA_NOPE - MLA_ROPE), F32)
    wq_pad = jnp.concatenate([nope, rope, tail], -1).reshape(d, MLA_HEADS * LANES)
    wq_rot = jnp.concatenate([jnp.zeros_like(nope), rot, tail], -1).reshape(d, MLA_HEADS * LANES)
    r = w_kv_up.shape[0]
    wkv = w_kv_up.reshape(r, MLA_HEADS, MLA_NOPE + MLA_V)
    wk = jnp.concatenate([wkv[..., :MLA_NOPE], jnp.zeros((r, MLA_HEADS, LANES - MLA_NOPE), F32)], -1)
    wv = wkv[..., MLA_NOPE:]
    return (wq_pad.astype(BF16), wq_rot.astype(BF16),
            wk.reshape(r, MLA_HEADS * LANES).astype(BF16), wv.reshape(r, MLA_OUT).astype(BF16))


def _lane_row(v, width):
    return jnp.pad(v, (0, width - v.shape[0])).reshape(1, width)


def kernel(x_prompt, x_sample, cache_mla_ckv, cache_mla_krope, state_ssd_f, state_ssd_b, state_ret_f, state_ret_b, c, c_ctx, ada_w, ada_b, ln1_g, ln1_b, ln2_g, ln2_b, w_in0, ssd_conv_w, ssd_conv_b, ssd_a_log_f, ssd_a_log_b, ssd_dt_bias_f, ssd_dt_bias_b, ssd_d, ssd_norm_g, mla_q_norm_g, mla_w_q_up, mla_kv_norm_g, mla_w_kv_up, w_out0, ffn_w_gate, ffn_w_up, ffn_w_down, w_in1, ret_decay_f, ret_decay_b, w_out1, moe_router, moe_w_gate, moe_w_up, moe_w_down):
    bc, lc, _ = x_prompt.shape
    bl, ll, _ = x_sample.shape
    past = cache_mla_ckv.shape[2]
    n_ctx, n_lat = bc * lc, bl * ll
    geo = dict(n_ctx=n_ctx, lat_len=ll)

    x_parts = [x_prompt.reshape(n_ctx, D_MODEL), x_sample.reshape(n_lat, D_MODEL)]
    n_cond = 1 + bl
    cond = jnp.concatenate([c_ctx[None, :], c, jnp.zeros((-n_cond % 8, D_MODEL), F32)], 0)
    mods = _ada_vectors(cond, ada_w, ada_b)

    seg_w = (SSD_INNER, SSD_XBC, LANES, MLA_Q_RANK, MLA_KV_RANK, 2 * LANES)
    z, xbc, dt, qlat, kvlat, kr = _inproj(
        x_parts, mods[0], _in0_weights(w_in0[0]), seg_w, (F32,) * 6,
        shift_row=0, tm=512, name="in0", **geo)

    cw = jnp.pad(ssd_conv_w[0], ((0, 8 - ssd_conv_w.shape[1]), (0, 0)))
    ssd_params = (cw, ssd_conv_b[0].reshape(1, SSD_XBC),
                  _lane_row(jnp.concatenate([ssd_dt_bias_f[0], ssd_dt_bias_b[0]]), LANES),
                  _lane_row(jnp.concatenate([ssd_a_log_f[0], ssd_a_log_b[0]]), LANES),
                  jnp.repeat(ssd_d[0], SSD_HEAD_DIM).reshape(1, SSD_INNER),
                  ssd_norm_g[0].reshape(1, SSD_INNER))

    def st_in(s):
        return jnp.transpose(s, (0, 3, 1, 2)).reshape(s.shape[0], SSD_STATE, SSD_INNER)

    def st_out(s):
        return jnp.transpose(s.reshape(s.shape[0], SSD_STATE, SSD_HEADS, SSD_HEAD_DIM), (0, 2, 3, 1))

    zero_ssd = jnp.zeros((bc, SSD_STATE, SSD_INNER), F32)
    y_ssd_c, ssd_f, ssd_b = _ssd(z, xbc, dt, zero_ssd, zero_ssd, ssd_params,
                                 row0=0, n_seq=bc, seq_len=lc, name="ssd_ctx")
    y_ssd_l, _, _ = _ssd(z, xbc, dt, st_in(state_ssd_f[:, 0]), st_in(state_ssd_b[:, 0]), ssd_params,
                         row0=n_ctx, n_seq=bl, seq_len=ll, name="ssd_lat")

    wq_pad, wq_rot, wk_pad, wv = _mla_weights(mla_w_q_up[0], mla_w_kv_up[0])
    mla_w = (mla_q_norm_g[0].reshape(1, MLA_Q_RANK), wq_pad, wq_rot,
             mla_kv_norm_g[0].reshape(1, MLA_KV_RANK), wk_pad, wv)
    q_c, k_c, v_c, ckv_c = _mla_prep(qlat, kvlat, kr, None, mla_w, row0=0, n_rows=n_ctx,
                                     do_q=True, do_norm=True, tm=256, name="mla_prep_ctx")
    q_l, k_l, v_l, _ = _mla_prep(qlat, kvlat, kr, _mla_tables(ll), mla_w, row0=n_ctx, n_rows=n_lat,
                                 do_q=True, do_norm=True, tm=256, name="mla_prep_lat")
    lpad = ((0, 0), (MLA_NOPE, 2 * LANES - MLA_NOPE - MLA_ROPE))
    k_p, v_p = _mla_prep(None, cache_mla_ckv[:, 0].reshape(bl * past, MLA_KV_RANK),
                         jnp.pad(cache_mla_krope[:, 0].reshape(bl * past, MLA_ROPE), lpad),
                         None, mla_w, row0=0, n_rows=bl * past,
                         do_q=False, do_norm=False, tm=256, name="mla_prep_cache")
    o_c = _attention(q_c, k_c, v_c, None, None, n_seq=bc, seq_len=lc, cache_len=0, tq=lc, name="attn_ctx")
    o_l = _attention(q_l, k_l, v_l, k_p, v_p, n_seq=bl, seq_len=ll, cache_len=past, tq=256, name="attn_lat")

    w_out0_bf = w_out0[0].astype(BF16)
    x = _outproj([[y_ssd_c, y_ssd_l], [o_c, o_l]],
                 [w_out0_bf[:SSD_INNER], w_out0_bf[SSD_INNER:]], x_parts, mods[0],
                 ln1_g[0].reshape(1, D_MODEL), ln1_b[0].reshape(1, D_MODEL),
                 gate_row=2, tm=512, name="out0", **geo)
    x = _ffn(x, mods[0], ffn_w_gate[0].astype(BF16), ffn_w_up[0].astype(BF16), ffn_w_down[0].astype(BF16),
             ln2_g[0].reshape(1, D_MODEL), ln2_b[0].reshape(1, D_MODEL), tm=512, **geo)

    hq = RET_HEADS * RET_QK
    q1, k1, v1, g1 = _inproj([x], mods[1], w_in1[0].astype(BF16), (hq, hq, MIX1, MIX1), (F32, F32, BF16, F32),
                             shift_row=0, tm=256, name="in1", **geo)
    dec = jnp.stack([ret_decay_f[0], ret_decay_b[0]], 1)
    dec = jnp.broadcast_to(jnp.pad(dec, ((0, 0), (0, 6)))[:, :, None], (RET_HEADS, 8, RET_V))
    zero_ret = jnp.zeros((bc, RET_HEADS, RET_QK, RET_V), F32)
    tr = lambda s: jnp.swapaxes(s, -1, -2)
    y_ret_c, ret_f, ret_b = _retention(q1, k1, v1, g1, None, dec, zero_ret, zero_ret,
                                       row0=0, n_seq=bc, seq_len=lc, name="ret_ctx")
    y_ret_l, _, _ = _retention(q1, k1, v1, g1, _ret_tables(ll), dec, tr(state_ret_f[:, 0]), tr(state_ret_b[:, 0]),
                               row0=n_ctx, n_seq=bl, seq_len=ll, name="ret_lat")
    x = _outproj([[y_ret_c, y_ret_l]], [w_out1[0].astype(BF16)], [x], mods[1],
                 ln1_g[1].reshape(1, D_MODEL), ln1_b[1].reshape(1, D_MODEL),
                 gate_row=2, tm=512, name="out1", **geo)
    y_c, y_l = _moe(x, mods[1], moe_router[0], moe_w_gate[0].astype(BF16), moe_w_up[0].astype(BF16),
                    moe_w_down[0].astype(BF16), ln2_g[1].reshape(1, D_MODEL), ln2_b[1].reshape(1, D_MODEL), **geo)

    y_prompt = y_c.reshape(bc, lc, D_MODEL)
    y_sample = y_l.reshape(bl, ll, D_MODEL)
    new_ckv = ckv_c.reshape(bc, 1, lc, MLA_KV_RANK)
    new_krope = kr[:n_ctx, MLA_NOPE:MLA_NOPE + MLA_ROPE].reshape(bc, 1, lc, MLA_ROPE)
    return (y_prompt, y_sample, new_ckv, new_krope,
            st_out(ssd_f)[:, None], st_out(ssd_b)[:, None], tr(ret_f)[:, None], tr(ret_b)[:, None])
```

```python
import functools
import math

import jax
import jax.numpy as jnp
from jax import lax
from jax.experimental import pallas as pl
from jax.experimental.pallas import tpu as pltpu

F32 = jnp.float32
BF16 = jnp.bfloat16
I32 = jnp.int32

D_MODEL = 1024
DEPTH = 2
GRID_W = 64
CHUNK = 128
SSD_HEADS = 8
SSD_HEAD_DIM = 64
SSD_INNER = SSD_HEADS * SSD_HEAD_DIM
SSD_GROUPS = 2
SSD_STATE = 64
SSD_XBC = SSD_INNER + 2 * SSD_GROUPS * SSD_STATE
MLA_HEADS = 8
MLA_NOPE = 64
MLA_ROPE = 32
MLA_V = 64
MLA_Q_RANK = 384
MLA_KV_RANK = 256
MLA_OUT = MLA_HEADS * MLA_V
RET_HEADS = 8
RET_QK = 128
RET_V = 256
MIX1 = RET_HEADS * RET_V
D_FF = 2816
N_EXPERTS = 8
D_FF_EXPERT = 3584
ALPHA = (2 * DEPTH) ** 0.25
LN_EPS = 1e-5
RMS_EPS = 1e-6
ROPE_BASE = 10000.0

LANES = 128
VMEM_LIMIT = 56 * 1024 * 1024
NEG_BIG = -1e30

MOE_SLOT_TILE = 512
MOE_TOK_CHUNK = 512
MOE_FF_CHUNK = 1792
MOE_CELL_ALIGN = 16
MOE_BIG_PIECE = 64
MOE_LOCAL_ROWS = 2 * MOE_TOK_CHUNK + N_EXPERTS * MOE_CELL_ALIGN
RET_CHUNK = 256

_NT = (((1,), (1,)), ((), ()))
_TN = (((0,), (0,)), ((), ()))


def _cparams(*sem):
    return pltpu.CompilerParams(dimension_semantics=sem, vmem_limit_bytes=VMEM_LIMIT)


def _resident(shape):
    nd = len(shape)
    return pl.BlockSpec(shape, lambda *_: (0,) * nd, pipeline_mode=pl.Buffered(1))


def _silu(x):
    return x * jax.nn.sigmoid(x)


def _dot(a, b):
    return jnp.dot(a, b, preferred_element_type=F32)


def _cond_index(row, n_ctx, lat_len):
    return jnp.where(row < n_ctx, 0, 1 + (row - n_ctx) // lat_len)


def _layer_norm(y, g, b):
    mu = jnp.mean(y, -1, keepdims=True)
    d = y - mu
    var = jnp.mean(d * d, -1, keepdims=True)
    return d * lax.rsqrt(var + LN_EPS) * g + b


def _ada_kernel(c_ref, w_ref, b_ref, o_ref):
    s = _silu(c_ref[...])
    o_ref[...] = jnp.dot(s, w_ref[...], precision=lax.Precision.HIGHEST,
                         preferred_element_type=F32) + b_ref[...]


def _ada_vectors(cond, ada_w, ada_b):
    r = cond.shape[0]
    tn = 1024
    out = pl.pallas_call(
        _ada_kernel,
        grid=(DEPTH, 6 * D_MODEL // tn),
        in_specs=[pl.BlockSpec((r, D_MODEL), lambda l, j: (0, 0)),
                  pl.BlockSpec((None, D_MODEL, tn), lambda l, j: (l, 0, j)),
                  pl.BlockSpec((None, 1, tn), lambda l, j: (l, 0, j))],
        out_specs=pl.BlockSpec((None, r, tn), lambda l, j: (l, 0, j)),
        out_shape=jax.ShapeDtypeStruct((DEPTH, r, 6 * D_MODEL), F32),
        compiler_params=_cparams("parallel", "parallel"),
        name="ada",
    )(cond, ada_w, ada_b.reshape(DEPTH, 1, 6 * D_MODEL))
    out = out.reshape(DEPTH, r, 6, D_MODEL)
    return jnp.pad(out, ((0, 0), (0, 0), (0, 2), (0, 0)))


def _row_specs(parts, tm):
    cols = parts[0].shape[1]
    if len(parts) == 1:
        return [pl.BlockSpec((tm, cols), lambda i, *_: (i, 0))]
    nct = parts[0].shape[0] // tm
    return [pl.BlockSpec((tm, cols), lambda i, *_: (jnp.minimum(i, nct - 1), 0)),
            pl.BlockSpec((tm, cols), lambda i, *_: (jnp.maximum(i - nct, 0), 0))]


def _row_tile(refs, n_ctx_tiles):
    if len(refs) == 1:
        return refs[0][...]
    return jnp.where(pl.program_id(0) < n_ctx_tiles, refs[0][...], refs[1][...])


def _inproj_kernel(*refs, n_x, n_ctx_tiles, segs, shift_row):
    x_refs, (mod_ref, w_ref), o_refs = refs[:n_x], refs[n_x:n_x + 2], refs[n_x + 2:]
    x = _row_tile(x_refs, n_ctx_tiles)
    h = x * (1.0 + mod_ref[shift_row + 1:shift_row + 2, :]) + mod_ref[shift_row:shift_row + 1, :]
    h = h.astype(BF16)
    for o_ref, (a, b) in zip(o_refs, segs):
        o_ref[...] = _dot(h, w_ref[:, a:b]).astype(o_ref.dtype)


def _inproj(x_parts, mods, w, seg_widths, seg_dtypes, *, shift_row, tm, n_ctx, lat_len, name):
    nt = sum(p.shape[0] for p in x_parts)
    segs, acc = [], 0
    for wd in seg_widths:
        segs.append((acc, acc + wd))
        acc += wd
    assert acc == w.shape[1]
    cmap = lambda i: (_cond_index(i * tm, n_ctx, lat_len), 0, 0)
    return pl.pallas_call(
        functools.partial(_inproj_kernel, n_x=len(x_parts), n_ctx_tiles=n_ctx // tm,
                          segs=tuple(segs), shift_row=shift_row),
        grid=(nt // tm,),
        in_specs=_row_specs(x_parts, tm) + [pl.BlockSpec((None, 8, D_MODEL), cmap), _resident(w.shape)],
        out_specs=[pl.BlockSpec((tm, wd), lambda i: (i, 0)) for wd in seg_widths],
        out_shape=[jax.ShapeDtypeStruct((nt, wd), dt) for wd, dt in zip(seg_widths, seg_dtypes)],
        compiler_params=_cparams("parallel"),
        name=name,
    )(*x_parts, mods, w)


def _cumsum_rows(tril_bf, x):
    hi = x.astype(BF16)
    r = x - hi.astype(F32)
    mid = r.astype(BF16)
    lo = (r - mid.astype(F32)).astype(BF16)
    return _dot(tril_bf, hi) + _dot(tril_bf, mid) + _dot(tril_bf, lo)


def _ssd_kernel(z_ref, xbc_ref, dt_ref, sf0_ref, sb0_ref, cw_ref, cb_ref, dtb_ref, alog_ref,
                dsk_ref, ng_ref, y_ref, sf_ref, sb_ref,
                yacc, xs_s, cm_s, bmt_s, xb_s, erb_s, *, seq_len):
    nc = seq_len // CHUNK
    hd, ns = SSD_HEAD_DIM, SSD_STATE
    gw = (SSD_HEADS // SSD_GROUPS) * hd
    ri = lax.broadcasted_iota(I32, (CHUNK, CHUNK), 0)
    ci = lax.broadcasted_iota(I32, (CHUNK, CHUNK), 1)
    lower = ri >= ci
    upper = ri <= ci
    tril_bf = jnp.where(lower, 1.0, 0.0).astype(BF16)
    rowid = lax.broadcasted_iota(I32, (CHUNK, 1), 0)
    lane = lax.broadcasted_iota(I32, (CHUNK, LANES), 1)
    src = lax.broadcasted_iota(I32, (LANES, SSD_INNER), 0)
    dst_head = lax.broadcasted_iota(I32, (LANES, SSD_INNER), 1) // hd
    spread_f = jnp.where(src == dst_head, 1.0, 0.0).astype(BF16)
    spread_b = jnp.where(src == dst_head + SSD_HEADS, 1.0, 0.0).astype(BF16)

    def per_head(v, spread):
        hi = v.astype(BF16)
        lo = (v - hi.astype(F32)).astype(BF16)
        return _dot(hi, spread) + _dot(lo, spread)

    sf_ref[...] = sf0_ref[...]
    sb_ref[...] = sb0_ref[...]

    def fwd(c, carry):
        r0 = pl.multiple_of(c * CHUNK, CHUNK)
        rows = pl.ds(r0, CHUNK)
        cur = xbc_ref[rows, :]
        pstart = pl.multiple_of(jnp.maximum(r0 - 8, 0), 8)
        nstart = pl.multiple_of(jnp.minimum(r0 + CHUNK, seq_len - 8), 8)
        prev_row = xbc_ref[pl.ds(pstart, 8), :][7:8, :] * jnp.where(c > 0, 1.0, 0.0)
        next_row = xbc_ref[pl.ds(nstart, 8), :][0:1, :] * jnp.where(c < nc - 1, 1.0, 0.0)
        sh_prev = jnp.where(rowid == 0, prev_row, pltpu.roll(cur, 1, axis=0))
        sh_next = jnp.where(rowid == CHUNK - 1, next_row, pltpu.roll(cur, CHUNK - 1, axis=0))
        conv = cw_ref[0:1, :] * sh_prev + cw_ref[1:2, :] * cur + cw_ref[2:3, :] * sh_next + cb_ref[...]
        u = _silu(conv)
        xs = u[:, :SSD_INNER]
        bm = u[:, SSD_INNER:SSD_INNER + LANES]
        cm = u[:, SSD_INNER + LANES:]

        xr = dt_ref[rows, :] + dtb_ref[...]
        dt = jnp.maximum(xr, 0.0) + jnp.log1p(jnp.exp(-jnp.abs(xr)))
        la = -dt * jnp.exp(alog_ref[...])
        facs = _cumsum_rows(tril_bf, la)
        racs = facs[CHUNK - 1:CHUNK, :] - facs + la
        packed = jnp.where(lane < SSD_HEADS, facs,
                           jnp.where(lane < 2 * SSD_HEADS, racs, pltpu.roll(dt, 2 * SSD_HEADS, axis=1)))
        packed_t = packed.T
        e_f = jnp.exp(facs)
        e_r = jnp.exp(racs)
        w_f = dt * jnp.exp(facs[CHUNK - 1:CHUNK, :] - facs)
        w_b = dt * jnp.exp(racs[0:1, :] - racs)

        cm_bf = cm.astype(BF16)
        bm_bf = bm.astype(BF16)
        bmt_bf = bm.T.astype(BF16)
        xs_bf = xs.astype(BF16)
        ef_full = per_head(e_f, spread_f)
        xf = (xs * per_head(w_f, spread_f)).astype(BF16)
        xb_s[rows, :] = (xs * per_head(w_b, spread_b)).astype(BF16)
        yoffs, news = [], []
        for g in range(SSD_GROUPS):
            gl = slice(g * ns, (g + 1) * ns)
            s_g = lax.dot_general(cm_bf[:, gl], bm_bf[:, gl], _NT, preferred_element_type=F32)
            yoffs.append(_dot(cm_bf[:, gl], sf_ref[:, g * gw:(g + 1) * gw].astype(BF16)))
            news.append(_dot(bmt_bf[g * ns:(g + 1) * ns, :], xf[:, g * gw:(g + 1) * gw]))
            for hh in range(SSD_HEADS // SSD_GROUPS):
                h = g * (SSD_HEADS // SSD_GROUPS) + hh
                hb = SSD_HEADS + h
                hs = slice(h * hd, (h + 1) * hd)
                seg_f = facs[:, h:h + 1] - packed_t[h:h + 1, :]
                seg_b = racs[:, hb:hb + 1] - packed_t[hb:hb + 1, :]
                d_f = jnp.exp(jnp.where(lower, seg_f, NEG_BIG))
                d_b = jnp.exp(jnp.where(upper, seg_b, NEG_BIG))
                dt_f_row = packed_t[2 * SSD_HEADS + h:2 * SSD_HEADS + h + 1, :]
                dt_b_row = packed_t[2 * SSD_HEADS + hb:2 * SSD_HEADS + hb + 1, :]
                m = (s_g * (d_f * dt_f_row + d_b * dt_b_row)).astype(BF16)
                yacc[rows, hs] = _dot(m, xs_bf[:, hs])
        yacc[rows, :] += ef_full * jnp.concatenate(yoffs, -1)
        sf_ref[...] = ef_full[CHUNK - 1:CHUNK, :] * sf_ref[...] + jnp.concatenate(news, -1)
        xs_s[rows, :] = xs
        cm_s[rows, :] = cm_bf
        bmt_s[c] = bmt_bf
        erb_s[rows, :] = e_r
        return carry

    lax.fori_loop(0, nc, fwd, 0)

    def bwd(i, carry):
        c = nc - 1 - i
        r0 = pl.multiple_of(c * CHUNK, CHUNK)
        rows = pl.ds(r0, CHUNK)
        cm_bf = cm_s[rows, :]
        bmt_bf = bmt_s[c]
        er_full = per_head(erb_s[rows, :], spread_b)
        yoffs, news = [], []
        for g in range(SSD_GROUPS):
            gl = slice(g * ns, (g + 1) * ns)
            yoffs.append(_dot(cm_bf[:, gl], sb_ref[:, g * gw:(g + 1) * gw].astype(BF16)))
            news.append(_dot(bmt_bf[g * ns:(g + 1) * ns, :], xb_s[rows, g * gw:(g + 1) * gw]))
        sb_ref[...] = er_full[0:1, :] * sb_ref[...] + jnp.concatenate(news, -1)
        yv = yacc[rows, :] + er_full * jnp.concatenate(yoffs, -1) + dsk_ref[...] * xs_s[rows, :]
        gz = yv * _silu(z_ref[rows, :])
        ms = jnp.mean(gz * gz, -1, keepdims=True)
        y_ref[rows, :] = (gz * lax.rsqrt(ms + RMS_EPS) * ng_ref[...]).astype(y_ref.dtype)
        return carry

    lax.fori_loop(0, nc, bwd, 0)


def _ssd(z, xbc, dt, sf0, sb0, params, *, row0, n_seq, seq_len, name):
    cw, cb, dtb, alog, dsk, ng = params
    nc = seq_len // CHUNK
    blk0 = row0 // seq_len
    rmap = lambda b: (b + blk0, 0)
    smap = lambda b: (b, 0, 0)
    const = lambda b: (0, 0)
    st = jax.ShapeDtypeStruct((n_seq, SSD_STATE, SSD_INNER), F32)
    return pl.pallas_call(
        functools.partial(_ssd_kernel, seq_len=seq_len),
        grid=(n_seq,),
        in_specs=[pl.BlockSpec((seq_len, SSD_INNER), rmap),
                  pl.BlockSpec((seq_len, SSD_XBC), rmap),
                  pl.BlockSpec((seq_len, LANES), rmap),
                  pl.BlockSpec((None, SSD_STATE, SSD_INNER), smap),
                  pl.BlockSpec((None, SSD_STATE, SSD_INNER), smap),
                  pl.BlockSpec(cw.shape, const), pl.BlockSpec(cb.shape, const),
                  pl.BlockSpec(dtb.shape, const), pl.BlockSpec(alog.shape, const),
                  pl.BlockSpec(dsk.shape, const), pl.BlockSpec(ng.shape, const)],
        out_specs=[pl.BlockSpec((seq_len, SSD_INNER), lambda b: (b, 0)),
                   pl.BlockSpec((None, SSD_STATE, SSD_INNER), smap),
                   pl.BlockSpec((None, SSD_STATE, SSD_INNER), smap)],
        out_shape=[jax.ShapeDtypeStruct((n_seq * seq_len, SSD_INNER), BF16), st, st],
        scratch_shapes=[pltpu.VMEM((seq_len, SSD_INNER), F32),
                        pltpu.VMEM((seq_len, SSD_INNER), F32),
                        pltpu.VMEM((seq_len, LANES), BF16),
                        pltpu.VMEM((nc, LANES, CHUNK), BF16),
                        pltpu.VMEM((seq_len, SSD_INNER), BF16),
                        pltpu.VMEM((seq_len, LANES), F32)],
        compiler_params=_cparams("parallel"),
        name=name,
    )(z, xbc, dt, sf0, sb0, cw, cb, dtb, alog, dsk, ng)


def _rms(x, g):
    return x * lax.rsqrt(jnp.mean(x * x, -1, keepdims=True) + RMS_EPS) * g


def _mla_prep_kernel(*refs, do_q, do_norm, do_rope):
    it = iter(refs)
    qlat_ref = next(it) if do_q else None
    kv_ref = next(it)
    kr_ref = next(it)
    cos_ref = next(it) if do_rope else None
    sin_ref = next(it) if do_rope else None
    if do_q:
        gq_ref, wq_ref = next(it), next(it)
        wqr_ref = next(it) if do_rope else None
    gkv_ref = next(it) if do_norm else None
    wk_ref, wv_ref = next(it), next(it)
    q_out = next(it) if do_q else None
    k_out, v_out = next(it), next(it)
    ckv_out = next(it) if do_norm else None

    if do_rope:
        cs, sn = cos_ref[...], sin_ref[...]
    if do_q:
        qn = _rms(qlat_ref[...], gq_ref[...]).astype(BF16)
        qa = _dot(qn, wq_ref[...])
        if do_rope:
            qb = _dot(qn, wqr_ref[...])
        scale = (MLA_NOPE + MLA_ROPE) ** -0.5 * math.log2(math.e)
        for h in range(MLA_HEADS):
            hs = slice(h * LANES, (h + 1) * LANES)
            qh = qa[:, hs] * cs + qb[:, hs] * sn if do_rope else qa[:, hs]
            q_out[h] = (qh * scale).astype(BF16)
    ckv = kv_ref[...]
    if do_norm:
        ckv = _rms(ckv, gkv_ref[...])
        ckv_out[...] = ckv
    ckv_bf = ckv.astype(BF16)
    kr = kr_ref[...]
    krp = kr[:, :LANES] * cs + kr[:, LANES:] * sn if do_rope else kr[:, :LANES]
    kn = _dot(ckv_bf, wk_ref[...])
    for h in range(MLA_HEADS):
        k_out[h] = (kn[:, h * LANES:(h + 1) * LANES] + krp).astype(BF16)
    v_out[...] = _dot(ckv_bf, wv_ref[...]).astype(BF16)


def _mla_prep(qlat, kv, kr, tables, weights, *, row0, n_rows, do_q, do_norm, tm, name):
    do_rope = tables is not None
    gq, wq, wqr, gkv, wk, wv = weights
    b0 = row0 // tm
    rmap = lambda i: (i + b0, 0)
    omap = lambda i: (i, 0)
    hmap = lambda i: (0, i, 0)
    ins, specs = [], []

    def add(a, spec):
        ins.append(a)
        specs.append(spec)

    if do_q:
        add(qlat, pl.BlockSpec((tm, MLA_Q_RANK), rmap))
    add(kv, pl.BlockSpec((tm, MLA_KV_RANK), rmap))
    add(kr, pl.BlockSpec((tm, 2 * LANES), rmap))
    if do_rope:
        lat_tiles = tables[0].shape[0] // tm
        tmap = lambda i: (i % lat_tiles, 0)
        add(tables[0], pl.BlockSpec((tm, LANES), tmap))
        add(tables[1], pl.BlockSpec((tm, LANES), tmap))
    if do_q:
        add(gq, _resident(gq.shape))
        add(wq, _resident(wq.shape))
        if do_rope:
            add(wqr, _resident(wqr.shape))
    if do_norm:
        add(gkv, _resident(gkv.shape))
    add(wk, _resident(wk.shape))
    add(wv, _resident(wv.shape))
    out_shape, out_specs = [], []
    if do_q:
        out_shape.append(jax.ShapeDtypeStruct((MLA_HEADS, n_rows, LANES), BF16))
        out_specs.append(pl.BlockSpec((MLA_HEADS, tm, LANES), hmap))
    out_shape.append(jax.ShapeDtypeStruct((MLA_HEADS, n_rows, LANES), BF16))
    out_specs.append(pl.BlockSpec((MLA_HEADS, tm, LANES), hmap))
    out_shape.append(jax.ShapeDtypeStruct((n_rows, MLA_OUT), BF16))
    out_specs.append(pl.BlockSpec((tm, MLA_OUT), omap))
    if do_norm:
        out_shape.append(jax.ShapeDtypeStruct((n_rows, MLA_KV_RANK), F32))
        out_specs.append(pl.BlockSpec((tm, MLA_KV_RANK), omap))
    return pl.pallas_call(
        functools.partial(_mla_prep_kernel, do_q=do_q, do_norm=do_norm, do_rope=do_rope),
        grid=(n_rows // tm,),
        in_specs=specs, out_specs=out_specs, out_shape=out_shape,
        compiler_params=_cparams("parallel"),
        name=name,
    )(*ins)


def _attn_kernel(*refs, seq_len, cache_len, kblk):
    if cache_len:
        q_ref, k_ref, v_ref, kc_ref, vc_ref, o_ref = refs
    else:
        q_ref, k_ref, v_ref, o_ref = refs
    tq = q_ref.shape[1]
    blocks = [(k_ref, v_ref, i * kblk) for i in range(seq_len // kblk)]
    if cache_len:
        blocks += [(kc_ref, vc_ref, i * kblk) for i in range(cache_len // kblk)]
    vlane = lax.broadcasted_iota(I32, (kblk, LANES), 1)
    outs = []
    for hh in range(2):
        own = (vlane < MLA_V) if hh == 0 else (vlane >= MLA_V)
        q = q_ref[hh]
        m = None
        for kr, vr, off in blocks:
            s = lax.dot_general(q, kr[hh, off:off + kblk, :], _NT, preferred_element_type=F32)
            v_aug = jnp.where(own, vr[off:off + kblk, :], jnp.ones((), BF16))
            bmax = jnp.max(s, -1, keepdims=True)
            if m is None:
                m = bmax
                acc = _dot(jnp.exp2(s - m).astype(BF16), v_aug)
            else:
                m_new = jnp.maximum(m, bmax)
                acc = jnp.exp2(m - m_new) * acc + _dot(jnp.exp2(s - m_new).astype(BF16), v_aug)
                m = m_new
        outs.append(acc / pltpu.roll(acc, MLA_V, axis=1))
    lane = lax.broadcasted_iota(I32, (tq, LANES), 1)
    o_ref[...] = jnp.where(lane < MLA_V, outs[0], outs[1]).astype(o_ref.dtype)


def _attention(q, k, v, kc, vc, *, n_seq, seq_len, cache_len, tq, name):
    nq = seq_len // tq
    kblk = min(512, seq_len)
    n = n_seq * seq_len
    ins = [q, k, v]
    specs = [pl.BlockSpec((2, tq, LANES), lambda b, hp, qi: (hp, b * nq + qi, 0)),
             pl.BlockSpec((2, seq_len, LANES), lambda b, hp, qi: (hp, b, 0)),
             pl.BlockSpec((seq_len, LANES), lambda b, hp, qi: (b, hp))]
    if cache_len:
        ins += [kc, vc]
        specs += [pl.BlockSpec((2, cache_len, LANES), lambda b, hp, qi: (hp, b, 0)),
                  pl.BlockSpec((cache_len, LANES), lambda b, hp, qi: (b, hp))]
    return pl.pallas_call(
        functools.partial(_attn_kernel, seq_len=seq_len, cache_len=cache_len, kblk=kblk),
        grid=(n_seq, MLA_HEADS // 2, nq),
        in_specs=specs,
        out_specs=pl.BlockSpec((tq, LANES), lambda b, hp, qi: (b * nq + qi, hp)),
        out_shape=jax.ShapeDtypeStruct((n, MLA_OUT), BF16),
        compiler_params=_cparams("parallel", "parallel", "arbitrary"),
        name=name,
    )(*ins)


def _outproj_kernel(*refs, n_parts, n_ctx_tiles, gate_row):
    it = iter(refs)
    acc = None
    for n in n_parts[:-1]:
        a = _row_tile([next(it) for _ in range(n)], n_ctx_tiles)
        part = _dot(a, next(it)[...])
        acc = part if acc is None else acc + part
    x = _row_tile([next(it) for _ in range(n_parts[-1])], n_ctx_tiles)
    mod_ref, g_ref, b_ref, o_ref = it
    y = ALPHA * x + mod_ref[gate_row:gate_row + 1, :] * acc
    o_ref[...] = _layer_norm(y, g_ref[...], b_ref[...])


def _outproj(acts, ws, x_parts, mods, g, b, *, gate_row, tm, n_ctx, lat_len, name):
    nt = sum(p.shape[0] for p in x_parts)
    cmap = lambda i: (_cond_index(i * tm, n_ctx, lat_len), 0, 0)
    ins, specs = [], []
    for parts, w in zip(acts, ws):
        ins += list(parts) + [w]
        specs += _row_specs(parts, tm) + [_resident(w.shape)]
    ins += list(x_parts) + [mods, g, b]
    specs += _row_specs(x_parts, tm) + [pl.BlockSpec((None, 8, D_MODEL), cmap),
                                        _resident(g.shape), _resident(b.shape)]
    n_parts = tuple(len(p) for p in acts) + (len(x_parts),)
    return pl.pallas_call(
        functools.partial(_outproj_kernel, n_parts=n_parts, n_ctx_tiles=n_ctx // tm, gate_row=gate_row),
        grid=(nt // tm,),
        in_specs=specs,
        out_specs=pl.BlockSpec((tm, D_MODEL), lambda i: (i, 0)),
        out_shape=jax.ShapeDtypeStruct((nt, D_MODEL), F32),
        compiler_params=_cparams("parallel"),
        name=name,
    )(*ins)


def _ffn_kernel(x_ref, mod_ref, wg_ref, wu_ref, wd_ref, g_ref, b_ref, o_ref, *, ff_chunks):
    x = x_ref[...]
    h = (x * (1.0 + mod_ref[4:5, :]) + mod_ref[3:4, :]).astype(BF16)
    acc = None
    for a, b in ff_chunks:
        gt = _dot(h, wg_ref[:, a:b])
        up = _dot(h, wu_ref[:, a:b])
        act = (_silu(gt) * up).astype(BF16)
        part = _dot(act, wd_ref[a:b, :])
        acc = part if acc is None else acc + part
    y = ALPHA * x + mod_ref[5:6, :] * acc
    o_ref[...] = _layer_norm(y, g_ref[...], b_ref[...])


def _ffn(x, mods, wg, wu, wd, g, b, *, tm, n_ctx, lat_len):
    nt = x.shape[0]
    ff = wg.shape[1]
    chunks, a = [], 0
    while a < ff:
        chunks.append((a, min(a + 512, ff)))
        a += 512
    cmap = lambda i: (_cond_index(i * tm, n_ctx, lat_len), 0, 0)
    return pl.pallas_call(
        functools.partial(_ffn_kernel, ff_chunks=tuple(chunks)),
        grid=(nt // tm,),
        in_specs=[pl.BlockSpec((tm, D_MODEL), lambda i: (i, 0)),
                  pl.BlockSpec((None, 8, D_MODEL), cmap),
                  _resident(wg.shape), _resident(wu.shape), _resident(wd.shape),
                  _resident(g.shape), _resident(b.shape)],
        out_specs=pl.BlockSpec((tm, D_MODEL), lambda i: (i, 0)),
        out_shape=jax.ShapeDtypeStruct((nt, D_MODEL), F32),
        compiler_params=_cparams("parallel"),
        name="ffn",
    )(x, mods, wg, wu, wd, g, b)


def _ret_kernel(*refs, seq_len, do_rope, n_heads):
    if do_rope:
        (q_ref, k_ref, v_ref, g_ref, cos_ref, sin_ref, dec_ref, sf0_ref, sb0_ref,
         y_ref, sf_ref, sb_ref, yacc, q_s, kb_s, dcomb_s, ev_s, wk_s, cd_s) = refs
    else:
        (q_ref, k_ref, v_ref, g_ref, dec_ref, sf0_ref, sb0_ref,
         y_ref, sf_ref, sb_ref, yacc, q_s, kb_s, dcomb_s, ev_s, wk_s, cd_s) = refs
    rc = RET_CHUNK
    nc = seq_len // rc
    unroll = min(2, nc)

    @pl.when(pl.program_id(1) == 0)
    def _():
        ri = lax.broadcasted_iota(I32, (rc, rc), 0)
        ci = lax.broadcasted_iota(I32, (rc, rc), 1)
        dij = (ri - ci).astype(F32)
        pos_k = lax.broadcasted_iota(I32, (rc, RET_QK), 0).astype(F32)
        pos_v = lax.broadcasted_iota(I32, (rc, RET_V), 0).astype(F32)
        for hh in range(n_heads):
            la_f = -jnp.exp(dec_ref[hh, 0:1, :])
            la_b = -jnp.exp(dec_ref[hh, 1:2, :])
            dcomb_s[hh] = (jnp.exp(jnp.where(ri >= ci, dij * la_f[:, :rc], NEG_BIG)) +
                           jnp.exp(jnp.where(ri <= ci, -dij * la_b[:, :rc], NEG_BIG)))
            ev_s[hh, 0] = jnp.exp((pos_v + 1.0) * la_f)
            ev_s[hh, 1] = jnp.exp((rc - pos_v) * la_b)
            wk_s[hh, 0] = jnp.exp((rc - 1.0 - pos_k) * la_f[:, :RET_QK])
            wk_s[hh, 1] = jnp.exp(pos_k * la_b[:, :RET_QK])
            cd_s[hh, 0:1, :] = jnp.exp(rc * la_f)
            cd_s[hh, 1:2, :] = jnp.exp(rc * la_b)

    sf_ref[...] = sf0_ref[...]
    sb_ref[...] = sb0_ref[...]

    def fwd(c, carry):
        rows = pl.ds(pl.multiple_of(c * rc, rc), rc)
        if do_rope:
            cs, sn = cos_ref[rows, :], sin_ref[rows, :]
        for hh in range(n_heads):
            qs = slice(hh * RET_QK, (hh + 1) * RET_QK)
            vs = slice(hh * RET_V, (hh + 1) * RET_V)
            q = q_ref[rows, qs]
            k = k_ref[rows, qs] * (RET_QK ** -0.5)
            if do_rope:
                q = q * cs + pltpu.roll(q, RET_QK // 2, axis=1) * sn
                k = k * cs + pltpu.roll(k, RET_QK // 2, axis=1) * sn
            q_bf = q.astype(BF16)
            v = v_ref[rows, vs]
            s = lax.dot_general(q_bf, k.astype(BF16), _NT, preferred_element_type=F32)
            y = _dot((s * dcomb_s[hh]).astype(BF16), v)
            y = y + _dot(q_bf, sf_ref[hh].astype(BF16)) * ev_s[hh, 0]
            yacc[rows, vs] = y
            upd = lax.dot_general((k * wk_s[hh, 0]).astype(BF16), v, _TN, preferred_element_type=F32)
            sf_ref[hh] = cd_s[hh, 0:1, :] * sf_ref[hh] + upd
            q_s[rows, qs] = q_bf
            kb_s[rows, qs] = (k * wk_s[hh, 1]).astype(BF16)
        return carry

    lax.fori_loop(0, nc, fwd, 0, unroll=unroll)

    def bwd(i, carry):
        rows = pl.ds(pl.multiple_of((nc - 1 - i) * rc, rc), rc)
        for hh in range(n_heads):
            qs = slice(hh * RET_QK, (hh + 1) * RET_QK)
            vs = slice(hh * RET_V, (hh + 1) * RET_V)
            v = v_ref[rows, vs]
            y = yacc[rows, vs] + _dot(q_s[rows, qs], sb_ref[hh].astype(BF16)) * ev_s[hh, 1]
            upd = lax.dot_general(kb_s[rows, qs], v, _TN, preferred_element_type=F32)
            sb_ref[hh] = cd_s[hh, 1:2, :] * sb_ref[hh] + upd
            mu = jnp.mean(y, -1, keepdims=True)
            d = y - mu
            var = jnp.mean(d * d, -1, keepdims=True)
            yn = d * lax.rsqrt(var + LN_EPS)
            y_ref[rows, vs] = (yn * _silu(g_ref[rows, vs].astype(F32))).astype(y_ref.dtype)
        return carry

    lax.fori_loop(0, nc, bwd, 0, unroll=unroll)


def _retention(q, k, v, g, tables, dec, sf0, sb0, *, row0, n_seq, seq_len, hps, name):
    do_rope = tables is not None
    blk0 = row0 // seq_len
    qmap = lambda h, b: (b + blk0, h)
    smap = lambda h, b: (b, h, 0, 0)
    ins = [q, k, v, g]
    specs = [pl.BlockSpec((seq_len, hps * RET_QK), qmap), pl.BlockSpec((seq_len, hps * RET_QK), qmap),
             pl.BlockSpec((seq_len, hps * RET_V), qmap), pl.BlockSpec((seq_len, hps * RET_V), qmap)]
    if do_rope:
        ins += list(tables)
        specs += [pl.BlockSpec((seq_len, RET_QK), lambda h, b: (0, 0))] * 2
    ins += [dec, sf0, sb0]
    specs += [pl.BlockSpec((hps, 8, RET_V), lambda h, b: (h, 0, 0)),
              pl.BlockSpec((None, hps, RET_QK, RET_V), smap),
              pl.BlockSpec((None, hps, RET_QK, RET_V), smap)]
    st = jax.ShapeDtypeStruct((n_seq, RET_HEADS, RET_QK, RET_V), F32)
    rc = RET_CHUNK
    return pl.pallas_call(
        functools.partial(_ret_kernel, seq_len=seq_len, do_rope=do_rope, n_heads=hps),
        grid=(RET_HEADS // hps, n_seq),
        in_specs=specs,
        out_specs=[pl.BlockSpec((seq_len, hps * RET_V), lambda h, b: (b, h)),
                   pl.BlockSpec((None, hps, RET_QK, RET_V), smap),
                   pl.BlockSpec((None, hps, RET_QK, RET_V), smap)],
        out_shape=[jax.ShapeDtypeStruct((n_seq * seq_len, MIX1), BF16), st, st],
        scratch_shapes=[pltpu.VMEM((seq_len, hps * RET_V), F32),
                        pltpu.VMEM((seq_len, hps * RET_QK), BF16),
                        pltpu.VMEM((seq_len, hps * RET_QK), BF16),
                        pltpu.VMEM((hps, rc, rc), F32),
                        pltpu.VMEM((hps, 2, rc, RET_V), F32),
                        pltpu.VMEM((hps, 2, rc, RET_QK), F32),
                        pltpu.VMEM((hps, 8, RET_V), F32)],
        compiler_params=_cparams("parallel", "arbitrary"),
        name=name,
    )(*ins)


def _router_kernel(x_ref, mod_ref, w_ref, hb_ref, tok_ref, lpt_ref, cap_ref):
    tm = x_ref.shape[0]
    x = x_ref[...]
    h = x * (1.0 + mod_ref[4:5, :]) + mod_ref[3:4, :]
    h_hi = h.astype(BF16)
    hb_ref[...] = h_hi
    h_lo = (h - h_hi.astype(F32)).astype(BF16)
    w = w_ref[...]
    w_hi = w.astype(BF16)
    w_lo = (w - w_hi.astype(F32)).astype(BF16)
    logits = _dot(h_hi, w_hi) + (_dot(h_hi, w_lo) + _dot(h_lo, w_hi))
    lane = lax.broadcasted_iota(I32, (tm, LANES), 1)
    logits = jnp.where(lane < N_EXPERTS, logits, NEG_BIG)
    m1 = jnp.max(logits, -1, keepdims=True)
    i1 = jnp.min(jnp.where(logits == m1, lane, LANES), -1, keepdims=True)
    rest = jnp.where(lane == i1, NEG_BIG, logits)
    m2 = jnp.max(rest, -1, keepdims=True)
    i2 = jnp.min(jnp.where(rest == m2, lane, LANES), -1, keepdims=True)
    e = jnp.exp(m2 - m1)
    g1 = 1.0 / (1.0 + e)
    g2 = e / (1.0 + e)
    sel1 = lane == i1
    sel2 = lane == i2
    onehot = jnp.where(sel1 | sel2, 1.0, 0.0)
    ri = lax.broadcasted_iota(I32, (tm, tm), 0)
    ci = lax.broadcasted_iota(I32, (tm, tm), 1)
    strict = jnp.where(ri > ci, 1.0, 0.0).astype(BF16)
    prefix = _dot(strict, onehot.astype(BF16))
    n_row = jnp.sum(onehot, 0, keepdims=True)
    cap_row = jnp.floor((n_row + (MOE_CELL_ALIGN - 1.0)) * (1.0 / MOE_CELL_ALIGN)) * MOE_CELL_ALIGN
    li = lax.broadcasted_iota(I32, (LANES, LANES), 0)
    lj = lax.broadcasted_iota(I32, (LANES, LANES), 1)
    lower_lanes = jnp.where(li < lj, 1.0, 0.0).astype(BF16)
    cap8 = jnp.broadcast_to(cap_row, (8, LANES))
    base_row = _dot(cap8.astype(BF16), lower_lanes)[0:1, :]
    local = prefix + base_row
    lpos1 = jnp.sum(jnp.where(sel1, local, 0.0), -1, keepdims=True)
    lpos2 = jnp.sum(jnp.where(sel2, local, 0.0), -1, keepdims=True)
    tok = jnp.where(lane == 0, g1,
          jnp.where(lane == 1, g2,
          jnp.where(lane == 2, lpos1,
          jnp.where(lane == 3, lpos2, 0.0))))
    tok_ref[...] = tok
    lpt_ref[...] = tok.T[0:8, :]
    cap_ref[...] = cap8


def _router(x, mods, w_pad, *, n_ctx, lat_len):
    nt = x.shape[0]
    tm = MOE_TOK_CHUNK
    nchunk = nt // tm
    cmap = lambda i: (_cond_index(i * tm, n_ctx, lat_len), 0, 0)
    return pl.pallas_call(
        _router_kernel,
        grid=(nchunk,),
        in_specs=[pl.BlockSpec((tm, D_MODEL), lambda i: (i, 0)),
                  pl.BlockSpec((None, 8, D_MODEL), cmap),
                  _resident(w_pad.shape)],
        out_specs=[pl.BlockSpec((tm, D_MODEL), lambda i: (i, 0)),
                   pl.BlockSpec((tm, LANES), lambda i: (i, 0)),
                   pl.BlockSpec((None, 8, tm), lambda i: (i, 0, 0)),
                   pl.BlockSpec((None, 8, LANES), lambda i: (i, 0, 0))],
        out_shape=[jax.ShapeDtypeStruct((nt, D_MODEL), BF16),
                   jax.ShapeDtypeStruct((nt, LANES), F32),
                   jax.ShapeDtypeStruct((nchunk, 8, tm), F32),
                   jax.ShapeDtypeStruct((nchunk, 8, LANES), F32)],
        compiler_params=_cparams("parallel"),
        name="router",
    )(x, mods, w_pad)


def _cell_copies(tabs, chunk, hbm_ref, buf_ref, slot, sem, *, to_local, wait):
    g_ref, a_ref, nbig_ref, nsmall_ref = tabs
    for e in range(N_EXPERTS):
        k = chunk * N_EXPERTS + e
        g0, a0, nbig, nsmall = g_ref[k], a_ref[k], nbig_ref[k], nsmall_ref[k]

        def piece(i, carry, rows, goff, aoff):
            g = pl.multiple_of(goff + i * rows, MOE_CELL_ALIGN)
            a = pl.multiple_of(aoff + i * rows, MOE_CELL_ALIGN)
            far = hbm_ref.at[pl.ds(g, rows), :]
            near = buf_ref.at[slot, pl.ds(a, rows), :]
            cp = (pltpu.make_async_copy(far, near, sem.at[slot]) if to_local
                  else pltpu.make_async_copy(near, far, sem.at[slot]))
            if wait:
                cp.wait()
            else:
                cp.start()
            return carry

        lax.fori_loop(0, nbig, functools.partial(piece, rows=MOE_BIG_PIECE, goff=g0, aoff=a0), 0)
        done = nbig * MOE_BIG_PIECE
        lax.fori_loop(0, nsmall, functools.partial(piece, rows=MOE_CELL_ALIGN, goff=g0 + done, aoff=a0 + done), 0)


def _dispatch_kernel(g_ref, a_ref, nbig_ref, nsmall_ref, hb_ref, lpt_ref, zeros_hbm, xs_hbm, ybuf, sem):
    del zeros_hbm
    c = pl.program_id(0)
    nc = pl.num_programs(0)
    slot = c % 2
    tabs = (g_ref, a_ref, nbig_ref, nsmall_ref)
    l1 = lpt_ref[2:3, :]
    l2 = lpt_ref[3:4, :]
    rb = MOE_LOCAL_ROWS // 3
    for r in range(3):
        rid = (lax.broadcasted_iota(I32, (rb, MOE_TOK_CHUNK), 0) + r * rb).astype(F32)
        onehot = jnp.where((l1 == rid) | (l2 == rid), 1.0, 0.0).astype(BF16)
        ybuf[slot, r * rb:(r + 1) * rb, :] = _dot(onehot, hb_ref[...]).astype(BF16)

    @pl.when(c > 0)
    def _():
        _cell_copies(tabs, c - 1, xs_hbm, ybuf, 1 - slot, sem, to_local=False, wait=True)

    _cell_copies(tabs, c, xs_hbm, ybuf, slot, sem, to_local=False, wait=False)

    @pl.when(c == nc - 1)
    def _():
        _cell_copies(tabs, c, xs_hbm, ybuf, slot, sem, to_local=False, wait=True)


def _dispatch(tabs, hb, lpt, *, n_slots):
    tm = MOE_TOK_CHUNK
    grid_spec = pltpu.PrefetchScalarGridSpec(
        num_scalar_prefetch=4,
        grid=(hb.shape[0] // tm,),
        in_specs=[pl.BlockSpec((tm, D_MODEL), lambda c, *_: (c, 0)),
                  pl.BlockSpec((None, 8, tm), lambda c, *_: (c, 0, 0)),
                  pl.BlockSpec(memory_space=pl.ANY)],
        out_specs=pl.BlockSpec(memory_space=pl.ANY),
        scratch_shapes=[pltpu.VMEM((2, MOE_LOCAL_ROWS, D_MODEL), BF16),
                        pltpu.SemaphoreType.DMA((2,))],
    )
    n_prefetch = len(tabs)
    return pl.pallas_call(
        _dispatch_kernel,
        grid_spec=grid_spec,
        out_shape=jax.ShapeDtypeStruct((n_slots, D_MODEL), BF16),
        input_output_aliases={n_prefetch + 2: 0},
        compiler_params=_cparams("arbitrary"),
        name="moe_dispatch",
    )(*tabs, hb, lpt, jnp.zeros((n_slots, D_MODEL), BF16))


def _expert_kernel(te_ref, nu_ref, x_ref, wg_ref, wu_ref, wd_ref, o_ref, acc):
    t = pl.program_id(0)
    f = pl.program_id(1)
    nf = pl.num_programs(1)

    @pl.when(t < nu_ref[0])
    def _():
        x = x_ref[...]
        gt = _dot(x, wg_ref[...])
        up = _dot(x, wu_ref[...])
        part = _dot((_silu(gt) * up).astype(BF16), wd_ref[...])

        @pl.when(f == 0)
        def _():
            acc[...] = part

        @pl.when((f > 0) & (f < nf - 1))
        def _():
            acc[...] += part

        @pl.when(f == nf - 1)
        def _():
            o_ref[...] = (acc[...] + part).astype(o_ref.dtype)

    @pl.when((t >= nu_ref[0]) & (f == nf - 1))
    def _():
        o_ref[...] = jnp.zeros_like(o_ref)


def _experts(te, nu, xs, wg, wu, wd):
    bs = MOE_SLOT_TILE
    n_slots = xs.shape[0]
    nf = D_FF_EXPERT // MOE_FF_CHUNK

    def tt(t, nu):
        return jnp.minimum(t, nu[0] - 1)

    def ff(t, f, nu):
        return jnp.where(t < nu[0], f, nf - 1)

    grid_spec = pltpu.PrefetchScalarGridSpec(
        num_scalar_prefetch=2,
        grid=(n_slots // bs, nf),
        in_specs=[pl.BlockSpec((bs, D_MODEL), lambda t, f, te, nu: (tt(t, nu), 0)),
                  pl.BlockSpec((None, D_MODEL, MOE_FF_CHUNK), lambda t, f, te, nu: (te[tt(t, nu)], 0, ff(t, f, nu))),
                  pl.BlockSpec((None, D_MODEL, MOE_FF_CHUNK), lambda t, f, te, nu: (te[tt(t, nu)], 0, ff(t, f, nu))),
                  pl.BlockSpec((None, MOE_FF_CHUNK, D_MODEL), lambda t, f, te, nu: (te[tt(t, nu)], ff(t, f, nu), 0))],
        out_specs=pl.BlockSpec((bs, D_MODEL), lambda t, f, te, nu: (t, 0)),
        scratch_shapes=[pltpu.VMEM((bs, D_MODEL), F32)],
    )
    return pl.pallas_call(
        _expert_kernel,
        grid_spec=grid_spec,
        out_shape=jax.ShapeDtypeStruct((n_slots, D_MODEL), BF16),
        compiler_params=_cparams("arbitrary", "arbitrary"),
        name="moe_experts",
    )(te, nu, xs, wg, wu, wd)


def _combine_kernel(g_ref, a_ref, nbig_ref, nsmall_ref, tok_ref, x_ref, mod_ref, lg_ref, lb_ref, ys_hbm,
                    oc_ref, ol_ref, ybuf, sem, *, n_ctx_tiles):
    c = pl.program_id(0)
    nc = pl.num_programs(0)
    slot = c % 2
    tabs = (g_ref, a_ref, nbig_ref, nsmall_ref)
    tm = tok_ref.shape[0]

    @pl.when(c == 0)
    def _():
        ybuf[...] = jnp.zeros_like(ybuf)
        _cell_copies(tabs, 0, ys_hbm, ybuf, 0, sem, to_local=True, wait=False)

    @pl.when(c + 1 < nc)
    def _():
        _cell_copies(tabs, c + 1, ys_hbm, ybuf, 1 - slot, sem, to_local=True, wait=False)

    _cell_copies(tabs, c, ys_hbm, ybuf, slot, sem, to_local=True, wait=True)

    tok = tok_ref[...]
    col = lax.broadcasted_iota(I32, (tm, MOE_LOCAL_ROWS), 1).astype(F32)
    rows = ybuf[slot]
    pick1 = jnp.where(tok[:, 2:3] == col, 1.0, 0.0).astype(BF16)
    pick2 = jnp.where(tok[:, 3:4] == col, 1.0, 0.0).astype(BF16)
    f = tok[:, 0:1] * _dot(pick1, rows) + tok[:, 1:2] * _dot(pick2, rows)
    y = ALPHA * x_ref[...] + mod_ref[5:6, :] * f
    res = _layer_norm(y, lg_ref[...], lb_ref[...])

    @pl.when(c < n_ctx_tiles)
    def _():
        oc_ref[...] = res

    @pl.when(c >= n_ctx_tiles)
    def _():
        ol_ref[...] = res


def _combine(tabs, tok, ys, x, mods, g, b, *, n_ctx, lat_len):
    nt = x.shape[0]
    tm = MOE_TOK_CHUNK
    nct = n_ctx // tm
    cmap = lambda c, *_: (_cond_index(c * tm, n_ctx, lat_len), 0, 0)
    grid_spec = pltpu.PrefetchScalarGridSpec(
        num_scalar_prefetch=4,
        grid=(nt // tm,),
        in_specs=[pl.BlockSpec((tm, LANES), lambda c, *_: (c, 0)),
                  pl.BlockSpec((tm, D_MODEL), lambda c, *_: (c, 0)),
                  pl.BlockSpec((None, 8, D_MODEL), cmap),
                  pl.BlockSpec((1, D_MODEL), lambda c, *_: (0, 0)),
                  pl.BlockSpec((1, D_MODEL), lambda c, *_: (0, 0)),
                  pl.BlockSpec(memory_space=pl.ANY)],
        out_specs=[pl.BlockSpec((tm, D_MODEL), lambda c, *_: (jnp.minimum(c, nct - 1), 0)),
                   pl.BlockSpec((tm, D_MODEL), lambda c, *_: (jnp.maximum(c - nct, 0), 0))],
        scratch_shapes=[pltpu.VMEM((2, MOE_LOCAL_ROWS, D_MODEL), BF16),
                        pltpu.SemaphoreType.DMA((2,))],
    )
    return pl.pallas_call(
        functools.partial(_combine_kernel, n_ctx_tiles=nct),
        grid_spec=grid_spec,
        out_shape=[jax.ShapeDtypeStruct((n_ctx, D_MODEL), F32),
                   jax.ShapeDtypeStruct((nt - n_ctx, D_MODEL), F32)],
        compiler_params=_cparams("arbitrary"),
        name="moe_combine",
    )(*tabs, tok, x, mods, g, b, ys)


def _moe(x, mods, router_w, wg, wu, wd, ln_g, ln_b, *, n_ctx, lat_len):
    nt = x.shape[0]
    bs, tc = MOE_SLOT_TILE, MOE_TOK_CHUNK
    nchunk = nt // tc
    max_rows = 2 * nt + nchunk * N_EXPERTS * (MOE_CELL_ALIGN - 1)
    n_tiles = -(-max_rows // bs) + N_EXPERTS
    n_slots = n_tiles * bs

    w_pad = jnp.pad(router_w, ((0, 0), (0, LANES - N_EXPERTS)))
    hb, tok, lpt, capt = _router(x, mods, w_pad, n_ctx=n_ctx, lat_len=lat_len)

    cap = capt[:, 0, :N_EXPERTS].astype(I32)
    a_loc = jnp.cumsum(cap, 1) - cap
    tot = jnp.sum(cap, 0)
    padded = ((tot + bs - 1) // bs) * bs
    gend = jnp.cumsum(padded)
    g_glob = (gend - padded)[None, :] + (jnp.cumsum(cap, 0) - cap)
    n_big = cap // MOE_BIG_PIECE
    n_small = (cap - n_big * MOE_BIG_PIECE) // MOE_CELL_ALIGN
    tabs = tuple(t.reshape(-1).astype(I32) for t in (g_glob, a_loc, n_big, n_small))
    tile_start = jnp.arange(n_tiles, dtype=I32) * bs
    tile_expert = jnp.minimum(jnp.sum((gend[None, :] <= tile_start[:, None]).astype(I32), 1), N_EXPERTS - 1)
    n_used = (gend[-1] // bs).astype(I32).reshape(1)

    xs = _dispatch(tabs, hb, lpt, n_slots=n_slots)
    ys = _experts(tile_expert, n_used, xs, wg, wu, wd)
    return _combine(tabs, tok, ys, x, mods, ln_g, ln_b, n_ctx=n_ctx, lat_len=lat_len)


def _axial_angles(n_tok, dim):
    rows = n_tok // GRID_W
    row = jnp.repeat(jnp.arange(rows), GRID_W).astype(F32)
    col = jnp.tile(jnp.arange(GRID_W), rows).astype(F32)
    axis_dim = dim // 2
    inv = 1.0 / (ROPE_BASE ** (jnp.arange(0, axis_dim, 2, dtype=F32) / axis_dim))
    ang = jnp.concatenate([row[:, None] * inv, col[:, None] * inv], -1)
    return jnp.cos(ang), jnp.sin(ang)


def _mla_tables(n_lat):
    cos, sin = _axial_angles(n_lat, MLA_ROPE)
    one = jnp.ones((n_lat, MLA_NOPE), F32)
    zero = jnp.zeros((n_lat, MLA_NOPE), F32)
    pad1 = jnp.ones((n_lat, LANES - MLA_NOPE - MLA_ROPE), F32)
    pad0 = jnp.zeros((n_lat, LANES - MLA_NOPE - MLA_ROPE), F32)
    return (jnp.concatenate([one, cos, cos, pad1], -1), jnp.concatenate([zero, sin, sin, pad0], -1))


def _ret_tables(n_lat):
    cos, sin = _axial_angles(n_lat, RET_QK)
    return jnp.concatenate([cos, cos], -1), jnp.concatenate([-sin, sin], -1)


def _rot_cols(w):
    half = w.shape[1] // 2
    return jnp.concatenate([-w[:, half:], w[:, :half]], 1)


def _in0_weights(w_in0):
    z, xbc, dt, ql, kvl, kr = jnp.split(
        w_in0, [SSD_INNER, SSD_INNER + SSD_XBC, SSD_INNER + SSD_XBC + 2 * SSD_HEADS,
                SSD_INNER + SSD_XBC + 2 * SSD_HEADS + MLA_Q_RANK,
                SSD_INNER + SSD_XBC + 2 * SSD_HEADS + MLA_Q_RANK + MLA_KV_RANK], axis=1)
    dtp = jnp.pad(dt, ((0, 0), (0, LANES - 2 * SSD_HEADS)))
    lpad = ((0, 0), (MLA_NOPE, LANES - MLA_NOPE - MLA_ROPE))
    krp = jnp.concatenate([jnp.pad(kr, lpad), jnp.pad(_rot_cols(kr), lpad)], 1)
    return jnp.concatenate([z, xbc, dtp, ql, kvl, krp], 1).astype(BF16)


def _mla_weights(w_q_up, w_kv_up):
    d = w_q_up.shape[0]
    wq = w_q_up.reshape(d, MLA_HEADS, MLA_NOPE + MLA_ROPE)
    nope, rope = wq[..., :MLA_NOPE], wq[..., MLA_NOPE:]
    half = MLA_ROPE // 2
    rot = jnp.concatenate([-rope[..., half:], rope[..., :half]], -1)
    tail = jnp.zeros((d, MLA_HEADS, LANES - MLA_NOPE - MLA_ROPE), F32)
    wq_pad = jnp.concatenate([nope, rope, tail], -1).reshape(d, MLA_HEADS * LANES)
    wq_rot = jnp.concatenate([jnp.zeros_like(nope), rot, tail], -1).reshape(d, MLA_HEADS * LANES)
    r = w_kv_up.shape[0]
    wkv = w_kv_up.reshape(r, MLA_HEADS, MLA_NOPE + MLA_V)
    wk = jnp.concatenate([wkv[..., :MLA_NOPE], jnp.zeros((r, MLA_HEADS, LANES - MLA_NOPE), F32)], -1)
    wv = wkv[..., MLA_NOPE:]
    return (wq_pad.astype(BF16), wq_rot.astype(BF16),
            wk.reshape(r, MLA_HEADS * LANES).astype(BF16), wv.reshape(r, MLA_OUT).astype(BF16))


def _lane_row(v, width):
    return jnp.pad(v, (0, width - v.shape[0])).reshape(1, width)


def kernel(x_prompt, x_sample, cache_mla_ckv, cache_mla_krope, state_ssd_f, state_ssd_b, state_ret_f, state_ret_b, c, c_ctx, ada_w, ada_b, ln1_g, ln1_b, ln2_g, ln2_b, w_in0, ssd_conv_w, ssd_conv_b, ssd_a_log_f, ssd_a_log_b, ssd_dt_bias_f, ssd_dt_bias_b, ssd_d, ssd_norm_g, mla_q_norm_g, mla_w_q_up, mla_kv_norm_g, mla_w_kv_up, w_out0, ffn_w_gate, ffn_w_up, ffn_w_down, w_in1, ret_decay_f, ret_decay_b, w_out1, moe_router, moe_w_gate, moe_w_up, moe_w_down):
    bc, lc, _ = x_prompt.shape
    bl, ll, _ = x_sample.shape
    past = cache_mla_ckv.shape[2]
    n_ctx, n_lat = bc * lc, bl * ll
    geo = dict(n_ctx=n_ctx, lat_len=ll)

    x_parts = [x_prompt.reshape(n_ctx, D_MODEL), x_sample.reshape(n_lat, D_MODEL)]
    n_cond = 1 + bl
    cond = jnp.concatenate([c_ctx[None, :], c, jnp.zeros((-n_cond % 8, D_MODEL), F32)], 0)
    mods = _ada_vectors(cond, ada_w, ada_b)

    seg_w = (SSD_INNER, SSD_XBC, LANES, MLA_Q_RANK, MLA_KV_RANK, 2 * LANES)
    z, xbc, dt, qlat, kvlat, kr = _inproj(
        x_parts, mods[0], _in0_weights(w_in0[0]), seg_w, (F32,) * 6,
        shift_row=0, tm=512, name="in0", **geo)

    cw = jnp.pad(ssd_conv_w[0], ((0, 8 - ssd_conv_w.shape[1]), (0, 0)))
    ssd_params = (cw, ssd_conv_b[0].reshape(1, SSD_XBC),
                  _lane_row(jnp.concatenate([ssd_dt_bias_f[0], ssd_dt_bias_b[0]]), LANES),
                  _lane_row(jnp.concatenate([ssd_a_log_f[0], ssd_a_log_b[0]]), LANES),
                  jnp.repeat(ssd_d[0], SSD_HEAD_DIM).reshape(1, SSD_INNER),
                  ssd_norm_g[0].reshape(1, SSD_INNER))

    def st_in(s):
        return jnp.transpose(s, (0, 3, 1, 2)).reshape(s.shape[0], SSD_STATE, SSD_INNER)

    def st_out(s):
        return jnp.transpose(s.reshape(s.shape[0], SSD_STATE, SSD_HEADS, SSD_HEAD_DIM), (0, 2, 3, 1))

    zero_ssd = jnp.zeros((bc, SSD_STATE, SSD_INNER), F32)
    y_ssd_c, ssd_f, ssd_b = _ssd(z, xbc, dt, zero_ssd, zero_ssd, ssd_params,
                                 row0=0, n_seq=bc, seq_len=lc, name="ssd_ctx")
    y_ssd_l, _, _ = _ssd(z, xbc, dt, st_in(state_ssd_f[:, 0]), st_in(state_ssd_b[:, 0]), ssd_params,
                         row0=n_ctx, n_seq=bl, seq_len=ll, name="ssd_lat")

    wq_pad, wq_rot, wk_pad, wv = _mla_weights(mla_w_q_up[0], mla_w_kv_up[0])
    mla_w = (mla_q_norm_g[0].reshape(1, MLA_Q_RANK), wq_pad, wq_rot,
             mla_kv_norm_g[0].reshape(1, MLA_KV_RANK), wk_pad, wv)
    q_c, k_c, v_c, ckv_c = _mla_prep(qlat, kvlat, kr, None, mla_w, row0=0, n_rows=n_ctx,
                                     do_q=True, do_norm=True, tm=256, name="mla_prep_ctx")
    q_l, k_l, v_l, _ = _mla_prep(qlat, kvlat, kr, _mla_tables(ll), mla_w, row0=n_ctx, n_rows=n_lat,
                                 do_q=True, do_norm=True, tm=256, name="mla_prep_lat")
    lpad = ((0, 0), (MLA_NOPE, 2 * LANES - MLA_NOPE - MLA_ROPE))
    k_p, v_p = _mla_prep(None, cache_mla_ckv[:, 0].reshape(bl * past, MLA_KV_RANK),
                         jnp.pad(cache_mla_krope[:, 0].reshape(bl * past, MLA_ROPE), lpad),
                         None, mla_w, row0=0, n_rows=bl * past,
                         do_q=False, do_norm=False, tm=256, name="mla_prep_cache")
    o_c = _attention(q_c, k_c, v_c, None, None, n_seq=bc, seq_len=lc, cache_len=0, tq=lc, name="attn_ctx")
    o_l = _attention(q_l, k_l, v_l, k_p, v_p, n_seq=bl, seq_len=ll, cache_len=past, tq=min(512, ll), name="attn_lat")

    w_out0_bf = w_out0[0].astype(BF16)
    x = _outproj([[y_ssd_c, y_ssd_l], [o_c, o_l]],
                 [w_out0_bf[:SSD_INNER], w_out0_bf[SSD_INNER:]], x_parts, mods[0],
                 ln1_g[0].reshape(1, D_MODEL), ln1_b[0].reshape(1, D_MODEL),
                 gate_row=2, tm=512, name="out0", **geo)
    x = _ffn(x, mods[0], ffn_w_gate[0].astype(BF16), ffn_w_up[0].astype(BF16), ffn_w_down[0].astype(BF16),
             ln2_g[0].reshape(1, D_MODEL), ln2_b[0].reshape(1, D_MODEL), tm=512, **geo)

    hq = RET_HEADS * RET_QK
    q1, k1, v1, g1 = _inproj([x], mods[1], w_in1[0].astype(BF16), (hq, hq, MIX1, MIX1), (F32, F32, BF16, BF16),
                             shift_row=0, tm=256, name="in1", **geo)
    dec = jnp.stack([ret_decay_f[0], ret_decay_b[0]], 1)
    dec = jnp.broadcast_to(jnp.pad(dec, ((0, 0), (0, 6)))[:, :, None], (RET_HEADS, 8, RET_V))
    zero_ret = jnp.zeros((bc, RET_HEADS, RET_QK, RET_V), F32)
    tr = lambda s: jnp.swapaxes(s, -1, -2)
    y_ret_c, ret_f, ret_b = _retention(q1, k1, v1, g1, None, dec, zero_ret, zero_ret,
                                       row0=0, n_seq=bc, seq_len=lc, hps=4, name="ret_ctx")
    y_ret_l, _, _ = _retention(q1, k1, v1, g1, _ret_tables(ll), dec, tr(state_ret_f[:, 0]), tr(state_ret_b[:, 0]),
                               row0=n_ctx, n_seq=bl, seq_len=ll, hps=2, name="ret_lat")
    x = _outproj([[y_ret_c, y_ret_l]], [w_out1[0].astype(BF16)], [x], mods[1],
                 ln1_g[1].reshape(1, D_MODEL), ln1_b[1].reshape(1, D_MODEL),
                 gate_row=2, tm=512, name="out1", **geo)
    y_c, y_l = _moe(x, mods[1], moe_router[0], moe_w_gate[0].astype(BF16), moe_w_up[0].astype(BF16),
                    moe_w_down[0].astype(BF16), ln2_g[1].reshape(1, D_MODEL), ln2_b[1].reshape(1, D_MODEL), **geo)

    y_prompt = y_c.reshape(bc, lc, D_MODEL)
    y_sample = y_l.reshape(bl, ll, D_MODEL)
    new_ckv = ckv_c.reshape(bc, 1, lc, MLA_KV_RANK)
    new_krope = kr[:n_ctx, MLA_NOPE:MLA_NOPE + MLA_ROPE].reshape(bc, 1, lc, MLA_ROPE)
    return (y_prompt, y_sample, new_ckv, new_krope,
            st_out(ssd_f)[:, None], st_out(ssd_b)[:, None], tr(ret_f)[:, None], tr(ret_b)[:, None])
```

```python
import functools
import math

import jax
import jax.numpy as jnp
from jax import lax
from jax.experimental import pallas as pl
from jax.experimental.pallas import tpu as pltpu

F32 = jnp.float32
BF16 = jnp.bfloat16
I32 = jnp.int32

D_MODEL = 1024
DEPTH = 2
GRID_W = 64
CHUNK = 128
SSD_HEADS = 8
SSD_HEAD_DIM = 64
SSD_INNER = SSD_HEADS * SSD_HEAD_DIM
SSD_GROUPS = 2
SSD_STATE = 64
SSD_XBC = SSD_INNER + 2 * SSD_GROUPS * SSD_STATE
MLA_HEADS = 8
MLA_NOPE = 64
MLA_ROPE = 32
MLA_V = 64
MLA_Q_RANK = 384
MLA_KV_RANK = 256
MLA_OUT = MLA_HEADS * MLA_V
RET_HEADS = 8
RET_QK = 128
RET_V = 256
MIX1 = RET_HEADS * RET_V
D_FF = 2816
N_EXPERTS = 8
D_FF_EXPERT = 3584
ALPHA = (2 * DEPTH) ** 0.25
LN_EPS = 1e-5
RMS_EPS = 1e-6
ROPE_BASE = 10000.0

LANES = 128
VMEM_LIMIT = 56 * 1024 * 1024
NEG_BIG = -1e30

MOE_SLOT_TILE = 512
MOE_TOK_CHUNK = 512
MOE_FF_CHUNK = 1792
MOE_CELL_ALIGN = 16
MOE_BIG_PIECE = 64
MOE_LOCAL_ROWS = 2 * MOE_TOK_CHUNK + N_EXPERTS * MOE_CELL_ALIGN
RET_CHUNK = 256

_NT = (((1,), (1,)), ((), ()))
_TN = (((0,), (0,)), ((), ()))


def _cparams(*sem):
    return pltpu.CompilerParams(dimension_semantics=sem, vmem_limit_bytes=VMEM_LIMIT)


def _resident(shape):
    nd = len(shape)
    return pl.BlockSpec(shape, lambda *_: (0,) * nd, pipeline_mode=pl.Buffered(1))


def _silu(x):
    return x * jax.nn.sigmoid(x)


def _dot(a, b):
    return jnp.dot(a, b, preferred_element_type=F32)


def _cond_index(row, n_ctx, lat_len):
    return jnp.where(row < n_ctx, 0, 1 + (row - n_ctx) // lat_len)


def _layer_norm(y, g, b):
    mu = jnp.mean(y, -1, keepdims=True)
    d = y - mu
    var = jnp.mean(d * d, -1, keepdims=True)
    return d * lax.rsqrt(var + LN_EPS) * g + b


def _ada_kernel(c_ref, w_ref, b_ref, o_ref):
    s = _silu(c_ref[...])
    o_ref[...] = jnp.dot(s, w_ref[...], precision=lax.Precision.HIGHEST,
                         preferred_element_type=F32) + b_ref[...]


def _ada_vectors(cond, ada_w, ada_b):
    r = cond.shape[0]
    tn = 1024
    out = pl.pallas_call(
        _ada_kernel,
        grid=(DEPTH, 6 * D_MODEL // tn),
        in_specs=[pl.BlockSpec((r, D_MODEL), lambda l, j: (0, 0)),
                  pl.BlockSpec((None, D_MODEL, tn), lambda l, j: (l, 0, j)),
                  pl.BlockSpec((None, 1, tn), lambda l, j: (l, 0, j))],
        out_specs=pl.BlockSpec((None, r, tn), lambda l, j: (l, 0, j)),
        out_shape=jax.ShapeDtypeStruct((DEPTH, r, 6 * D_MODEL), F32),
        compiler_params=_cparams("parallel", "parallel"),
        name="ada",
    )(cond, ada_w, ada_b.reshape(DEPTH, 1, 6 * D_MODEL))
    out = out.reshape(DEPTH, r, 6, D_MODEL)
    return jnp.pad(out, ((0, 0), (0, 0), (0, 2), (0, 0)))


def _row_specs(parts, tm):
    cols = parts[0].shape[1]
    if len(parts) == 1:
        return [pl.BlockSpec((tm, cols), lambda i, *_: (i, 0))]
    nct = parts[0].shape[0] // tm
    return [pl.BlockSpec((tm, cols), lambda i, *_: (jnp.minimum(i, nct - 1), 0)),
            pl.BlockSpec((tm, cols), lambda i, *_: (jnp.maximum(i - nct, 0), 0))]


def _row_tile(refs, n_ctx_tiles):
    if len(refs) == 1:
        return refs[0][...]
    return jnp.where(pl.program_id(0) < n_ctx_tiles, refs[0][...], refs[1][...])


def _inproj_kernel(*refs, n_x, n_ctx_tiles, segs, seg_silu, shift_row):
    x_refs, (mod_ref, w_ref), o_refs = refs[:n_x], refs[n_x:n_x + 2], refs[n_x + 2:]
    x = _row_tile(x_refs, n_ctx_tiles)
    h = x * (1.0 + mod_ref[shift_row + 1:shift_row + 2, :]) + mod_ref[shift_row:shift_row + 1, :]
    h = h.astype(BF16)
    for o_ref, (a, b), act in zip(o_refs, segs, seg_silu):
        y = _dot(h, w_ref[:, a:b])
        o_ref[...] = (_silu(y) if act else y).astype(o_ref.dtype)


def _inproj(x_parts, mods, w, seg_widths, seg_dtypes, seg_silu, *, shift_row, tm, n_ctx, lat_len, name):
    nt = sum(p.shape[0] for p in x_parts)
    segs, acc = [], 0
    for wd in seg_widths:
        segs.append((acc, acc + wd))
        acc += wd
    assert acc == w.shape[1]
    cmap = lambda i: (_cond_index(i * tm, n_ctx, lat_len), 0, 0)
    return pl.pallas_call(
        functools.partial(_inproj_kernel, n_x=len(x_parts), n_ctx_tiles=n_ctx // tm,
                          segs=tuple(segs), seg_silu=tuple(seg_silu), shift_row=shift_row),
        grid=(nt // tm,),
        in_specs=_row_specs(x_parts, tm) + [pl.BlockSpec((None, 8, D_MODEL), cmap), _resident(w.shape)],
        out_specs=[pl.BlockSpec((tm, wd), lambda i: (i, 0)) for wd in seg_widths],
        out_shape=[jax.ShapeDtypeStruct((nt, wd), dt) for wd, dt in zip(seg_widths, seg_dtypes)],
        compiler_params=_cparams("parallel"),
        name=name,
    )(*x_parts, mods, w)


def _cumsum_rows(tril_bf, x):
    hi = x.astype(BF16)
    r = x - hi.astype(F32)
    mid = r.astype(BF16)
    lo = (r - mid.astype(F32)).astype(BF16)
    return _dot(tril_bf, hi) + _dot(tril_bf, mid) + _dot(tril_bf, lo)


def _ssd_kernel(z_ref, xbc_ref, dt_ref, sf0_ref, sb0_ref, cw_ref, cb_ref, dtb_ref, alog_ref,
                dsk_ref, ng_ref, y_ref, sf_ref, sb_ref,
                yacc, xs_s, cm_s, bmt_s, xb_s, erb_s, *, seq_len):
    nc = seq_len // CHUNK
    hd, ns = SSD_HEAD_DIM, SSD_STATE
    gw = (SSD_HEADS // SSD_GROUPS) * hd
    ri = lax.broadcasted_iota(I32, (CHUNK, CHUNK), 0)
    ci = lax.broadcasted_iota(I32, (CHUNK, CHUNK), 1)
    lower = ri >= ci
    upper = ri <= ci
    tril_bf = jnp.where(lower, 1.0, 0.0).astype(BF16)
    rowid = lax.broadcasted_iota(I32, (CHUNK, 1), 0)
    lane = lax.broadcasted_iota(I32, (CHUNK, LANES), 1)
    src = lax.broadcasted_iota(I32, (LANES, SSD_INNER), 0)
    dst_head = lax.broadcasted_iota(I32, (LANES, SSD_INNER), 1) // hd
    spread_f = jnp.where(src == dst_head, 1.0, 0.0).astype(BF16)
    spread_b = jnp.where(src == dst_head + SSD_HEADS, 1.0, 0.0).astype(BF16)

    def per_head(v, spread):
        hi = v.astype(BF16)
        lo = (v - hi.astype(F32)).astype(BF16)
        return _dot(hi, spread) + _dot(lo, spread)

    sf_ref[...] = sf0_ref[...]
    sb_ref[...] = sb0_ref[...]

    def fwd(c, carry):
        r0 = pl.multiple_of(c * CHUNK, CHUNK)
        rows = pl.ds(r0, CHUNK)
        cur = xbc_ref[rows, :]
        pstart = pl.multiple_of(jnp.maximum(r0 - 8, 0), 8)
        nstart = pl.multiple_of(jnp.minimum(r0 + CHUNK, seq_len - 8), 8)
        prev_row = xbc_ref[pl.ds(pstart, 8), :][7:8, :] * jnp.where(c > 0, 1.0, 0.0)
        next_row = xbc_ref[pl.ds(nstart, 8), :][0:1, :] * jnp.where(c < nc - 1, 1.0, 0.0)
        sh_prev = jnp.where(rowid == 0, prev_row, pltpu.roll(cur, 1, axis=0))
        sh_next = jnp.where(rowid == CHUNK - 1, next_row, pltpu.roll(cur, CHUNK - 1, axis=0))
        conv = cw_ref[0:1, :] * sh_prev + cw_ref[1:2, :] * cur + cw_ref[2:3, :] * sh_next + cb_ref[...]
        u = _silu(conv)
        xs = u[:, :SSD_INNER]
        bm = u[:, SSD_INNER:SSD_INNER + LANES]
        cm = u[:, SSD_INNER + LANES:]

        xr = dt_ref[rows, :] + dtb_ref[...]
        dt = jnp.maximum(xr, 0.0) + jnp.log1p(jnp.exp(-jnp.abs(xr)))
        la = -dt * jnp.exp(alog_ref[...])
        facs = _cumsum_rows(tril_bf, la)
        racs = facs[CHUNK - 1:CHUNK, :] - facs + la
        packed = jnp.where(lane < SSD_HEADS, facs,
                           jnp.where(lane < 2 * SSD_HEADS, racs, pltpu.roll(dt, 2 * SSD_HEADS, axis=1)))
        packed_t = packed.T
        e_f = jnp.exp(facs)
        e_r = jnp.exp(racs)
        w_f = dt * jnp.exp(facs[CHUNK - 1:CHUNK, :] - facs)
        w_b = dt * jnp.exp(racs[0:1, :] - racs)

        cm_bf = cm.astype(BF16)
        bm_bf = bm.astype(BF16)
        bmt_bf = bm.T.astype(BF16)
        xs_bf = xs.astype(BF16)
        ef_full = per_head(e_f, spread_f)
        xf = (xs * per_head(w_f, spread_f)).astype(BF16)
        xb_s[rows, :] = (xs * per_head(w_b, spread_b)).astype(BF16)
        yoffs, news = [], []
        for g in range(SSD_GROUPS):
            gl = slice(g * ns, (g + 1) * ns)
            s_g = lax.dot_general(cm_bf[:, gl], bm_bf[:, gl], _NT, preferred_element_type=F32)
            yoffs.append(_dot(cm_bf[:, gl], sf_ref[:, g * gw:(g + 1) * gw].astype(BF16)))
            news.append(_dot(bmt_bf[g * ns:(g + 1) * ns, :], xf[:, g * gw:(g + 1) * gw]))
            for hh in range(SSD_HEADS // SSD_GROUPS):
                h = g * (SSD_HEADS // SSD_GROUPS) + hh
                hb = SSD_HEADS + h
                hs = slice(h * hd, (h + 1) * hd)
                seg_f = facs[:, h:h + 1] - packed_t[h:h + 1, :]
                seg_b = racs[:, hb:hb + 1] - packed_t[hb:hb + 1, :]
                d_f = jnp.exp(jnp.where(lower, seg_f, NEG_BIG))
                d_b = jnp.exp(jnp.where(upper, seg_b, NEG_BIG))
                dt_f_row = packed_t[2 * SSD_HEADS + h:2 * SSD_HEADS + h + 1, :]
                dt_b_row = packed_t[2 * SSD_HEADS + hb:2 * SSD_HEADS + hb + 1, :]
                m = (s_g * (d_f * dt_f_row + d_b * dt_b_row)).astype(BF16)
                yacc[rows, hs] = _dot(m, xs_bf[:, hs])
        yacc[rows, :] += ef_full * jnp.concatenate(yoffs, -1)
        sf_ref[...] = ef_full[CHUNK - 1:CHUNK, :] * sf_ref[...] + jnp.concatenate(news, -1)
        xs_s[rows, :] = xs
        cm_s[rows, :] = cm_bf
        bmt_s[c] = bmt_bf
        erb_s[rows, :] = e_r
        return carry

    lax.fori_loop(0, nc, fwd, 0)

    def bwd(i, carry):
        c = nc - 1 - i
        r0 = pl.multiple_of(c * CHUNK, CHUNK)
        rows = pl.ds(r0, CHUNK)
        cm_bf = cm_s[rows, :]
        bmt_bf = bmt_s[c]
        er_full = per_head(erb_s[rows, :], spread_b)
        yoffs, news = [], []
        for g in range(SSD_GROUPS):
            gl = slice(g * ns, (g + 1) * ns)
            yoffs.append(_dot(cm_bf[:, gl], sb_ref[:, g * gw:(g + 1) * gw].astype(BF16)))
            news.append(_dot(bmt_bf[g * ns:(g + 1) * ns, :], xb_s[rows, g * gw:(g + 1) * gw]))
        sb_ref[...] = er_full[0:1, :] * sb_ref[...] + jnp.concatenate(news, -1)
        yv = yacc[rows, :] + er_full * jnp.concatenate(yoffs, -1) + dsk_ref[...] * xs_s[rows, :]
        gz = yv * z_ref[rows, :].astype(F32)
        ms = jnp.mean(gz * gz, -1, keepdims=True)
        y_ref[rows, :] = (gz * lax.rsqrt(ms + RMS_EPS) * ng_ref[...]).astype(y_ref.dtype)
        return carry

    lax.fori_loop(0, nc, bwd, 0)


def _ssd(z, xbc, dt, sf0, sb0, params, *, row0, n_seq, seq_len, name):
    cw, cb, dtb, alog, dsk, ng = params
    nc = seq_len // CHUNK
    blk0 = row0 // seq_len
    rmap = lambda b: (b + blk0, 0)
    smap = lambda b: (b, 0, 0)
    const = lambda b: (0, 0)
    st = jax.ShapeDtypeStruct((n_seq, SSD_STATE, SSD_INNER), F32)
    return pl.pallas_call(
        functools.partial(_ssd_kernel, seq_len=seq_len),
        grid=(n_seq,),
        in_specs=[pl.BlockSpec((seq_len, SSD_INNER), rmap),
                  pl.BlockSpec((seq_len, SSD_XBC), rmap),
                  pl.BlockSpec((seq_len, LANES), rmap),
                  pl.BlockSpec((None, SSD_STATE, SSD_INNER), smap),
                  pl.BlockSpec((None, SSD_STATE, SSD_INNER), smap),
                  pl.BlockSpec(cw.shape, const), pl.BlockSpec(cb.shape, const),
                  pl.BlockSpec(dtb.shape, const), pl.BlockSpec(alog.shape, const),
                  pl.BlockSpec(dsk.shape, const), pl.BlockSpec(ng.shape, const)],
        out_specs=[pl.BlockSpec((seq_len, SSD_INNER), lambda b: (b, 0)),
                   pl.BlockSpec((None, SSD_STATE, SSD_INNER), smap),
                   pl.BlockSpec((None, SSD_STATE, SSD_INNER), smap)],
        out_shape=[jax.ShapeDtypeStruct((n_seq * seq_len, SSD_INNER), BF16), st, st],
        scratch_shapes=[pltpu.VMEM((seq_len, SSD_INNER), F32),
                        pltpu.VMEM((seq_len, SSD_INNER), F32),
                        pltpu.VMEM((seq_len, LANES), BF16),
                        pltpu.VMEM((nc, LANES, CHUNK), BF16),
                        pltpu.VMEM((seq_len, SSD_INNER), BF16),
                        pltpu.VMEM((seq_len, LANES), F32)],
        compiler_params=_cparams("parallel"),
        name=name,
    )(z, xbc, dt, sf0, sb0, cw, cb, dtb, alog, dsk, ng)


def _rms(x, g):
    return x * lax.rsqrt(jnp.mean(x * x, -1, keepdims=True) + RMS_EPS) * g


def _mla_prep_kernel(*refs, do_q, do_norm, do_rope):
    it = iter(refs)
    qlat_ref = next(it) if do_q else None
    kv_ref = next(it)
    kr_ref = next(it)
    cos_ref = next(it) if do_rope else None
    sin_ref = next(it) if do_rope else None
    if do_q:
        gq_ref, wq_ref = next(it), next(it)
        wqr_ref = next(it) if do_rope else None
    gkv_ref = next(it) if do_norm else None
    wk_ref, wv_ref = next(it), next(it)
    q_out = next(it) if do_q else None
    k_out, v_out = next(it), next(it)
    ckv_out = next(it) if do_norm else None
    cs, sn = (cos_ref[...], sin_ref[...]) if do_rope else (None, None)
    _mla_project(qlat_ref[...] if do_q else None, kv_ref[...], kr_ref[...], cs, sn,
                 (gq_ref, wq_ref, wqr_ref) if do_q else None, gkv_ref, wk_ref, wv_ref,
                 q_out, k_out, v_out, ckv_out)


def _mla_project(qlat, kvlat, kr, cs, sn, q_w, gkv_ref, wk_ref, wv_ref, q_out, k_out, v_out, ckv_out):
    do_rope = cs is not None
    if q_w is not None:
        gq_ref, wq_ref, wqr_ref = q_w
        qn = _rms(qlat, gq_ref[...]).astype(BF16)
        qa = _dot(qn, wq_ref[...])
        if do_rope:
            qb = _dot(qn, wqr_ref[...])
        scale = (MLA_NOPE + MLA_ROPE) ** -0.5 * math.log2(math.e)
        for h in range(MLA_HEADS):
            hs = slice(h * LANES, (h + 1) * LANES)
            qh = qa[:, hs] * cs + qb[:, hs] * sn if do_rope else qa[:, hs]
            q_out[h] = (qh * scale).astype(BF16)
    ckv = kvlat
    if gkv_ref is not None:
        ckv = _rms(ckv, gkv_ref[...])
        ckv_out[...] = ckv
    ckv_bf = ckv.astype(BF16)
    krp = kr[:, :LANES] * cs + kr[:, LANES:] * sn if do_rope else kr[:, :LANES]
    kn = _dot(ckv_bf, wk_ref[...])
    for h in range(MLA_HEADS):
        k_out[h] = (kn[:, h * LANES:(h + 1) * LANES] + krp).astype(BF16)
    v_out[...] = _dot(ckv_bf, wv_ref[...]).astype(BF16)


def _mla_prep(qlat, kv, kr, tables, weights, *, row0, n_rows, do_q, do_norm, tm, name):
    do_rope = tables is not None
    gq, wq, wqr, gkv, wk, wv = weights
    b0 = row0 // tm
    rmap = lambda i: (i + b0, 0)
    omap = lambda i: (i, 0)
    hmap = lambda i: (0, i, 0)
    ins, specs = [], []

    def add(a, spec):
        ins.append(a)
        specs.append(spec)

    if do_q:
        add(qlat, pl.BlockSpec((tm, MLA_Q_RANK), rmap))
    add(kv, pl.BlockSpec((tm, MLA_KV_RANK), rmap))
    add(kr, pl.BlockSpec((tm, 2 * LANES), rmap))
    if do_rope:
        lat_tiles = tables[0].shape[0] // tm
        tmap = lambda i: (i % lat_tiles, 0)
        add(tables[0], pl.BlockSpec((tm, LANES), tmap))
        add(tables[1], pl.BlockSpec((tm, LANES), tmap))
    if do_q:
        add(gq, _resident(gq.shape))
        add(wq, _resident(wq.shape))
        if do_rope:
            add(wqr, _resident(wqr.shape))
    if do_norm:
        add(gkv, _resident(gkv.shape))
    add(wk, _resident(wk.shape))
    add(wv, _resident(wv.shape))
    out_shape, out_specs = [], []
    if do_q:
        out_shape.append(jax.ShapeDtypeStruct((MLA_HEADS, n_rows, LANES), BF16))
        out_specs.append(pl.BlockSpec((MLA_HEADS, tm, LANES), hmap))
    out_shape.append(jax.ShapeDtypeStruct((MLA_HEADS, n_rows, LANES), BF16))
    out_specs.append(pl.BlockSpec((MLA_HEADS, tm, LANES), hmap))
    out_shape.append(jax.ShapeDtypeStruct((n_rows, MLA_OUT), BF16))
    out_specs.append(pl.BlockSpec((tm, MLA_OUT), omap))
    if do_norm:
        out_shape.append(jax.ShapeDtypeStruct((n_rows, MLA_KV_RANK), F32))
        out_specs.append(pl.BlockSpec((tm, MLA_KV_RANK), omap))
    return pl.pallas_call(
        functools.partial(_mla_prep_kernel, do_q=do_q, do_norm=do_norm, do_rope=do_rope),
        grid=(n_rows // tm,),
        in_specs=specs, out_specs=out_specs, out_shape=out_shape,
        compiler_params=_cparams("parallel"),
        name=name,
    )(*ins)


_IN0_SEGS = (SSD_INNER, SSD_XBC, LANES, MLA_Q_RANK, MLA_KV_RANK, 2 * LANES)


def _in0_kernel(*refs, n_x, n_ctx_tiles):
    x_refs = refs[:n_x]
    (mod_ref, w_ref, cos_ref, sin_ref, gq_ref, wq_ref, wqr_ref, gkv_ref, wk_ref, wv_ref,
     z_out, xbc_out, dt_out, q_out, k_out, v_out, ckv_out, kr_out) = refs[n_x:]
    x = _row_tile(x_refs, n_ctx_tiles)
    h = (x * (1.0 + mod_ref[1:2, :]) + mod_ref[0:1, :]).astype(BF16)
    segs, acc = [], 0
    for wd in _IN0_SEGS:
        segs.append(_dot(h, w_ref[:, acc:acc + wd]))
        acc += wd
    z, xbc, dt, qlat, kvlat, kr = segs
    z_out[...] = _silu(z).astype(z_out.dtype)
    xbc_out[...] = xbc
    dt_out[...] = dt
    kr_out[...] = kr[:, :LANES]
    _mla_project(qlat, kvlat, kr, cos_ref[...], sin_ref[...], (gq_ref, wq_ref, wqr_ref), gkv_ref,
                 wk_ref, wv_ref, q_out, k_out, v_out, ckv_out)


def _in0(x_parts, mods, w, tables, mla_w, *, tm, n_ctx, lat_len):
    nt = sum(p.shape[0] for p in x_parts)
    nct = n_ctx // tm
    lat_tiles = lat_len // tm
    gq, wq, wqr, gkv, wk, wv = mla_w
    cmap = lambda i: (_cond_index(i * tm, n_ctx, lat_len), 0, 0)
    tmap = lambda i: (jnp.where(i < nct, 0, 1 + (i - nct) % lat_tiles), 0)
    row = lambda cols: pl.BlockSpec((tm, cols), lambda i: (i, 0))
    heads = pl.BlockSpec((MLA_HEADS, tm, LANES), lambda i: (0, i, 0))
    consts = [w, gq, wq, wqr, gkv, wk, wv]
    return pl.pallas_call(
        functools.partial(_in0_kernel, n_x=len(x_parts), n_ctx_tiles=nct),
        grid=(nt // tm,),
        in_specs=(_row_specs(x_parts, tm) + [pl.BlockSpec((None, 8, D_MODEL), cmap), _resident(w.shape),
                                             pl.BlockSpec((tm, LANES), tmap), pl.BlockSpec((tm, LANES), tmap)]
                  + [_resident(a.shape) for a in consts[1:]]),
        out_specs=[row(SSD_INNER), row(SSD_XBC), row(LANES), heads, heads, row(MLA_OUT),
                   row(MLA_KV_RANK), row(LANES)],
        out_shape=[jax.ShapeDtypeStruct((nt, SSD_INNER), BF16),
                   jax.ShapeDtypeStruct((nt, SSD_XBC), F32),
                   jax.ShapeDtypeStruct((nt, LANES), F32),
                   jax.ShapeDtypeStruct((MLA_HEADS, nt, LANES), BF16),
                   jax.ShapeDtypeStruct((MLA_HEADS, nt, LANES), BF16),
                   jax.ShapeDtypeStruct((nt, MLA_OUT), BF16),
                   jax.ShapeDtypeStruct((nt, MLA_KV_RANK), F32),
                   jax.ShapeDtypeStruct((nt, LANES), F32)],
        compiler_params=_cparams("parallel"),
        name="in0",
    )(*x_parts, mods, w, tables[0], tables[1], gq, wq, wqr, gkv, wk, wv)


def _attn_kernel(*refs, seq_len, cache_len, kblk, n_pairs):
    if cache_len:
        q_ref, k_ref, v_ref, kc_ref, vc_ref, o_ref = refs
    else:
        q_ref, k_ref, v_ref, o_ref = refs
    tq = q_ref.shape[1]
    blocks = [(k_ref, v_ref, i * kblk) for i in range(seq_len // kblk)]
    if cache_len:
        blocks += [(kc_ref, vc_ref, i * kblk) for i in range(cache_len // kblk)]
    vlane = lax.broadcasted_iota(I32, (kblk, LANES), 1)
    lane = lax.broadcasted_iota(I32, (tq, LANES), 1)
    for pp in range(n_pairs):
        ps = slice(pp * LANES, (pp + 1) * LANES)
        outs = []
        for hh in range(2):
            own = (vlane < MLA_V) if hh == 0 else (vlane >= MLA_V)
            q = q_ref[2 * pp + hh]
            m = None
            for kr, vr, off in blocks:
                s = lax.dot_general(q, kr[2 * pp + hh, off:off + kblk, :], _NT, preferred_element_type=F32)
                v_aug = jnp.where(own, vr[off:off + kblk, ps], jnp.ones((), BF16))
                bmax = jnp.max(s, -1, keepdims=True)
                if m is None:
                    m = bmax
                    acc = _dot(jnp.exp2(s - m).astype(BF16), v_aug)
                else:
                    m_new = jnp.maximum(m, bmax)
                    acc = jnp.exp2(m - m_new) * acc + _dot(jnp.exp2(s - m_new).astype(BF16), v_aug)
                    m = m_new
            outs.append(acc / pltpu.roll(acc, MLA_V, axis=1))
        o_ref[:, ps] = jnp.where(lane < MLA_V, outs[0], outs[1]).astype(o_ref.dtype)


def _attention(q, k, v, kc, vc, *, row0, n_seq, seq_len, cache_len, tq, n_pairs, name):
    nq = seq_len // tq
    kblk = min(512, seq_len)
    n = n_seq * seq_len
    q0, s0 = row0 // tq, row0 // seq_len
    hb, vw = 2 * n_pairs, n_pairs * LANES
    ins = [q, k, v]
    specs = [pl.BlockSpec((hb, tq, LANES), lambda b, hp, qi: (hp, q0 + b * nq + qi, 0)),
             pl.BlockSpec((hb, seq_len, LANES), lambda b, hp, qi: (hp, s0 + b, 0)),
             pl.BlockSpec((seq_len, vw), lambda b, hp, qi: (s0 + b, hp))]
    if cache_len:
        ins += [kc, vc]
        specs += [pl.BlockSpec((hb, cache_len, LANES), lambda b, hp, qi: (hp, b, 0)),
                  pl.BlockSpec((cache_len, vw), lambda b, hp, qi: (b, hp))]
    return pl.pallas_call(
        functools.partial(_attn_kernel, seq_len=seq_len, cache_len=cache_len, kblk=kblk, n_pairs=n_pairs),
        grid=(n_seq, MLA_HEADS // hb, nq),
        in_specs=specs,
        out_specs=pl.BlockSpec((tq, vw), lambda b, hp, qi: (b * nq + qi, hp)),
        out_shape=jax.ShapeDtypeStruct((n, MLA_OUT), BF16),
        compiler_params=_cparams("parallel", "parallel", "arbitrary"),
        name=name,
    )(*ins)


def _outproj_kernel(*refs, n_parts, n_ctx_tiles, gate_row, with_router):
    it = iter(refs)
    acc = None
    for n in n_parts[:-1]:
        a = _row_tile([next(it) for _ in range(n)], n_ctx_tiles)
        part = _dot(a, next(it)[...])
        acc = part if acc is None else acc + part
    x = _row_tile([next(it) for _ in range(n_parts[-1])], n_ctx_tiles)
    mod_ref, g_ref, b_ref = next(it), next(it), next(it)
    router_ref = next(it) if with_router else None
    o_ref = next(it)
    y = ALPHA * x + mod_ref[gate_row:gate_row + 1, :] * acc
    res = _layer_norm(y, g_ref[...], b_ref[...])
    o_ref[...] = res
    if with_router:
        _route(res, mod_ref, router_ref, *it)


def _outproj(acts, ws, x_parts, mods, g, b, *, gate_row, tm, n_ctx, lat_len, name, router_w=None):
    nt = sum(p.shape[0] for p in x_parts)
    cmap = lambda i: (_cond_index(i * tm, n_ctx, lat_len), 0, 0)
    ins, specs = [], []
    for parts, w in zip(acts, ws):
        ins += list(parts) + [w]
        specs += _row_specs(parts, tm) + [_resident(w.shape)]
    ins += list(x_parts) + [mods, g, b]
    specs += _row_specs(x_parts, tm) + [pl.BlockSpec((None, 8, D_MODEL), cmap),
                                        _resident(g.shape), _resident(b.shape)]
    out_specs = [pl.BlockSpec((tm, D_MODEL), lambda i: (i, 0))]
    out_shape = [jax.ShapeDtypeStruct((nt, D_MODEL), F32)]
    if router_w is not None:
        assert tm == MOE_TOK_CHUNK
        ins.append(router_w)
        specs.append(_resident(router_w.shape))
        out_specs += [pl.BlockSpec((tm, D_MODEL), lambda i: (i, 0)),
                      pl.BlockSpec((tm, LANES), lambda i: (i, 0)),
                      pl.BlockSpec((None, 8, tm), lambda i: (i, 0, 0)),
                      pl.BlockSpec((None, 8, LANES), lambda i: (i, 0, 0))]
        out_shape += [jax.ShapeDtypeStruct((nt, D_MODEL), BF16),
                      jax.ShapeDtypeStruct((nt, LANES), F32),
                      jax.ShapeDtypeStruct((nt // tm, 8, tm), F32),
                      jax.ShapeDtypeStruct((nt // tm, 8, LANES), F32)]
    n_parts = tuple(len(p) for p in acts) + (len(x_parts),)
    outs = pl.pallas_call(
        functools.partial(_outproj_kernel, n_parts=n_parts, n_ctx_tiles=n_ctx // tm, gate_row=gate_row,
                          with_router=router_w is not None),
        grid=(nt // tm,),
        in_specs=specs,
        out_specs=out_specs,
        out_shape=out_shape,
        compiler_params=_cparams("parallel"),
        name=name,
    )(*ins)
    return outs[0] if router_w is None else outs


def _ffn_kernel(x_ref, mod_ref, wg_ref, wu_ref, wd_ref, g_ref, b_ref, o_ref, *, ff_chunks):
    x = x_ref[...]
    h = (x * (1.0 + mod_ref[4:5, :]) + mod_ref[3:4, :]).astype(BF16)
    acc = None
    for a, b in ff_chunks:
        gt = _dot(h, wg_ref[:, a:b])
        up = _dot(h, wu_ref[:, a:b])
        act = (_silu(gt) * up).astype(BF16)
        part = _dot(act, wd_ref[a:b, :])
        acc = part if acc is None else acc + part
    y = ALPHA * x + mod_ref[5:6, :] * acc
    o_ref[...] = _layer_norm(y, g_ref[...], b_ref[...])


def _ffn(x, mods, wg, wu, wd, g, b, *, tm, n_ctx, lat_len):
    nt = x.shape[0]
    ff = wg.shape[1]
    chunks, a = [], 0
    while a < ff:
        chunks.append((a, min(a + 512, ff)))
        a += 512
    cmap = lambda i: (_cond_index(i * tm, n_ctx, lat_len), 0, 0)
    return pl.pallas_call(
        functools.partial(_ffn_kernel, ff_chunks=tuple(chunks)),
        grid=(nt // tm,),
        in_specs=[pl.BlockSpec((tm, D_MODEL), lambda i: (i, 0)),
                  pl.BlockSpec((None, 8, D_MODEL), cmap),
                  _resident(wg.shape), _resident(wu.shape), _resident(wd.shape),
                  _resident(g.shape), _resident(b.shape)],
        out_specs=pl.BlockSpec((tm, D_MODEL), lambda i: (i, 0)),
        out_shape=jax.ShapeDtypeStruct((nt, D_MODEL), F32),
        compiler_params=_cparams("parallel"),
        name="ffn",
    )(x, mods, wg, wu, wd, g, b)


def _ret_kernel(*refs, seq_len, do_rope, n_heads):
    if do_rope:
        (q_ref, k_ref, v_ref, g_ref, cos_ref, sin_ref, dec_ref, sf0_ref, sb0_ref,
         y_ref, sf_ref, sb_ref, yacc, q_s, kb_s, dcomb_s, ev_s, wk_s, cd_s) = refs
    else:
        (q_ref, k_ref, v_ref, g_ref, dec_ref, sf0_ref, sb0_ref,
         y_ref, sf_ref, sb_ref, yacc, q_s, kb_s, dcomb_s, ev_s, wk_s, cd_s) = refs
    rc = RET_CHUNK
    nc = seq_len // rc
    unroll = min(2, nc)

    @pl.when(pl.program_id(1) == 0)
    def _():
        ri = lax.broadcasted_iota(I32, (rc, rc), 0)
        ci = lax.broadcasted_iota(I32, (rc, rc), 1)
        dij = (ri - ci).astype(F32)
        pos_k = lax.broadcasted_iota(I32, (rc, RET_QK), 0).astype(F32)
        pos_v = lax.broadcasted_iota(I32, (rc, RET_V), 0).astype(F32)
        for hh in range(n_heads):
            la_f = -jnp.exp(dec_ref[hh, 0:1, :])
            la_b = -jnp.exp(dec_ref[hh, 1:2, :])
            dcomb_s[hh] = (jnp.exp(jnp.where(ri >= ci, dij * la_f[:, :rc], NEG_BIG)) +
                           jnp.exp(jnp.where(ri <= ci, -dij * la_b[:, :rc], NEG_BIG)))
            ev_s[hh, 0] = jnp.exp((pos_v + 1.0) * la_f)
            ev_s[hh, 1] = jnp.exp((rc - pos_v) * la_b)
            wk_s[hh, 0] = jnp.exp((rc - 1.0 - pos_k) * la_f[:, :RET_QK])
            wk_s[hh, 1] = jnp.exp(pos_k * la_b[:, :RET_QK])
            cd_s[hh, 0:1, :] = jnp.exp(rc * la_f)
            cd_s[hh, 1:2, :] = jnp.exp(rc * la_b)

    sf_ref[...] = sf0_ref[...]
    sb_ref[...] = sb0_ref[...]

    def fwd(c, carry):
        rows = pl.ds(pl.multiple_of(c * rc, rc), rc)
        if do_rope:
            cs, sn = cos_ref[rows, :], sin_ref[rows, :]
        for hh in range(n_heads):
            qs = slice(hh * RET_QK, (hh + 1) * RET_QK)
            vs = slice(hh * RET_V, (hh + 1) * RET_V)
            q = q_ref[rows, qs]
            k = k_ref[rows, qs] * (RET_QK ** -0.5)
            if do_rope:
                q = q * cs + pltpu.roll(q, RET_QK // 2, axis=1) * sn
                k = k * cs + pltpu.roll(k, RET_QK // 2, axis=1) * sn
            q_bf = q.astype(BF16)
            v = v_ref[rows, vs]
            s = lax.dot_general(q_bf, k.astype(BF16), _NT, preferred_element_type=F32)
            y = _dot((s * dcomb_s[hh]).astype(BF16), v)
            y = y + _dot(q_bf, sf_ref[hh].astype(BF16)) * ev_s[hh, 0]
            yacc[rows, vs] = y
            upd = lax.dot_general((k * wk_s[hh, 0]).astype(BF16), v, _TN, preferred_element_type=F32)
            sf_ref[hh] = cd_s[hh, 0:1, :] * sf_ref[hh] + upd
            q_s[rows, qs] = q_bf
            kb_s[rows, qs] = (k * wk_s[hh, 1]).astype(BF16)
        return carry

    lax.fori_loop(0, nc, fwd, 0, unroll=unroll)

    def bwd(i, carry):
        rows = pl.ds(pl.multiple_of((nc - 1 - i) * rc, rc), rc)
        for hh in range(n_heads):
            qs = slice(hh * RET_QK, (hh + 1) * RET_QK)
            vs = slice(hh * RET_V, (hh + 1) * RET_V)
            v = v_ref[rows, vs]
            y = yacc[rows, vs] + _dot(q_s[rows, qs], sb_ref[hh].astype(BF16)) * ev_s[hh, 1]
            upd = lax.dot_general(kb_s[rows, qs], v, _TN, preferred_element_type=F32)
            sb_ref[hh] = cd_s[hh, 1:2, :] * sb_ref[hh] + upd
            mu = jnp.mean(y, -1, keepdims=True)
            d = y - mu
            var = jnp.mean(d * d, -1, keepdims=True)
            yn = d * lax.rsqrt(var + LN_EPS)
            y_ref[rows, vs] = (yn * g_ref[rows, vs].astype(F32)).astype(y_ref.dtype)
        return carry

    lax.fori_loop(0, nc, bwd, 0, unroll=unroll)


def _retention(q, k, v, g, tables, dec, sf0, sb0, *, row0, n_seq, seq_len, hps, name):
    do_rope = tables is not None
    blk0 = row0 // seq_len
    qmap = lambda h, b: (b + blk0, h)
    smap = lambda h, b: (b, h, 0, 0)
    ins = [q, k, v, g]
    specs = [pl.BlockSpec((seq_len, hps * RET_QK), qmap), pl.BlockSpec((seq_len, hps * RET_QK), qmap),
             pl.BlockSpec((seq_len, hps * RET_V), qmap), pl.BlockSpec((seq_len, hps * RET_V), qmap)]
    if do_rope:
        ins += list(tables)
        specs += [pl.BlockSpec((seq_len, RET_QK), lambda h, b: (0, 0))] * 2
    ins += [dec, sf0, sb0]
    specs += [pl.BlockSpec((hps, 8, RET_V), lambda h, b: (h, 0, 0)),
              pl.BlockSpec((None, hps, RET_QK, RET_V), smap),
              pl.BlockSpec((None, hps, RET_QK, RET_V), smap)]
    st = jax.ShapeDtypeStruct((n_seq, RET_HEADS, RET_QK, RET_V), F32)
    rc = RET_CHUNK
    return pl.pallas_call(
        functools.partial(_ret_kernel, seq_len=seq_len, do_rope=do_rope, n_heads=hps),
        grid=(RET_HEADS // hps, n_seq),
        in_specs=specs,
        out_specs=[pl.BlockSpec((seq_len, hps * RET_V), lambda h, b: (b, h)),
                   pl.BlockSpec((None, hps, RET_QK, RET_V), smap),
                   pl.BlockSpec((None, hps, RET_QK, RET_V), smap)],
        out_shape=[jax.ShapeDtypeStruct((n_seq * seq_len, MIX1), BF16), st, st],
        scratch_shapes=[pltpu.VMEM((seq_len, hps * RET_V), F32),
                        pltpu.VMEM((seq_len, hps * RET_QK), BF16),
                        pltpu.VMEM((seq_len, hps * RET_QK), BF16),
                        pltpu.VMEM((hps, rc, rc), F32),
                        pltpu.VMEM((hps, 2, rc, RET_V), F32),
                        pltpu.VMEM((hps, 2, rc, RET_QK), F32),
                        pltpu.VMEM((hps, 8, RET_V), F32)],
        compiler_params=_cparams("parallel", "arbitrary"),
        name=name,
    )(*ins)


def _route(x, mod_ref, w_ref, hb_ref, tok_ref, lpt_ref, cap_ref):
    tm = x.shape[0]
    h = x * (1.0 + mod_ref[4:5, :]) + mod_ref[3:4, :]
    h_hi = h.astype(BF16)
    hb_ref[...] = h_hi
    h_lo = (h - h_hi.astype(F32)).astype(BF16)
    w = w_ref[...]
    w_hi = w.astype(BF16)
    w_lo = (w - w_hi.astype(F32)).astype(BF16)
    w_both = w_hi + pltpu.roll(w_lo.astype(F32), N_EXPERTS, axis=1).astype(BF16)
    part = _dot(h_hi, w_both) + _dot(h_lo, w_both)
    logits = part + pltpu.roll(part, LANES - N_EXPERTS, axis=1)
    lane = lax.broadcasted_iota(I32, (tm, LANES), 1)
    logits = jnp.where(lane < N_EXPERTS, logits, NEG_BIG)
    m1 = jnp.max(logits, -1, keepdims=True)
    i1 = jnp.min(jnp.where(logits == m1, lane, LANES), -1, keepdims=True)
    rest = jnp.where(lane == i1, NEG_BIG, logits)
    m2 = jnp.max(rest, -1, keepdims=True)
    i2 = jnp.min(jnp.where(rest == m2, lane, LANES), -1, keepdims=True)
    e = jnp.exp(m2 - m1)
    g1 = 1.0 / (1.0 + e)
    g2 = e / (1.0 + e)
    sel1 = lane == i1
    sel2 = lane == i2
    onehot = jnp.where(sel1 | sel2, 1.0, 0.0)
    ri = lax.broadcasted_iota(I32, (tm, tm), 0)
    ci = lax.broadcasted_iota(I32, (tm, tm), 1)
    strict = jnp.where(ri > ci, 1.0, 0.0).astype(BF16)
    prefix = _dot(strict, onehot.astype(BF16))
    n_row = jnp.sum(onehot, 0, keepdims=True)
    cap_row = jnp.floor((n_row + (MOE_CELL_ALIGN - 1.0)) * (1.0 / MOE_CELL_ALIGN)) * MOE_CELL_ALIGN
    li = lax.broadcasted_iota(I32, (LANES, LANES), 0)
    lj = lax.broadcasted_iota(I32, (LANES, LANES), 1)
    lower_lanes = jnp.where(li < lj, 1.0, 0.0).astype(BF16)
    cap8 = jnp.broadcast_to(cap_row, (8, LANES))
    base_row = _dot(cap8.astype(BF16), lower_lanes)[0:1, :]
    local = prefix + base_row
    lpos1 = jnp.sum(jnp.where(sel1, local, 0.0), -1, keepdims=True)
    lpos2 = jnp.sum(jnp.where(sel2, local, 0.0), -1, keepdims=True)
    tok = jnp.where(lane == 0, g1,
          jnp.where(lane == 1, g2,
          jnp.where(lane == 2, lpos1,
          jnp.where(lane == 3, lpos2, 0.0))))
    tok_ref[...] = tok
    lpt_ref[...] = tok.T[0:8, :]
    cap_ref[...] = cap8


def _cell_copies(tabs, chunk, hbm_ref, buf_ref, slot, sem, *, to_local, wait):
    g_ref, a_ref, nbig_ref, nsmall_ref = tabs
    for e in range(N_EXPERTS):
        k = chunk * N_EXPERTS + e
        g0, a0, nbig, nsmall = g_ref[k], a_ref[k], nbig_ref[k], nsmall_ref[k]

        def piece(i, carry, rows, goff, aoff):
            g = pl.multiple_of(goff + i * rows, MOE_CELL_ALIGN)
            a = pl.multiple_of(aoff + i * rows, MOE_CELL_ALIGN)
            far = hbm_ref.at[pl.ds(g, rows), :]
            near = buf_ref.at[slot, pl.ds(a, rows), :]
            cp = (pltpu.make_async_copy(far, near, sem.at[slot]) if to_local
                  else pltpu.make_async_copy(near, far, sem.at[slot]))
            if wait:
                cp.wait()
            else:
                cp.start()
            return carry

        lax.fori_loop(0, nbig, functools.partial(piece, rows=MOE_BIG_PIECE, goff=g0, aoff=a0), 0)
        done = nbig * MOE_BIG_PIECE
        lax.fori_loop(0, nsmall, functools.partial(piece, rows=MOE_CELL_ALIGN, goff=g0 + done, aoff=a0 + done), 0)


def _dispatch_kernel(g_ref, a_ref, nbig_ref, nsmall_ref, hb_ref, lpt_ref, xs_hbm, ybuf, sem, *, n_fill):
    c = pl.program_id(0)
    nc = pl.num_programs(0)
    slot = c % 2
    tabs = (g_ref, a_ref, nbig_ref, nsmall_ref)

    @pl.when(c == 0)
    def _():
        ybuf[2] = jnp.zeros(ybuf.shape[1:], ybuf.dtype)
        for j in range(n_fill):
            _cell_copies(tabs, nc + j, xs_hbm, ybuf, 2, sem, to_local=False, wait=False)

    @pl.when(c == nc - 1)
    def _():
        for j in range(n_fill):
            _cell_copies(tabs, nc + j, xs_hbm, ybuf, 2, sem, to_local=False, wait=True)

    l1 = lpt_ref[2:3, :]
    l2 = lpt_ref[3:4, :]
    rb = MOE_LOCAL_ROWS // 3
    for r in range(3):
        rid = (lax.broadcasted_iota(I32, (rb, MOE_TOK_CHUNK), 0) + r * rb).astype(F32)
        onehot = jnp.where((l1 == rid) | (l2 == rid), 1.0, 0.0).astype(BF16)
        ybuf[slot, r * rb:(r + 1) * rb, :] = _dot(onehot, hb_ref[...]).astype(BF16)

    @pl.when(c > 0)
    def _():
        _cell_copies(tabs, c - 1, xs_hbm, ybuf, 1 - slot, sem, to_local=False, wait=True)

    _cell_copies(tabs, c, xs_hbm, ybuf, slot, sem, to_local=False, wait=False)

    @pl.when(c == nc - 1)
    def _():
        _cell_copies(tabs, c, xs_hbm, ybuf, slot, sem, to_local=False, wait=True)


def _dispatch(tabs, hb, lpt, *, n_slots, n_fill):
    tm = MOE_TOK_CHUNK
    grid_spec = pltpu.PrefetchScalarGridSpec(
        num_scalar_prefetch=4,
        grid=(hb.shape[0] // tm,),
        in_specs=[pl.BlockSpec((tm, D_MODEL), lambda c, *_: (c, 0)),
                  pl.BlockSpec((None, 8, tm), lambda c, *_: (c, 0, 0))],
        out_specs=pl.BlockSpec(memory_space=pl.ANY),
        scratch_shapes=[pltpu.VMEM((3, MOE_LOCAL_ROWS, D_MODEL), BF16),
                        pltpu.SemaphoreType.DMA((3,))],
    )
    return pl.pallas_call(
        functools.partial(_dispatch_kernel, n_fill=n_fill),
        grid_spec=grid_spec,
        out_shape=jax.ShapeDtypeStruct((n_slots, D_MODEL), BF16),
        compiler_params=_cparams("arbitrary"),
        name="moe_dispatch",
    )(*tabs, hb, lpt)


def _expert_kernel(te_ref, nu_ref, x_ref, wg_ref, wu_ref, wd_ref, o_ref, acc):
    t = pl.program_id(0)
    f = pl.program_id(1)
    nf = pl.num_programs(1)

    @pl.when(t < nu_ref[0])
    def _():
        x = x_ref[...]
        gt = _dot(x, wg_ref[...])
        up = _dot(x, wu_ref[...])
        part = _dot((_silu(gt) * up).astype(BF16), wd_ref[...])

        @pl.when(f == 0)
        def _():
            acc[...] = part

        @pl.when((f > 0) & (f < nf - 1))
        def _():
            acc[...] += part

        @pl.when(f == nf - 1)
        def _():
            o_ref[...] = (acc[...] + part).astype(o_ref.dtype)

    @pl.when((t >= nu_ref[0]) & (f == nf - 1))
    def _():
        o_ref[...] = jnp.zeros_like(o_ref)


def _experts(te, nu, xs, wg, wu, wd):
    bs = MOE_SLOT_TILE
    n_slots = xs.shape[0]
    nf = D_FF_EXPERT // MOE_FF_CHUNK

    def tt(t, nu):
        return jnp.minimum(t, nu[0] - 1)

    def ff(t, f, nu):
        return jnp.where(t < nu[0], f, nf - 1)

    grid_spec = pltpu.PrefetchScalarGridSpec(
        num_scalar_prefetch=2,
        grid=(n_slots // bs, nf),
        in_specs=[pl.BlockSpec((bs, D_MODEL), lambda t, f, te, nu: (tt(t, nu), 0)),
                  pl.BlockSpec((None, D_MODEL, MOE_FF_CHUNK), lambda t, f, te, nu: (te[tt(t, nu)], 0, ff(t, f, nu))),
                  pl.BlockSpec((None, D_MODEL, MOE_FF_CHUNK), lambda t, f, te, nu: (te[tt(t, nu)], 0, ff(t, f, nu))),
                  pl.BlockSpec((None, MOE_FF_CHUNK, D_MODEL), lambda t, f, te, nu: (te[tt(t, nu)], ff(t, f, nu), 0))],
        out_specs=pl.BlockSpec((bs, D_MODEL), lambda t, f, te, nu: (t, 0)),
        scratch_shapes=[pltpu.VMEM((bs, D_MODEL), F32)],
    )
    return pl.pallas_call(
        _expert_kernel,
        grid_spec=grid_spec,
        out_shape=jax.ShapeDtypeStruct((n_slots, D_MODEL), BF16),
        compiler_params=_cparams("arbitrary", "arbitrary"),
        name="moe_experts",
    )(te, nu, xs, wg, wu, wd)


def _combine_kernel(g_ref, a_ref, nbig_ref, nsmall_ref, tok_ref, x_ref, mod_ref, lg_ref, lb_ref, ys_hbm,
                    oc_ref, ol_ref, ybuf, sem, *, n_ctx_tiles):
    c = pl.program_id(0)
    nc = pl.num_programs(0)
    slot = c % 2
    tabs = (g_ref, a_ref, nbig_ref, nsmall_ref)
    tm = tok_ref.shape[0]

    @pl.when(c == 0)
    def _():
        ybuf[...] = jnp.zeros_like(ybuf)
        _cell_copies(tabs, 0, ys_hbm, ybuf, 0, sem, to_local=True, wait=False)

    @pl.when(c + 1 < nc)
    def _():
        _cell_copies(tabs, c + 1, ys_hbm, ybuf, 1 - slot, sem, to_local=True, wait=False)

    _cell_copies(tabs, c, ys_hbm, ybuf, slot, sem, to_local=True, wait=True)

    tok = tok_ref[...]
    col = lax.broadcasted_iota(I32, (tm, MOE_LOCAL_ROWS), 1).astype(F32)
    rows = ybuf[slot]
    pick1 = jnp.where(tok[:, 2:3] == col, 1.0, 0.0).astype(BF16)
    pick2 = jnp.where(tok[:, 3:4] == col, 1.0, 0.0).astype(BF16)
    f = tok[:, 0:1] * _dot(pick1, rows) + tok[:, 1:2] * _dot(pick2, rows)
    y = ALPHA * x_ref[...] + mod_ref[5:6, :] * f
    res = _layer_norm(y, lg_ref[...], lb_ref[...])

    @pl.when(c < n_ctx_tiles)
    def _():
        oc_ref[...] = res

    @pl.when(c >= n_ctx_tiles)
    def _():
        ol_ref[...] = res


def _combine(tabs, tok, ys, x, mods, g, b, *, n_ctx, lat_len):
    nt = x.shape[0]
    tm = MOE_TOK_CHUNK
    nct = n_ctx // tm
    cmap = lambda c, *_: (_cond_index(c * tm, n_ctx, lat_len), 0, 0)
    grid_spec = pltpu.PrefetchScalarGridSpec(
        num_scalar_prefetch=4,
        grid=(nt // tm,),
        in_specs=[pl.BlockSpec((tm, LANES), lambda c, *_: (c, 0)),
                  pl.BlockSpec((tm, D_MODEL), lambda c, *_: (c, 0)),
                  pl.BlockSpec((None, 8, D_MODEL), cmap),
                  pl.BlockSpec((1, D_MODEL), lambda c, *_: (0, 0)),
                  pl.BlockSpec((1, D_MODEL), lambda c, *_: (0, 0)),
                  pl.BlockSpec(memory_space=pl.ANY)],
        out_specs=[pl.BlockSpec((tm, D_MODEL), lambda c, *_: (jnp.minimum(c, nct - 1), 0)),
                   pl.BlockSpec((tm, D_MODEL), lambda c, *_: (jnp.maximum(c - nct, 0), 0))],
        scratch_shapes=[pltpu.VMEM((2, MOE_LOCAL_ROWS, D_MODEL), BF16),
                        pltpu.SemaphoreType.DMA((2,))],
    )
    return pl.pallas_call(
        functools.partial(_combine_kernel, n_ctx_tiles=nct),
        grid_spec=grid_spec,
        out_shape=[jax.ShapeDtypeStruct((n_ctx, D_MODEL), F32),
                   jax.ShapeDtypeStruct((nt - n_ctx, D_MODEL), F32)],
        compiler_params=_cparams("arbitrary"),
        name="moe_combine",
    )(*tabs, tok, x, mods, g, b, ys)


def _moe(x, routed, mods, wg, wu, wd, ln_g, ln_b, *, n_ctx, lat_len):
    nt = x.shape[0]
    bs, tc = MOE_SLOT_TILE, MOE_TOK_CHUNK
    nchunk = nt // tc
    max_rows = 2 * nt + nchunk * N_EXPERTS * (MOE_CELL_ALIGN - 1)
    n_tiles = -(-max_rows // bs) + N_EXPERTS
    n_slots = n_tiles * bs

    hb, tok, lpt, capt = routed

    cap = capt[:, 0, :N_EXPERTS].astype(I32)
    a_loc = jnp.cumsum(cap, 1) - cap
    tot = jnp.sum(cap, 0)
    padded = ((tot + bs - 1) // bs) * bs
    gend = jnp.cumsum(padded)
    g_glob = (gend - padded)[None, :] + (jnp.cumsum(cap, 0) - cap)
    fill_cell = (MOE_LOCAL_ROWS // MOE_BIG_PIECE) * MOE_BIG_PIECE
    n_tail_cells = -(-(n_slots - 2 * nt) // fill_cell)
    n_fill = 1 + -(-n_tail_cells // N_EXPERTS)
    tail_k = jnp.arange((n_fill - 1) * N_EXPERTS, dtype=I32) * fill_cell
    tail_rows = jnp.clip(n_slots - gend[-1] - tail_k, 0, fill_cell)
    g_all = jnp.concatenate([g_glob.reshape(-1), gend - padded + tot, gend[-1] + tail_k])
    a_all = jnp.concatenate([a_loc.reshape(-1), jnp.zeros((n_fill * N_EXPERTS,), I32)])
    rows_all = jnp.concatenate([cap.reshape(-1), padded - tot, tail_rows])
    n_big = rows_all // MOE_BIG_PIECE
    n_small = (rows_all - n_big * MOE_BIG_PIECE) // MOE_CELL_ALIGN
    tabs = tuple(t.astype(I32) for t in (g_all, a_all, n_big, n_small))
    tile_start = jnp.arange(n_tiles, dtype=I32) * bs
    tile_expert = jnp.minimum(jnp.sum((gend[None, :] <= tile_start[:, None]).astype(I32), 1), N_EXPERTS - 1)
    n_used = (gend[-1] // bs).astype(I32).reshape(1)

    xs = _dispatch(tabs, hb, lpt, n_slots=n_slots, n_fill=n_fill)
    ys = _experts(tile_expert, n_used, xs, wg, wu, wd)
    return _combine(tabs, tok, ys, x, mods, ln_g, ln_b, n_ctx=n_ctx, lat_len=lat_len)


def _axial_angles(n_tok, dim):
    rows = n_tok // GRID_W
    row = jnp.repeat(jnp.arange(rows), GRID_W).astype(F32)
    col = jnp.tile(jnp.arange(GRID_W), rows).astype(F32)
    axis_dim = dim // 2
    inv = 1.0 / (ROPE_BASE ** (jnp.arange(0, axis_dim, 2, dtype=F32) / axis_dim))
    ang = jnp.concatenate([row[:, None] * inv, col[:, None] * inv], -1)
    return jnp.cos(ang), jnp.sin(ang)


def _mla_tables(n_lat):
    cos, sin = _axial_angles(n_lat, MLA_ROPE)
    one = jnp.ones((n_lat, MLA_NOPE), F32)
    zero = jnp.zeros((n_lat, MLA_NOPE), F32)
    pad1 = jnp.ones((n_lat, LANES - MLA_NOPE - MLA_ROPE), F32)
    pad0 = jnp.zeros((n_lat, LANES - MLA_NOPE - MLA_ROPE), F32)
    return (jnp.concatenate([one, cos, cos, pad1], -1), jnp.concatenate([zero, sin, sin, pad0], -1))


def _ret_tables(n_lat):
    cos, sin = _axial_angles(n_lat, RET_QK)
    return jnp.concatenate([cos, cos], -1), jnp.concatenate([-sin, sin], -1)


def _rot_cols(w):
    half = w.shape[1] // 2
    return jnp.concatenate([-w[:, half:], w[:, :half]], 1)


def _in0_weights(w_in0):
    z, xbc, dt, ql, kvl, kr = jnp.split(
        w_in0, [SSD_INNER, SSD_INNER + SSD_XBC, SSD_INNER + SSD_XBC + 2 * SSD_HEADS,
                SSD_INNER + SSD_XBC + 2 * SSD_HEADS + MLA_Q_RANK,
                SSD_INNER + SSD_XBC + 2 * SSD_HEADS + MLA_Q_RANK + MLA_KV_RANK], axis=1)
    dtp = jnp.pad(dt, ((0, 0), (0, LANES - 2 * SSD_HEADS)))
    lpad = ((0, 0), (MLA_NOPE, LANES - MLA_NOPE - MLA_ROPE))
    krp = jnp.concatenate([jnp.pad(kr, lpad), jnp.pad(_rot_cols(kr), lpad)], 1)
    return jnp.concatenate([z, xbc, dtp, ql, kvl, krp], 1).astype(BF16)


def _mla_weights(w_q_up, w_kv_up):
    d = w_q_up.shape[0]
    wq = w_q_up.reshape(d, MLA_HEADS, MLA_NOPE + MLA_ROPE)
    nope, rope = wq[..., :MLA_NOPE], wq[..., MLA_NOPE:]
    half = MLA_ROPE // 2
    rot = jnp.concatenate([-rope[..., half:], rope[..., :half]], -1)
    tail = jnp.zeros((d, MLA_HEADS, LANES - MLA_NOPE - MLA_ROPE), F32)
    wq_pad = jnp.concatenate([nope, rope, tail], -1).reshape(d, MLA_HEADS * LANES)
    wq_rot = jnp.concatenate([jnp.zeros_like(nope), rot, tail], -1).reshape(d, MLA_HEADS * LANES)
    r = w_kv_up.shape[0]
    wkv = w_kv_up.reshape(r, MLA_HEADS, MLA_NOPE + MLA_V)
    wk = jnp.concatenate([wkv[..., :MLA_NOPE], jnp.zeros((r, MLA_HEADS, LANES - MLA_NOPE), F32)], -1)
    wv = wkv[..., MLA_NOPE:]
    return (wq_pad.astype(BF16), wq_rot.astype(BF16),
            wk.reshape(r, MLA_HEADS * LANES).astype(BF16), wv.reshape(r, MLA_OUT).astype(BF16))


def _lane_row(v, width):
    return jnp.pad(v, (0, width - v.shape[0])).reshape(1, width)


def kernel(x_prompt, x_sample, cache_mla_ckv, cache_mla_krope, state_ssd_f, state_ssd_b, state_ret_f, state_ret_b, c, c_ctx, ada_w, ada_b, ln1_g, ln1_b, ln2_g, ln2_b, w_in0, ssd_conv_w, ssd_conv_b, ssd_a_log_f, ssd_a_log_b, ssd_dt_bias_f, ssd_dt_bias_b, ssd_d, ssd_norm_g, mla_q_norm_g, mla_w_q_up, mla_kv_norm_g, mla_w_kv_up, w_out0, ffn_w_gate, ffn_w_up, ffn_w_down, w_in1, ret_decay_f, ret_decay_b, w_out1, moe_router, moe_w_gate, moe_w_up, moe_w_down):
    bc, lc, _ = x_prompt.shape
    bl, ll, _ = x_sample.shape
    past = cache_mla_ckv.shape[2]
    n_ctx, n_lat = bc * lc, bl * ll
    geo = dict(n_ctx=n_ctx, lat_len=ll)

    x_parts = [x_prompt.reshape(n_ctx, D_MODEL), x_sample.reshape(n_lat, D_MODEL)]
    n_cond = 1 + bl
    cond = jnp.concatenate([c_ctx[None, :], c, jnp.zeros((-n_cond % 8, D_MODEL), F32)], 0)
    mods = _ada_vectors(cond, ada_w, ada_b)

    tm0 = 512
    wq_pad, wq_rot, wk_pad, wv = _mla_weights(mla_w_q_up[0], mla_w_kv_up[0])
    mla_w = (mla_q_norm_g[0].reshape(1, MLA_Q_RANK), wq_pad, wq_rot,
             mla_kv_norm_g[0].reshape(1, MLA_KV_RANK), wk_pad, wv)
    cos_l, sin_l = _mla_tables(ll)
    tables = (jnp.concatenate([jnp.ones((tm0, LANES), F32), cos_l], 0),
              jnp.concatenate([jnp.zeros((tm0, LANES), F32), sin_l], 0))
    z, xbc, dt, q_all, k_all, v_all, ckv_all, kr_all = _in0(
        x_parts, mods[0], _in0_weights(w_in0[0]), tables, mla_w, tm=tm0, **geo)

    cw = jnp.pad(ssd_conv_w[0], ((0, 8 - ssd_conv_w.shape[1]), (0, 0)))
    ssd_params = (cw, ssd_conv_b[0].reshape(1, SSD_XBC),
                  _lane_row(jnp.concatenate([ssd_dt_bias_f[0], ssd_dt_bias_b[0]]), LANES),
                  _lane_row(jnp.concatenate([ssd_a_log_f[0], ssd_a_log_b[0]]), LANES),
                  jnp.repeat(ssd_d[0], SSD_HEAD_DIM).reshape(1, SSD_INNER),
                  ssd_norm_g[0].reshape(1, SSD_INNER))

    def st_in(s):
        return jnp.transpose(s, (0, 3, 1, 2)).reshape(s.shape[0], SSD_STATE, SSD_INNER)

    def st_out(s):
        return jnp.transpose(s.reshape(s.shape[0], SSD_STATE, SSD_HEADS, SSD_HEAD_DIM), (0, 2, 3, 1))

    zero_ssd = jnp.zeros((bc, SSD_STATE, SSD_INNER), F32)
    y_ssd_c, ssd_f, ssd_b = _ssd(z, xbc, dt, zero_ssd, zero_ssd, ssd_params,
                                 row0=0, n_seq=bc, seq_len=lc, name="ssd_ctx")
    y_ssd_l, _, _ = _ssd(z, xbc, dt, st_in(state_ssd_f[:, 0]), st_in(state_ssd_b[:, 0]), ssd_params,
                         row0=n_ctx, n_seq=bl, seq_len=ll, name="ssd_lat")

    lpad = ((0, 0), (MLA_NOPE, 2 * LANES - MLA_NOPE - MLA_ROPE))
    k_p, v_p = _mla_prep(None, cache_mla_ckv[:, 0].reshape(bl * past, MLA_KV_RANK),
                         jnp.pad(cache_mla_krope[:, 0].reshape(bl * past, MLA_ROPE), lpad),
                         None, mla_w, row0=0, n_rows=bl * past,
                         do_q=False, do_norm=False, tm=256, name="mla_prep_cache")
    o_c = _attention(q_all, k_all, v_all, None, None, row0=0, n_seq=bc, seq_len=lc, cache_len=0,
                     tq=lc, n_pairs=MLA_HEADS // 2, name="attn_ctx")
    o_l = _attention(q_all, k_all, v_all, k_p, v_p, row0=n_ctx, n_seq=bl, seq_len=ll, cache_len=past,
                     tq=min(512, ll), n_pairs=1, name="attn_lat")

    w_out0_bf = w_out0[0].astype(BF16)
    x = _outproj([[y_ssd_c, y_ssd_l], [o_c, o_l]],
                 [w_out0_bf[:SSD_INNER], w_out0_bf[SSD_INNER:]], x_parts, mods[0],
                 ln1_g[0].reshape(1, D_MODEL), ln1_b[0].reshape(1, D_MODEL),
                 gate_row=2, tm=512, name="out0", **geo)
    x = _ffn(x, mods[0], ffn_w_gate[0].astype(BF16), ffn_w_up[0].astype(BF16), ffn_w_down[0].astype(BF16),
             ln2_g[0].reshape(1, D_MODEL), ln2_b[0].reshape(1, D_MODEL), tm=512, **geo)

    hq = RET_HEADS * RET_QK
    q1, k1, v1, g1 = _inproj([x], mods[1], w_in1[0].astype(BF16), (hq, hq, MIX1, MIX1), (F32, F32, BF16, BF16),
                             (False, False, False, True), shift_row=0, tm=256, name="in1", **geo)
    dec = jnp.stack([ret_decay_f[0], ret_decay_b[0]], 1)
    dec = jnp.broadcast_to(jnp.pad(dec, ((0, 0), (0, 6)))[:, :, None], (RET_HEADS, 8, RET_V))
    zero_ret = jnp.zeros((bc, RET_HEADS, RET_QK, RET_V), F32)
    tr = lambda s: jnp.swapaxes(s, -1, -2)
    y_ret_c, ret_f, ret_b = _retention(q1, k1, v1, g1, None, dec, zero_ret, zero_ret,
                                       row0=0, n_seq=bc, seq_len=lc, hps=4, name="ret_ctx")
    y_ret_l, _, _ = _retention(q1, k1, v1, g1, _ret_tables(ll), dec, tr(state_ret_f[:, 0]), tr(state_ret_b[:, 0]),
                               row0=n_ctx, n_seq=bl, seq_len=ll, hps=2, name="ret_lat")
    router_w = jnp.pad(moe_router[0], ((0, 0), (0, LANES - N_EXPERTS)))
    x, *routed = _outproj([[y_ret_c, y_ret_l]], [w_out1[0].astype(BF16)], [x], mods[1],
                          ln1_g[1].reshape(1, D_MODEL), ln1_b[1].reshape(1, D_MODEL),
                          gate_row=2, tm=MOE_TOK_CHUNK, name="out1", router_w=router_w, **geo)
    y_c, y_l = _moe(x, routed, mods[1], moe_w_gate[0].astype(BF16), moe_w_up[0].astype(BF16),
                    moe_w_down[0].astype(BF16), ln2_g[1].reshape(1, D_MODEL), ln2_b[1].reshape(1, D_MODEL), **geo)

    y_prompt = y_c.reshape(bc, lc, D_MODEL)
    y_sample = y_l.reshape(bl, ll, D_MODEL)
    new_ckv = ckv_all[:n_ctx].reshape(bc, 1, lc, MLA_KV_RANK)
    new_krope = kr_all[:n_ctx, MLA_NOPE:MLA_NOPE + MLA_ROPE].reshape(bc, 1, lc, MLA_ROPE)
    return (y_prompt, y_sample, new_ckv, new_krope,
            st_out(ssd_f)[:, None], st_out(ssd_b)[:, None], tr(ret_f)[:, None], tr(ret_b)[:, None])
```

```python
import functools
import math

import jax
import jax.numpy as jnp
from jax import lax
from jax.experimental import pallas as pl
from jax.experimental.pallas import tpu as pltpu

F32 = jnp.float32
BF16 = jnp.bfloat16
I32 = jnp.int32

D_MODEL = 1024
DEPTH = 2
GRID_W = 64
CHUNK = 128
SSD_HEADS = 8
SSD_HEAD_DIM = 64
SSD_INNER = SSD_HEADS * SSD_HEAD_DIM
SSD_GROUPS = 2
SSD_STATE = 64
SSD_XBC = SSD_INNER + 2 * SSD_GROUPS * SSD_STATE
MLA_HEADS = 8
MLA_NOPE = 64
MLA_ROPE = 32
MLA_V = 64
MLA_Q_RANK = 384
MLA_KV_RANK = 256
MLA_OUT = MLA_HEADS * MLA_V
RET_HEADS = 8
RET_QK = 128
RET_V = 256
MIX1 = RET_HEADS * RET_V
D_FF = 2816
N_EXPERTS = 8
D_FF_EXPERT = 3584
ALPHA = (2 * DEPTH) ** 0.25
LN_EPS = 1e-5
RMS_EPS = 1e-6
ROPE_BASE = 10000.0

LANES = 128
VMEM_LIMIT = 56 * 1024 * 1024
NEG_BIG = -1e30

MOE_SLOT_TILE = 512
MOE_TOK_CHUNK = 512
MOE_FF_CHUNK = 1792
MOE_CELL_ALIGN = 16
MOE_BIG_PIECE = 64
MOE_LOCAL_ROWS = 2 * MOE_TOK_CHUNK + N_EXPERTS * MOE_CELL_ALIGN
RET_CHUNK = 256

_NT = (((1,), (1,)), ((), ()))
_TN = (((0,), (0,)), ((), ()))


def _cparams(*sem):
    return pltpu.CompilerParams(dimension_semantics=sem, vmem_limit_bytes=VMEM_LIMIT)


def _resident(shape):
    nd = len(shape)
    return pl.BlockSpec(shape, lambda *_: (0,) * nd, pipeline_mode=pl.Buffered(1))


def _silu(x):
    return x * jax.nn.sigmoid(x)


def _dot(a, b):
    return jnp.dot(a, b, preferred_element_type=F32)


def _cond_index(row, n_ctx, lat_len):
    return jnp.where(row < n_ctx, 0, 1 + (row - n_ctx) // lat_len)


def _layer_norm(y, g, b):
    mu = jnp.mean(y, -1, keepdims=True)
    d = y - mu
    var = jnp.mean(d * d, -1, keepdims=True)
    return d * lax.rsqrt(var + LN_EPS) * g + b


def _ada_kernel(c_ref, w_ref, b_ref, o_ref):
    s = _silu(c_ref[...])
    o_ref[...] = jnp.dot(s, w_ref[...], precision=lax.Precision.HIGHEST,
                         preferred_element_type=F32) + b_ref[...]


def _ada_vectors(cond, ada_w, ada_b):
    r = cond.shape[0]
    tn = 1024
    out = pl.pallas_call(
        _ada_kernel,
        grid=(DEPTH, 6 * D_MODEL // tn),
        in_specs=[pl.BlockSpec((r, D_MODEL), lambda l, j: (0, 0)),
                  pl.BlockSpec((None, D_MODEL, tn), lambda l, j: (l, 0, j)),
                  pl.BlockSpec((None, 1, tn), lambda l, j: (l, 0, j))],
        out_specs=pl.BlockSpec((None, r, tn), lambda l, j: (l, 0, j)),
        out_shape=jax.ShapeDtypeStruct((DEPTH, r, 6 * D_MODEL), F32),
        compiler_params=_cparams("parallel", "parallel"),
        name="ada",
    )(cond, ada_w, ada_b.reshape(DEPTH, 1, 6 * D_MODEL))
    out = out.reshape(DEPTH, r, 6, D_MODEL)
    return jnp.pad(out, ((0, 0), (0, 0), (0, 2), (0, 0)))


def _row_specs(parts, tm):
    cols = parts[0].shape[1]
    if len(parts) == 1:
        return [pl.BlockSpec((tm, cols), lambda i, *_: (i, 0))]
    nct = parts[0].shape[0] // tm
    return [pl.BlockSpec((tm, cols), lambda i, *_: (jnp.minimum(i, nct - 1), 0)),
            pl.BlockSpec((tm, cols), lambda i, *_: (jnp.maximum(i - nct, 0), 0))]


def _row_tile(refs, n_ctx_tiles):
    if len(refs) == 1:
        return refs[0][...]
    return jnp.where(pl.program_id(0) < n_ctx_tiles, refs[0][...], refs[1][...])


def _inproj_kernel(*refs, n_x, n_ctx_tiles, segs, seg_silu, shift_row):
    x_refs, (mod_ref, w_ref), o_refs = refs[:n_x], refs[n_x:n_x + 2], refs[n_x + 2:]
    x = _row_tile(x_refs, n_ctx_tiles)
    h = x * (1.0 + mod_ref[shift_row + 1:shift_row + 2, :]) + mod_ref[shift_row:shift_row + 1, :]
    h = h.astype(BF16)
    for o_ref, (a, b), act in zip(o_refs, segs, seg_silu):
        y = _dot(h, w_ref[:, a:b])
        o_ref[...] = (_silu(y) if act else y).astype(o_ref.dtype)


def _inproj(x_parts, mods, w, seg_widths, seg_dtypes, seg_silu, *, shift_row, tm, n_ctx, lat_len, name):
    nt = sum(p.shape[0] for p in x_parts)
    segs, acc = [], 0
    for wd in seg_widths:
        segs.append((acc, acc + wd))
        acc += wd
    assert acc == w.shape[1]
    cmap = lambda i: (_cond_index(i * tm, n_ctx, lat_len), 0, 0)
    return pl.pallas_call(
        functools.partial(_inproj_kernel, n_x=len(x_parts), n_ctx_tiles=n_ctx // tm,
                          segs=tuple(segs), seg_silu=tuple(seg_silu), shift_row=shift_row),
        grid=(nt // tm,),
        in_specs=_row_specs(x_parts, tm) + [pl.BlockSpec((None, 8, D_MODEL), cmap), _resident(w.shape)],
        out_specs=[pl.BlockSpec((tm, wd), lambda i: (i, 0)) for wd in seg_widths],
        out_shape=[jax.ShapeDtypeStruct((nt, wd), dt) for wd, dt in zip(seg_widths, seg_dtypes)],
        compiler_params=_cparams("parallel"),
        name=name,
    )(*x_parts, mods, w)


def _cumsum_rows(tril_bf, x):
    hi = x.astype(BF16)
    r = x - hi.astype(F32)
    mid = r.astype(BF16)
    lo = (r - mid.astype(F32)).astype(BF16)
    return _dot(tril_bf, hi) + _dot(tril_bf, mid) + _dot(tril_bf, lo)


def _ssd_kernel(z_ref, xbc_ref, dt_ref, sf0_ref, sb0_ref, cw_ref, cb_ref, dtb_ref, alog_ref,
                dsk_ref, ng_ref, y_ref, sf_ref, sb_ref,
                yacc, xs_s, cm_s, bmt_s, xb_s, erb_s, *, seq_len):
    nc = seq_len // CHUNK
    hd, ns = SSD_HEAD_DIM, SSD_STATE
    gw = (SSD_HEADS // SSD_GROUPS) * hd
    ri = lax.broadcasted_iota(I32, (CHUNK, CHUNK), 0)
    ci = lax.broadcasted_iota(I32, (CHUNK, CHUNK), 1)
    lower = ri >= ci
    upper = ri <= ci
    tril_bf = jnp.where(lower, 1.0, 0.0).astype(BF16)
    rowid = lax.broadcasted_iota(I32, (CHUNK, 1), 0)
    lane = lax.broadcasted_iota(I32, (CHUNK, LANES), 1)
    src = lax.broadcasted_iota(I32, (LANES, SSD_INNER), 0)
    dst_head = lax.broadcasted_iota(I32, (LANES, SSD_INNER), 1) // hd
    spread_f = jnp.where(src == dst_head, 1.0, 0.0).astype(BF16)
    spread_b = jnp.where(src == dst_head + SSD_HEADS, 1.0, 0.0).astype(BF16)

    def per_head(v, spread):
        hi = v.astype(BF16)
        lo = (v - hi.astype(F32)).astype(BF16)
        return _dot(hi, spread) + _dot(lo, spread)

    sf_ref[...] = sf0_ref[...]
    sb_ref[...] = sb0_ref[...]

    def fwd(c, carry):
        r0 = pl.multiple_of(c * CHUNK, CHUNK)
        rows = pl.ds(r0, CHUNK)
        cur = xbc_ref[rows, :]
        pstart = pl.multiple_of(jnp.maximum(r0 - 8, 0), 8)
        nstart = pl.multiple_of(jnp.minimum(r0 + CHUNK, seq_len - 8), 8)
        prev_row = xbc_ref[pl.ds(pstart, 8), :][7:8, :] * jnp.where(c > 0, 1.0, 0.0)
        next_row = xbc_ref[pl.ds(nstart, 8), :][0:1, :] * jnp.where(c < nc - 1, 1.0, 0.0)
        sh_prev = jnp.where(rowid == 0, prev_row, pltpu.roll(cur, 1, axis=0))
        sh_next = jnp.where(rowid == CHUNK - 1, next_row, pltpu.roll(cur, CHUNK - 1, axis=0))
        conv = cw_ref[0:1, :] * sh_prev + cw_ref[1:2, :] * cur + cw_ref[2:3, :] * sh_next + cb_ref[...]
        u = _silu(conv)
        xs = u[:, :SSD_INNER]
        bm = u[:, SSD_INNER:SSD_INNER + LANES]
        cm = u[:, SSD_INNER + LANES:]

        xr = dt_ref[rows, :] + dtb_ref[...]
        dt = jnp.maximum(xr, 0.0) + jnp.log1p(jnp.exp(-jnp.abs(xr)))
        la = -dt * jnp.exp(alog_ref[...])
        facs = _cumsum_rows(tril_bf, la)
        racs = facs[CHUNK - 1:CHUNK, :] - facs + la
        packed = jnp.where(lane < SSD_HEADS, facs,
                           jnp.where(lane < 2 * SSD_HEADS, racs, pltpu.roll(dt, 2 * SSD_HEADS, axis=1)))
        packed_t = packed.T
        e_f = jnp.exp(facs)
        e_r = jnp.exp(racs)
        w_f = dt * jnp.exp(facs[CHUNK - 1:CHUNK, :] - facs)
        w_b = dt * jnp.exp(racs[0:1, :] - racs)

        cm_bf = cm.astype(BF16)
        bm_bf = bm.astype(BF16)
        bmt_bf = bm.T.astype(BF16)
        xs_bf = xs.astype(BF16)
        ef_full = per_head(e_f, spread_f)
        xf = (xs * per_head(w_f, spread_f)).astype(BF16)
        xb_s[rows, :] = (xs * per_head(w_b, spread_b)).astype(BF16)
        yoffs, news = [], []
        for g in range(SSD_GROUPS):
            gl = slice(g * ns, (g + 1) * ns)
            s_g = lax.dot_general(cm_bf[:, gl], bm_bf[:, gl], _NT, preferred_element_type=F32)
            yoffs.append(_dot(cm_bf[:, gl], sf_ref[:, g * gw:(g + 1) * gw].astype(BF16)))
            news.append(_dot(bmt_bf[g * ns:(g + 1) * ns, :], xf[:, g * gw:(g + 1) * gw]))
            for hh in range(SSD_HEADS // SSD_GROUPS):
                h = g * (SSD_HEADS // SSD_GROUPS) + hh
                hb = SSD_HEADS + h
                hs = slice(h * hd, (h + 1) * hd)
                seg_f = facs[:, h:h + 1] - packed_t[h:h + 1, :]
                seg_b = racs[:, hb:hb + 1] - packed_t[hb:hb + 1, :]
                d_f = jnp.exp(jnp.where(lower, seg_f, NEG_BIG))
                d_b = jnp.exp(jnp.where(upper, seg_b, NEG_BIG))
                dt_f_row = packed_t[2 * SSD_HEADS + h:2 * SSD_HEADS + h + 1, :]
                dt_b_row = packed_t[2 * SSD_HEADS + hb:2 * SSD_HEADS + hb + 1, :]
                m = (s_g * (d_f * dt_f_row + d_b * dt_b_row)).astype(BF16)
                yacc[rows, hs] = _dot(m, xs_bf[:, hs])
        yacc[rows, :] += ef_full * jnp.concatenate(yoffs, -1)
        sf_ref[...] = ef_full[CHUNK - 1:CHUNK, :] * sf_ref[...] + jnp.concatenate(news, -1)
        xs_s[rows, :] = xs
        cm_s[rows, :] = cm_bf
        bmt_s[c] = bmt_bf
        erb_s[rows, :] = e_r
        return carry

    lax.fori_loop(0, nc, fwd, 0)

    def bwd(i, carry):
        c = nc - 1 - i
        r0 = pl.multiple_of(c * CHUNK, CHUNK)
        rows = pl.ds(r0, CHUNK)
        cm_bf = cm_s[rows, :]
        bmt_bf = bmt_s[c]
        er_full = per_head(erb_s[rows, :], spread_b)
        yoffs, news = [], []
        for g in range(SSD_GROUPS):
            gl = slice(g * ns, (g + 1) * ns)
            yoffs.append(_dot(cm_bf[:, gl], sb_ref[:, g * gw:(g + 1) * gw].astype(BF16)))
            news.append(_dot(bmt_bf[g * ns:(g + 1) * ns, :], xb_s[rows, g * gw:(g + 1) * gw]))
        sb_ref[...] = er_full[0:1, :] * sb_ref[...] + jnp.concatenate(news, -1)
        yv = yacc[rows, :] + er_full * jnp.concatenate(yoffs, -1) + dsk_ref[...] * xs_s[rows, :]
        gz = yv * z_ref[rows, :].astype(F32)
        ms = jnp.mean(gz * gz, -1, keepdims=True)
        y_ref[rows, :] = (gz * lax.rsqrt(ms + RMS_EPS) * ng_ref[...]).astype(y_ref.dtype)
        return carry

    lax.fori_loop(0, nc, bwd, 0)


def _ssd(z, xbc, dt, sf0, sb0, params, *, row0, n_seq, seq_len, name):
    cw, cb, dtb, alog, dsk, ng = params
    nc = seq_len // CHUNK
    blk0 = row0 // seq_len
    rmap = lambda b: (b + blk0, 0)
    smap = lambda b: (b, 0, 0)
    const = lambda b: (0, 0)
    st = jax.ShapeDtypeStruct((n_seq, SSD_STATE, SSD_INNER), F32)
    return pl.pallas_call(
        functools.partial(_ssd_kernel, seq_len=seq_len),
        grid=(n_seq,),
        in_specs=[pl.BlockSpec((seq_len, SSD_INNER), rmap),
                  pl.BlockSpec((seq_len, SSD_XBC), rmap),
                  pl.BlockSpec((seq_len, LANES), rmap),
                  pl.BlockSpec((None, SSD_STATE, SSD_INNER), smap),
                  pl.BlockSpec((None, SSD_STATE, SSD_INNER), smap),
                  pl.BlockSpec(cw.shape, const), pl.BlockSpec(cb.shape, const),
                  pl.BlockSpec(dtb.shape, const), pl.BlockSpec(alog.shape, const),
                  pl.BlockSpec(dsk.shape, const), pl.BlockSpec(ng.shape, const)],
        out_specs=[pl.BlockSpec((seq_len, SSD_INNER), lambda b: (b, 0)),
                   pl.BlockSpec((None, SSD_STATE, SSD_INNER), smap),
                   pl.BlockSpec((None, SSD_STATE, SSD_INNER), smap)],
        out_shape=[jax.ShapeDtypeStruct((n_seq * seq_len, SSD_INNER), BF16), st, st],
        scratch_shapes=[pltpu.VMEM((seq_len, SSD_INNER), F32),
                        pltpu.VMEM((seq_len, SSD_INNER), F32),
                        pltpu.VMEM((seq_len, LANES), BF16),
                        pltpu.VMEM((nc, LANES, CHUNK), BF16),
                        pltpu.VMEM((seq_len, SSD_INNER), BF16),
                        pltpu.VMEM((seq_len, LANES), F32)],
        compiler_params=_cparams("parallel"),
        name=name,
    )(z, xbc, dt, sf0, sb0, cw, cb, dtb, alog, dsk, ng)


def _rms(x, g):
    return x * lax.rsqrt(jnp.mean(x * x, -1, keepdims=True) + RMS_EPS) * g


def _mla_prep_kernel(*refs, do_q, do_norm, do_rope):
    it = iter(refs)
    qlat_ref = next(it) if do_q else None
    kv_ref = next(it)
    kr_ref = next(it)
    cos_ref = next(it) if do_rope else None
    sin_ref = next(it) if do_rope else None
    if do_q:
        gq_ref, wq_ref = next(it), next(it)
        wqr_ref = next(it) if do_rope else None
    gkv_ref = next(it) if do_norm else None
    wk_ref, wv_ref = next(it), next(it)
    q_out = next(it) if do_q else None
    k_out, v_out = next(it), next(it)
    ckv_out = next(it) if do_norm else None
    cs, sn = (cos_ref[...], sin_ref[...]) if do_rope else (None, None)
    _mla_project(qlat_ref[...] if do_q else None, kv_ref[...], kr_ref[...], cs, sn,
                 (gq_ref, wq_ref, wqr_ref) if do_q else None, gkv_ref, wk_ref, wv_ref,
                 q_out, k_out, v_out, ckv_out)


def _mla_project(qlat, kvlat, kr, cs, sn, q_w, gkv_ref, wk_ref, wv_ref, q_out, k_out, v_out, ckv_out):
    do_rope = cs is not None
    if q_w is not None:
        gq_ref, wq_ref, wqr_ref = q_w
        qn = _rms(qlat, gq_ref[...]).astype(BF16)
        qa = _dot(qn, wq_ref[...])
        if do_rope:
            qb = _dot(qn, wqr_ref[...])
        scale = (MLA_NOPE + MLA_ROPE) ** -0.5 * math.log2(math.e)
        for h in range(MLA_HEADS):
            hs = slice(h * LANES, (h + 1) * LANES)
            qh = qa[:, hs] * cs + qb[:, hs] * sn if do_rope else qa[:, hs]
            q_out[h] = (qh * scale).astype(BF16)
    ckv = kvlat
    if gkv_ref is not None:
        ckv = _rms(ckv, gkv_ref[...])
        ckv_out[...] = ckv
    ckv_bf = ckv.astype(BF16)
    krp = kr[:, :LANES] * cs + kr[:, LANES:] * sn if do_rope else kr[:, :LANES]
    kn = _dot(ckv_bf, wk_ref[...])
    for h in range(MLA_HEADS):
        k_out[h] = (kn[:, h * LANES:(h + 1) * LANES] + krp).astype(BF16)
    v_out[...] = _dot(ckv_bf, wv_ref[...]).astype(BF16)


def _mla_prep(qlat, kv, kr, tables, weights, *, row0, n_rows, do_q, do_norm, tm, name):
    do_rope = tables is not None
    gq, wq, wqr, gkv, wk, wv = weights
    b0 = row0 // tm
    rmap = lambda i: (i + b0, 0)
    omap = lambda i: (i, 0)
    hmap = lambda i: (0, i, 0)
    ins, specs = [], []

    def add(a, spec):
        ins.append(a)
        specs.append(spec)

    if do_q:
        add(qlat, pl.BlockSpec((tm, MLA_Q_RANK), rmap))
    add(kv, pl.BlockSpec((tm, MLA_KV_RANK), rmap))
    add(kr, pl.BlockSpec((tm, 2 * LANES), rmap))
    if do_rope:
        lat_tiles = tables[0].shape[0] // tm
        tmap = lambda i: (i % lat_tiles, 0)
        add(tables[0], pl.BlockSpec((tm, LANES), tmap))
        add(tables[1], pl.BlockSpec((tm, LANES), tmap))
    if do_q:
        add(gq, _resident(gq.shape))
        add(wq, _resident(wq.shape))
        if do_rope:
            add(wqr, _resident(wqr.shape))
    if do_norm:
        add(gkv, _resident(gkv.shape))
    add(wk, _resident(wk.shape))
    add(wv, _resident(wv.shape))
    out_shape, out_specs = [], []
    if do_q:
        out_shape.append(jax.ShapeDtypeStruct((MLA_HEADS, n_rows, LANES), BF16))
        out_specs.append(pl.BlockSpec((MLA_HEADS, tm, LANES), hmap))
    out_shape.append(jax.ShapeDtypeStruct((MLA_HEADS, n_rows, LANES), BF16))
    out_specs.append(pl.BlockSpec((MLA_HEADS, tm, LANES), hmap))
    out_shape.append(jax.ShapeDtypeStruct((n_rows, MLA_OUT), BF16))
    out_specs.append(pl.BlockSpec((tm, MLA_OUT), omap))
    if do_norm:
        out_shape.append(jax.ShapeDtypeStruct((n_rows, MLA_KV_RANK), F32))
        out_specs.append(pl.BlockSpec((tm, MLA_KV_RANK), omap))
    return pl.pallas_call(
        functools.partial(_mla_prep_kernel, do_q=do_q, do_norm=do_norm, do_rope=do_rope),
        grid=(n_rows // tm,),
        in_specs=specs, out_specs=out_specs, out_shape=out_shape,
        compiler_params=_cparams("parallel"),
        name=name,
    )(*ins)


_IN0_SEGS = (SSD_INNER, SSD_XBC, LANES, MLA_Q_RANK, MLA_KV_RANK, 2 * LANES)


def _in0_kernel(*refs, n_x, n_ctx_tiles):
    x_refs = refs[:n_x]
    (mod_ref, w_ref, cos_ref, sin_ref, gq_ref, wq_ref, wqr_ref, gkv_ref, wk_ref, wv_ref,
     z_out, xbc_out, dt_out, q_out, k_out, v_out, ckv_out, kr_out) = refs[n_x:]
    x = _row_tile(x_refs, n_ctx_tiles)
    h = (x * (1.0 + mod_ref[1:2, :]) + mod_ref[0:1, :]).astype(BF16)
    segs, acc = [], 0
    for wd in _IN0_SEGS:
        segs.append(_dot(h, w_ref[:, acc:acc + wd]))
        acc += wd
    z, xbc, dt, qlat, kvlat, kr = segs
    z_out[...] = _silu(z).astype(z_out.dtype)
    xbc_out[...] = xbc
    dt_out[...] = dt
    kr_out[...] = kr[:, :LANES]
    _mla_project(qlat, kvlat, kr, cos_ref[...], sin_ref[...], (gq_ref, wq_ref, wqr_ref), gkv_ref,
                 wk_ref, wv_ref, q_out, k_out, v_out, ckv_out)


def _in0(x_parts, mods, w, tables, mla_w, *, tm, n_ctx, lat_len):
    nt = sum(p.shape[0] for p in x_parts)
    nct = n_ctx // tm
    lat_tiles = lat_len // tm
    gq, wq, wqr, gkv, wk, wv = mla_w
    cmap = lambda i: (_cond_index(i * tm, n_ctx, lat_len), 0, 0)
    tmap = lambda i: (jnp.where(i < nct, 0, 1 + (i - nct) % lat_tiles), 0)
    row = lambda cols: pl.BlockSpec((tm, cols), lambda i: (i, 0))
    heads = pl.BlockSpec((MLA_HEADS, tm, LANES), lambda i: (0, i, 0))
    consts = [w, gq, wq, wqr, gkv, wk, wv]
    return pl.pallas_call(
        functools.partial(_in0_kernel, n_x=len(x_parts), n_ctx_tiles=nct),
        grid=(nt // tm,),
        in_specs=(_row_specs(x_parts, tm) + [pl.BlockSpec((None, 8, D_MODEL), cmap), _resident(w.shape),
                                             pl.BlockSpec((tm, LANES), tmap), pl.BlockSpec((tm, LANES), tmap)]
                  + [_resident(a.shape) for a in consts[1:]]),
        out_specs=[row(SSD_INNER), row(SSD_XBC), row(LANES), heads, heads, row(MLA_OUT),
                   row(MLA_KV_RANK), row(LANES)],
        out_shape=[jax.ShapeDtypeStruct((nt, SSD_INNER), BF16),
                   jax.ShapeDtypeStruct((nt, SSD_XBC), F32),
                   jax.ShapeDtypeStruct((nt, LANES), F32),
                   jax.ShapeDtypeStruct((MLA_HEADS, nt, LANES), BF16),
                   jax.ShapeDtypeStruct((MLA_HEADS, nt, LANES), BF16),
                   jax.ShapeDtypeStruct((nt, MLA_OUT), BF16),
                   jax.ShapeDtypeStruct((nt, MLA_KV_RANK), F32),
                   jax.ShapeDtypeStruct((nt, LANES), F32)],
        compiler_params=_cparams("parallel"),
        name="in0",
    )(*x_parts, mods, w, tables[0], tables[1], gq, wq, wqr, gkv, wk, wv)


def _attn_kernel(*refs, seq_len, cache_len, kblk, n_pairs):
    if cache_len:
        q_ref, k_ref, v_ref, kc_ref, vc_ref, o_ref = refs
    else:
        q_ref, k_ref, v_ref, o_ref = refs
    tq = q_ref.shape[1]
    blocks = [(k_ref, v_ref, i * kblk) for i in range(seq_len // kblk)]
    if cache_len:
        blocks += [(kc_ref, vc_ref, i * kblk) for i in range(cache_len // kblk)]
    vlane = lax.broadcasted_iota(I32, (kblk, LANES), 1)
    lane = lax.broadcasted_iota(I32, (tq, LANES), 1)
    for pp in range(n_pairs):
        ps = slice(pp * LANES, (pp + 1) * LANES)
        outs = []
        for hh in range(2):
            own = (vlane < MLA_V) if hh == 0 else (vlane >= MLA_V)
            q = q_ref[2 * pp + hh]
            m = None
            for kr, vr, off in blocks:
                s = lax.dot_general(q, kr[2 * pp + hh, off:off + kblk, :], _NT, preferred_element_type=F32)
                v_aug = jnp.where(own, vr[off:off + kblk, ps], jnp.ones((), BF16))
                bmax = jnp.max(s, -1, keepdims=True)
                if m is None:
                    m = bmax
                    acc = _dot(jnp.exp2(s - m).astype(BF16), v_aug)
                else:
                    m_new = jnp.maximum(m, bmax)
                    acc = jnp.exp2(m - m_new) * acc + _dot(jnp.exp2(s - m_new).astype(BF16), v_aug)
                    m = m_new
            outs.append(acc / pltpu.roll(acc, MLA_V, axis=1))
        o_ref[:, ps] = jnp.where(lane < MLA_V, outs[0], outs[1]).astype(o_ref.dtype)


def _attention(q, k, v, kc, vc, *, row0, n_seq, seq_len, cache_len, tq, n_pairs, name):
    nq = seq_len // tq
    kblk = min(512, seq_len)
    n = n_seq * seq_len
    q0, s0 = row0 // tq, row0 // seq_len
    hb, vw = 2 * n_pairs, n_pairs * LANES
    ins = [q, k, v]
    specs = [pl.BlockSpec((hb, tq, LANES), lambda b, hp, qi: (hp, q0 + b * nq + qi, 0)),
             pl.BlockSpec((hb, seq_len, LANES), lambda b, hp, qi: (hp, s0 + b, 0)),
             pl.BlockSpec((seq_len, vw), lambda b, hp, qi: (s0 + b, hp))]
    if cache_len:
        ins += [kc, vc]
        specs += [pl.BlockSpec((hb, cache_len, LANES), lambda b, hp, qi: (hp, b, 0)),
                  pl.BlockSpec((cache_len, vw), lambda b, hp, qi: (b, hp))]
    return pl.pallas_call(
        functools.partial(_attn_kernel, seq_len=seq_len, cache_len=cache_len, kblk=kblk, n_pairs=n_pairs),
        grid=(n_seq, MLA_HEADS // hb, nq),
        in_specs=specs,
        out_specs=pl.BlockSpec((tq, vw), lambda b, hp, qi: (b * nq + qi, hp)),
        out_shape=jax.ShapeDtypeStruct((n, MLA_OUT), BF16),
        compiler_params=_cparams("parallel", "parallel", "arbitrary"),
        name=name,
    )(*ins)


def _outproj_kernel(*refs, n_parts, n_ctx_tiles, gate_row, with_router):
    it = iter(refs)
    acc = None
    for n in n_parts[:-1]:
        a = _row_tile([next(it) for _ in range(n)], n_ctx_tiles)
        part = _dot(a, next(it)[...])
        acc = part if acc is None else acc + part
    x = _row_tile([next(it) for _ in range(n_parts[-1])], n_ctx_tiles)
    mod_ref, g_ref, b_ref = next(it), next(it), next(it)
    router_ref = next(it) if with_router else None
    o_ref = next(it)
    y = ALPHA * x + mod_ref[gate_row:gate_row + 1, :] * acc
    res = _layer_norm(y, g_ref[...], b_ref[...])
    o_ref[...] = res
    if with_router:
        _route(res, mod_ref, router_ref, *it)


def _outproj(acts, ws, x_parts, mods, g, b, *, gate_row, tm, n_ctx, lat_len, name, router_w=None):
    nt = sum(p.shape[0] for p in x_parts)
    cmap = lambda i: (_cond_index(i * tm, n_ctx, lat_len), 0, 0)
    ins, specs = [], []
    for parts, w in zip(acts, ws):
        ins += list(parts) + [w]
        specs += _row_specs(parts, tm) + [_resident(w.shape)]
    ins += list(x_parts) + [mods, g, b]
    specs += _row_specs(x_parts, tm) + [pl.BlockSpec((None, 8, D_MODEL), cmap),
                                        _resident(g.shape), _resident(b.shape)]
    out_specs = [pl.BlockSpec((tm, D_MODEL), lambda i: (i, 0))]
    out_shape = [jax.ShapeDtypeStruct((nt, D_MODEL), F32)]
    if router_w is not None:
        assert tm == MOE_TOK_CHUNK
        ins.append(router_w)
        specs.append(_resident(router_w.shape))
        out_specs += [pl.BlockSpec((tm, D_MODEL), lambda i: (i, 0)),
                      pl.BlockSpec((tm, LANES), lambda i: (i, 0)),
                      pl.BlockSpec((None, 8, tm), lambda i: (i, 0, 0)),
                      pl.BlockSpec((None, 8, LANES), lambda i: (i, 0, 0))]
        out_shape += [jax.ShapeDtypeStruct((nt, D_MODEL), BF16),
                      jax.ShapeDtypeStruct((nt, LANES), F32),
                      jax.ShapeDtypeStruct((nt // tm, 8, tm), F32),
                      jax.ShapeDtypeStruct((nt // tm, 8, LANES), F32)]
    n_parts = tuple(len(p) for p in acts) + (len(x_parts),)
    outs = pl.pallas_call(
        functools.partial(_outproj_kernel, n_parts=n_parts, n_ctx_tiles=n_ctx // tm, gate_row=gate_row,
                          with_router=router_w is not None),
        grid=(nt // tm,),
        in_specs=specs,
        out_specs=out_specs,
        out_shape=out_shape,
        compiler_params=_cparams("parallel"),
        name=name,
    )(*ins)
    return outs[0] if router_w is None else outs


def _ffn_kernel(*refs, n_parts, n_ctx_tiles, ff_chunks):
    it = iter(refs)
    mix = None
    for n in n_parts[:-1]:
        a = _row_tile([next(it) for _ in range(n)], n_ctx_tiles)
        part = _dot(a, next(it)[...])
        mix = part if mix is None else mix + part
    x0 = _row_tile([next(it) for _ in range(n_parts[-1])], n_ctx_tiles)
    mod_ref, g1_ref, b1_ref, wg_ref, wu_ref, wd_ref, g_ref, b_ref, o_ref = it
    x = _layer_norm(ALPHA * x0 + mod_ref[2:3, :] * mix, g1_ref[...], b1_ref[...])
    h = (x * (1.0 + mod_ref[4:5, :]) + mod_ref[3:4, :]).astype(BF16)
    acc = None
    for a, b in ff_chunks:
        gt = _dot(h, wg_ref[:, a:b])
        up = _dot(h, wu_ref[:, a:b])
        act = (_silu(gt) * up).astype(BF16)
        part = _dot(act, wd_ref[a:b, :])
        acc = part if acc is None else acc + part
    y = ALPHA * x + mod_ref[5:6, :] * acc
    o_ref[...] = _layer_norm(y, g_ref[...], b_ref[...])


def _ffn(acts, ws, x_parts, mods, g1, b1, wg, wu, wd, g, b, *, tm, n_ctx, lat_len):
    nt = sum(p.shape[0] for p in x_parts)
    ff = wg.shape[1]
    chunks, a = [], 0
    while a < ff:
        chunks.append((a, min(a + 512, ff)))
        a += 512
    cmap = lambda i: (_cond_index(i * tm, n_ctx, lat_len), 0, 0)
    ins, specs = [], []
    for parts, w in zip(acts, ws):
        ins += list(parts) + [w]
        specs += _row_specs(parts, tm) + [_resident(w.shape)]
    consts = [g1, b1, wg, wu, wd, g, b]
    ins += list(x_parts) + [mods] + consts
    specs += _row_specs(x_parts, tm) + [pl.BlockSpec((None, 8, D_MODEL), cmap)]
    specs += [_resident(a.shape) for a in consts]
    n_parts = tuple(len(p) for p in acts) + (len(x_parts),)
    return pl.pallas_call(
        functools.partial(_ffn_kernel, n_parts=n_parts, n_ctx_tiles=n_ctx // tm, ff_chunks=tuple(chunks)),
        grid=(nt // tm,),
        in_specs=specs,
        out_specs=pl.BlockSpec((tm, D_MODEL), lambda i: (i, 0)),
        out_shape=jax.ShapeDtypeStruct((nt, D_MODEL), F32),
        compiler_params=_cparams("parallel"),
        name="out0_ffn",
    )(*ins)


def _ret_kernel(*refs, seq_len, do_rope, n_heads, from_zero):
    it = iter(refs)
    q_ref, k_ref, v_ref, g_ref = next(it), next(it), next(it), next(it)
    cos_ref, sin_ref = (next(it), next(it)) if do_rope else (None, None)
    dec_ref = next(it)
    sf0_ref, sb0_ref = (None, None) if from_zero else (next(it), next(it))
    y_ref = next(it)
    sf_out, sb_out = (next(it), next(it)) if from_zero else (None, None)
    yacc, q_s, kb_s, dcomb_s, ev_s, wk_s, cd_s, sf_ref, sb_ref = it
    rc = RET_CHUNK
    nc = seq_len // rc
    unroll = min(2, nc)

    @pl.when(pl.program_id(1) == 0)
    def _():
        ri = lax.broadcasted_iota(I32, (rc, rc), 0)
        ci = lax.broadcasted_iota(I32, (rc, rc), 1)
        dij = (ri - ci).astype(F32)
        pos_k = lax.broadcasted_iota(I32, (rc, RET_QK), 0).astype(F32)
        pos_v = lax.broadcasted_iota(I32, (rc, RET_V), 0).astype(F32)
        for hh in range(n_heads):
            la_f = -jnp.exp(dec_ref[hh, 0:1, :])
            la_b = -jnp.exp(dec_ref[hh, 1:2, :])
            dcomb_s[hh] = (jnp.exp(jnp.where(ri >= ci, dij * la_f[:, :rc], NEG_BIG)) +
                           jnp.exp(jnp.where(ri <= ci, -dij * la_b[:, :rc], NEG_BIG)))
            ev_s[hh, 0] = jnp.exp((pos_v + 1.0) * la_f)
            ev_s[hh, 1] = jnp.exp((rc - pos_v) * la_b)
            wk_s[hh, 0] = jnp.exp((rc - 1.0 - pos_k) * la_f[:, :RET_QK])
            wk_s[hh, 1] = jnp.exp(pos_k * la_b[:, :RET_QK])
            cd_s[hh, 0:1, :] = jnp.exp(rc * la_f)
            cd_s[hh, 1:2, :] = jnp.exp(rc * la_b)

    if from_zero:
        sf_ref[...] = jnp.zeros_like(sf_ref)
        sb_ref[...] = jnp.zeros_like(sb_ref)
    else:
        for hh in range(n_heads):
            sf_ref[hh] = sf0_ref[hh].T
            sb_ref[hh] = sb0_ref[hh].T

    def fwd(c, carry):
        rows = pl.ds(pl.multiple_of(c * rc, rc), rc)
        if do_rope:
            cs, sn = cos_ref[rows, :], sin_ref[rows, :]
        for hh in range(n_heads):
            qs = slice(hh * RET_QK, (hh + 1) * RET_QK)
            vs = slice(hh * RET_V, (hh + 1) * RET_V)
            q = q_ref[rows, qs]
            k = k_ref[rows, qs] * (RET_QK ** -0.5)
            if do_rope:
                q = q * cs + pltpu.roll(q, RET_QK // 2, axis=1) * sn
                k = k * cs + pltpu.roll(k, RET_QK // 2, axis=1) * sn
            q_bf = q.astype(BF16)
            v = v_ref[rows, vs]
            s = lax.dot_general(q_bf, k.astype(BF16), _NT, preferred_element_type=F32)
            y = _dot((s * dcomb_s[hh]).astype(BF16), v)
            y = y + _dot(q_bf, sf_ref[hh].astype(BF16)) * ev_s[hh, 0]
            yacc[rows, vs] = y
            upd = lax.dot_general((k * wk_s[hh, 0]).astype(BF16), v, _TN, preferred_element_type=F32)
            sf_ref[hh] = cd_s[hh, 0:1, :] * sf_ref[hh] + upd
            q_s[rows, qs] = q_bf
            kb_s[rows, qs] = (k * wk_s[hh, 1]).astype(BF16)
        return carry

    lax.fori_loop(0, nc, fwd, 0, unroll=unroll)

    def bwd(i, carry):
        rows = pl.ds(pl.multiple_of((nc - 1 - i) * rc, rc), rc)
        for hh in range(n_heads):
            qs = slice(hh * RET_QK, (hh + 1) * RET_QK)
            vs = slice(hh * RET_V, (hh + 1) * RET_V)
            v = v_ref[rows, vs]
            y = yacc[rows, vs] + _dot(q_s[rows, qs], sb_ref[hh].astype(BF16)) * ev_s[hh, 1]
            upd = lax.dot_general(kb_s[rows, qs], v, _TN, preferred_element_type=F32)
            sb_ref[hh] = cd_s[hh, 1:2, :] * sb_ref[hh] + upd
            mu = jnp.mean(y, -1, keepdims=True)
            d = y - mu
            var = jnp.mean(d * d, -1, keepdims=True)
            yn = d * lax.rsqrt(var + LN_EPS)
            y_ref[rows, vs] = (yn * g_ref[rows, vs].astype(F32)).astype(y_ref.dtype)
        return carry

    lax.fori_loop(0, nc, bwd, 0, unroll=unroll)

    if from_zero:
        for hh in range(n_heads):
            sf_out[hh] = sf_ref[hh].T
            sb_out[hh] = sb_ref[hh].T


def _retention(q, k, v, g, tables, dec, states, *, row0, n_seq, seq_len, hps, name):
    do_rope = tables is not None
    from_zero = states is None
    blk0 = row0 // seq_len
    qmap = lambda h, b: (b + blk0, h)
    state_spec = pl.BlockSpec((None, hps, RET_V, RET_QK), lambda h, b: (b, h, 0, 0))
    ins = [q, k, v, g]
    specs = [pl.BlockSpec((seq_len, hps * RET_QK), qmap), pl.BlockSpec((seq_len, hps * RET_QK), qmap),
             pl.BlockSpec((seq_len, hps * RET_V), qmap), pl.BlockSpec((seq_len, hps * RET_V), qmap)]
    if do_rope:
        ins += list(tables)
        specs += [pl.BlockSpec((seq_len, RET_QK), lambda h, b: (0, 0))] * 2
    ins.append(dec)
    specs.append(pl.BlockSpec((hps, 8, RET_V), lambda h, b: (h, 0, 0)))
    out_specs = [pl.BlockSpec((seq_len, hps * RET_V), lambda h, b: (b, h))]
    out_shape = [jax.ShapeDtypeStruct((n_seq * seq_len, MIX1), BF16)]
    if from_zero:
        st = jax.ShapeDtypeStruct((n_seq, RET_HEADS, RET_V, RET_QK), F32)
        out_specs += [state_spec, state_spec]
        out_shape += [st, st]
    else:
        ins += list(states)
        specs += [state_spec, state_spec]
    rc = RET_CHUNK
    return pl.pallas_call(
        functools.partial(_ret_kernel, seq_len=seq_len, do_rope=do_rope, n_heads=hps, from_zero=from_zero),
        grid=(RET_HEADS // hps, n_seq),
        in_specs=specs,
        out_specs=out_specs,
        out_shape=out_shape,
        scratch_shapes=[pltpu.VMEM((seq_len, hps * RET_V), F32),
                        pltpu.VMEM((seq_len, hps * RET_QK), BF16),
                        pltpu.VMEM((seq_len, hps * RET_QK), BF16),
                        pltpu.VMEM((hps, rc, rc), F32),
                        pltpu.VMEM((hps, 2, rc, RET_V), F32),
                        pltpu.VMEM((hps, 2, rc, RET_QK), F32),
                        pltpu.VMEM((hps, 8, RET_V), F32),
                        pltpu.VMEM((hps, RET_QK, RET_V), F32),
                        pltpu.VMEM((hps, RET_QK, RET_V), F32)],
        compiler_params=_cparams("parallel", "arbitrary"),
        name=name,
    )(*ins)


def _route(x, mod_ref, w_ref, hb_ref, tok_ref, lpt_ref, cap_ref):
    tm = x.shape[0]
    h = x * (1.0 + mod_ref[4:5, :]) + mod_ref[3:4, :]
    h_hi = h.astype(BF16)
    hb_ref[...] = h_hi
    h_lo = (h - h_hi.astype(F32)).astype(BF16)
    w = w_ref[...]
    w_hi = w.astype(BF16)
    w_lo = (w - w_hi.astype(F32)).astype(BF16)
    w_both = w_hi + pltpu.roll(w_lo.astype(F32), N_EXPERTS, axis=1).astype(BF16)
    part = _dot(h_hi, w_both) + _dot(h_lo, w_both)
    logits = part + pltpu.roll(part, LANES - N_EXPERTS, axis=1)
    lane = lax.broadcasted_iota(I32, (tm, LANES), 1)
    logits = jnp.where(lane < N_EXPERTS, logits, NEG_BIG)
    m1 = jnp.max(logits, -1, keepdims=True)
    i1 = jnp.min(jnp.where(logits == m1, lane, LANES), -1, keepdims=True)
    rest = jnp.where(lane == i1, NEG_BIG, logits)
    m2 = jnp.max(rest, -1, keepdims=True)
    i2 = jnp.min(jnp.where(rest == m2, lane, LANES), -1, keepdims=True)
    e = jnp.exp(m2 - m1)
    g1 = 1.0 / (1.0 + e)
    g2 = e / (1.0 + e)
    sel1 = lane == i1
    sel2 = lane == i2
    onehot = jnp.where(sel1 | sel2, 1.0, 0.0)
    ri = lax.broadcasted_iota(I32, (tm, tm), 0)
    ci = lax.broadcasted_iota(I32, (tm, tm), 1)
    strict = jnp.where(ri > ci, 1.0, 0.0).astype(BF16)
    prefix = _dot(strict, onehot.astype(BF16))
    n_row = jnp.sum(onehot, 0, keepdims=True)
    cap_row = jnp.floor((n_row + (MOE_CELL_ALIGN - 1.0)) * (1.0 / MOE_CELL_ALIGN)) * MOE_CELL_ALIGN
    li = lax.broadcasted_iota(I32, (LANES, LANES), 0)
    lj = lax.broadcasted_iota(I32, (LANES, LANES), 1)
    lower_lanes = jnp.where(li < lj, 1.0, 0.0).astype(BF16)
    cap8 = jnp.broadcast_to(cap_row, (8, LANES))
    base_row = _dot(cap8.astype(BF16), lower_lanes)[0:1, :]
    local = prefix + base_row
    lpos1 = jnp.sum(jnp.where(sel1, local, 0.0), -1, keepdims=True)
    lpos2 = jnp.sum(jnp.where(sel2, local, 0.0), -1, keepdims=True)
    tok = jnp.where(lane == 0, g1,
          jnp.where(lane == 1, g2,
          jnp.where(lane == 2, lpos1,
          jnp.where(lane == 3, lpos2, 0.0))))
    tok_ref[...] = tok
    lpt_ref[...] = tok.T[0:8, :]
    cap_ref[...] = cap8


def _cell_copies(tabs, chunk, hbm_ref, buf_ref, slot, sem, *, to_local, wait):
    g_ref, a_ref, nbig_ref, nsmall_ref = tabs
    for e in range(N_EXPERTS):
        k = chunk * N_EXPERTS + e
        g0, a0, nbig, nsmall = g_ref[k], a_ref[k], nbig_ref[k], nsmall_ref[k]

        def piece(i, carry, rows, goff, aoff):
            g = pl.multiple_of(goff + i * rows, MOE_CELL_ALIGN)
            a = pl.multiple_of(aoff + i * rows, MOE_CELL_ALIGN)
            far = hbm_ref.at[pl.ds(g, rows), :]
            near = buf_ref.at[slot, pl.ds(a, rows), :]
            cp = (pltpu.make_async_copy(far, near, sem.at[slot]) if to_local
                  else pltpu.make_async_copy(near, far, sem.at[slot]))
            if wait:
                cp.wait()
            else:
                cp.start()
            return carry

        lax.fori_loop(0, nbig, functools.partial(piece, rows=MOE_BIG_PIECE, goff=g0, aoff=a0), 0)
        done = nbig * MOE_BIG_PIECE
        lax.fori_loop(0, nsmall, functools.partial(piece, rows=MOE_CELL_ALIGN, goff=g0 + done, aoff=a0 + done), 0)


def _dispatch_kernel(g_ref, a_ref, nbig_ref, nsmall_ref, hb_ref, lpt_ref, xs_hbm, ybuf, sem, *, n_fill):
    c = pl.program_id(0)
    nc = pl.num_programs(0)
    slot = c % 2
    tabs = (g_ref, a_ref, nbig_ref, nsmall_ref)

    @pl.when(c == 0)
    def _():
        ybuf[2] = jnp.zeros(ybuf.shape[1:], ybuf.dtype)
        for j in range(n_fill):
            _cell_copies(tabs, nc + j, xs_hbm, ybuf, 2, sem, to_local=False, wait=False)

    @pl.when(c == nc - 1)
    def _():
        for j in range(n_fill):
            _cell_copies(tabs, nc + j, xs_hbm, ybuf, 2, sem, to_local=False, wait=True)

    l1 = lpt_ref[2:3, :]
    l2 = lpt_ref[3:4, :]
    rb = MOE_LOCAL_ROWS // 3
    for r in range(3):
        rid = (lax.broadcasted_iota(I32, (rb, MOE_TOK_CHUNK), 0) + r * rb).astype(F32)
        onehot = jnp.where((l1 == rid) | (l2 == rid), 1.0, 0.0).astype(BF16)
        ybuf[slot, r * rb:(r + 1) * rb, :] = _dot(onehot, hb_ref[...]).astype(BF16)

    @pl.when(c > 0)
    def _():
        _cell_copies(tabs, c - 1, xs_hbm, ybuf, 1 - slot, sem, to_local=False, wait=True)

    _cell_copies(tabs, c, xs_hbm, ybuf, slot, sem, to_local=False, wait=False)

    @pl.when(c == nc - 1)
    def _():
        _cell_copies(tabs, c, xs_hbm, ybuf, slot, sem, to_local=False, wait=True)


def _dispatch(tabs, hb, lpt, *, n_slots, n_fill):
    tm = MOE_TOK_CHUNK
    grid_spec = pltpu.PrefetchScalarGridSpec(
        num_scalar_prefetch=4,
        grid=(hb.shape[0] // tm,),
        in_specs=[pl.BlockSpec((tm, D_MODEL), lambda c, *_: (c, 0)),
                  pl.BlockSpec((None, 8, tm), lambda c, *_: (c, 0, 0))],
        out_specs=pl.BlockSpec(memory_space=pl.ANY),
        scratch_shapes=[pltpu.VMEM((3, MOE_LOCAL_ROWS, D_MODEL), BF16),
                        pltpu.SemaphoreType.DMA((3,))],
    )
    return pl.pallas_call(
        functools.partial(_dispatch_kernel, n_fill=n_fill),
        grid_spec=grid_spec,
        out_shape=jax.ShapeDtypeStruct((n_slots, D_MODEL), BF16),
        compiler_params=_cparams("arbitrary"),
        name="moe_dispatch",
    )(*tabs, hb, lpt)


def _expert_kernel(te_ref, nu_ref, x_ref, wg_ref, wu_ref, wd_ref, o_ref, acc):
    t = pl.program_id(0)
    f = pl.program_id(1)
    nf = pl.num_programs(1)

    @pl.when(t < nu_ref[0])
    def _():
        x = x_ref[...]
        gt = _dot(x, wg_ref[...])
        up = _dot(x, wu_ref[...])
        part = _dot((_silu(gt) * up).astype(BF16), wd_ref[...])

        @pl.when(f == 0)
        def _():
            acc[...] = part

        @pl.when((f > 0) & (f < nf - 1))
        def _():
            acc[...] += part

        @pl.when(f == nf - 1)
        def _():
            o_ref[...] = (acc[...] + part).astype(o_ref.dtype)

    @pl.when((t >= nu_ref[0]) & (f == nf - 1))
    def _():
        o_ref[...] = jnp.zeros_like(o_ref)


def _experts(te, nu, xs, wg, wu, wd):
    bs = MOE_SLOT_TILE
    n_slots = xs.shape[0]
    nf = D_FF_EXPERT // MOE_FF_CHUNK

    def tt(t, nu):
        return jnp.minimum(t, nu[0] - 1)

    def ff(t, f, nu):
        return jnp.where(t < nu[0], f, nf - 1)

    grid_spec = pltpu.PrefetchScalarGridSpec(
        num_scalar_prefetch=2,
        grid=(n_slots // bs, nf),
        in_specs=[pl.BlockSpec((bs, D_MODEL), lambda t, f, te, nu: (tt(t, nu), 0)),
                  pl.BlockSpec((None, D_MODEL, MOE_FF_CHUNK), lambda t, f, te, nu: (te[tt(t, nu)], 0, ff(t, f, nu))),
                  pl.BlockSpec((None, D_MODEL, MOE_FF_CHUNK), lambda t, f, te, nu: (te[tt(t, nu)], 0, ff(t, f, nu))),
                  pl.BlockSpec((None, MOE_FF_CHUNK, D_MODEL), lambda t, f, te, nu: (te[tt(t, nu)], ff(t, f, nu), 0))],
        out_specs=pl.BlockSpec((bs, D_MODEL), lambda t, f, te, nu: (t, 0)),
        scratch_shapes=[pltpu.VMEM((bs, D_MODEL), F32)],
    )
    return pl.pallas_call(
        _expert_kernel,
        grid_spec=grid_spec,
        out_shape=jax.ShapeDtypeStruct((n_slots, D_MODEL), BF16),
        compiler_params=_cparams("arbitrary", "arbitrary"),
        name="moe_experts",
    )(te, nu, xs, wg, wu, wd)


def _combine_kernel(g_ref, a_ref, nbig_ref, nsmall_ref, tok_ref, x_ref, mod_ref, lg_ref, lb_ref, ys_hbm,
                    oc_ref, ol_ref, ybuf, sem, *, n_ctx_tiles):
    c = pl.program_id(0)
    nc = pl.num_programs(0)
    slot = c % 2
    tabs = (g_ref, a_ref, nbig_ref, nsmall_ref)
    tm = tok_ref.shape[0]

    @pl.when(c == 0)
    def _():
        ybuf[...] = jnp.zeros_like(ybuf)
        _cell_copies(tabs, 0, ys_hbm, ybuf, 0, sem, to_local=True, wait=False)

    @pl.when(c + 1 < nc)
    def _():
        _cell_copies(tabs, c + 1, ys_hbm, ybuf, 1 - slot, sem, to_local=True, wait=False)

    _cell_copies(tabs, c, ys_hbm, ybuf, slot, sem, to_local=True, wait=True)

    tok = tok_ref[...]
    col = lax.broadcasted_iota(I32, (tm, MOE_LOCAL_ROWS), 1).astype(F32)
    rows = ybuf[slot]
    pick1 = jnp.where(tok[:, 2:3] == col, 1.0, 0.0).astype(BF16)
    pick2 = jnp.where(tok[:, 3:4] == col, 1.0, 0.0).astype(BF16)
    f = tok[:, 0:1] * _dot(pick1, rows) + tok[:, 1:2] * _dot(pick2, rows)
    y = ALPHA * x_ref[...] + mod_ref[5:6, :] * f
    res = _layer_norm(y, lg_ref[...], lb_ref[...])

    @pl.when(c < n_ctx_tiles)
    def _():
        oc_ref[...] = res

    @pl.when(c >= n_ctx_tiles)
    def _():
        ol_ref[...] = res


def _combine(tabs, tok, ys, x, mods, g, b, *, n_ctx, lat_len):
    nt = x.shape[0]
    tm = MOE_TOK_CHUNK
    nct = n_ctx // tm
    cmap = lambda c, *_: (_cond_index(c * tm, n_ctx, lat_len), 0, 0)
    grid_spec = pltpu.PrefetchScalarGridSpec(
        num_scalar_prefetch=4,
        grid=(nt // tm,),
        in_specs=[pl.BlockSpec((tm, LANES), lambda c, *_: (c, 0)),
                  pl.BlockSpec((tm, D_MODEL), lambda c, *_: (c, 0)),
                  pl.BlockSpec((None, 8, D_MODEL), cmap),
                  pl.BlockSpec((1, D_MODEL), lambda c, *_: (0, 0)),
                  pl.BlockSpec((1, D_MODEL), lambda c, *_: (0, 0)),
                  pl.BlockSpec(memory_space=pl.ANY)],
        out_specs=[pl.BlockSpec((tm, D_MODEL), lambda c, *_: (jnp.minimum(c, nct - 1), 0)),
                   pl.BlockSpec((tm, D_MODEL), lambda c, *_: (jnp.maximum(c - nct, 0), 0))],
        scratch_shapes=[pltpu.VMEM((2, MOE_LOCAL_ROWS, D_MODEL), BF16),
                        pltpu.SemaphoreType.DMA((2,))],
    )
    return pl.pallas_call(
        functools.partial(_combine_kernel, n_ctx_tiles=nct),
        grid_spec=grid_spec,
        out_shape=[jax.ShapeDtypeStruct((n_ctx, D_MODEL), F32),
                   jax.ShapeDtypeStruct((nt - n_ctx, D_MODEL), F32)],
        compiler_params=_cparams("arbitrary"),
        name="moe_combine",
    )(*tabs, tok, x, mods, g, b, ys)


def _moe(x, routed, mods, wg, wu, wd, ln_g, ln_b, *, n_ctx, lat_len):
    nt = x.shape[0]
    bs, tc = MOE_SLOT_TILE, MOE_TOK_CHUNK
    nchunk = nt // tc
    max_rows = 2 * nt + nchunk * N_EXPERTS * (MOE_CELL_ALIGN - 1)
    n_tiles = -(-max_rows // bs) + N_EXPERTS
    n_slots = n_tiles * bs

    hb, tok, lpt, capt = routed

    cap = capt[:, 0, :N_EXPERTS].astype(I32)
    a_loc = jnp.cumsum(cap, 1) - cap
    tot = jnp.sum(cap, 0)
    padded = ((tot + bs - 1) // bs) * bs
    gend = jnp.cumsum(padded)
    g_glob = (gend - padded)[None, :] + (jnp.cumsum(cap, 0) - cap)
    fill_cell = (MOE_LOCAL_ROWS // MOE_BIG_PIECE) * MOE_BIG_PIECE
    n_tail_cells = -(-(n_slots - 2 * nt) // fill_cell)
    n_fill = 1 + -(-n_tail_cells // N_EXPERTS)
    tail_k = jnp.arange((n_fill - 1) * N_EXPERTS, dtype=I32) * fill_cell
    tail_rows = jnp.clip(n_slots - gend[-1] - tail_k, 0, fill_cell)
    g_all = jnp.concatenate([g_glob.reshape(-1), gend - padded + tot, gend[-1] + tail_k])
    a_all = jnp.concatenate([a_loc.reshape(-1), jnp.zeros((n_fill * N_EXPERTS,), I32)])
    rows_all = jnp.concatenate([cap.reshape(-1), padded - tot, tail_rows])
    n_big = rows_all // MOE_BIG_PIECE
    n_small = (rows_all - n_big * MOE_BIG_PIECE) // MOE_CELL_ALIGN
    tabs = tuple(t.astype(I32) for t in (g_all, a_all, n_big, n_small))
    tile_start = jnp.arange(n_tiles, dtype=I32) * bs
    tile_expert = jnp.minimum(jnp.sum((gend[None, :] <= tile_start[:, None]).astype(I32), 1), N_EXPERTS - 1)
    n_used = (gend[-1] // bs).astype(I32).reshape(1)

    xs = _dispatch(tabs, hb, lpt, n_slots=n_slots, n_fill=n_fill)
    ys = _experts(tile_expert, n_used, xs, wg, wu, wd)
    return _combine(tabs, tok, ys, x, mods, ln_g, ln_b, n_ctx=n_ctx, lat_len=lat_len)


def _axial_angles(n_tok, dim):
    rows = n_tok // GRID_W
    row = jnp.repeat(jnp.arange(rows), GRID_W).astype(F32)
    col = jnp.tile(jnp.arange(GRID_W), rows).astype(F32)
    axis_dim = dim // 2
    inv = 1.0 / (ROPE_BASE ** (jnp.arange(0, axis_dim, 2, dtype=F32) / axis_dim))
    ang = jnp.concatenate([row[:, None] * inv, col[:, None] * inv], -1)
    return jnp.cos(ang), jnp.sin(ang)


def _mla_tables(n_lat):
    cos, sin = _axial_angles(n_lat, MLA_ROPE)
    one = jnp.ones((n_lat, MLA_NOPE), F32)
    zero = jnp.zeros((n_lat, MLA_NOPE), F32)
    pad1 = jnp.ones((n_lat, LANES - MLA_NOPE - MLA_ROPE), F32)
    pad0 = jnp.zeros((n_lat, LANES - MLA_NOPE - MLA_ROPE), F32)
    return (jnp.concatenate([one, cos, cos, pad1], -1), jnp.concatenate([zero, sin, sin, pad0], -1))


def _ret_tables(n_lat):
    cos, sin = _axial_angles(n_lat, RET_QK)
    return jnp.concatenate([cos, cos], -1), jnp.concatenate([-sin, sin], -1)


def _rot_cols(w):
    half = w.shape[1] // 2
    return jnp.concatenate([-w[:, half:], w[:, :half]], 1)


def _in0_weights(w_in0):
    z, xbc, dt, ql, kvl, kr = jnp.split(
        w_in0, [SSD_INNER, SSD_INNER + SSD_XBC, SSD_INNER + SSD_XBC + 2 * SSD_HEADS,
                SSD_INNER + SSD_XBC + 2 * SSD_HEADS + MLA_Q_RANK,
                SSD_INNER + SSD_XBC + 2 * SSD_HEADS + MLA_Q_RANK + MLA_KV_RANK], axis=1)
    dtp = jnp.pad(dt, ((0, 0), (0, LANES - 2 * SSD_HEADS)))
    lpad = ((0, 0), (MLA_NOPE, LANES - MLA_NOPE - MLA_ROPE))
    krp = jnp.concatenate([jnp.pad(kr, lpad), jnp.pad(_rot_cols(kr), lpad)], 1)
    return jnp.concatenate([z, xbc, dtp, ql, kvl, krp], 1).astype(BF16)


def _mla_weights(w_q_up, w_kv_up):
    d = w_q_up.shape[0]
    wq = w_q_up.reshape(d, MLA_HEADS, MLA_NOPE + MLA_ROPE)
    nope, rope = wq[..., :MLA_NOPE], wq[..., MLA_NOPE:]
    half = MLA_ROPE // 2
    rot = jnp.concatenate([-rope[..., half:], rope[..., :half]], -1)
    tail = jnp.zeros((d, MLA_HEADS, LANES - MLA_NOPE - MLA_ROPE), F32)
    wq_pad = jnp.concatenate([nope, rope, tail], -1).reshape(d, MLA_HEADS * LANES)
    wq_rot = jnp.concatenate([jnp.zeros_like(nope), rot, tail], -1).reshape(d, MLA_HEADS * LANES)
    r = w_kv_up.shape[0]
    wkv = w_kv_up.reshape(r, MLA_HEADS, MLA_NOPE + MLA_V)
    wk = jnp.concatenate([wkv[..., :MLA_NOPE], jnp.zeros((r, MLA_HEADS, LANES - MLA_NOPE), F32)], -1)
    wv = wkv[..., MLA_NOPE:]
    return (wq_pad.astype(BF16), wq_rot.astype(BF16),
            wk.reshape(r, MLA_HEADS * LANES).astype(BF16), wv.reshape(r, MLA_OUT).astype(BF16))


def _lane_row(v, width):
    return jnp.pad(v, (0, width - v.shape[0])).reshape(1, width)


def kernel(x_prompt, x_sample, cache_mla_ckv, cache_mla_krope, state_ssd_f, state_ssd_b, state_ret_f, state_ret_b, c, c_ctx, ada_w, ada_b, ln1_g, ln1_b, ln2_g, ln2_b, w_in0, ssd_conv_w, ssd_conv_b, ssd_a_log_f, ssd_a_log_b, ssd_dt_bias_f, ssd_dt_bias_b, ssd_d, ssd_norm_g, mla_q_norm_g, mla_w_q_up, mla_kv_norm_g, mla_w_kv_up, w_out0, ffn_w_gate, ffn_w_up, ffn_w_down, w_in1, ret_decay_f, ret_decay_b, w_out1, moe_router, moe_w_gate, moe_w_up, moe_w_down):
    bc, lc, _ = x_prompt.shape
    bl, ll, _ = x_sample.shape
    past = cache_mla_ckv.shape[2]
    n_ctx, n_lat = bc * lc, bl * ll
    geo = dict(n_ctx=n_ctx, lat_len=ll)

    x_parts = [x_prompt.reshape(n_ctx, D_MODEL), x_sample.reshape(n_lat, D_MODEL)]
    n_cond = 1 + bl
    cond = jnp.concatenate([c_ctx[None, :], c, jnp.zeros((-n_cond % 8, D_MODEL), F32)], 0)
    mods = _ada_vectors(cond, ada_w, ada_b)

    tm0 = 512
    wq_pad, wq_rot, wk_pad, wv = _mla_weights(mla_w_q_up[0], mla_w_kv_up[0])
    mla_w = (mla_q_norm_g[0].reshape(1, MLA_Q_RANK), wq_pad, wq_rot,
             mla_kv_norm_g[0].reshape(1, MLA_KV_RANK), wk_pad, wv)
    cos_l, sin_l = _mla_tables(ll)
    tables = (jnp.concatenate([jnp.ones((tm0, LANES), F32), cos_l], 0),
              jnp.concatenate([jnp.zeros((tm0, LANES), F32), sin_l], 0))
    z, xbc, dt, q_all, k_all, v_all, ckv_all, kr_all = _in0(
        x_parts, mods[0], _in0_weights(w_in0[0]), tables, mla_w, tm=tm0, **geo)

    cw = jnp.pad(ssd_conv_w[0], ((0, 8 - ssd_conv_w.shape[1]), (0, 0)))
    ssd_params = (cw, ssd_conv_b[0].reshape(1, SSD_XBC),
                  _lane_row(jnp.concatenate([ssd_dt_bias_f[0], ssd_dt_bias_b[0]]), LANES),
                  _lane_row(jnp.concatenate([ssd_a_log_f[0], ssd_a_log_b[0]]), LANES),
                  jnp.repeat(ssd_d[0], SSD_HEAD_DIM).reshape(1, SSD_INNER),
                  ssd_norm_g[0].reshape(1, SSD_INNER))

    def st_in(s):
        return jnp.transpose(s, (0, 3, 1, 2)).reshape(s.shape[0], SSD_STATE, SSD_INNER)

    def st_out(s):
        return jnp.transpose(s.reshape(s.shape[0], SSD_STATE, SSD_HEADS, SSD_HEAD_DIM), (0, 2, 3, 1))

    zero_ssd = jnp.zeros((bc, SSD_STATE, SSD_INNER), F32)
    y_ssd_c, ssd_f, ssd_b = _ssd(z, xbc, dt, zero_ssd, zero_ssd, ssd_params,
                                 row0=0, n_seq=bc, seq_len=lc, name="ssd_ctx")
    y_ssd_l, _, _ = _ssd(z, xbc, dt, st_in(state_ssd_f[:, 0]), st_in(state_ssd_b[:, 0]), ssd_params,
                         row0=n_ctx, n_seq=bl, seq_len=ll, name="ssd_lat")

    lpad = ((0, 0), (MLA_NOPE, 2 * LANES - MLA_NOPE - MLA_ROPE))
    k_p, v_p = _mla_prep(None, cache_mla_ckv[:, 0].reshape(bl * past, MLA_KV_RANK),
                         jnp.pad(cache_mla_krope[:, 0].reshape(bl * past, MLA_ROPE), lpad),
                         None, mla_w, row0=0, n_rows=bl * past,
                         do_q=False, do_norm=False, tm=256, name="mla_prep_cache")
    o_c = _attention(q_all, k_all, v_all, None, None, row0=0, n_seq=bc, seq_len=lc, cache_len=0,
                     tq=lc, n_pairs=MLA_HEADS // 2, name="attn_ctx")
    o_l = _attention(q_all, k_all, v_all, k_p, v_p, row0=n_ctx, n_seq=bl, seq_len=ll, cache_len=past,
                     tq=min(512, ll), n_pairs=1, name="attn_lat")

    w_out0_bf = w_out0[0].astype(BF16)
    x = _ffn([[y_ssd_c, y_ssd_l], [o_c, o_l]], [w_out0_bf[:SSD_INNER], w_out0_bf[SSD_INNER:]], x_parts, mods[0],
             ln1_g[0].reshape(1, D_MODEL), ln1_b[0].reshape(1, D_MODEL),
             ffn_w_gate[0].astype(BF16), ffn_w_up[0].astype(BF16), ffn_w_down[0].astype(BF16),
             ln2_g[0].reshape(1, D_MODEL), ln2_b[0].reshape(1, D_MODEL), tm=512, **geo)

    hq = RET_HEADS * RET_QK
    q1, k1, v1, g1 = _inproj([x], mods[1], w_in1[0].astype(BF16), (hq, hq, MIX1, MIX1), (F32, F32, BF16, BF16),
                             (False, False, False, True), shift_row=0, tm=256, name="in1", **geo)
    dec = jnp.stack([ret_decay_f[0], ret_decay_b[0]], 1)
    dec = jnp.broadcast_to(jnp.pad(dec, ((0, 0), (0, 6)))[:, :, None], (RET_HEADS, 8, RET_V))
    y_ret_c, ret_f, ret_b = _retention(q1, k1, v1, g1, None, dec, None,
                                       row0=0, n_seq=bc, seq_len=lc, hps=4, name="ret_ctx")
    y_ret_l, = _retention(q1, k1, v1, g1, _ret_tables(ll), dec, (state_ret_f[:, 0], state_ret_b[:, 0]),
                          row0=n_ctx, n_seq=bl, seq_len=ll, hps=2, name="ret_lat")
    router_w = jnp.pad(moe_router[0], ((0, 0), (0, LANES - N_EXPERTS)))
    x, *routed = _outproj([[y_ret_c, y_ret_l]], [w_out1[0].astype(BF16)], [x], mods[1],
                          ln1_g[1].reshape(1, D_MODEL), ln1_b[1].reshape(1, D_MODEL),
                          gate_row=2, tm=MOE_TOK_CHUNK, name="out1", router_w=router_w, **geo)
    y_c, y_l = _moe(x, routed, mods[1], moe_w_gate[0].astype(BF16), moe_w_up[0].astype(BF16),
                    moe_w_down[0].astype(BF16), ln2_g[1].reshape(1, D_MODEL), ln2_b[1].reshape(1, D_MODEL), **geo)

    y_prompt = y_c.reshape(bc, lc, D_MODEL)
    y_sample = y_l.reshape(bl, ll, D_MODEL)
    new_ckv = ckv_all[:n_ctx].reshape(bc, 1, lc, MLA_KV_RANK)
    new_krope = kr_all[:n_ctx, MLA_NOPE:MLA_NOPE + MLA_ROPE].reshape(bc, 1, lc, MLA_ROPE)
    return (y_prompt, y_sample, new_ckv, new_krope,
            st_out(ssd_f)[:, None], st_out(ssd_b)[:, None], ret_f[:, None], ret_b[:, None])
```

```python
import functools
import math

import jax
import jax.numpy as jnp
from jax import lax
from jax.experimental import pallas as pl
from jax.experimental.pallas import tpu as pltpu

F32 = jnp.float32
BF16 = jnp.bfloat16
I32 = jnp.int32

D_MODEL = 1024
DEPTH = 2
GRID_W = 64
CHUNK = 128
SSD_HEADS = 8
SSD_HEAD_DIM = 64
SSD_INNER = SSD_HEADS * SSD_HEAD_DIM
SSD_GROUPS = 2
SSD_STATE = 64
SSD_XBC = SSD_INNER + 2 * SSD_GROUPS * SSD_STATE
MLA_HEADS = 8
MLA_NOPE = 64
MLA_ROPE = 32
MLA_V = 64
MLA_Q_RANK = 384
MLA_KV_RANK = 256
MLA_OUT = MLA_HEADS * MLA_V
RET_HEADS = 8
RET_QK = 128
RET_V = 256
MIX1 = RET_HEADS * RET_V
D_FF = 2816
N_EXPERTS = 8
D_FF_EXPERT = 3584
ALPHA = (2 * DEPTH) ** 0.25
LN_EPS = 1e-5
RMS_EPS = 1e-6
ROPE_BASE = 10000.0

LANES = 128
VMEM_LIMIT = 56 * 1024 * 1024
NEG_BIG = -1e30

MOE_SLOT_TILE = 512
MOE_TOK_CHUNK = 512
MOE_FF_CHUNK = 1792
MOE_CELL_ALIGN = 16
MOE_BIG_PIECE = 64
MOE_LOCAL_ROWS = 2 * MOE_TOK_CHUNK + N_EXPERTS * MOE_CELL_ALIGN
FFN_CHUNK = 512
RET_CHUNK = 256

_NT = (((1,), (1,)), ((), ()))
_TN = (((0,), (0,)), ((), ()))


def _cparams(*sem):
    return pltpu.CompilerParams(dimension_semantics=sem, vmem_limit_bytes=VMEM_LIMIT)


def _resident(shape):
    nd = len(shape)
    return pl.BlockSpec(shape, lambda *_: (0,) * nd, pipeline_mode=pl.Buffered(1))


def _silu(x):
    return x * jax.nn.sigmoid(x)


def _dot(a, b):
    return jnp.dot(a, b, preferred_element_type=F32)


def _cond_index(row, n_ctx, lat_len):
    return jnp.where(row < n_ctx, 0, 1 + (row - n_ctx) // lat_len)


def _layer_norm(y, g, b):
    mu = jnp.mean(y, -1, keepdims=True)
    d = y - mu
    var = jnp.mean(d * d, -1, keepdims=True)
    return d * lax.rsqrt(var + LN_EPS) * g + b


def _ada_kernel(c_ref, w_ref, b_ref, o_ref):
    s = _silu(c_ref[...])
    o_ref[...] = jnp.dot(s, w_ref[...], precision=lax.Precision.HIGHEST,
                         preferred_element_type=F32) + b_ref[...]


def _ada_vectors(cond, ada_w, ada_b):
    r = cond.shape[0]
    tn = 1024
    out = pl.pallas_call(
        _ada_kernel,
        grid=(DEPTH, 6 * D_MODEL // tn),
        in_specs=[pl.BlockSpec((r, D_MODEL), lambda l, j: (0, 0)),
                  pl.BlockSpec((None, D_MODEL, tn), lambda l, j: (l, 0, j)),
                  pl.BlockSpec((None, 1, tn), lambda l, j: (l, 0, j))],
        out_specs=pl.BlockSpec((None, r, tn), lambda l, j: (l, 0, j)),
        out_shape=jax.ShapeDtypeStruct((DEPTH, r, 6 * D_MODEL), F32),
        compiler_params=_cparams("parallel", "parallel"),
        name="ada",
    )(cond, ada_w, ada_b.reshape(DEPTH, 1, 6 * D_MODEL))
    out = out.reshape(DEPTH, r, 6, D_MODEL)
    return jnp.pad(out, ((0, 0), (0, 0), (0, 2), (0, 0)))


def _row_specs(parts, tm):
    cols = parts[0].shape[1]
    if len(parts) == 1:
        return [pl.BlockSpec((tm, cols), lambda i, *_: (i, 0))]
    nct = parts[0].shape[0] // tm
    return [pl.BlockSpec((tm, cols), lambda i, *_: (jnp.minimum(i, nct - 1), 0)),
            pl.BlockSpec((tm, cols), lambda i, *_: (jnp.maximum(i - nct, 0), 0))]


def _row_tile(refs, n_ctx_tiles):
    if len(refs) == 1:
        return refs[0][...]
    return jnp.where(pl.program_id(0) < n_ctx_tiles, refs[0][...], refs[1][...])


def _inproj_kernel(*refs, n_x, n_ctx_tiles, segs, seg_silu, shift_row):
    x_refs, (mod_ref, w_ref), o_refs = refs[:n_x], refs[n_x:n_x + 2], refs[n_x + 2:]
    x = _row_tile(x_refs, n_ctx_tiles)
    h = x * (1.0 + mod_ref[shift_row + 1:shift_row + 2, :]) + mod_ref[shift_row:shift_row + 1, :]
    h = h.astype(BF16)
    for o_ref, (a, b), act in zip(o_refs, segs, seg_silu):
        y = _dot(h, w_ref[:, a:b])
        o_ref[...] = (_silu(y) if act else y).astype(o_ref.dtype)


def _inproj(x_parts, mods, w, seg_widths, seg_dtypes, seg_silu, *, shift_row, tm, n_ctx, lat_len, name):
    nt = sum(p.shape[0] for p in x_parts)
    segs, acc = [], 0
    for wd in seg_widths:
        segs.append((acc, acc + wd))
        acc += wd
    assert acc == w.shape[1]
    cmap = lambda i: (_cond_index(i * tm, n_ctx, lat_len), 0, 0)
    return pl.pallas_call(
        functools.partial(_inproj_kernel, n_x=len(x_parts), n_ctx_tiles=n_ctx // tm,
                          segs=tuple(segs), seg_silu=tuple(seg_silu), shift_row=shift_row),
        grid=(nt // tm,),
        in_specs=_row_specs(x_parts, tm) + [pl.BlockSpec((None, 8, D_MODEL), cmap), _resident(w.shape)],
        out_specs=[pl.BlockSpec((tm, wd), lambda i: (i, 0)) for wd in seg_widths],
        out_shape=[jax.ShapeDtypeStruct((nt, wd), dt) for wd, dt in zip(seg_widths, seg_dtypes)],
        compiler_params=_cparams("parallel"),
        name=name,
    )(*x_parts, mods, w)


def _cumsum_rows(tril_bf, x):
    hi = x.astype(BF16)
    r = x - hi.astype(F32)
    mid = r.astype(BF16)
    lo = (r - mid.astype(F32)).astype(BF16)
    return _dot(tril_bf, hi) + _dot(tril_bf, mid) + _dot(tril_bf, lo)


def _ssd_kernel(z_ref, xbc_ref, dt_ref, sf0_ref, sb0_ref, cw_ref, cb_ref, dtb_ref, alog_ref,
                dsk_ref, ng_ref, y_ref, sf_ref, sb_ref,
                yacc, xs_s, cm_s, bmt_s, xb_s, erb_s, *, seq_len):
    nc = seq_len // CHUNK
    hd, ns = SSD_HEAD_DIM, SSD_STATE
    gw = (SSD_HEADS // SSD_GROUPS) * hd
    ri = lax.broadcasted_iota(I32, (CHUNK, CHUNK), 0)
    ci = lax.broadcasted_iota(I32, (CHUNK, CHUNK), 1)
    lower = ri >= ci
    upper = ri <= ci
    tril_bf = jnp.where(lower, 1.0, 0.0).astype(BF16)
    rowid = lax.broadcasted_iota(I32, (CHUNK, 1), 0)
    lane = lax.broadcasted_iota(I32, (CHUNK, LANES), 1)
    src = lax.broadcasted_iota(I32, (LANES, SSD_INNER), 0)
    dst_head = lax.broadcasted_iota(I32, (LANES, SSD_INNER), 1) // hd
    spread_f = jnp.where(src == dst_head, 1.0, 0.0).astype(BF16)
    spread_b = jnp.where(src == dst_head + SSD_HEADS, 1.0, 0.0).astype(BF16)

    def per_head(v, spread, split=True):
        hi = v.astype(BF16)
        if not split:
            return _dot(hi, spread)
        lo = (v - hi.astype(F32)).astype(BF16)
        return _dot(hi, spread) + _dot(lo, spread)

    sf_ref[...] = sf0_ref[...]
    sb_ref[...] = sb0_ref[...]

    def fwd(c, carry):
        r0 = pl.multiple_of(c * CHUNK, CHUNK)
        rows = pl.ds(r0, CHUNK)
        cur = xbc_ref[rows, :]
        pstart = pl.multiple_of(jnp.maximum(r0 - 8, 0), 8)
        nstart = pl.multiple_of(jnp.minimum(r0 + CHUNK, seq_len - 8), 8)
        prev_row = xbc_ref[pl.ds(pstart, 8), :][7:8, :] * jnp.where(c > 0, 1.0, 0.0)
        next_row = xbc_ref[pl.ds(nstart, 8), :][0:1, :] * jnp.where(c < nc - 1, 1.0, 0.0)
        sh_prev = jnp.where(rowid == 0, prev_row, pltpu.roll(cur, 1, axis=0))
        sh_next = jnp.where(rowid == CHUNK - 1, next_row, pltpu.roll(cur, CHUNK - 1, axis=0))
        conv = cw_ref[0:1, :] * sh_prev + cw_ref[1:2, :] * cur + cw_ref[2:3, :] * sh_next + cb_ref[...]
        u = _silu(conv)
        xs = u[:, :SSD_INNER]
        bm = u[:, SSD_INNER:SSD_INNER + LANES]
        cm = u[:, SSD_INNER + LANES:]

        xr = dt_ref[rows, :] + dtb_ref[...]
        dt = jnp.maximum(xr, 0.0) + jnp.log1p(jnp.exp(-jnp.abs(xr)))
        la = -dt * jnp.exp(alog_ref[...])
        facs = _cumsum_rows(tril_bf, la)
        racs = facs[CHUNK - 1:CHUNK, :] - facs + la
        packed = jnp.where(lane < SSD_HEADS, facs,
                           jnp.where(lane < 2 * SSD_HEADS, racs, pltpu.roll(dt, 2 * SSD_HEADS, axis=1)))
        packed_t = packed.T
        e_f = jnp.exp(facs)
        e_r = jnp.exp(racs)
        w_f = dt * jnp.exp(facs[CHUNK - 1:CHUNK, :] - facs)
        w_b = dt * jnp.exp(racs[0:1, :] - racs)

        cm_bf = cm.astype(BF16)
        bm_bf = bm.astype(BF16)
        bmt_bf = bm.T.astype(BF16)
        xs_bf = xs.astype(BF16)
        ef_full = per_head(e_f, spread_f)
        xf = (xs * per_head(w_f, spread_f, split=False)).astype(BF16)
        xb_s[rows, :] = (xs * per_head(w_b, spread_b, split=False)).astype(BF16)
        yoffs, news, mixes = [], [], []
        for g in range(SSD_GROUPS):
            gl = slice(g * ns, (g + 1) * ns)
            s_g = lax.dot_general(cm_bf[:, gl], bm_bf[:, gl], _NT, preferred_element_type=F32)
            yoffs.append(_dot(cm_bf[:, gl], sf_ref[:, g * gw:(g + 1) * gw].astype(BF16)))
            news.append(_dot(bmt_bf[g * ns:(g + 1) * ns, :], xf[:, g * gw:(g + 1) * gw]))
            for hh in range(SSD_HEADS // SSD_GROUPS):
                h = g * (SSD_HEADS // SSD_GROUPS) + hh
                hb = SSD_HEADS + h
                seg_f = facs[:, h:h + 1] - packed_t[h:h + 1, :]
                seg_b = racs[:, hb:hb + 1] - packed_t[hb:hb + 1, :]
                d_f = jnp.exp(jnp.where(lower, seg_f, NEG_BIG))
                d_b = jnp.exp(jnp.where(upper, seg_b, NEG_BIG))
                dt_f_row = packed_t[2 * SSD_HEADS + h:2 * SSD_HEADS + h + 1, :]
                dt_b_row = packed_t[2 * SSD_HEADS + hb:2 * SSD_HEADS + hb + 1, :]
                mixes.append((s_g * (d_f * dt_f_row + d_b * dt_b_row)).astype(BF16))
        for p in range(SSD_HEADS // 2):
            ps = slice(p * LANES, (p + 1) * LANES)
            x_pair = xs_bf[:, ps]
            x_diag = jnp.concatenate([jnp.where(lane < hd, x_pair, jnp.zeros((), BF16)),
                                      jnp.where(lane >= hd, x_pair, jnp.zeros((), BF16))], 0)
            yacc[rows, ps] = _dot(jnp.concatenate([mixes[2 * p], mixes[2 * p + 1]], 1), x_diag)
        yacc[rows, :] += ef_full * jnp.concatenate(yoffs, -1)
        sf_ref[...] = ef_full[CHUNK - 1:CHUNK, :] * sf_ref[...] + jnp.concatenate(news, -1)
        xs_s[rows, :] = xs
        cm_s[rows, :] = cm_bf
        bmt_s[c] = bmt_bf
        erb_s[rows, :] = e_r
        return carry

    lax.fori_loop(0, nc, fwd, 0)

    def bwd(i, carry):
        c = nc - 1 - i
        r0 = pl.multiple_of(c * CHUNK, CHUNK)
        rows = pl.ds(r0, CHUNK)
        cm_bf = cm_s[rows, :]
        bmt_bf = bmt_s[c]
        er_full = per_head(erb_s[rows, :], spread_b)
        yoffs, news = [], []
        for g in range(SSD_GROUPS):
            gl = slice(g * ns, (g + 1) * ns)
            yoffs.append(_dot(cm_bf[:, gl], sb_ref[:, g * gw:(g + 1) * gw].astype(BF16)))
            news.append(_dot(bmt_bf[g * ns:(g + 1) * ns, :], xb_s[rows, g * gw:(g + 1) * gw]))
        sb_ref[...] = er_full[0:1, :] * sb_ref[...] + jnp.concatenate(news, -1)
        yv = yacc[rows, :] + er_full * jnp.concatenate(yoffs, -1) + dsk_ref[...] * xs_s[rows, :]
        gz = yv * z_ref[rows, :].astype(F32)
        ms = jnp.mean(gz * gz, -1, keepdims=True)
        y_ref[rows, :] = (gz * lax.rsqrt(ms + RMS_EPS) * ng_ref[...]).astype(y_ref.dtype)
        return carry

    lax.fori_loop(0, nc, bwd, 0)


def _ssd(z, xbc, dt, sf0, sb0, params, *, row0, n_seq, seq_len, name):
    cw, cb, dtb, alog, dsk, ng = params
    nc = seq_len // CHUNK
    blk0 = row0 // seq_len
    rmap = lambda b: (b + blk0, 0)
    smap = lambda b: (b, 0, 0)
    const = lambda b: (0, 0)
    st = jax.ShapeDtypeStruct((n_seq, SSD_STATE, SSD_INNER), F32)
    return pl.pallas_call(
        functools.partial(_ssd_kernel, seq_len=seq_len),
        grid=(n_seq,),
        in_specs=[pl.BlockSpec((seq_len, SSD_INNER), rmap),
                  pl.BlockSpec((seq_len, SSD_XBC), rmap),
                  pl.BlockSpec((seq_len, LANES), rmap),
                  pl.BlockSpec((None, SSD_STATE, SSD_INNER), smap),
                  pl.BlockSpec((None, SSD_STATE, SSD_INNER), smap),
                  pl.BlockSpec(cw.shape, const), pl.BlockSpec(cb.shape, const),
                  pl.BlockSpec(dtb.shape, const), pl.BlockSpec(alog.shape, const),
                  pl.BlockSpec(dsk.shape, const), pl.BlockSpec(ng.shape, const)],
        out_specs=[pl.BlockSpec((seq_len, SSD_INNER), lambda b: (b, 0)),
                   pl.BlockSpec((None, SSD_STATE, SSD_INNER), smap),
                   pl.BlockSpec((None, SSD_STATE, SSD_INNER), smap)],
        out_shape=[jax.ShapeDtypeStruct((n_seq * seq_len, SSD_INNER), BF16), st, st],
        scratch_shapes=[pltpu.VMEM((seq_len, SSD_INNER), F32),
                        pltpu.VMEM((seq_len, SSD_INNER), F32),
                        pltpu.VMEM((seq_len, LANES), BF16),
                        pltpu.VMEM((nc, LANES, CHUNK), BF16),
                        pltpu.VMEM((seq_len, SSD_INNER), BF16),
                        pltpu.VMEM((seq_len, LANES), F32)],
        compiler_params=_cparams("parallel"),
        name=name,
    )(z, xbc, dt, sf0, sb0, cw, cb, dtb, alog, dsk, ng)


def _rms(x, g):
    return x * lax.rsqrt(jnp.mean(x * x, -1, keepdims=True) + RMS_EPS) * g


def _mla_prep_kernel(*refs, do_q, do_norm, do_rope):
    it = iter(refs)
    qlat_ref = next(it) if do_q else None
    kv_ref = next(it)
    kr_ref = next(it)
    cos_ref = next(it) if do_rope else None
    sin_ref = next(it) if do_rope else None
    if do_q:
        gq_ref, wq_ref = next(it), next(it)
        wqr_ref = next(it) if do_rope else None
    gkv_ref = next(it) if do_norm else None
    wk_ref, wv_ref = next(it), next(it)
    q_out = next(it) if do_q else None
    k_out, v_out = next(it), next(it)
    ckv_out = next(it) if do_norm else None
    cs, sn = (cos_ref[...], sin_ref[...]) if do_rope else (None, None)
    _mla_project(qlat_ref[...] if do_q else None, kv_ref[...], kr_ref[...], cs, sn,
                 (gq_ref, wq_ref, wqr_ref) if do_q else None, gkv_ref, wk_ref, wv_ref,
                 q_out, k_out, v_out, ckv_out)


def _mla_project(qlat, kvlat, kr, cs, sn, q_w, gkv_ref, wk_ref, wv_ref, q_out, k_out, v_out, ckv_out):
    do_rope = cs is not None
    if q_w is not None:
        gq_ref, wq_ref, wqr_ref = q_w
        qn = _rms(qlat, gq_ref[...]).astype(BF16)
        qa = _dot(qn, wq_ref[...])
        if do_rope:
            qb = _dot(qn, wqr_ref[...])
        scale = (MLA_NOPE + MLA_ROPE) ** -0.5 * math.log2(math.e)
        for h in range(MLA_HEADS):
            hs = slice(h * LANES, (h + 1) * LANES)
            qh = qa[:, hs] * cs + qb[:, hs] * sn if do_rope else qa[:, hs]
            q_out[h] = (qh * scale).astype(BF16)
    ckv = kvlat
    if gkv_ref is not None:
        ckv = _rms(ckv, gkv_ref[...])
        ckv_out[...] = ckv
    ckv_bf = ckv.astype(BF16)
    krp = kr[:, :LANES] * cs + kr[:, LANES:] * sn if do_rope else kr[:, :LANES]
    kn = _dot(ckv_bf, wk_ref[...])
    for h in range(MLA_HEADS):
        k_out[h] = (kn[:, h * LANES:(h + 1) * LANES] + krp).astype(BF16)
    v_out[...] = _dot(ckv_bf, wv_ref[...]).astype(BF16)


def _mla_prep(qlat, kv, kr, tables, weights, *, row0, n_rows, do_q, do_norm, tm, name):
    do_rope = tables is not None
    gq, wq, wqr, gkv, wk, wv = weights
    b0 = row0 // tm
    rmap = lambda i: (i + b0, 0)
    omap = lambda i: (i, 0)
    hmap = lambda i: (0, i, 0)
    ins, specs = [], []

    def add(a, spec):
        ins.append(a)
        specs.append(spec)

    if do_q:
        add(qlat, pl.BlockSpec((tm, MLA_Q_RANK), rmap))
    add(kv, pl.BlockSpec((tm, MLA_KV_RANK), rmap))
    add(kr, pl.BlockSpec((tm, 2 * LANES), rmap))
    if do_rope:
        lat_tiles = tables[0].shape[0] // tm
        tmap = lambda i: (i % lat_tiles, 0)
        add(tables[0], pl.BlockSpec((tm, LANES), tmap))
        add(tables[1], pl.BlockSpec((tm, LANES), tmap))
    if do_q:
        add(gq, _resident(gq.shape))
        add(wq, _resident(wq.shape))
        if do_rope:
            add(wqr, _resident(wqr.shape))
    if do_norm:
        add(gkv, _resident(gkv.shape))
    add(wk, _resident(wk.shape))
    add(wv, _resident(wv.shape))
    out_shape, out_specs = [], []
    if do_q:
        out_shape.append(jax.ShapeDtypeStruct((MLA_HEADS, n_rows, LANES), BF16))
        out_specs.append(pl.BlockSpec((MLA_HEADS, tm, LANES), hmap))
    out_shape.append(jax.ShapeDtypeStruct((MLA_HEADS, n_rows, LANES), BF16))
    out_specs.append(pl.BlockSpec((MLA_HEADS, tm, LANES), hmap))
    out_shape.append(jax.ShapeDtypeStruct((n_rows, MLA_OUT), BF16))
    out_specs.append(pl.BlockSpec((tm, MLA_OUT), omap))
    if do_norm:
        out_shape.append(jax.ShapeDtypeStruct((n_rows, MLA_KV_RANK), F32))
        out_specs.append(pl.BlockSpec((tm, MLA_KV_RANK), omap))
    return pl.pallas_call(
        functools.partial(_mla_prep_kernel, do_q=do_q, do_norm=do_norm, do_rope=do_rope),
        grid=(n_rows // tm,),
        in_specs=specs, out_specs=out_specs, out_shape=out_shape,
        compiler_params=_cparams("parallel"),
        name=name,
    )(*ins)


_IN0_SEGS = (SSD_INNER, SSD_XBC, LANES, MLA_Q_RANK, MLA_KV_RANK, 2 * LANES)


def _in0_kernel(*refs, n_x, n_ctx_tiles):
    x_refs = refs[:n_x]
    (mod_ref, w_ref, cos_ref, sin_ref, gq_ref, wq_ref, wqr_ref, gkv_ref, wk_ref, wv_ref,
     z_out, xbc_out, dt_out, q_out, k_out, v_out, ckv_out, kr_out) = refs[n_x:]
    x = _row_tile(x_refs, n_ctx_tiles)
    h = (x * (1.0 + mod_ref[1:2, :]) + mod_ref[0:1, :]).astype(BF16)
    segs, acc = [], 0
    for wd in _IN0_SEGS:
        segs.append(_dot(h, w_ref[:, acc:acc + wd]))
        acc += wd
    z, xbc, dt, qlat, kvlat, kr = segs
    z_out[...] = _silu(z).astype(z_out.dtype)
    xbc_out[...] = xbc
    dt_out[...] = dt
    kr_out[...] = kr[:, :LANES]
    _mla_project(qlat, kvlat, kr, cos_ref[...], sin_ref[...], (gq_ref, wq_ref, wqr_ref), gkv_ref,
                 wk_ref, wv_ref, q_out, k_out, v_out, ckv_out)


def _in0(x_parts, mods, w, tables, mla_w, *, tm, n_ctx, lat_len):
    nt = sum(p.shape[0] for p in x_parts)
    nct = n_ctx // tm
    lat_tiles = lat_len // tm
    gq, wq, wqr, gkv, wk, wv = mla_w
    cmap = lambda i: (_cond_index(i * tm, n_ctx, lat_len), 0, 0)
    tmap = lambda i: (jnp.where(i < nct, 0, 1 + (i - nct) % lat_tiles), 0)
    row = lambda cols: pl.BlockSpec((tm, cols), lambda i: (i, 0))
    heads = pl.BlockSpec((MLA_HEADS, tm, LANES), lambda i: (0, i, 0))
    consts = [w, gq, wq, wqr, gkv, wk, wv]
    return pl.pallas_call(
        functools.partial(_in0_kernel, n_x=len(x_parts), n_ctx_tiles=nct),
        grid=(nt // tm,),
        in_specs=(_row_specs(x_parts, tm) + [pl.BlockSpec((None, 8, D_MODEL), cmap), _resident(w.shape),
                                             pl.BlockSpec((tm, LANES), tmap), pl.BlockSpec((tm, LANES), tmap)]
                  + [_resident(a.shape) for a in consts[1:]]),
        out_specs=[row(SSD_INNER), row(SSD_XBC), row(LANES), heads, heads, row(MLA_OUT),
                   row(MLA_KV_RANK), row(LANES)],
        out_shape=[jax.ShapeDtypeStruct((nt, SSD_INNER), BF16),
                   jax.ShapeDtypeStruct((nt, SSD_XBC), F32),
                   jax.ShapeDtypeStruct((nt, LANES), F32),
                   jax.ShapeDtypeStruct((MLA_HEADS, nt, LANES), BF16),
                   jax.ShapeDtypeStruct((MLA_HEADS, nt, LANES), BF16),
                   jax.ShapeDtypeStruct((nt, MLA_OUT), BF16),
                   jax.ShapeDtypeStruct((nt, MLA_KV_RANK), F32),
                   jax.ShapeDtypeStruct((nt, LANES), F32)],
        compiler_params=_cparams("parallel"),
        name="in0",
    )(*x_parts, mods, w, tables[0], tables[1], gq, wq, wqr, gkv, wk, wv)


def _attn_kernel(*refs, seq_len, cache_len, kblk, n_pairs):
    if cache_len:
        q_ref, k_ref, v_ref, kc_ref, vc_ref, o_ref = refs
    else:
        q_ref, k_ref, v_ref, o_ref = refs
    tq = q_ref.shape[1]
    blocks = [(k_ref, v_ref, i * kblk) for i in range(seq_len // kblk)]
    if cache_len:
        blocks += [(kc_ref, vc_ref, i * kblk) for i in range(cache_len // kblk)]
    vlane = lax.broadcasted_iota(I32, (kblk, LANES), 1)
    lane = lax.broadcasted_iota(I32, (tq, LANES), 1)
    for pp in range(n_pairs):
        ps = slice(pp * LANES, (pp + 1) * LANES)
        outs = []
        for hh in range(2):
            own = (vlane < MLA_V) if hh == 0 else (vlane >= MLA_V)
            q = q_ref[2 * pp + hh]
            m = None
            for kr, vr, off in blocks:
                s = lax.dot_general(q, kr[2 * pp + hh, off:off + kblk, :], _NT, preferred_element_type=F32)
                v_aug = jnp.where(own, vr[off:off + kblk, ps], jnp.ones((), BF16))
                bmax = jnp.max(s, -1, keepdims=True)
                if m is None:
                    m = bmax
                    acc = _dot(jnp.exp2(s - m).astype(BF16), v_aug)
                else:
                    m_new = jnp.maximum(m, bmax)
                    acc = jnp.exp2(m - m_new) * acc + _dot(jnp.exp2(s - m_new).astype(BF16), v_aug)
                    m = m_new
            outs.append(acc / pltpu.roll(acc, MLA_V, axis=1))
        o_ref[:, ps] = jnp.where(lane < MLA_V, outs[0], outs[1]).astype(o_ref.dtype)


def _attention(q, k, v, kc, vc, *, row0, n_seq, seq_len, cache_len, tq, n_pairs, name):
    nq = seq_len // tq
    kblk = min(512, seq_len)
    n = n_seq * seq_len
    q0, s0 = row0 // tq, row0 // seq_len
    hb, vw = 2 * n_pairs, n_pairs * LANES
    ins = [q, k, v]
    specs = [pl.BlockSpec((hb, tq, LANES), lambda b, hp, qi: (hp, q0 + b * nq + qi, 0)),
             pl.BlockSpec((hb, seq_len, LANES), lambda b, hp, qi: (hp, s0 + b, 0)),
             pl.BlockSpec((seq_len, vw), lambda b, hp, qi: (s0 + b, hp))]
    if cache_len:
        ins += [kc, vc]
        specs += [pl.BlockSpec((hb, cache_len, LANES), lambda b, hp, qi: (hp, b, 0)),
                  pl.BlockSpec((cache_len, vw), lambda b, hp, qi: (b, hp))]
    return pl.pallas_call(
        functools.partial(_attn_kernel, seq_len=seq_len, cache_len=cache_len, kblk=kblk, n_pairs=n_pairs),
        grid=(n_seq, MLA_HEADS // hb, nq),
        in_specs=specs,
        out_specs=pl.BlockSpec((tq, vw), lambda b, hp, qi: (b * nq + qi, hp)),
        out_shape=jax.ShapeDtypeStruct((n, MLA_OUT), BF16),
        compiler_params=_cparams("parallel", "parallel", "arbitrary"),
        name=name,
    )(*ins)


def _outproj_kernel(*refs, n_parts, n_ctx_tiles, gate_row, with_router):
    it = iter(refs)
    acc = None
    for n in n_parts[:-1]:
        a = _row_tile([next(it) for _ in range(n)], n_ctx_tiles)
        part = _dot(a, next(it)[...])
        acc = part if acc is None else acc + part
    x = _row_tile([next(it) for _ in range(n_parts[-1])], n_ctx_tiles)
    mod_ref, g_ref, b_ref = next(it), next(it), next(it)
    router_ref = next(it) if with_router else None
    o_ref = next(it)
    y = ALPHA * x + mod_ref[gate_row:gate_row + 1, :] * acc
    res = _layer_norm(y, g_ref[...], b_ref[...])
    o_ref[...] = res
    if with_router:
        _route(res, mod_ref, router_ref, *it)


def _outproj(acts, ws, x_parts, mods, g, b, *, gate_row, tm, n_ctx, lat_len, name, router_w=None):
    nt = sum(p.shape[0] for p in x_parts)
    cmap = lambda i: (_cond_index(i * tm, n_ctx, lat_len), 0, 0)
    ins, specs = [], []
    for parts, w in zip(acts, ws):
        ins += list(parts) + [w]
        specs += _row_specs(parts, tm) + [_resident(w.shape)]
    ins += list(x_parts) + [mods, g, b]
    specs += _row_specs(x_parts, tm) + [pl.BlockSpec((None, 8, D_MODEL), cmap),
                                        _resident(g.shape), _resident(b.shape)]
    out_specs = [pl.BlockSpec((tm, D_MODEL), lambda i: (i, 0))]
    out_shape = [jax.ShapeDtypeStruct((nt, D_MODEL), F32)]
    if router_w is not None:
        assert tm == MOE_TOK_CHUNK
        ins.append(router_w)
        specs.append(_resident(router_w.shape))
        out_specs += [pl.BlockSpec((tm, D_MODEL), lambda i: (i, 0)),
                      pl.BlockSpec((tm, LANES), lambda i: (i, 0)),
                      pl.BlockSpec((None, 8, tm), lambda i: (i, 0, 0)),
                      pl.BlockSpec((None, 8, LANES), lambda i: (i, 0, 0))]
        out_shape += [jax.ShapeDtypeStruct((nt, D_MODEL), BF16),
                      jax.ShapeDtypeStruct((nt, LANES), F32),
                      jax.ShapeDtypeStruct((nt // tm, 8, tm), F32),
                      jax.ShapeDtypeStruct((nt // tm, 8, LANES), F32)]
    n_parts = tuple(len(p) for p in acts) + (len(x_parts),)
    outs = pl.pallas_call(
        functools.partial(_outproj_kernel, n_parts=n_parts, n_ctx_tiles=n_ctx // tm, gate_row=gate_row,
                          with_router=router_w is not None),
        grid=(nt // tm,),
        in_specs=specs,
        out_specs=out_specs,
        out_shape=out_shape,
        compiler_params=_cparams("parallel"),
        name=name,
    )(*ins)
    return outs[0] if router_w is None else outs


def _ffn_kernel(*refs, n_parts, n_ctx_tiles, ff_chunks):
    it = iter(refs)
    mix = None
    for n in n_parts[:-1]:
        a = _row_tile([next(it) for _ in range(n)], n_ctx_tiles)
        part = _dot(a, next(it)[...])
        mix = part if mix is None else mix + part
    x0 = _row_tile([next(it) for _ in range(n_parts[-1])], n_ctx_tiles)
    mod_ref, g1_ref, b1_ref, wg_ref, wu_ref, wd_ref, g_ref, b_ref, o_ref = it
    x = _layer_norm(ALPHA * x0 + mod_ref[2:3, :] * mix, g1_ref[...], b1_ref[...])
    h = (x * (1.0 + mod_ref[4:5, :]) + mod_ref[3:4, :]).astype(BF16)
    acc = None
    for a, b in ff_chunks:
        gt = _dot(h, wg_ref[:, a:b])
        up = _dot(h, wu_ref[:, a:b])
        act = (_silu(gt) * up).astype(BF16)
        part = _dot(act, wd_ref[a:b, :])
        acc = part if acc is None else acc + part
    y = ALPHA * x + mod_ref[5:6, :] * acc
    o_ref[...] = _layer_norm(y, g_ref[...], b_ref[...])


def _ffn(acts, ws, x_parts, mods, g1, b1, wg, wu, wd, g, b, *, tm, n_ctx, lat_len):
    nt = sum(p.shape[0] for p in x_parts)
    ff = wg.shape[1]
    chunks, a = [], 0
    while a < ff:
        chunks.append((a, min(a + FFN_CHUNK, ff)))
        a += FFN_CHUNK
    cmap = lambda i: (_cond_index(i * tm, n_ctx, lat_len), 0, 0)
    ins, specs = [], []
    for parts, w in zip(acts, ws):
        ins += list(parts) + [w]
        specs += _row_specs(parts, tm) + [_resident(w.shape)]
    consts = [g1, b1, wg, wu, wd, g, b]
    ins += list(x_parts) + [mods] + consts
    specs += _row_specs(x_parts, tm) + [pl.BlockSpec((None, 8, D_MODEL), cmap)]
    specs += [_resident(a.shape) for a in consts]
    n_parts = tuple(len(p) for p in acts) + (len(x_parts),)
    return pl.pallas_call(
        functools.partial(_ffn_kernel, n_parts=n_parts, n_ctx_tiles=n_ctx // tm, ff_chunks=tuple(chunks)),
        grid=(nt // tm,),
        in_specs=specs,
        out_specs=pl.BlockSpec((tm, D_MODEL), lambda i: (i, 0)),
        out_shape=jax.ShapeDtypeStruct((nt, D_MODEL), F32),
        compiler_params=_cparams("parallel"),
        name="out0_ffn",
    )(*ins)


def _ret_kernel(*refs, seq_len, do_rope, n_heads, from_zero):
    it = iter(refs)
    q_ref, k_ref, v_ref, g_ref = next(it), next(it), next(it), next(it)
    cos_ref, sin_ref = (next(it), next(it)) if do_rope else (None, None)
    dec_ref = next(it)
    sf0_ref, sb0_ref = (None, None) if from_zero else (next(it), next(it))
    y_ref = next(it)
    sf_out, sb_out = (next(it), next(it)) if from_zero else (None, None)
    yacc, q_s, kb_s, dcomb_s, ev_s, wk_s, cd_s, sf_ref, sb_ref = it
    rc = RET_CHUNK
    nc = seq_len // rc
    unroll = min(2, nc)

    @pl.when(pl.program_id(1) == 0)
    def _():
        ri = lax.broadcasted_iota(I32, (rc, rc), 0)
        ci = lax.broadcasted_iota(I32, (rc, rc), 1)
        dij = (ri - ci).astype(F32)
        pos_k = lax.broadcasted_iota(I32, (rc, RET_QK), 0).astype(F32)
        pos_v = lax.broadcasted_iota(I32, (rc, RET_V), 0).astype(F32)
        for hh in range(n_heads):
            la_f = -jnp.exp(dec_ref[hh, 0:1, :])
            la_b = -jnp.exp(dec_ref[hh, 1:2, :])
            dcomb_s[hh] = (jnp.exp(jnp.where(ri >= ci, dij * la_f[:, :rc], NEG_BIG)) +
                           jnp.exp(jnp.where(ri <= ci, -dij * la_b[:, :rc], NEG_BIG)))
            ev_s[hh, 0] = jnp.exp((pos_v + 1.0) * la_f)
            ev_s[hh, 1] = jnp.exp((rc - pos_v) * la_b)
            wk_s[hh, 0] = jnp.exp((rc - 1.0 - pos_k) * la_f[:, :RET_QK])
            wk_s[hh, 1] = jnp.exp(pos_k * la_b[:, :RET_QK])
            cd_s[hh, 0:1, :] = jnp.exp(rc * la_f)
            cd_s[hh, 1:2, :] = jnp.exp(rc * la_b)

    if from_zero:
        sf_ref[...] = jnp.zeros_like(sf_ref)
        sb_ref[...] = jnp.zeros_like(sb_ref)
    else:
        for hh in range(n_heads):
            sf_ref[hh] = sf0_ref[hh].T
            sb_ref[hh] = sb0_ref[hh].T

    def fwd(c, carry):
        rows = pl.ds(pl.multiple_of(c * rc, rc), rc)
        if do_rope:
            cs, sn = cos_ref[rows, :], sin_ref[rows, :]
        for hh in range(n_heads):
            qs = slice(hh * RET_QK, (hh + 1) * RET_QK)
            vs = slice(hh * RET_V, (hh + 1) * RET_V)
            q = q_ref[rows, qs]
            k = k_ref[rows, qs] * (RET_QK ** -0.5)
            if do_rope:
                q = q * cs + pltpu.roll(q, RET_QK // 2, axis=1) * sn
                k = k * cs + pltpu.roll(k, RET_QK // 2, axis=1) * sn
            q_bf = q.astype(BF16)
            v = v_ref[rows, vs]
            s = lax.dot_general(q_bf, k.astype(BF16), _NT, preferred_element_type=F32)
            y = _dot((s * dcomb_s[hh]).astype(BF16), v)
            y = y + _dot(q_bf, sf_ref[hh].astype(BF16)) * ev_s[hh, 0]
            yacc[rows, vs] = y
            upd = lax.dot_general((k * wk_s[hh, 0]).astype(BF16), v, _TN, preferred_element_type=F32)
            sf_ref[hh] = cd_s[hh, 0:1, :] * sf_ref[hh] + upd
            q_s[rows, qs] = q_bf
            kb_s[rows, qs] = (k * wk_s[hh, 1]).astype(BF16)
        return carry

    lax.fori_loop(0, nc, fwd, 0, unroll=unroll)

    def bwd(i, carry):
        rows = pl.ds(pl.multiple_of((nc - 1 - i) * rc, rc), rc)
        for hh in range(n_heads):
            qs = slice(hh * RET_QK, (hh + 1) * RET_QK)
            vs = slice(hh * RET_V, (hh + 1) * RET_V)
            v = v_ref[rows, vs]
            y = yacc[rows, vs] + _dot(q_s[rows, qs], sb_ref[hh].astype(BF16)) * ev_s[hh, 1]
            upd = lax.dot_general(kb_s[rows, qs], v, _TN, preferred_element_type=F32)
            sb_ref[hh] = cd_s[hh, 1:2, :] * sb_ref[hh] + upd
            mu = jnp.mean(y, -1, keepdims=True)
            d = y - mu
            var = jnp.mean(d * d, -1, keepdims=True)
            yn = d * lax.rsqrt(var + LN_EPS)
            y_ref[rows, vs] = (yn * g_ref[rows, vs].astype(F32)).astype(y_ref.dtype)
        return carry

    lax.fori_loop(0, nc, bwd, 0, unroll=unroll)

    if from_zero:
        for hh in range(n_heads):
            sf_out[hh] = sf_ref[hh].T
            sb_out[hh] = sb_ref[hh].T


def _retention(q, k, v, g, tables, dec, states, *, row0, n_seq, seq_len, hps, name):
    do_rope = tables is not None
    from_zero = states is None
    blk0 = row0 // seq_len
    qmap = lambda h, b: (b + blk0, h)
    state_spec = pl.BlockSpec((None, hps, RET_V, RET_QK), lambda h, b: (b, h, 0, 0))
    ins = [q, k, v, g]
    specs = [pl.BlockSpec((seq_len, hps * RET_QK), qmap), pl.BlockSpec((seq_len, hps * RET_QK), qmap),
             pl.BlockSpec((seq_len, hps * RET_V), qmap), pl.BlockSpec((seq_len, hps * RET_V), qmap)]
    if do_rope:
        ins += list(tables)
        specs += [pl.BlockSpec((seq_len, RET_QK), lambda h, b: (0, 0))] * 2
    ins.append(dec)
    specs.append(pl.BlockSpec((hps, 8, RET_V), lambda h, b: (h, 0, 0)))
    out_specs = [pl.BlockSpec((seq_len, hps * RET_V), lambda h, b: (b, h))]
    out_shape = [jax.ShapeDtypeStruct((n_seq * seq_len, MIX1), BF16)]
    if from_zero:
        st = jax.ShapeDtypeStruct((n_seq, RET_HEADS, RET_V, RET_QK), F32)
        out_specs += [state_spec, state_spec]
        out_shape += [st, st]
    else:
        ins += list(states)
        specs += [state_spec, state_spec]
    rc = RET_CHUNK
    return pl.pallas_call(
        functools.partial(_ret_kernel, seq_len=seq_len, do_rope=do_rope, n_heads=hps, from_zero=from_zero),
        grid=(RET_HEADS // hps, n_seq),
        in_specs=specs,
        out_specs=out_specs,
        out_shape=out_shape,
        scratch_shapes=[pltpu.VMEM((seq_len, hps * RET_V), F32),
                        pltpu.VMEM((seq_len, hps * RET_QK), BF16),
                        pltpu.VMEM((seq_len, hps * RET_QK), BF16),
                        pltpu.VMEM((hps, rc, rc), F32),
                        pltpu.VMEM((hps, 2, rc, RET_V), F32),
                        pltpu.VMEM((hps, 2, rc, RET_QK), F32),
                        pltpu.VMEM((hps, 8, RET_V), F32),
                        pltpu.VMEM((hps, RET_QK, RET_V), F32),
                        pltpu.VMEM((hps, RET_QK, RET_V), F32)],
        compiler_params=_cparams("parallel", "arbitrary"),
        name=name,
    )(*ins)


def _route(x, mod_ref, w_ref, hb_ref, tok_ref, lpt_ref, cap_ref):
    tm = x.shape[0]
    h = x * (1.0 + mod_ref[4:5, :]) + mod_ref[3:4, :]
    h_hi = h.astype(BF16)
    hb_ref[...] = h_hi
    h_lo = (h - h_hi.astype(F32)).astype(BF16)
    w = w_ref[...]
    w_hi = w.astype(BF16)
    w_lo = (w - w_hi.astype(F32)).astype(BF16)
    w_both = w_hi + pltpu.roll(w_lo.astype(F32), N_EXPERTS, axis=1).astype(BF16)
    part = _dot(h_hi, w_both) + _dot(h_lo, w_both)
    logits = part + pltpu.roll(part, LANES - N_EXPERTS, axis=1)
    lane = lax.broadcasted_iota(I32, (tm, LANES), 1)
    logits = jnp.where(lane < N_EXPERTS, logits, NEG_BIG)
    m1 = jnp.max(logits, -1, keepdims=True)
    i1 = jnp.min(jnp.where(logits == m1, lane, LANES), -1, keepdims=True)
    rest = jnp.where(lane == i1, NEG_BIG, logits)
    m2 = jnp.max(rest, -1, keepdims=True)
    i2 = jnp.min(jnp.where(rest == m2, lane, LANES), -1, keepdims=True)
    e = jnp.exp(m2 - m1)
    g1 = 1.0 / (1.0 + e)
    g2 = e / (1.0 + e)
    sel1 = lane == i1
    sel2 = lane == i2
    onehot = jnp.where(sel1 | sel2, 1.0, 0.0)
    ri = lax.broadcasted_iota(I32, (tm, tm), 0)
    ci = lax.broadcasted_iota(I32, (tm, tm), 1)
    strict = jnp.where(ri > ci, 1.0, 0.0).astype(BF16)
    prefix = _dot(strict, onehot.astype(BF16))
    n_row = jnp.sum(onehot, 0, keepdims=True)
    cap_row = jnp.floor((n_row + (MOE_CELL_ALIGN - 1.0)) * (1.0 / MOE_CELL_ALIGN)) * MOE_CELL_ALIGN
    li = lax.broadcasted_iota(I32, (LANES, LANES), 0)
    lj = lax.broadcasted_iota(I32, (LANES, LANES), 1)
    lower_lanes = jnp.where(li < lj, 1.0, 0.0).astype(BF16)
    cap8 = jnp.broadcast_to(cap_row, (8, LANES))
    base_row = _dot(cap8.astype(BF16), lower_lanes)[0:1, :]
    local = prefix + base_row
    lpos1 = jnp.sum(jnp.where(sel1, local, 0.0), -1, keepdims=True)
    lpos2 = jnp.sum(jnp.where(sel2, local, 0.0), -1, keepdims=True)
    tok = jnp.where(lane == 0, g1,
          jnp.where(lane == 1, g2,
          jnp.where(lane == 2, lpos1,
          jnp.where(lane == 3, lpos2, 0.0))))
    tok_ref[...] = tok
    lpt_ref[...] = tok.T[0:8, :]
    cap_ref[...] = cap8


def _cell_copies(tabs, chunk, hbm_ref, buf_ref, slot, sem, *, to_local, wait):
    g_ref, a_ref, nbig_ref, nsmall_ref = tabs
    for e in range(N_EXPERTS):
        k = chunk * N_EXPERTS + e
        g0, a0, nbig, nsmall = g_ref[k], a_ref[k], nbig_ref[k], nsmall_ref[k]

        def piece(i, carry, rows, goff, aoff):
            g = pl.multiple_of(goff + i * rows, MOE_CELL_ALIGN)
            a = pl.multiple_of(aoff + i * rows, MOE_CELL_ALIGN)
            far = hbm_ref.at[pl.ds(g, rows), :]
            near = buf_ref.at[slot, pl.ds(a, rows), :]
            cp = (pltpu.make_async_copy(far, near, sem.at[slot]) if to_local
                  else pltpu.make_async_copy(near, far, sem.at[slot]))
            if wait:
                cp.wait()
            else:
                cp.start()
            return carry

        lax.fori_loop(0, nbig, functools.partial(piece, rows=MOE_BIG_PIECE, goff=g0, aoff=a0), 0)
        done = nbig * MOE_BIG_PIECE
        lax.fori_loop(0, nsmall, functools.partial(piece, rows=MOE_CELL_ALIGN, goff=g0 + done, aoff=a0 + done), 0)


def _dispatch_kernel(g_ref, a_ref, nbig_ref, nsmall_ref, hb_ref, lpt_ref, xs_hbm, ybuf, sem, *, n_fill):
    c = pl.program_id(0)
    nc = pl.num_programs(0)
    slot = c % 2
    tabs = (g_ref, a_ref, nbig_ref, nsmall_ref)

    @pl.when(c == 0)
    def _():
        ybuf[2] = jnp.zeros(ybuf.shape[1:], ybuf.dtype)
        for j in range(n_fill):
            _cell_copies(tabs, nc + j, xs_hbm, ybuf, 2, sem, to_local=False, wait=False)

    @pl.when(c == nc - 1)
    def _():
        for j in range(n_fill):
            _cell_copies(tabs, nc + j, xs_hbm, ybuf, 2, sem, to_local=False, wait=True)

    l1 = lpt_ref[2:3, :]
    l2 = lpt_ref[3:4, :]
    rb = MOE_LOCAL_ROWS // 3
    for r in range(3):
        rid = (lax.broadcasted_iota(I32, (rb, MOE_TOK_CHUNK), 0) + r * rb).astype(F32)
        onehot = jnp.where((l1 == rid) | (l2 == rid), 1.0, 0.0).astype(BF16)
        ybuf[slot, r * rb:(r + 1) * rb, :] = _dot(onehot, hb_ref[...]).astype(BF16)

    @pl.when(c > 0)
    def _():
        _cell_copies(tabs, c - 1, xs_hbm, ybuf, 1 - slot, sem, to_local=False, wait=True)

    _cell_copies(tabs, c, xs_hbm, ybuf, slot, sem, to_local=False, wait=False)

    @pl.when(c == nc - 1)
    def _():
        _cell_copies(tabs, c, xs_hbm, ybuf, slot, sem, to_local=False, wait=True)


def _dispatch(tabs, hb, lpt, *, n_slots, n_fill):
    tm = MOE_TOK_CHUNK
    grid_spec = pltpu.PrefetchScalarGridSpec(
        num_scalar_prefetch=4,
        grid=(hb.shape[0] // tm,),
        in_specs=[pl.BlockSpec((tm, D_MODEL), lambda c, *_: (c, 0)),
                  pl.BlockSpec((None, 8, tm), lambda c, *_: (c, 0, 0))],
        out_specs=pl.BlockSpec(memory_space=pl.ANY),
        scratch_shapes=[pltpu.VMEM((3, MOE_LOCAL_ROWS, D_MODEL), BF16),
                        pltpu.SemaphoreType.DMA((3,))],
    )
    return pl.pallas_call(
        functools.partial(_dispatch_kernel, n_fill=n_fill),
        grid_spec=grid_spec,
        out_shape=jax.ShapeDtypeStruct((n_slots, D_MODEL), BF16),
        compiler_params=_cparams("arbitrary"),
        name="moe_dispatch",
    )(*tabs, hb, lpt)


def _expert_kernel(te_ref, nu_ref, x_ref, wg_ref, wu_ref, wd_ref, o_ref, acc):
    t = pl.program_id(0)
    f = pl.program_id(1)
    nf = pl.num_programs(1)

    @pl.when(t < nu_ref[0])
    def _():
        x = x_ref[...]
        gt = _dot(x, wg_ref[...])
        up = _dot(x, wu_ref[...])
        part = _dot((_silu(gt) * up).astype(BF16), wd_ref[...])

        @pl.when(f == 0)
        def _():
            acc[...] = part

        @pl.when((f > 0) & (f < nf - 1))
        def _():
            acc[...] += part

        @pl.when(f == nf - 1)
        def _():
            o_ref[...] = (acc[...] + part).astype(o_ref.dtype)

    @pl.when((t >= nu_ref[0]) & (f == nf - 1))
    def _():
        o_ref[...] = jnp.zeros_like(o_ref)


def _experts(te, nu, xs, wg, wu, wd):
    bs = MOE_SLOT_TILE
    n_slots = xs.shape[0]
    nf = D_FF_EXPERT // MOE_FF_CHUNK

    def tt(t, nu):
        return jnp.minimum(t, nu[0] - 1)

    def ff(t, f, nu):
        return jnp.where(t < nu[0], f, nf - 1)

    grid_spec = pltpu.PrefetchScalarGridSpec(
        num_scalar_prefetch=2,
        grid=(n_slots // bs, nf),
        in_specs=[pl.BlockSpec((bs, D_MODEL), lambda t, f, te, nu: (tt(t, nu), 0)),
                  pl.BlockSpec((None, D_MODEL, MOE_FF_CHUNK), lambda t, f, te, nu: (te[tt(t, nu)], 0, ff(t, f, nu))),
                  pl.BlockSpec((None, D_MODEL, MOE_FF_CHUNK), lambda t, f, te, nu: (te[tt(t, nu)], 0, ff(t, f, nu))),
                  pl.BlockSpec((None, MOE_FF_CHUNK, D_MODEL), lambda t, f, te, nu: (te[tt(t, nu)], ff(t, f, nu), 0))],
        out_specs=pl.BlockSpec((bs, D_MODEL), lambda t, f, te, nu: (t, 0)),
        scratch_shapes=[pltpu.VMEM((bs, D_MODEL), F32)],
    )
    return pl.pallas_call(
        _expert_kernel,
        grid_spec=grid_spec,
        out_shape=jax.ShapeDtypeStruct((n_slots, D_MODEL), BF16),
        compiler_params=_cparams("arbitrary", "arbitrary"),
        name="moe_experts",
    )(te, nu, xs, wg, wu, wd)


def _combine_kernel(g_ref, a_ref, nbig_ref, nsmall_ref, tok_ref, lpt_ref, x_ref, mod_ref, lg_ref, lb_ref, ys_hbm,
                    oc_ref, ol_ref, ybuf, wbuf, sem, *, n_ctx_tiles):
    c = pl.program_id(0)
    nc = pl.num_programs(0)
    slot = c % 2
    tabs = (g_ref, a_ref, nbig_ref, nsmall_ref)
    tm = tok_ref.shape[0]

    @pl.when(c == 0)
    def _():
        ybuf[...] = jnp.zeros_like(ybuf)
        _cell_copies(tabs, 0, ys_hbm, ybuf, 0, sem, to_local=True, wait=False)

    @pl.when(c + 1 < nc)
    def _():
        _cell_copies(tabs, c + 1, ys_hbm, ybuf, 1 - slot, sem, to_local=True, wait=False)

    _cell_copies(tabs, c, ys_hbm, ybuf, slot, sem, to_local=True, wait=True)

    g1, g2, l1, l2 = lpt_ref[0:1, :], lpt_ref[1:2, :], lpt_ref[2:3, :], lpt_ref[3:4, :]
    rb = MOE_LOCAL_ROWS // 3
    for r in range(3):
        rid = (lax.broadcasted_iota(I32, (rb, tm), 0) + r * rb).astype(F32)
        gate = jnp.sum(jnp.where(l1 == rid, g1, 0.0) + jnp.where(l2 == rid, g2, 0.0), 1, keepdims=True)
        rs = slice(r * rb, (r + 1) * rb)
        wbuf[rs, :] = (ybuf[slot, rs, :].astype(F32) * gate).astype(BF16)
    tok = tok_ref[...]
    col = lax.broadcasted_iota(I32, (tm, MOE_LOCAL_ROWS), 1).astype(F32)
    pick = jnp.where((tok[:, 2:3] == col) | (tok[:, 3:4] == col), 1.0, 0.0).astype(BF16)
    f = _dot(pick, wbuf[...])
    y = ALPHA * x_ref[...] + mod_ref[5:6, :] * f
    res = _layer_norm(y, lg_ref[...], lb_ref[...])

    @pl.when(c < n_ctx_tiles)
    def _():
        oc_ref[...] = res

    @pl.when(c >= n_ctx_tiles)
    def _():
        ol_ref[...] = res


def _combine(tabs, tok, lpt, ys, x, mods, g, b, *, n_ctx, lat_len):
    nt = x.shape[0]
    tm = MOE_TOK_CHUNK
    nct = n_ctx // tm
    cmap = lambda c, *_: (_cond_index(c * tm, n_ctx, lat_len), 0, 0)
    grid_spec = pltpu.PrefetchScalarGridSpec(
        num_scalar_prefetch=4,
        grid=(nt // tm,),
        in_specs=[pl.BlockSpec((tm, LANES), lambda c, *_: (c, 0)),
                  pl.BlockSpec((None, 8, tm), lambda c, *_: (c, 0, 0)),
                  pl.BlockSpec((tm, D_MODEL), lambda c, *_: (c, 0)),
                  pl.BlockSpec((None, 8, D_MODEL), cmap),
                  pl.BlockSpec((1, D_MODEL), lambda c, *_: (0, 0)),
                  pl.BlockSpec((1, D_MODEL), lambda c, *_: (0, 0)),
                  pl.BlockSpec(memory_space=pl.ANY)],
        out_specs=[pl.BlockSpec((tm, D_MODEL), lambda c, *_: (jnp.minimum(c, nct - 1), 0)),
                   pl.BlockSpec((tm, D_MODEL), lambda c, *_: (jnp.maximum(c - nct, 0), 0))],
        scratch_shapes=[pltpu.VMEM((2, MOE_LOCAL_ROWS, D_MODEL), BF16),
                        pltpu.VMEM((MOE_LOCAL_ROWS, D_MODEL), BF16),
                        pltpu.SemaphoreType.DMA((2,))],
    )
    return pl.pallas_call(
        functools.partial(_combine_kernel, n_ctx_tiles=nct),
        grid_spec=grid_spec,
        out_shape=[jax.ShapeDtypeStruct((n_ctx, D_MODEL), F32),
                   jax.ShapeDtypeStruct((nt - n_ctx, D_MODEL), F32)],
        compiler_params=_cparams("arbitrary"),
        name="moe_combine",
    )(*tabs, tok, lpt, x, mods, g, b, ys)


def _moe(x, routed, mods, wg, wu, wd, ln_g, ln_b, *, n_ctx, lat_len):
    nt = x.shape[0]
    bs, tc = MOE_SLOT_TILE, MOE_TOK_CHUNK
    nchunk = nt // tc
    max_rows = 2 * nt + nchunk * N_EXPERTS * (MOE_CELL_ALIGN - 1)
    n_tiles = -(-max_rows // bs) + N_EXPERTS
    n_slots = n_tiles * bs

    hb, tok, lpt, capt = routed

    cap = capt[:, 0, :N_EXPERTS].astype(I32)
    a_loc = jnp.cumsum(cap, 1) - cap
    tot = jnp.sum(cap, 0)
    padded = ((tot + bs - 1) // bs) * bs
    gend = jnp.cumsum(padded)
    g_glob = (gend - padded)[None, :] + (jnp.cumsum(cap, 0) - cap)
    fill_cell = (MOE_LOCAL_ROWS // MOE_BIG_PIECE) * MOE_BIG_PIECE
    n_tail_cells = -(-(n_slots - 2 * nt) // fill_cell)
    n_fill = 1 + -(-n_tail_cells // N_EXPERTS)
    tail_k = jnp.arange((n_fill - 1) * N_EXPERTS, dtype=I32) * fill_cell
    tail_rows = jnp.clip(n_slots - gend[-1] - tail_k, 0, fill_cell)
    g_all = jnp.concatenate([g_glob.reshape(-1), gend - padded + tot, gend[-1] + tail_k])
    a_all = jnp.concatenate([a_loc.reshape(-1), jnp.zeros((n_fill * N_EXPERTS,), I32)])
    rows_all = jnp.concatenate([cap.reshape(-1), padded - tot, tail_rows])
    n_big = rows_all // MOE_BIG_PIECE
    n_small = (rows_all - n_big * MOE_BIG_PIECE) // MOE_CELL_ALIGN
    tabs = tuple(t.astype(I32) for t in (g_all, a_all, n_big, n_small))
    tile_start = jnp.arange(n_tiles, dtype=I32) * bs
    tile_expert = jnp.minimum(jnp.sum((gend[None, :] <= tile_start[:, None]).astype(I32), 1), N_EXPERTS - 1)
    n_used = (gend[-1] // bs).astype(I32).reshape(1)

    xs = _dispatch(tabs, hb, lpt, n_slots=n_slots, n_fill=n_fill)
    ys = _experts(tile_expert, n_used, xs, wg, wu, wd)
    return _combine(tabs, tok, lpt, ys, x, mods, ln_g, ln_b, n_ctx=n_ctx, lat_len=lat_len)


def _axial_angles(n_tok, dim):
    rows = n_tok // GRID_W
    row = jnp.repeat(jnp.arange(rows), GRID_W).astype(F32)
    col = jnp.tile(jnp.arange(GRID_W), rows).astype(F32)
    axis_dim = dim // 2
    inv = 1.0 / (ROPE_BASE ** (jnp.arange(0, axis_dim, 2, dtype=F32) / axis_dim))
    ang = jnp.concatenate([row[:, None] * inv, col[:, None] * inv], -1)
    return jnp.cos(ang), jnp.sin(ang)


def _mla_tables(n_lat):
    cos, sin = _axial_angles(n_lat, MLA_ROPE)
    one = jnp.ones((n_lat, MLA_NOPE), F32)
    zero = jnp.zeros((n_lat, MLA_NOPE), F32)
    pad1 = jnp.ones((n_lat, LANES - MLA_NOPE - MLA_ROPE), F32)
    pad0 = jnp.zeros((n_lat, LANES - MLA_NOPE - MLA_ROPE), F32)
    return (jnp.concatenate([one, cos, cos, pad1], -1), jnp.concatenate([zero, sin, sin, pad0], -1))


def _ret_tables(n_lat):
    cos, sin = _axial_angles(n_lat, RET_QK)
    return jnp.concatenate([cos, cos], -1), jnp.concatenate([-sin, sin], -1)


def _rot_cols(w):
    half = w.shape[1] // 2
    return jnp.concatenate([-w[:, half:], w[:, :half]], 1)


def _in0_weights(w_in0):
    z, xbc, dt, ql, kvl, kr = jnp.split(
        w_in0, [SSD_INNER, SSD_INNER + SSD_XBC, SSD_INNER + SSD_XBC + 2 * SSD_HEADS,
                SSD_INNER + SSD_XBC + 2 * SSD_HEADS + MLA_Q_RANK,
                SSD_INNER + SSD_XBC + 2 * SSD_HEADS + MLA_Q_RANK + MLA_KV_RANK], axis=1)
    dtp = jnp.pad(dt, ((0, 0), (0, LANES - 2 * SSD_HEADS)))
    lpad = ((0, 0), (MLA_NOPE, LANES - MLA_NOPE - MLA_ROPE))
    krp = jnp.concatenate([jnp.pad(kr, lpad), jnp.pad(_rot_cols(kr), lpad)], 1)
    return jnp.concatenate([z, xbc, dtp, ql, kvl, krp], 1).astype(BF16)


def _mla_weights(w_q_up, w_kv_up):
    d = w_q_up.shape[0]
    wq = w_q_up.reshape(d, MLA_HEADS, MLA_NOPE + MLA_ROPE)
    nope, rope = wq[..., :MLA_NOPE], wq[..., MLA_NOPE:]
    half = MLA_ROPE // 2
    rot = jnp.concatenate([-rope[..., half:], rope[..., :half]], -1)
    tail = jnp.zeros((d, MLA_HEADS, LANES - MLA_NOPE - MLA_ROPE), F32)
    wq_pad = jnp.concatenate([nope, rope, tail], -1).reshape(d, MLA_HEADS * LANES)
    wq_rot = jnp.concatenate([jnp.zeros_like(nope), rot, tail], -1).reshape(d, MLA_HEADS * LANES)
    r = w_kv_up.shape[0]
    wkv = w_kv_up.reshape(r, MLA_HEADS, MLA_NOPE + MLA_V)
    wk = jnp.concatenate([wkv[..., :MLA_NOPE], jnp.zeros((r, MLA_HEADS, LANES - MLA_NOPE), F32)], -1)
    wv = wkv[..., MLA_NOPE:]
    return (wq_pad.astype(BF16), wq_rot.astype(BF16),
            wk.reshape(r, MLA_HEADS * LANES).astype(BF16), wv.reshape(r, MLA_OUT).astype(BF16))


def _lane_row(v, width):
    return jnp.pad(v, (0, width - v.shape[0])).reshape(1, width)


def kernel(x_prompt, x_sample, cache_mla_ckv, cache_mla_krope, state_ssd_f, state_ssd_b, state_ret_f, state_ret_b, c, c_ctx, ada_w, ada_b, ln1_g, ln1_b, ln2_g, ln2_b, w_in0, ssd_conv_w, ssd_conv_b, ssd_a_log_f, ssd_a_log_b, ssd_dt_bias_f, ssd_dt_bias_b, ssd_d, ssd_norm_g, mla_q_norm_g, mla_w_q_up, mla_kv_norm_g, mla_w_kv_up, w_out0, ffn_w_gate, ffn_w_up, ffn_w_down, w_in1, ret_decay_f, ret_decay_b, w_out1, moe_router, moe_w_gate, moe_w_up, moe_w_down):
    bc, lc, _ = x_prompt.shape
    bl, ll, _ = x_sample.shape
    past = cache_mla_ckv.shape[2]
    n_ctx, n_lat = bc * lc, bl * ll
    geo = dict(n_ctx=n_ctx, lat_len=ll)

    x_parts = [x_prompt.reshape(n_ctx, D_MODEL), x_sample.reshape(n_lat, D_MODEL)]
    n_cond = 1 + bl
    cond = jnp.concatenate([c_ctx[None, :], c, jnp.zeros((-n_cond % 8, D_MODEL), F32)], 0)
    mods = _ada_vectors(cond, ada_w, ada_b)

    tm0 = 512
    wq_pad, wq_rot, wk_pad, wv = _mla_weights(mla_w_q_up[0], mla_w_kv_up[0])
    mla_w = (mla_q_norm_g[0].reshape(1, MLA_Q_RANK), wq_pad, wq_rot,
             mla_kv_norm_g[0].reshape(1, MLA_KV_RANK), wk_pad, wv)
    cos_l, sin_l = _mla_tables(ll)
    tables = (jnp.concatenate([jnp.ones((tm0, LANES), F32), cos_l], 0),
              jnp.concatenate([jnp.zeros((tm0, LANES), F32), sin_l], 0))
    z, xbc, dt, q_all, k_all, v_all, ckv_all, kr_all = _in0(
        x_parts, mods[0], _in0_weights(w_in0[0]), tables, mla_w, tm=tm0, **geo)

    cw = jnp.pad(ssd_conv_w[0], ((0, 8 - ssd_conv_w.shape[1]), (0, 0)))
    ssd_params = (cw, ssd_conv_b[0].reshape(1, SSD_XBC),
                  _lane_row(jnp.concatenate([ssd_dt_bias_f[0], ssd_dt_bias_b[0]]), LANES),
                  _lane_row(jnp.concatenate([ssd_a_log_f[0], ssd_a_log_b[0]]), LANES),
                  jnp.repeat(ssd_d[0], SSD_HEAD_DIM).reshape(1, SSD_INNER),
                  ssd_norm_g[0].reshape(1, SSD_INNER))

    def st_in(s):
        return jnp.transpose(s, (0, 3, 1, 2)).reshape(s.shape[0], SSD_STATE, SSD_INNER)

    def st_out(s):
        return jnp.transpose(s.reshape(s.shape[0], SSD_STATE, SSD_HEADS, SSD_HEAD_DIM), (0, 2, 3, 1))

    zero_ssd = jnp.zeros((bc, SSD_STATE, SSD_INNER), F32)
    y_ssd_c, ssd_f, ssd_b = _ssd(z, xbc, dt, zero_ssd, zero_ssd, ssd_params,
                                 row0=0, n_seq=bc, seq_len=lc, name="ssd_ctx")
    y_ssd_l, _, _ = _ssd(z, xbc, dt, st_in(state_ssd_f[:, 0]), st_in(state_ssd_b[:, 0]), ssd_params,
                         row0=n_ctx, n_seq=bl, seq_len=ll, name="ssd_lat")

    lpad = ((0, 0), (MLA_NOPE, 2 * LANES - MLA_NOPE - MLA_ROPE))
    k_p, v_p = _mla_prep(None, cache_mla_ckv[:, 0].reshape(bl * past, MLA_KV_RANK),
                         jnp.pad(cache_mla_krope[:, 0].reshape(bl * past, MLA_ROPE), lpad),
                         None, mla_w, row0=0, n_rows=bl * past,
                         do_q=False, do_norm=False, tm=256, name="mla_prep_cache")
    o_c = _attention(q_all, k_all, v_all, None, None, row0=0, n_seq=bc, seq_len=lc, cache_len=0,
                     tq=lc, n_pairs=MLA_HEADS // 2, name="attn_ctx")
    o_l = _attention(q_all, k_all, v_all, k_p, v_p, row0=n_ctx, n_seq=bl, seq_len=ll, cache_len=past,
                     tq=min(1024, ll), n_pairs=1, name="attn_lat")

    w_out0_bf = w_out0[0].astype(BF16)
    x = _ffn([[y_ssd_c, y_ssd_l], [o_c, o_l]], [w_out0_bf[:SSD_INNER], w_out0_bf[SSD_INNER:]], x_parts, mods[0],
             ln1_g[0].reshape(1, D_MODEL), ln1_b[0].reshape(1, D_MODEL),
             ffn_w_gate[0].astype(BF16), ffn_w_up[0].astype(BF16), ffn_w_down[0].astype(BF16),
             ln2_g[0].reshape(1, D_MODEL), ln2_b[0].reshape(1, D_MODEL), tm=512, **geo)

    hq = RET_HEADS * RET_QK
    q1, k1, v1, g1 = _inproj([x], mods[1], w_in1[0].astype(BF16), (hq, hq, MIX1, MIX1), (F32, F32, BF16, BF16),
                             (False, False, False, True), shift_row=0, tm=256, name="in1", **geo)
    dec = jnp.stack([ret_decay_f[0], ret_decay_b[0]], 1)
    dec = jnp.broadcast_to(jnp.pad(dec, ((0, 0), (0, 6)))[:, :, None], (RET_HEADS, 8, RET_V))
    y_ret_c, ret_f, ret_b = _retention(q1, k1, v1, g1, None, dec, None,
                                       row0=0, n_seq=bc, seq_len=lc, hps=4, name="ret_ctx")
    y_ret_l, = _retention(q1, k1, v1, g1, _ret_tables(ll), dec, (state_ret_f[:, 0], state_ret_b[:, 0]),
                          row0=n_ctx, n_seq=bl, seq_len=ll, hps=2, name="ret_lat")
    router_w = jnp.pad(moe_router[0], ((0, 0), (0, LANES - N_EXPERTS)))
    x, *routed = _outproj([[y_ret_c, y_ret_l]], [w_out1[0].astype(BF16)], [x], mods[1],
                          ln1_g[1].reshape(1, D_MODEL), ln1_b[1].reshape(1, D_MODEL),
                          gate_row=2, tm=MOE_TOK_CHUNK, name="out1", router_w=router_w, **geo)
    y_c, y_l = _moe(x, routed, mods[1], moe_w_gate[0].astype(BF16), moe_w_up[0].astype(BF16),
                    moe_w_down[0].astype(BF16), ln2_g[1].reshape(1, D_MODEL), ln2_b[1].reshape(1, D_MODEL), **geo)

    y_prompt = y_c.reshape(bc, lc, D_MODEL)
    y_sample = y_l.reshape(bl, ll, D_MODEL)
    new_ckv = ckv_all[:n_ctx].reshape(bc, 1, lc, MLA_KV_RANK)
    new_krope = kr_all[:n_ctx, MLA_NOPE:MLA_NOPE + MLA_ROPE].reshape(bc, 1, lc, MLA_ROPE)
    return (y_prompt, y_sample, new_ckv, new_krope,
            st_out(ssd_f)[:, None], st_out(ssd_b)[:, None], ret_f[:, None], ret_b[:, None])
```

```python
import functools
import math

import jax
import jax.numpy as jnp
from jax import lax
from jax.experimental import pallas as pl
from jax.experimental.pallas import tpu as pltpu

F32 = jnp.float32
BF16 = jnp.bfloat16
I32 = jnp.int32

D_MODEL = 1024
DEPTH = 2
GRID_W = 64
CHUNK = 128
SSD_HEADS = 8
SSD_HEAD_DIM = 64
SSD_INNER = SSD_HEADS * SSD_HEAD_DIM
SSD_GROUPS = 2
SSD_STATE = 64
SSD_XBC = SSD_INNER + 2 * SSD_GROUPS * SSD_STATE
MLA_HEADS = 8
MLA_NOPE = 64
MLA_ROPE = 32
MLA_V = 64
MLA_Q_RANK = 384
MLA_KV_RANK = 256
MLA_OUT = MLA_HEADS * MLA_V
RET_HEADS = 8
RET_QK = 128
RET_V = 256
MIX1 = RET_HEADS * RET_V
D_FF = 2816
N_EXPERTS = 8
D_FF_EXPERT = 3584
ALPHA = (2 * DEPTH) ** 0.25
LN_EPS = 1e-5
RMS_EPS = 1e-6
ROPE_BASE = 10000.0

LANES = 128
VMEM_LIMIT = 56 * 1024 * 1024
NEG_BIG = -1e30

MOE_SLOT_TILE = 512
MOE_TOK_CHUNK = 512
MOE_FF_CHUNK = 1792
MOE_CELL_ALIGN = 16
MOE_BIG_PIECE = 64
MOE_LOCAL_ROWS = 2 * MOE_TOK_CHUNK + N_EXPERTS * MOE_CELL_ALIGN
FFN_CHUNK = 512
RET_CHUNK = 256

_NT = (((1,), (1,)), ((), ()))
_TN = (((0,), (0,)), ((), ()))


def _cparams(*sem):
    return pltpu.CompilerParams(dimension_semantics=sem, vmem_limit_bytes=VMEM_LIMIT)


def _resident(shape):
    nd = len(shape)
    return pl.BlockSpec(shape, lambda *_: (0,) * nd, pipeline_mode=pl.Buffered(1))


def _silu(x):
    return x * jax.nn.sigmoid(x)


def _dot(a, b):
    return jnp.dot(a, b, preferred_element_type=F32)


def _cond_index(row, n_ctx, lat_len):
    return jnp.where(row < n_ctx, 0, 1 + (row - n_ctx) // lat_len)


def _layer_norm(y, g, b):
    mu = jnp.mean(y, -1, keepdims=True)
    d = y - mu
    var = jnp.mean(d * d, -1, keepdims=True)
    return d * lax.rsqrt(var + LN_EPS) * g + b


def _ada_kernel(c_ref, w_ref, b_ref, o_ref):
    s = _silu(c_ref[...])
    o_ref[...] = jnp.dot(s, w_ref[...], precision=lax.Precision.HIGHEST,
                         preferred_element_type=F32) + b_ref[...]


def _ada_vectors(cond, ada_w, ada_b):
    r = cond.shape[0]
    tn = 1024
    out = pl.pallas_call(
        _ada_kernel,
        grid=(DEPTH, 6 * D_MODEL // tn),
        in_specs=[pl.BlockSpec((r, D_MODEL), lambda l, j: (0, 0)),
                  pl.BlockSpec((None, D_MODEL, tn), lambda l, j: (l, 0, j)),
                  pl.BlockSpec((None, 1, tn), lambda l, j: (l, 0, j))],
        out_specs=pl.BlockSpec((None, r, tn), lambda l, j: (l, 0, j)),
        out_shape=jax.ShapeDtypeStruct((DEPTH, r, 6 * D_MODEL), F32),
        compiler_params=_cparams("parallel", "parallel"),
        name="ada",
    )(cond, ada_w, ada_b.reshape(DEPTH, 1, 6 * D_MODEL))
    out = out.reshape(DEPTH, r, 6, D_MODEL)
    return jnp.pad(out, ((0, 0), (0, 0), (0, 2), (0, 0)))


def _row_specs(parts, tm):
    cols = parts[0].shape[1]
    if len(parts) == 1:
        return [pl.BlockSpec((tm, cols), lambda i, *_: (i, 0))]
    nct = parts[0].shape[0] // tm
    return [pl.BlockSpec((tm, cols), lambda i, *_: (jnp.minimum(i, nct - 1), 0)),
            pl.BlockSpec((tm, cols), lambda i, *_: (jnp.maximum(i - nct, 0), 0))]


def _row_tile(refs, n_ctx_tiles):
    if len(refs) == 1:
        return refs[0][...]
    return jnp.where(pl.program_id(0) < n_ctx_tiles, refs[0][...], refs[1][...])


def _inproj_kernel(*refs, n_x, n_ctx_tiles, segs, seg_silu, shift_row):
    x_refs, (mod_ref, w_ref), o_refs = refs[:n_x], refs[n_x:n_x + 2], refs[n_x + 2:]
    x = _row_tile(x_refs, n_ctx_tiles)
    h = x * (1.0 + mod_ref[shift_row + 1:shift_row + 2, :]) + mod_ref[shift_row:shift_row + 1, :]
    h = h.astype(BF16)
    for o_ref, (a, b), act in zip(o_refs, segs, seg_silu):
        y = _dot(h, w_ref[:, a:b])
        o_ref[...] = (_silu(y) if act else y).astype(o_ref.dtype)


def _inproj(x_parts, mods, w, seg_widths, seg_dtypes, seg_silu, *, shift_row, tm, n_ctx, lat_len, name):
    nt = sum(p.shape[0] for p in x_parts)
    segs, acc = [], 0
    for wd in seg_widths:
        segs.append((acc, acc + wd))
        acc += wd
    assert acc == w.shape[1]
    cmap = lambda i: (_cond_index(i * tm, n_ctx, lat_len), 0, 0)
    return pl.pallas_call(
        functools.partial(_inproj_kernel, n_x=len(x_parts), n_ctx_tiles=n_ctx // tm,
                          segs=tuple(segs), seg_silu=tuple(seg_silu), shift_row=shift_row),
        grid=(nt // tm,),
        in_specs=_row_specs(x_parts, tm) + [pl.BlockSpec((None, 8, D_MODEL), cmap), _resident(w.shape)],
        out_specs=[pl.BlockSpec((tm, wd), lambda i: (i, 0)) for wd in seg_widths],
        out_shape=[jax.ShapeDtypeStruct((nt, wd), dt) for wd, dt in zip(seg_widths, seg_dtypes)],
        compiler_params=_cparams("parallel"),
        name=name,
    )(*x_parts, mods, w)


def _cumsum_rows(tril_bf, x):
    hi = x.astype(BF16)
    r = x - hi.astype(F32)
    mid = r.astype(BF16)
    lo = (r - mid.astype(F32)).astype(BF16)
    return _dot(tril_bf, hi) + _dot(tril_bf, mid) + _dot(tril_bf, lo)


def _ssd_kernel(z_ref, xbc_ref, dt_ref, sf0_ref, sb0_ref, cw_ref, cb_ref, dtb_ref, alog_ref,
                dsk_ref, ng_ref, y_ref, sf_ref, sb_ref,
                yacc, xs_s, cm_s, bmt_s, xb_s, erb_s, *, seq_len):
    nc = seq_len // CHUNK
    hd, ns = SSD_HEAD_DIM, SSD_STATE
    gw = (SSD_HEADS // SSD_GROUPS) * hd
    ri = lax.broadcasted_iota(I32, (CHUNK, CHUNK), 0)
    ci = lax.broadcasted_iota(I32, (CHUNK, CHUNK), 1)
    lower = ri >= ci
    upper = ri <= ci
    tril_bf = jnp.where(lower, 1.0, 0.0).astype(BF16)
    rowid = lax.broadcasted_iota(I32, (CHUNK, 1), 0)
    lane = lax.broadcasted_iota(I32, (CHUNK, LANES), 1)
    src = lax.broadcasted_iota(I32, (LANES, SSD_INNER), 0)
    dst_head = lax.broadcasted_iota(I32, (LANES, SSD_INNER), 1) // hd
    spread_f = jnp.where(src == dst_head, 1.0, 0.0).astype(BF16)
    spread_b = jnp.where(src == dst_head + SSD_HEADS, 1.0, 0.0).astype(BF16)

    def per_head(v, spread, split=True):
        hi = v.astype(BF16)
        if not split:
            return _dot(hi, spread)
        lo = (v - hi.astype(F32)).astype(BF16)
        return _dot(hi, spread) + _dot(lo, spread)

    sf_ref[...] = sf0_ref[...]
    sb_ref[...] = sb0_ref[...]

    def fwd(c, carry):
        r0 = pl.multiple_of(c * CHUNK, CHUNK)
        rows = pl.ds(r0, CHUNK)
        cur = xbc_ref[rows, :]
        pstart = pl.multiple_of(jnp.maximum(r0 - 8, 0), 8)
        nstart = pl.multiple_of(jnp.minimum(r0 + CHUNK, seq_len - 8), 8)
        prev_row = xbc_ref[pl.ds(pstart, 8), :][7:8, :] * jnp.where(c > 0, 1.0, 0.0)
        next_row = xbc_ref[pl.ds(nstart, 8), :][0:1, :] * jnp.where(c < nc - 1, 1.0, 0.0)
        sh_prev = jnp.where(rowid == 0, prev_row, pltpu.roll(cur, 1, axis=0))
        sh_next = jnp.where(rowid == CHUNK - 1, next_row, pltpu.roll(cur, CHUNK - 1, axis=0))
        conv = cw_ref[0:1, :] * sh_prev + cw_ref[1:2, :] * cur + cw_ref[2:3, :] * sh_next + cb_ref[...]
        u = _silu(conv)
        xs = u[:, :SSD_INNER]
        bm = u[:, SSD_INNER:SSD_INNER + LANES]
        cm = u[:, SSD_INNER + LANES:]

        xr = dt_ref[rows, :] + dtb_ref[...]
        dt = jnp.maximum(xr, 0.0) + jnp.log1p(jnp.exp(-jnp.abs(xr)))
        la = -dt * jnp.exp(alog_ref[...])
        facs = _cumsum_rows(tril_bf, la)
        racs = facs[CHUNK - 1:CHUNK, :] - facs + la
        packed = jnp.where(lane < SSD_HEADS, facs,
                           jnp.where(lane < 2 * SSD_HEADS, racs, pltpu.roll(dt, 2 * SSD_HEADS, axis=1)))
        packed_t = packed.T
        e_f = jnp.exp(facs)
        e_r = jnp.exp(racs)
        w_f = dt * jnp.exp(facs[CHUNK - 1:CHUNK, :] - facs)
        w_b = dt * jnp.exp(racs[0:1, :] - racs)

        cm_bf = cm.astype(BF16)
        bm_bf = bm.astype(BF16)
        bmt_bf = bm.T.astype(BF16)
        xs_bf = xs.astype(BF16)
        ef_full = per_head(e_f, spread_f)
        xf = (xs * per_head(w_f, spread_f, split=False)).astype(BF16)
        xb_s[rows, :] = (xs * per_head(w_b, spread_b, split=False)).astype(BF16)
        yoffs, news, mixes = [], [], []
        for g in range(SSD_GROUPS):
            gl = slice(g * ns, (g + 1) * ns)
            s_g = lax.dot_general(cm_bf[:, gl], bm_bf[:, gl], _NT, preferred_element_type=F32)
            yoffs.append(_dot(cm_bf[:, gl], sf_ref[:, g * gw:(g + 1) * gw].astype(BF16)))
            news.append(_dot(bmt_bf[g * ns:(g + 1) * ns, :], xf[:, g * gw:(g + 1) * gw]))
            for hh in range(SSD_HEADS // SSD_GROUPS):
                h = g * (SSD_HEADS // SSD_GROUPS) + hh
                hb = SSD_HEADS + h
                seg_f = facs[:, h:h + 1] - packed_t[h:h + 1, :]
                seg_b = racs[:, hb:hb + 1] - packed_t[hb:hb + 1, :]
                d_f = jnp.exp(jnp.where(lower, seg_f, NEG_BIG))
                d_b = jnp.exp(jnp.where(upper, seg_b, NEG_BIG))
                dt_f_row = packed_t[2 * SSD_HEADS + h:2 * SSD_HEADS + h + 1, :]
                dt_b_row = packed_t[2 * SSD_HEADS + hb:2 * SSD_HEADS + hb + 1, :]
                mixes.append((s_g * (d_f * dt_f_row + d_b * dt_b_row)).astype(BF16))
        for p in range(SSD_HEADS // 2):
            ps = slice(p * LANES, (p + 1) * LANES)
            x_pair = xs_bf[:, ps]
            x_diag = jnp.concatenate([jnp.where(lane < hd, x_pair, jnp.zeros((), BF16)),
                                      jnp.where(lane >= hd, x_pair, jnp.zeros((), BF16))], 0)
            yacc[rows, ps] = _dot(jnp.concatenate([mixes[2 * p], mixes[2 * p + 1]], 1), x_diag)
        yacc[rows, :] += ef_full * jnp.concatenate(yoffs, -1)
        sf_ref[...] = ef_full[CHUNK - 1:CHUNK, :] * sf_ref[...] + jnp.concatenate(news, -1)
        xs_s[rows, :] = xs
        cm_s[rows, :] = cm_bf
        bmt_s[c] = bmt_bf
        erb_s[rows, :] = e_r
        return carry

    lax.fori_loop(0, nc, fwd, 0)

    def bwd(i, carry):
        c = nc - 1 - i
        r0 = pl.multiple_of(c * CHUNK, CHUNK)
        rows = pl.ds(r0, CHUNK)
        cm_bf = cm_s[rows, :]
        bmt_bf = bmt_s[c]
        er_full = per_head(erb_s[rows, :], spread_b)
        yoffs, news = [], []
        for g in range(SSD_GROUPS):
            gl = slice(g * ns, (g + 1) * ns)
            yoffs.append(_dot(cm_bf[:, gl], sb_ref[:, g * gw:(g + 1) * gw].astype(BF16)))
            news.append(_dot(bmt_bf[g * ns:(g + 1) * ns, :], xb_s[rows, g * gw:(g + 1) * gw]))
        sb_ref[...] = er_full[0:1, :] * sb_ref[...] + jnp.concatenate(news, -1)
        yv = yacc[rows, :] + er_full * jnp.concatenate(yoffs, -1) + dsk_ref[...] * xs_s[rows, :]
        gz = yv * z_ref[rows, :].astype(F32)
        ms = jnp.mean(gz * gz, -1, keepdims=True)
        y_ref[rows, :] = (gz * lax.rsqrt(ms + RMS_EPS) * ng_ref[...]).astype(y_ref.dtype)
        return carry

    lax.fori_loop(0, nc, bwd, 0)


def _ssd(z, xbc, dt, sf0, sb0, params, *, row0, n_seq, seq_len, name):
    cw, cb, dtb, alog, dsk, ng = params
    nc = seq_len // CHUNK
    blk0 = row0 // seq_len
    rmap = lambda b: (b + blk0, 0)
    smap = lambda b: (b, 0, 0)
    const = lambda b: (0, 0)
    st = jax.ShapeDtypeStruct((n_seq, SSD_STATE, SSD_INNER), F32)
    return pl.pallas_call(
        functools.partial(_ssd_kernel, seq_len=seq_len),
        grid=(n_seq,),
        in_specs=[pl.BlockSpec((seq_len, SSD_INNER), rmap),
                  pl.BlockSpec((seq_len, SSD_XBC), rmap),
                  pl.BlockSpec((seq_len, LANES), rmap),
                  pl.BlockSpec((None, SSD_STATE, SSD_INNER), smap),
                  pl.BlockSpec((None, SSD_STATE, SSD_INNER), smap),
                  pl.BlockSpec(cw.shape, const), pl.BlockSpec(cb.shape, const),
                  pl.BlockSpec(dtb.shape, const), pl.BlockSpec(alog.shape, const),
                  pl.BlockSpec(dsk.shape, const), pl.BlockSpec(ng.shape, const)],
        out_specs=[pl.BlockSpec((seq_len, SSD_INNER), lambda b: (b, 0)),
                   pl.BlockSpec((None, SSD_STATE, SSD_INNER), smap),
                   pl.BlockSpec((None, SSD_STATE, SSD_INNER), smap)],
        out_shape=[jax.ShapeDtypeStruct((n_seq * seq_len, SSD_INNER), BF16), st, st],
        scratch_shapes=[pltpu.VMEM((seq_len, SSD_INNER), F32),
                        pltpu.VMEM((seq_len, SSD_INNER), F32),
                        pltpu.VMEM((seq_len, LANES), BF16),
                        pltpu.VMEM((nc, LANES, CHUNK), BF16),
                        pltpu.VMEM((seq_len, SSD_INNER), BF16),
                        pltpu.VMEM((seq_len, LANES), F32)],
        compiler_params=_cparams("parallel"),
        name=name,
    )(z, xbc, dt, sf0, sb0, cw, cb, dtb, alog, dsk, ng)


def _rms(x, g):
    return x * lax.rsqrt(jnp.mean(x * x, -1, keepdims=True) + RMS_EPS) * g


def _mla_prep_kernel(*refs, do_q, do_norm, do_rope):
    it = iter(refs)
    qlat_ref = next(it) if do_q else None
    kv_ref = next(it)
    kr_ref = next(it)
    cos_ref = next(it) if do_rope else None
    sin_ref = next(it) if do_rope else None
    if do_q:
        gq_ref, wq_ref = next(it), next(it)
        wqr_ref = next(it) if do_rope else None
    gkv_ref = next(it) if do_norm else None
    wk_ref, wv_ref = next(it), next(it)
    q_out = next(it) if do_q else None
    k_out, v_out = next(it), next(it)
    ckv_out = next(it) if do_norm else None
    cs, sn = (cos_ref[...], sin_ref[...]) if do_rope else (None, None)
    _mla_project(qlat_ref[...] if do_q else None, kv_ref[...], kr_ref[...], cs, sn,
                 (gq_ref, wq_ref, wqr_ref) if do_q else None, gkv_ref, wk_ref, wv_ref,
                 q_out, k_out, v_out, ckv_out)


def _mla_project(qlat, kvlat, kr, cs, sn, q_w, gkv_ref, wk_ref, wv_ref, q_out, k_out, v_out, ckv_out):
    do_rope = cs is not None
    if q_w is not None:
        gq_ref, wq_ref, wqr_ref = q_w
        qn = _rms(qlat, gq_ref[...]).astype(BF16)
        qa = _dot(qn, wq_ref[...])
        if do_rope:
            qb = _dot(qn, wqr_ref[...])
        scale = (MLA_NOPE + MLA_ROPE) ** -0.5 * math.log2(math.e)
        for h in range(MLA_HEADS):
            hs = slice(h * LANES, (h + 1) * LANES)
            qh = qa[:, hs] * cs + qb[:, hs] * sn if do_rope else qa[:, hs]
            q_out[h] = (qh * scale).astype(BF16)
    ckv = kvlat
    if gkv_ref is not None:
        ckv = _rms(ckv, gkv_ref[...])
        ckv_out[...] = ckv
    ckv_bf = ckv.astype(BF16)
    krp = kr[:, :LANES] * cs + kr[:, LANES:] * sn if do_rope else kr[:, :LANES]
    kn = _dot(ckv_bf, wk_ref[...])
    for h in range(MLA_HEADS):
        k_out[h] = (kn[:, h * LANES:(h + 1) * LANES] + krp).astype(BF16)
    v_out[...] = _dot(ckv_bf, wv_ref[...]).astype(BF16)


def _mla_prep(qlat, kv, kr, tables, weights, *, row0, n_rows, do_q, do_norm, tm, name):
    do_rope = tables is not None
    gq, wq, wqr, gkv, wk, wv = weights
    b0 = row0 // tm
    rmap = lambda i: (i + b0, 0)
    omap = lambda i: (i, 0)
    hmap = lambda i: (0, i, 0)
    ins, specs = [], []

    def add(a, spec):
        ins.append(a)
        specs.append(spec)

    if do_q:
        add(qlat, pl.BlockSpec((tm, MLA_Q_RANK), rmap))
    add(kv, pl.BlockSpec((tm, MLA_KV_RANK), rmap))
    add(kr, pl.BlockSpec((tm, 2 * LANES), rmap))
    if do_rope:
        lat_tiles = tables[0].shape[0] // tm
        tmap = lambda i: (i % lat_tiles, 0)
        add(tables[0], pl.BlockSpec((tm, LANES), tmap))
        add(tables[1], pl.BlockSpec((tm, LANES), tmap))
    if do_q:
        add(gq, _resident(gq.shape))
        add(wq, _resident(wq.shape))
        if do_rope:
            add(wqr, _resident(wqr.shape))
    if do_norm:
        add(gkv, _resident(gkv.shape))
    add(wk, _resident(wk.shape))
    add(wv, _resident(wv.shape))
    out_shape, out_specs = [], []
    if do_q:
        out_shape.append(jax.ShapeDtypeStruct((MLA_HEADS, n_rows, LANES), BF16))
        out_specs.append(pl.BlockSpec((MLA_HEADS, tm, LANES), hmap))
    out_shape.append(jax.ShapeDtypeStruct((MLA_HEADS, n_rows, LANES), BF16))
    out_specs.append(pl.BlockSpec((MLA_HEADS, tm, LANES), hmap))
    out_shape.append(jax.ShapeDtypeStruct((n_rows, MLA_OUT), BF16))
    out_specs.append(pl.BlockSpec((tm, MLA_OUT), omap))
    if do_norm:
        out_shape.append(jax.ShapeDtypeStruct((n_rows, MLA_KV_RANK), F32))
        out_specs.append(pl.BlockSpec((tm, MLA_KV_RANK), omap))
    return pl.pallas_call(
        functools.partial(_mla_prep_kernel, do_q=do_q, do_norm=do_norm, do_rope=do_rope),
        grid=(n_rows // tm,),
        in_specs=specs, out_specs=out_specs, out_shape=out_shape,
        compiler_params=_cparams("parallel"),
        name=name,
    )(*ins)


_IN0_SEGS = (SSD_INNER, SSD_XBC, LANES, MLA_Q_RANK, MLA_KV_RANK, 2 * LANES)


def _in0_kernel(*refs, n_x, n_ctx_tiles):
    x_refs = refs[:n_x]
    (mod_ref, w_ref, cos_ref, sin_ref, gq_ref, wq_ref, wqr_ref, gkv_ref, wk_ref, wv_ref,
     z_out, xbc_out, dt_out, q_out, k_out, v_out, ckv_out, kr_out) = refs[n_x:]
    x = _row_tile(x_refs, n_ctx_tiles)
    h = (x * (1.0 + mod_ref[1:2, :]) + mod_ref[0:1, :]).astype(BF16)
    segs, acc = [], 0
    for wd in _IN0_SEGS:
        segs.append(_dot(h, w_ref[:, acc:acc + wd]))
        acc += wd
    z, xbc, dt, qlat, kvlat, kr = segs
    z_out[...] = _silu(z).astype(z_out.dtype)
    xbc_out[...] = xbc
    dt_out[...] = dt
    kr_out[...] = kr[:, :LANES]
    _mla_project(qlat, kvlat, kr, cos_ref[...], sin_ref[...], (gq_ref, wq_ref, wqr_ref), gkv_ref,
                 wk_ref, wv_ref, q_out, k_out, v_out, ckv_out)


def _in0(x_parts, mods, w, tables, mla_w, *, tm, n_ctx, lat_len):
    nt = sum(p.shape[0] for p in x_parts)
    nct = n_ctx // tm
    lat_tiles = lat_len // tm
    gq, wq, wqr, gkv, wk, wv = mla_w
    cmap = lambda i: (_cond_index(i * tm, n_ctx, lat_len), 0, 0)
    tmap = lambda i: (jnp.where(i < nct, 0, 1 + (i - nct) % lat_tiles), 0)
    row = lambda cols: pl.BlockSpec((tm, cols), lambda i: (i, 0))
    heads = pl.BlockSpec((MLA_HEADS, tm, LANES), lambda i: (0, i, 0))
    consts = [w, gq, wq, wqr, gkv, wk, wv]
    return pl.pallas_call(
        functools.partial(_in0_kernel, n_x=len(x_parts), n_ctx_tiles=nct),
        grid=(nt // tm,),
        in_specs=(_row_specs(x_parts, tm) + [pl.BlockSpec((None, 8, D_MODEL), cmap), _resident(w.shape),
                                             pl.BlockSpec((tm, LANES), tmap), pl.BlockSpec((tm, LANES), tmap)]
                  + [_resident(a.shape) for a in consts[1:]]),
        out_specs=[row(SSD_INNER), row(SSD_XBC), row(LANES), heads, heads, row(MLA_OUT),
                   row(MLA_KV_RANK), row(LANES)],
        out_shape=[jax.ShapeDtypeStruct((nt, SSD_INNER), BF16),
                   jax.ShapeDtypeStruct((nt, SSD_XBC), F32),
                   jax.ShapeDtypeStruct((nt, LANES), F32),
                   jax.ShapeDtypeStruct((MLA_HEADS, nt, LANES), BF16),
                   jax.ShapeDtypeStruct((MLA_HEADS, nt, LANES), BF16),
                   jax.ShapeDtypeStruct((nt, MLA_OUT), BF16),
                   jax.ShapeDtypeStruct((nt, MLA_KV_RANK), F32),
                   jax.ShapeDtypeStruct((nt, LANES), F32)],
        compiler_params=_cparams("parallel"),
        name="in0",
    )(*x_parts, mods, w, tables[0], tables[1], gq, wq, wqr, gkv, wk, wv)


def _attn_kernel(*refs, seq_len, cache_len, kblk, n_pairs):
    if cache_len:
        q_ref, k_ref, v_ref, kc_ref, vc_ref, o_ref = refs
    else:
        q_ref, k_ref, v_ref, o_ref = refs
    tq = q_ref.shape[1]
    blocks = [(k_ref, v_ref, i * kblk) for i in range(seq_len // kblk)]
    if cache_len:
        blocks += [(kc_ref, vc_ref, i * kblk) for i in range(cache_len // kblk)]
    vlane = lax.broadcasted_iota(I32, (kblk, LANES), 1)
    lane = lax.broadcasted_iota(I32, (tq, LANES), 1)
    for pp in range(n_pairs):
        ps = slice(pp * LANES, (pp + 1) * LANES)
        outs = []
        for hh in range(2):
            own = (vlane < MLA_V) if hh == 0 else (vlane >= MLA_V)
            q = q_ref[2 * pp + hh]
            m = None
            for kr, vr, off in blocks:
                s = lax.dot_general(q, kr[2 * pp + hh, off:off + kblk, :], _NT, preferred_element_type=F32)
                v_aug = jnp.where(own, vr[off:off + kblk, ps], jnp.ones((), BF16))
                bmax = jnp.max(s, -1, keepdims=True)
                if m is None:
                    m = bmax
                    acc = _dot(jnp.exp2(s - m).astype(BF16), v_aug)
                else:
                    m_new = jnp.maximum(m, bmax)
                    acc = jnp.exp2(m - m_new) * acc + _dot(jnp.exp2(s - m_new).astype(BF16), v_aug)
                    m = m_new
            outs.append(acc / pltpu.roll(acc, MLA_V, axis=1))
        o_ref[:, ps] = jnp.where(lane < MLA_V, outs[0], outs[1]).astype(o_ref.dtype)


def _attention(q, k, v, kc, vc, *, row0, n_seq, seq_len, cache_len, tq, n_pairs, name):
    nq = seq_len // tq
    kblk = min(512, seq_len)
    n = n_seq * seq_len
    q0, s0 = row0 // tq, row0 // seq_len
    hb, vw = 2 * n_pairs, n_pairs * LANES
    ins = [q, k, v]
    specs = [pl.BlockSpec((hb, tq, LANES), lambda b, hp, qi: (hp, q0 + b * nq + qi, 0)),
             pl.BlockSpec((hb, seq_len, LANES), lambda b, hp, qi: (hp, s0 + b, 0)),
             pl.BlockSpec((seq_len, vw), lambda b, hp, qi: (s0 + b, hp))]
    if cache_len:
        ins += [kc, vc]
        specs += [pl.BlockSpec((hb, cache_len, LANES), lambda b, hp, qi: (hp, b, 0)),
                  pl.BlockSpec((cache_len, vw), lambda b, hp, qi: (b, hp))]
    return pl.pallas_call(
        functools.partial(_attn_kernel, seq_len=seq_len, cache_len=cache_len, kblk=kblk, n_pairs=n_pairs),
        grid=(n_seq, MLA_HEADS // hb, nq),
        in_specs=specs,
        out_specs=pl.BlockSpec((tq, vw), lambda b, hp, qi: (b * nq + qi, hp)),
        out_shape=jax.ShapeDtypeStruct((n, MLA_OUT), BF16),
        compiler_params=_cparams("parallel", "parallel", "arbitrary"),
        name=name,
    )(*ins)


def _outproj_kernel(*refs, n_parts, n_ctx_tiles, gate_row, with_router):
    it = iter(refs)
    acc = None
    for n in n_parts[:-1]:
        a = _row_tile([next(it) for _ in range(n)], n_ctx_tiles)
        part = _dot(a, next(it)[...])
        acc = part if acc is None else acc + part
    x = _row_tile([next(it) for _ in range(n_parts[-1])], n_ctx_tiles)
    mod_ref, g_ref, b_ref = next(it), next(it), next(it)
    router_ref = next(it) if with_router else None
    o_ref = next(it)
    y = ALPHA * x + mod_ref[gate_row:gate_row + 1, :] * acc
    res = _layer_norm(y, g_ref[...], b_ref[...])
    o_ref[...] = res
    if with_router:
        _route(res, mod_ref, router_ref, *it)


def _outproj(acts, ws, x_parts, mods, g, b, *, gate_row, tm, n_ctx, lat_len, name, router_w=None):
    nt = sum(p.shape[0] for p in x_parts)
    cmap = lambda i: (_cond_index(i * tm, n_ctx, lat_len), 0, 0)
    ins, specs = [], []
    for parts, w in zip(acts, ws):
        ins += list(parts) + [w]
        specs += _row_specs(parts, tm) + [_resident(w.shape)]
    ins += list(x_parts) + [mods, g, b]
    specs += _row_specs(x_parts, tm) + [pl.BlockSpec((None, 8, D_MODEL), cmap),
                                        _resident(g.shape), _resident(b.shape)]
    out_specs = [pl.BlockSpec((tm, D_MODEL), lambda i: (i, 0))]
    out_shape = [jax.ShapeDtypeStruct((nt, D_MODEL), F32)]
    if router_w is not None:
        assert tm == MOE_TOK_CHUNK
        ins.append(router_w)
        specs.append(_resident(router_w.shape))
        out_specs += [pl.BlockSpec((tm, D_MODEL), lambda i: (i, 0)),
                      pl.BlockSpec((tm, LANES), lambda i: (i, 0)),
                      pl.BlockSpec((None, 8, tm), lambda i: (i, 0, 0)),
                      pl.BlockSpec((None, 8, LANES), lambda i: (i, 0, 0))]
        out_shape += [jax.ShapeDtypeStruct((nt, D_MODEL), BF16),
                      jax.ShapeDtypeStruct((nt, LANES), F32),
                      jax.ShapeDtypeStruct((nt // tm, 8, tm), F32),
                      jax.ShapeDtypeStruct((nt // tm, 8, LANES), F32)]
    n_parts = tuple(len(p) for p in acts) + (len(x_parts),)
    outs = pl.pallas_call(
        functools.partial(_outproj_kernel, n_parts=n_parts, n_ctx_tiles=n_ctx // tm, gate_row=gate_row,
                          with_router=router_w is not None),
        grid=(nt // tm,),
        in_specs=specs,
        out_specs=out_specs,
        out_shape=out_shape,
        compiler_params=_cparams("parallel"),
        name=name,
    )(*ins)
    return outs[0] if router_w is None else outs


def _ffn_kernel(*refs, n_parts, n_ctx_tiles, ff_chunks):
    it = iter(refs)
    mix = None
    for n in n_parts[:-1]:
        a = _row_tile([next(it) for _ in range(n)], n_ctx_tiles)
        part = _dot(a, next(it)[...])
        mix = part if mix is None else mix + part
    x0 = _row_tile([next(it) for _ in range(n_parts[-1])], n_ctx_tiles)
    mod_ref, g1_ref, b1_ref, wg_ref, wu_ref, wd_ref, g_ref, b_ref, o_ref = it
    x = _layer_norm(ALPHA * x0 + mod_ref[2:3, :] * mix, g1_ref[...], b1_ref[...])
    h = (x * (1.0 + mod_ref[4:5, :]) + mod_ref[3:4, :]).astype(BF16)
    acc = None
    for a, b in ff_chunks:
        gt = _dot(h, wg_ref[:, a:b])
        up = _dot(h, wu_ref[:, a:b])
        act = (_silu(gt) * up).astype(BF16)
        part = _dot(act, wd_ref[a:b, :])
        acc = part if acc is None else acc + part
    y = ALPHA * x + mod_ref[5:6, :] * acc
    o_ref[...] = _layer_norm(y, g_ref[...], b_ref[...])


def _ffn(acts, ws, x_parts, mods, g1, b1, wg, wu, wd, g, b, *, tm, n_ctx, lat_len):
    nt = sum(p.shape[0] for p in x_parts)
    ff = wg.shape[1]
    chunks, a = [], 0
    while a < ff:
        chunks.append((a, min(a + FFN_CHUNK, ff)))
        a += FFN_CHUNK
    cmap = lambda i: (_cond_index(i * tm, n_ctx, lat_len), 0, 0)
    ins, specs = [], []
    for parts, w in zip(acts, ws):
        ins += list(parts) + [w]
        specs += _row_specs(parts, tm) + [_resident(w.shape)]
    consts = [g1, b1, wg, wu, wd, g, b]
    ins += list(x_parts) + [mods] + consts
    specs += _row_specs(x_parts, tm) + [pl.BlockSpec((None, 8, D_MODEL), cmap)]
    specs += [_resident(a.shape) for a in consts]
    n_parts = tuple(len(p) for p in acts) + (len(x_parts),)
    return pl.pallas_call(
        functools.partial(_ffn_kernel, n_parts=n_parts, n_ctx_tiles=n_ctx // tm, ff_chunks=tuple(chunks)),
        grid=(nt // tm,),
        in_specs=specs,
        out_specs=pl.BlockSpec((tm, D_MODEL), lambda i: (i, 0)),
        out_shape=jax.ShapeDtypeStruct((nt, D_MODEL), F32),
        compiler_params=_cparams("parallel"),
        name="out0_ffn",
    )(*ins)


def _ret_kernel(*refs, seq_len, do_rope, n_heads, from_zero):
    it = iter(refs)
    q_ref, k_ref, v_ref, g_ref = next(it), next(it), next(it), next(it)
    cos_ref, sin_ref = (next(it), next(it)) if do_rope else (None, None)
    dec_ref = next(it)
    sf0_ref, sb0_ref = (None, None) if from_zero else (next(it), next(it))
    y_ref = next(it)
    sf_out, sb_out = (next(it), next(it)) if from_zero else (None, None)
    yacc, q_s, kb_s, dcomb_s, ev_s, wk_s, cd_s, sf_ref, sb_ref = it
    rc = RET_CHUNK
    nc = seq_len // rc
    unroll = min(2, nc)

    @pl.when(pl.program_id(1) == 0)
    def _():
        ri = lax.broadcasted_iota(I32, (rc, rc), 0)
        ci = lax.broadcasted_iota(I32, (rc, rc), 1)
        dij = (ri - ci).astype(F32)
        pos_k = lax.broadcasted_iota(I32, (rc, RET_QK), 0).astype(F32)
        pos_v = lax.broadcasted_iota(I32, (rc, RET_V), 0).astype(F32)
        for hh in range(n_heads):
            la_f = -jnp.exp(dec_ref[hh, 0:1, :])
            la_b = -jnp.exp(dec_ref[hh, 1:2, :])
            dcomb_s[hh] = (jnp.exp(jnp.where(ri >= ci, dij * la_f[:, :rc], NEG_BIG)) +
                           jnp.exp(jnp.where(ri <= ci, -dij * la_b[:, :rc], NEG_BIG)))
            ev_s[hh, 0] = jnp.exp((pos_v + 1.0) * la_f)
            ev_s[hh, 1] = jnp.exp((rc - pos_v) * la_b)
            wk_s[hh, 0] = jnp.exp((rc - 1.0 - pos_k) * la_f[:, :RET_QK])
            wk_s[hh, 1] = jnp.exp(pos_k * la_b[:, :RET_QK])
            cd_s[hh, 0:1, :] = jnp.exp(rc * la_f)
            cd_s[hh, 1:2, :] = jnp.exp(rc * la_b)

    if from_zero:
        sf_ref[...] = jnp.zeros_like(sf_ref)
        sb_ref[...] = jnp.zeros_like(sb_ref)
    else:
        for hh in range(n_heads):
            sf_ref[hh] = sf0_ref[hh].T
            sb_ref[hh] = sb0_ref[hh].T

    def fwd(c, carry):
        rows = pl.ds(pl.multiple_of(c * rc, rc), rc)
        if do_rope:
            cs, sn = cos_ref[rows, :], sin_ref[rows, :]
        for hh in range(n_heads):
            qs = slice(hh * RET_QK, (hh + 1) * RET_QK)
            vs = slice(hh * RET_V, (hh + 1) * RET_V)
            q = q_ref[rows, qs]
            k = k_ref[rows, qs] * (RET_QK ** -0.5)
            if do_rope:
                q = q * cs + pltpu.roll(q, RET_QK // 2, axis=1) * sn
                k = k * cs + pltpu.roll(k, RET_QK // 2, axis=1) * sn
            q_bf = q.astype(BF16)
            v = v_ref[rows, vs]
            s = lax.dot_general(q_bf, k.astype(BF16), _NT, preferred_element_type=F32)
            y = _dot((s * dcomb_s[hh]).astype(BF16), v)
            y = y + _dot(q_bf, sf_ref[hh].astype(BF16)) * ev_s[hh, 0]
            yacc[rows, vs] = y
            upd = lax.dot_general((k * wk_s[hh, 0]).astype(BF16), v, _TN, preferred_element_type=F32)
            sf_ref[hh] = cd_s[hh, 0:1, :] * sf_ref[hh] + upd
            q_s[rows, qs] = q_bf
            kb_s[rows, qs] = (k * wk_s[hh, 1]).astype(BF16)
        return carry

    lax.fori_loop(0, nc, fwd, 0, unroll=unroll)

    def bwd(i, carry):
        rows = pl.ds(pl.multiple_of((nc - 1 - i) * rc, rc), rc)
        for hh in range(n_heads):
            qs = slice(hh * RET_QK, (hh + 1) * RET_QK)
            vs = slice(hh * RET_V, (hh + 1) * RET_V)
            v = v_ref[rows, vs]
            y = yacc[rows, vs] + _dot(q_s[rows, qs], sb_ref[hh].astype(BF16)) * ev_s[hh, 1]
            upd = lax.dot_general(kb_s[rows, qs], v, _TN, preferred_element_type=F32)
            sb_ref[hh] = cd_s[hh, 1:2, :] * sb_ref[hh] + upd
            mu = jnp.mean(y, -1, keepdims=True)
            d = y - mu
            var = jnp.mean(d * d, -1, keepdims=True)
            yn = d * lax.rsqrt(var + LN_EPS)
            y_ref[rows, vs] = (yn * g_ref[rows, vs].astype(F32)).astype(y_ref.dtype)
        return carry

    lax.fori_loop(0, nc, bwd, 0, unroll=unroll)

    if from_zero:
        for hh in range(n_heads):
            sf_out[hh] = sf_ref[hh].T
            sb_out[hh] = sb_ref[hh].T


def _retention(q, k, v, g, tables, dec, states, *, row0, n_seq, seq_len, hps, name):
    do_rope = tables is not None
    from_zero = states is None
    blk0 = row0 // seq_len
    qmap = lambda h, b: (b + blk0, h)
    state_spec = pl.BlockSpec((None, hps, RET_V, RET_QK), lambda h, b: (b, h, 0, 0))
    ins = [q, k, v, g]
    specs = [pl.BlockSpec((seq_len, hps * RET_QK), qmap), pl.BlockSpec((seq_len, hps * RET_QK), qmap),
             pl.BlockSpec((seq_len, hps * RET_V), qmap), pl.BlockSpec((seq_len, hps * RET_V), qmap)]
    if do_rope:
        ins += list(tables)
        specs += [pl.BlockSpec((seq_len, RET_QK), lambda h, b: (0, 0))] * 2
    ins.append(dec)
    specs.append(pl.BlockSpec((hps, 8, RET_V), lambda h, b: (h, 0, 0)))
    out_specs = [pl.BlockSpec((seq_len, hps * RET_V), lambda h, b: (b, h))]
    out_shape = [jax.ShapeDtypeStruct((n_seq * seq_len, MIX1), BF16)]
    if from_zero:
        st = jax.ShapeDtypeStruct((n_seq, RET_HEADS, RET_V, RET_QK), F32)
        out_specs += [state_spec, state_spec]
        out_shape += [st, st]
    else:
        ins += list(states)
        specs += [state_spec, state_spec]
    rc = RET_CHUNK
    return pl.pallas_call(
        functools.partial(_ret_kernel, seq_len=seq_len, do_rope=do_rope, n_heads=hps, from_zero=from_zero),
        grid=(RET_HEADS // hps, n_seq),
        in_specs=specs,
        out_specs=out_specs,
        out_shape=out_shape,
        scratch_shapes=[pltpu.VMEM((seq_len, hps * RET_V), F32),
                        pltpu.VMEM((seq_len, hps * RET_QK), BF16),
                        pltpu.VMEM((seq_len, hps * RET_QK), BF16),
                        pltpu.VMEM((hps, rc, rc), F32),
                        pltpu.VMEM((hps, 2, rc, RET_V), F32),
                        pltpu.VMEM((hps, 2, rc, RET_QK), F32),
                        pltpu.VMEM((hps, 8, RET_V), F32),
                        pltpu.VMEM((hps, RET_QK, RET_V), F32),
                        pltpu.VMEM((hps, RET_QK, RET_V), F32)],
        compiler_params=_cparams("parallel", "arbitrary"),
        name=name,
    )(*ins)


def _route(x, mod_ref, w_ref, hb_ref, tok_ref, lpt_ref, cap_ref):
    tm = x.shape[0]
    h = x * (1.0 + mod_ref[4:5, :]) + mod_ref[3:4, :]
    h_hi = h.astype(BF16)
    hb_ref[...] = h_hi
    h_lo = (h - h_hi.astype(F32)).astype(BF16)
    w = w_ref[...]
    w_hi = w.astype(BF16)
    w_lo = (w - w_hi.astype(F32)).astype(BF16)
    w_both = w_hi + pltpu.roll(w_lo.astype(F32), N_EXPERTS, axis=1).astype(BF16)
    part = _dot(h_hi, w_both) + _dot(h_lo, w_both)
    logits = part + pltpu.roll(part, LANES - N_EXPERTS, axis=1)
    lt = logits.T[0:N_EXPERTS, :]
    row = lax.broadcasted_iota(I32, (N_EXPERTS, tm), 0)
    m1 = jnp.max(lt, 0, keepdims=True)
    i1 = jnp.min(jnp.where(lt == m1, row, N_EXPERTS), 0, keepdims=True)
    rest = jnp.where(row == i1, NEG_BIG, lt)
    m2 = jnp.max(rest, 0, keepdims=True)
    i2 = jnp.min(jnp.where(rest == m2, row, N_EXPERTS), 0, keepdims=True)
    e = jnp.exp(m2 - m1)
    g1 = 1.0 / (1.0 + e)
    g2 = e / (1.0 + e)
    sel1 = row == i1
    sel2 = row == i2
    onehot = jnp.where(sel1 | sel2, 1.0, 0.0)
    ri = lax.broadcasted_iota(I32, (tm, tm), 0)
    ci = lax.broadcasted_iota(I32, (tm, tm), 1)
    earlier = jnp.where(ri < ci, 1.0, 0.0).astype(BF16)
    prefix = _dot(onehot.astype(BF16), earlier)
    n_col = jnp.sum(onehot, 1, keepdims=True)
    cap_col = jnp.floor((n_col + (MOE_CELL_ALIGN - 1.0)) * (1.0 / MOE_CELL_ALIGN)) * MOE_CELL_ALIGN
    sub = lax.broadcasted_iota(I32, (N_EXPERTS, 1), 0)
    base_col = jnp.zeros((N_EXPERTS, 1), F32)
    for ex in range(N_EXPERTS - 1):
        base_col = base_col + jnp.where(sub > ex, cap_col[ex:ex + 1, :], 0.0)
    local = prefix + base_col
    lpos1 = jnp.sum(jnp.where(sel1, local, 0.0), 0, keepdims=True)
    lpos2 = jnp.sum(jnp.where(sel2, local, 0.0), 0, keepdims=True)
    lpt = jnp.where(row == 0, g1,
          jnp.where(row == 1, g2,
          jnp.where(row == 2, lpos1,
          jnp.where(row == 3, lpos2, 0.0))))
    lpt_ref[...] = lpt
    tok_ref[...] = jnp.concatenate([lpt, jnp.zeros((LANES - N_EXPERTS, tm), F32)], 0).T
    elane = lax.broadcasted_iota(I32, (N_EXPERTS, LANES), 1)
    cap_row = jnp.sum(jnp.where(elane == sub, cap_col, 0.0), 0, keepdims=True)
    cap_ref[...] = jnp.broadcast_to(cap_row, cap_ref.shape)


def _cell_copies(tabs, chunk, hbm_ref, buf_ref, slot, sem, *, to_local, wait):
    g_ref, a_ref, nbig_ref, nsmall_ref = tabs
    for e in range(N_EXPERTS):
        k = chunk * N_EXPERTS + e
        g0, a0, nbig, nsmall = g_ref[k], a_ref[k], nbig_ref[k], nsmall_ref[k]

        def piece(i, carry, rows, goff, aoff):
            g = pl.multiple_of(goff + i * rows, MOE_CELL_ALIGN)
            a = pl.multiple_of(aoff + i * rows, MOE_CELL_ALIGN)
            far = hbm_ref.at[pl.ds(g, rows), :]
            near = buf_ref.at[slot, pl.ds(a, rows), :]
            cp = (pltpu.make_async_copy(far, near, sem.at[slot]) if to_local
                  else pltpu.make_async_copy(near, far, sem.at[slot]))
            if wait:
                cp.wait()
            else:
                cp.start()
            return carry

        lax.fori_loop(0, nbig, functools.partial(piece, rows=MOE_BIG_PIECE, goff=g0, aoff=a0), 0)
        done = nbig * MOE_BIG_PIECE
        lax.fori_loop(0, nsmall, functools.partial(piece, rows=MOE_CELL_ALIGN, goff=g0 + done, aoff=a0 + done), 0)


def _dispatch_kernel(g_ref, a_ref, nbig_ref, nsmall_ref, hb_ref, lpt_ref, xs_hbm, ybuf, sem, *, n_fill):
    c = pl.program_id(0)
    nc = pl.num_programs(0)
    slot = c % 2
    tabs = (g_ref, a_ref, nbig_ref, nsmall_ref)

    @pl.when(c == 0)
    def _():
        ybuf[2] = jnp.zeros(ybuf.shape[1:], ybuf.dtype)
        for j in range(n_fill):
            _cell_copies(tabs, nc + j, xs_hbm, ybuf, 2, sem, to_local=False, wait=False)

    @pl.when(c == nc - 1)
    def _():
        for j in range(n_fill):
            _cell_copies(tabs, nc + j, xs_hbm, ybuf, 2, sem, to_local=False, wait=True)

    l1 = lpt_ref[2:3, :]
    l2 = lpt_ref[3:4, :]
    rb = MOE_LOCAL_ROWS // 3
    for r in range(3):
        rid = (lax.broadcasted_iota(I32, (rb, MOE_TOK_CHUNK), 0) + r * rb).astype(F32)
        onehot = jnp.where((l1 == rid) | (l2 == rid), 1.0, 0.0).astype(BF16)
        ybuf[slot, r * rb:(r + 1) * rb, :] = _dot(onehot, hb_ref[...]).astype(BF16)

    @pl.when(c > 0)
    def _():
        _cell_copies(tabs, c - 1, xs_hbm, ybuf, 1 - slot, sem, to_local=False, wait=True)

    _cell_copies(tabs, c, xs_hbm, ybuf, slot, sem, to_local=False, wait=False)

    @pl.when(c == nc - 1)
    def _():
        _cell_copies(tabs, c, xs_hbm, ybuf, slot, sem, to_local=False, wait=True)


def _dispatch(tabs, hb, lpt, *, n_slots, n_fill):
    tm = MOE_TOK_CHUNK
    grid_spec = pltpu.PrefetchScalarGridSpec(
        num_scalar_prefetch=4,
        grid=(hb.shape[0] // tm,),
        in_specs=[pl.BlockSpec((tm, D_MODEL), lambda c, *_: (c, 0)),
                  pl.BlockSpec((None, 8, tm), lambda c, *_: (c, 0, 0))],
        out_specs=pl.BlockSpec(memory_space=pl.ANY),
        scratch_shapes=[pltpu.VMEM((3, MOE_LOCAL_ROWS, D_MODEL), BF16),
                        pltpu.SemaphoreType.DMA((3,))],
    )
    return pl.pallas_call(
        functools.partial(_dispatch_kernel, n_fill=n_fill),
        grid_spec=grid_spec,
        out_shape=jax.ShapeDtypeStruct((n_slots, D_MODEL), BF16),
        compiler_params=_cparams("arbitrary"),
        name="moe_dispatch",
    )(*tabs, hb, lpt)


def _expert_kernel(te_ref, nu_ref, x_ref, wg_ref, wu_ref, wd_ref, o_ref, acc):
    t = pl.program_id(0)
    f = pl.program_id(1)
    nf = pl.num_programs(1)

    @pl.when(t < nu_ref[0])
    def _():
        x = x_ref[...]
        gt = _dot(x, wg_ref[...])
        up = _dot(x, wu_ref[...])
        part = _dot((_silu(gt) * up).astype(BF16), wd_ref[...])

        @pl.when(f == 0)
        def _():
            acc[...] = part

        @pl.when((f > 0) & (f < nf - 1))
        def _():
            acc[...] += part

        @pl.when(f == nf - 1)
        def _():
            o_ref[...] = (acc[...] + part).astype(o_ref.dtype)

    @pl.when((t >= nu_ref[0]) & (f == nf - 1))
    def _():
        o_ref[...] = jnp.zeros_like(o_ref)


def _experts(te, nu, xs, wg, wu, wd):
    bs = MOE_SLOT_TILE
    n_slots = xs.shape[0]
    nf = D_FF_EXPERT // MOE_FF_CHUNK

    def tt(t, nu):
        return jnp.minimum(t, nu[0] - 1)

    def ff(t, f, nu):
        return jnp.where(t < nu[0], f, nf - 1)

    grid_spec = pltpu.PrefetchScalarGridSpec(
        num_scalar_prefetch=2,
        grid=(n_slots // bs, nf),
        in_specs=[pl.BlockSpec((bs, D_MODEL), lambda t, f, te, nu: (tt(t, nu), 0)),
                  pl.BlockSpec((None, D_MODEL, MOE_FF_CHUNK), lambda t, f, te, nu: (te[tt(t, nu)], 0, ff(t, f, nu))),
                  pl.BlockSpec((None, D_MODEL, MOE_FF_CHUNK), lambda t, f, te, nu: (te[tt(t, nu)], 0, ff(t, f, nu))),
                  pl.BlockSpec((None, MOE_FF_CHUNK, D_MODEL), lambda t, f, te, nu: (te[tt(t, nu)], ff(t, f, nu), 0))],
        out_specs=pl.BlockSpec((bs, D_MODEL), lambda t, f, te, nu: (t, 0)),
        scratch_shapes=[pltpu.VMEM((bs, D_MODEL), F32)],
    )
    return pl.pallas_call(
        _expert_kernel,
        grid_spec=grid_spec,
        out_shape=jax.ShapeDtypeStruct((n_slots, D_MODEL), BF16),
        compiler_params=_cparams("arbitrary", "arbitrary"),
        name="moe_experts",
    )(te, nu, xs, wg, wu, wd)


def _combine_kernel(g_ref, a_ref, nbig_ref, nsmall_ref, tok_ref, lpt_ref, x_ref, mod_ref, lg_ref, lb_ref, ys_hbm,
                    oc_ref, ol_ref, ybuf, wbuf, sem, *, n_ctx_tiles):
    c = pl.program_id(0)
    nc = pl.num_programs(0)
    slot = c % 2
    tabs = (g_ref, a_ref, nbig_ref, nsmall_ref)
    tm = tok_ref.shape[0]

    @pl.when(c == 0)
    def _():
        ybuf[...] = jnp.zeros_like(ybuf)
        _cell_copies(tabs, 0, ys_hbm, ybuf, 0, sem, to_local=True, wait=False)

    @pl.when(c + 1 < nc)
    def _():
        _cell_copies(tabs, c + 1, ys_hbm, ybuf, 1 - slot, sem, to_local=True, wait=False)

    _cell_copies(tabs, c, ys_hbm, ybuf, slot, sem, to_local=True, wait=True)

    g1, g2, l1, l2 = lpt_ref[0:1, :], lpt_ref[1:2, :], lpt_ref[2:3, :], lpt_ref[3:4, :]
    rb = MOE_LOCAL_ROWS // 3
    for r in range(3):
        rid = (lax.broadcasted_iota(I32, (rb, tm), 0) + r * rb).astype(F32)
        gate = jnp.sum(jnp.where(l1 == rid, g1, 0.0) + jnp.where(l2 == rid, g2, 0.0), 1, keepdims=True)
        rs = slice(r * rb, (r + 1) * rb)
        wbuf[rs, :] = (ybuf[slot, rs, :].astype(F32) * gate).astype(BF16)
    tok = tok_ref[...]
    col = lax.broadcasted_iota(I32, (tm, MOE_LOCAL_ROWS), 1).astype(F32)
    pick = jnp.where((tok[:, 2:3] == col) | (tok[:, 3:4] == col), 1.0, 0.0).astype(BF16)
    f = _dot(pick, wbuf[...])
    y = ALPHA * x_ref[...] + mod_ref[5:6, :] * f
    res = _layer_norm(y, lg_ref[...], lb_ref[...])

    @pl.when(c < n_ctx_tiles)
    def _():
        oc_ref[...] = res

    @pl.when(c >= n_ctx_tiles)
    def _():
        ol_ref[...] = res


def _combine(tabs, tok, lpt, ys, x, mods, g, b, *, n_ctx, lat_len):
    nt = x.shape[0]
    tm = MOE_TOK_CHUNK
    nct = n_ctx // tm
    cmap = lambda c, *_: (_cond_index(c * tm, n_ctx, lat_len), 0, 0)
    grid_spec = pltpu.PrefetchScalarGridSpec(
        num_scalar_prefetch=4,
        grid=(nt // tm,),
        in_specs=[pl.BlockSpec((tm, LANES), lambda c, *_: (c, 0)),
                  pl.BlockSpec((None, 8, tm), lambda c, *_: (c, 0, 0)),
                  pl.BlockSpec((tm, D_MODEL), lambda c, *_: (c, 0)),
                  pl.BlockSpec((None, 8, D_MODEL), cmap),
                  pl.BlockSpec((1, D_MODEL), lambda c, *_: (0, 0)),
                  pl.BlockSpec((1, D_MODEL), lambda c, *_: (0, 0)),
                  pl.BlockSpec(memory_space=pl.ANY)],
        out_specs=[pl.BlockSpec((tm, D_MODEL), lambda c, *_: (jnp.minimum(c, nct - 1), 0)),
                   pl.BlockSpec((tm, D_MODEL), lambda c, *_: (jnp.maximum(c - nct, 0), 0))],
        scratch_shapes=[pltpu.VMEM((2, MOE_LOCAL_ROWS, D_MODEL), BF16),
                        pltpu.VMEM((MOE_LOCAL_ROWS, D_MODEL), BF16),
                        pltpu.SemaphoreType.DMA((2,))],
    )
    return pl.pallas_call(
        functools.partial(_combine_kernel, n_ctx_tiles=nct),
        grid_spec=grid_spec,
        out_shape=[jax.ShapeDtypeStruct((n_ctx, D_MODEL), F32),
                   jax.ShapeDtypeStruct((nt - n_ctx, D_MODEL), F32)],
        compiler_params=_cparams("arbitrary"),
        name="moe_combine",
    )(*tabs, tok, lpt, x, mods, g, b, ys)


def _moe(x, routed, mods, wg, wu, wd, ln_g, ln_b, *, n_ctx, lat_len):
    nt = x.shape[0]
    bs, tc = MOE_SLOT_TILE, MOE_TOK_CHUNK
    nchunk = nt // tc
    max_rows = 2 * nt + nchunk * N_EXPERTS * (MOE_CELL_ALIGN - 1)
    n_tiles = -(-max_rows // bs) + N_EXPERTS
    n_slots = n_tiles * bs

    hb, tok, lpt, capt = routed

    cap = capt[:, 0, :N_EXPERTS].astype(I32)
    a_loc = jnp.cumsum(cap, 1) - cap
    tot = jnp.sum(cap, 0)
    padded = ((tot + bs - 1) // bs) * bs
    gend = jnp.cumsum(padded)
    g_glob = (gend - padded)[None, :] + (jnp.cumsum(cap, 0) - cap)
    fill_cell = (MOE_LOCAL_ROWS // MOE_BIG_PIECE) * MOE_BIG_PIECE
    n_tail_cells = -(-(n_slots - 2 * nt) // fill_cell)
    n_fill = 1 + -(-n_tail_cells // N_EXPERTS)
    tail_k = jnp.arange((n_fill - 1) * N_EXPERTS, dtype=I32) * fill_cell
    tail_rows = jnp.clip(n_slots - gend[-1] - tail_k, 0, fill_cell)
    g_all = jnp.concatenate([g_glob.reshape(-1), gend - padded + tot, gend[-1] + tail_k])
    a_all = jnp.concatenate([a_loc.reshape(-1), jnp.zeros((n_fill * N_EXPERTS,), I32)])
    rows_all = jnp.concatenate([cap.reshape(-1), padded - tot, tail_rows])
    n_big = rows_all // MOE_BIG_PIECE
    n_small = (rows_all - n_big * MOE_BIG_PIECE) // MOE_CELL_ALIGN
    tabs = tuple(t.astype(I32) for t in (g_all, a_all, n_big, n_small))
    tile_start = jnp.arange(n_tiles, dtype=I32) * bs
    tile_expert = jnp.minimum(jnp.sum((gend[None, :] <= tile_start[:, None]).astype(I32), 1), N_EXPERTS - 1)
    n_used = (gend[-1] // bs).astype(I32).reshape(1)

    xs = _dispatch(tabs, hb, lpt, n_slots=n_slots, n_fill=n_fill)
    ys = _experts(tile_expert, n_used, xs, wg, wu, wd)
    return _combine(tabs, tok, lpt, ys, x, mods, ln_g, ln_b, n_ctx=n_ctx, lat_len=lat_len)


def _axial_angles(n_tok, dim):
    rows = n_tok // GRID_W
    row = jnp.repeat(jnp.arange(rows), GRID_W).astype(F32)
    col = jnp.tile(jnp.arange(GRID_W), rows).astype(F32)
    axis_dim = dim // 2
    inv = 1.0 / (ROPE_BASE ** (jnp.arange(0, axis_dim, 2, dtype=F32) / axis_dim))
    ang = jnp.concatenate([row[:, None] * inv, col[:, None] * inv], -1)
    return jnp.cos(ang), jnp.sin(ang)


def _mla_tables(n_lat):
    cos, sin = _axial_angles(n_lat, MLA_ROPE)
    one = jnp.ones((n_lat, MLA_NOPE), F32)
    zero = jnp.zeros((n_lat, MLA_NOPE), F32)
    pad1 = jnp.ones((n_lat, LANES - MLA_NOPE - MLA_ROPE), F32)
    pad0 = jnp.zeros((n_lat, LANES - MLA_NOPE - MLA_ROPE), F32)
    return (jnp.concatenate([one, cos, cos, pad1], -1), jnp.concatenate([zero, sin, sin, pad0], -1))


def _ret_tables(n_lat):
    cos, sin = _axial_angles(n_lat, RET_QK)
    return jnp.concatenate([cos, cos], -1), jnp.concatenate([-sin, sin], -1)


def _rot_cols(w):
    half = w.shape[1] // 2
    return jnp.concatenate([-w[:, half:], w[:, :half]], 1)


def _in0_weights(w_in0):
    z, xbc, dt, ql, kvl, kr = jnp.split(
        w_in0, [SSD_INNER, SSD_INNER + SSD_XBC, SSD_INNER + SSD_XBC + 2 * SSD_HEADS,
                SSD_INNER + SSD_XBC + 2 * SSD_HEADS + MLA_Q_RANK,
                SSD_INNER + SSD_XBC + 2 * SSD_HEADS + MLA_Q_RANK + MLA_KV_RANK], axis=1)
    dtp = jnp.pad(dt, ((0, 0), (0, LANES - 2 * SSD_HEADS)))
    lpad = ((0, 0), (MLA_NOPE, LANES - MLA_NOPE - MLA_ROPE))
    krp = jnp.concatenate([jnp.pad(kr, lpad), jnp.pad(_rot_cols(kr), lpad)], 1)
    return jnp.concatenate([z, xbc, dtp, ql, kvl, krp], 1).astype(BF16)


def _mla_weights(w_q_up, w_kv_up):
    d = w_q_up.shape[0]
    wq = w_q_up.reshape(d, MLA_HEADS, MLA_NOPE + MLA_ROPE)
    nope, rope = wq[..., :MLA_NOPE], wq[..., MLA_NOPE:]
    half = MLA_ROPE // 2
    rot = jnp.concatenate([-rope[..., half:], rope[..., :half]], -1)
    tail = jnp.zeros((d, MLA_HEADS, LANES - MLA_NOPE - MLA_ROPE), F32)
    wq_pad = jnp.concatenate([nope, rope, tail], -1).reshape(d, MLA_HEADS * LANES)
    wq_rot = jnp.concatenate([jnp.zeros_like(nope), rot, tail], -1).reshape(d, MLA_HEADS * LANES)
    r = w_kv_up.shape[0]
    wkv = w_kv_up.reshape(r, MLA_HEADS, MLA_NOPE + MLA_V)
    wk = jnp.concatenate([wkv[..., :MLA_NOPE], jnp.zeros((r, MLA_HEADS, LANES - MLA_NOPE), F32)], -1)
    wv = wkv[..., MLA_NOPE:]
    return (wq_pad.astype(BF16), wq_rot.astype(BF16),
            wk.reshape(r, MLA_HEADS * LANES).astype(BF16), wv.reshape(r, MLA_OUT).astype(BF16))


def _lane_row(v, width):
    return jnp.pad(v, (0, width - v.shape[0])).reshape(1, width)


def kernel(x_prompt, x_sample, cache_mla_ckv, cache_mla_krope, state_ssd_f, state_ssd_b, state_ret_f, state_ret_b, c, c_ctx, ada_w, ada_b, ln1_g, ln1_b, ln2_g, ln2_b, w_in0, ssd_conv_w, ssd_conv_b, ssd_a_log_f, ssd_a_log_b, ssd_dt_bias_f, ssd_dt_bias_b, ssd_d, ssd_norm_g, mla_q_norm_g, mla_w_q_up, mla_kv_norm_g, mla_w_kv_up, w_out0, ffn_w_gate, ffn_w_up, ffn_w_down, w_in1, ret_decay_f, ret_decay_b, w_out1, moe_router, moe_w_gate, moe_w_up, moe_w_down):
    bc, lc, _ = x_prompt.shape
    bl, ll, _ = x_sample.shape
    past = cache_mla_ckv.shape[2]
    n_ctx, n_lat = bc * lc, bl * ll
    geo = dict(n_ctx=n_ctx, lat_len=ll)

    x_parts = [x_prompt.reshape(n_ctx, D_MODEL), x_sample.reshape(n_lat, D_MODEL)]
    n_cond = 1 + bl
    cond = jnp.concatenate([c_ctx[None, :], c, jnp.zeros((-n_cond % 8, D_MODEL), F32)], 0)
    mods = _ada_vectors(cond, ada_w, ada_b)

    tm0 = 512
    wq_pad, wq_rot, wk_pad, wv = _mla_weights(mla_w_q_up[0], mla_w_kv_up[0])
    mla_w = (mla_q_norm_g[0].reshape(1, MLA_Q_RANK), wq_pad, wq_rot,
             mla_kv_norm_g[0].reshape(1, MLA_KV_RANK), wk_pad, wv)
    cos_l, sin_l = _mla_tables(ll)
    tables = (jnp.concatenate([jnp.ones((tm0, LANES), F32), cos_l], 0),
              jnp.concatenate([jnp.zeros((tm0, LANES), F32), sin_l], 0))
    z, xbc, dt, q_all, k_all, v_all, ckv_all, kr_all = _in0(
        x_parts, mods[0], _in0_weights(w_in0[0]), tables, mla_w, tm=tm0, **geo)

    cw = jnp.pad(ssd_conv_w[0], ((0, 8 - ssd_conv_w.shape[1]), (0, 0)))
    ssd_params = (cw, ssd_conv_b[0].reshape(1, SSD_XBC),
                  _lane_row(jnp.concatenate([ssd_dt_bias_f[0], ssd_dt_bias_b[0]]), LANES),
                  _lane_row(jnp.concatenate([ssd_a_log_f[0], ssd_a_log_b[0]]), LANES),
                  jnp.repeat(ssd_d[0], SSD_HEAD_DIM).reshape(1, SSD_INNER),
                  ssd_norm_g[0].reshape(1, SSD_INNER))

    def st_in(s):
        return jnp.transpose(s, (0, 3, 1, 2)).reshape(s.shape[0], SSD_STATE, SSD_INNER)

    def st_out(s):
        return jnp.transpose(s.reshape(s.shape[0], SSD_STATE, SSD_HEADS, SSD_HEAD_DIM), (0, 2, 3, 1))

    zero_ssd = jnp.zeros((bc, SSD_STATE, SSD_INNER), F32)
    y_ssd_c, ssd_f, ssd_b = _ssd(z, xbc, dt, zero_ssd, zero_ssd, ssd_params,
                                 row0=0, n_seq=bc, seq_len=lc, name="ssd_ctx")
    y_ssd_l, _, _ = _ssd(z, xbc, dt, st_in(state_ssd_f[:, 0]), st_in(state_ssd_b[:, 0]), ssd_params,
                         row0=n_ctx, n_seq=bl, seq_len=ll, name="ssd_lat")

    lpad = ((0, 0), (MLA_NOPE, 2 * LANES - MLA_NOPE - MLA_ROPE))
    k_p, v_p = _mla_prep(None, cache_mla_ckv[:, 0].reshape(bl * past, MLA_KV_RANK),
                         jnp.pad(cache_mla_krope[:, 0].reshape(bl * past, MLA_ROPE), lpad),
                         None, mla_w, row0=0, n_rows=bl * past,
                         do_q=False, do_norm=False, tm=256, name="mla_prep_cache")
    o_c = _attention(q_all, k_all, v_all, None, None, row0=0, n_seq=bc, seq_len=lc, cache_len=0,
                     tq=lc, n_pairs=MLA_HEADS // 2, name="attn_ctx")
    o_l = _attention(q_all, k_all, v_all, k_p, v_p, row0=n_ctx, n_seq=bl, seq_len=ll, cache_len=past,
                     tq=min(1024, ll), n_pairs=1, name="attn_lat")

    w_out0_bf = w_out0[0].astype(BF16)
    x = _ffn([[y_ssd_c, y_ssd_l], [o_c, o_l]], [w_out0_bf[:SSD_INNER], w_out0_bf[SSD_INNER:]], x_parts, mods[0],
             ln1_g[0].reshape(1, D_MODEL), ln1_b[0].reshape(1, D_MODEL),
             ffn_w_gate[0].astype(BF16), ffn_w_up[0].astype(BF16), ffn_w_down[0].astype(BF16),
             ln2_g[0].reshape(1, D_MODEL), ln2_b[0].reshape(1, D_MODEL), tm=512, **geo)

    hq = RET_HEADS * RET_QK
    q1, k1, v1, g1 = _inproj([x], mods[1], w_in1[0].astype(BF16), (hq, hq, MIX1, MIX1), (F32, F32, BF16, BF16),
                             (False, False, False, True), shift_row=0, tm=256, name="in1", **geo)
    dec = jnp.stack([ret_decay_f[0], ret_decay_b[0]], 1)
    dec = jnp.broadcast_to(jnp.pad(dec, ((0, 0), (0, 6)))[:, :, None], (RET_HEADS, 8, RET_V))
    y_ret_c, ret_f, ret_b = _retention(q1, k1, v1, g1, None, dec, None,
                                       row0=0, n_seq=bc, seq_len=lc, hps=4, name="ret_ctx")
    y_ret_l, = _retention(q1, k1, v1, g1, _ret_tables(ll), dec, (state_ret_f[:, 0], state_ret_b[:, 0]),
                          row0=n_ctx, n_seq=bl, seq_len=ll, hps=2, name="ret_lat")
    router_w = jnp.pad(moe_router[0], ((0, 0), (0, LANES - N_EXPERTS)))
    x, *routed = _outproj([[y_ret_c, y_ret_l]], [w_out1[0].astype(BF16)], [x], mods[1],
                          ln1_g[1].reshape(1, D_MODEL), ln1_b[1].reshape(1, D_MODEL),
                          gate_row=2, tm=MOE_TOK_CHUNK, name="out1", router_w=router_w, **geo)
    y_c, y_l = _moe(x, routed, mods[1], moe_w_gate[0].astype(BF16), moe_w_up[0].astype(BF16),
                    moe_w_down[0].astype(BF16), ln2_g[1].reshape(1, D_MODEL), ln2_b[1].reshape(1, D_MODEL), **geo)

    y_prompt = y_c.reshape(bc, lc, D_MODEL)
    y_sample = y_l.reshape(bl, ll, D_MODEL)
    new_ckv = ckv_all[:n_ctx].reshape(bc, 1, lc, MLA_KV_RANK)
    new_krope = kr_all[:n_ctx, MLA_NOPE:MLA_NOPE + MLA_ROPE].reshape(bc, 1, lc, MLA_ROPE)
    return (y_prompt, y_sample, new_ckv, new_krope,
            st_out(ssd_f)[:, None], st_out(ssd_b)[:, None], ret_f[:, None], ret_b[:, None])
```

```python
import functools
import math

import jax
import jax.numpy as jnp
from jax import lax
from jax.experimental import pallas as pl
from jax.experimental.pallas import tpu as pltpu

F32 = jnp.float32
BF16 = jnp.bfloat16
I32 = jnp.int32

D_MODEL = 1024
DEPTH = 2
GRID_W = 64
CHUNK = 128
SSD_HEADS = 8
SSD_HEAD_DIM = 64
SSD_INNER = SSD_HEADS * SSD_HEAD_DIM
SSD_GROUPS = 2
SSD_STATE = 64
SSD_XBC = SSD_INNER + 2 * SSD_GROUPS * SSD_STATE
MLA_HEADS = 8
MLA_NOPE = 64
MLA_ROPE = 32
MLA_V = 64
MLA_Q_RANK = 384
MLA_KV_RANK = 256
MLA_OUT = MLA_HEADS * MLA_V
RET_HEADS = 8
RET_QK = 128
RET_V = 256
MIX1 = RET_HEADS * RET_V
D_FF = 2816
N_EXPERTS = 8
D_FF_EXPERT = 3584
ALPHA = (2 * DEPTH) ** 0.25
LN_EPS = 1e-5
RMS_EPS = 1e-6
ROPE_BASE = 10000.0

LANES = 128
VMEM_LIMIT = 56 * 1024 * 1024
NEG_BIG = -1e30

MOE_SLOT_TILE = 512
MOE_TOK_CHUNK = 512
MOE_FF_CHUNK = 1792
MOE_CELL_ALIGN = 16
MOE_BIG_PIECE = 64
MOE_LOCAL_ROWS = 2 * MOE_TOK_CHUNK + N_EXPERTS * MOE_CELL_ALIGN
FFN_CHUNK = 512
RET_CHUNK = 256

_NT = (((1,), (1,)), ((), ()))
_TN = (((0,), (0,)), ((), ()))


def _cparams(*sem):
    return pltpu.CompilerParams(dimension_semantics=sem, vmem_limit_bytes=VMEM_LIMIT)


def _resident(shape):
    nd = len(shape)
    return pl.BlockSpec(shape, lambda *_: (0,) * nd, pipeline_mode=pl.Buffered(1))


def _silu(x):
    return x * jax.nn.sigmoid(x)


def _dot(a, b):
    return jnp.dot(a, b, preferred_element_type=F32)


def _cond_index(row, n_ctx, lat_len):
    return jnp.where(row < n_ctx, 0, 1 + (row - n_ctx) // lat_len)


def _layer_norm(y, g, b):
    mu = jnp.mean(y, -1, keepdims=True)
    d = y - mu
    var = jnp.mean(d * d, -1, keepdims=True)
    return d * lax.rsqrt(var + LN_EPS) * g + b


def _ada_kernel(c_ref, w_ref, b_ref, o_ref):
    s = _silu(c_ref[...])
    o_ref[...] = jnp.dot(s, w_ref[...], precision=lax.Precision.HIGHEST,
                         preferred_element_type=F32) + b_ref[...]


def _ada_vectors(cond, ada_w, ada_b):
    r = cond.shape[0]
    tn = 1024
    out = pl.pallas_call(
        _ada_kernel,
        grid=(DEPTH, 6 * D_MODEL // tn),
        in_specs=[pl.BlockSpec((r, D_MODEL), lambda l, j: (0, 0)),
                  pl.BlockSpec((None, D_MODEL, tn), lambda l, j: (l, 0, j)),
                  pl.BlockSpec((None, 1, tn), lambda l, j: (l, 0, j))],
        out_specs=pl.BlockSpec((None, r, tn), lambda l, j: (l, 0, j)),
        out_shape=jax.ShapeDtypeStruct((DEPTH, r, 6 * D_MODEL), F32),
        compiler_params=_cparams("parallel", "parallel"),
        name="ada",
    )(cond, ada_w, ada_b.reshape(DEPTH, 1, 6 * D_MODEL))
    out = out.reshape(DEPTH, r, 6, D_MODEL)
    return jnp.pad(out, ((0, 0), (0, 0), (0, 2), (0, 0)))


def _row_specs(parts, tm):
    cols = parts[0].shape[1]
    if len(parts) == 1:
        return [pl.BlockSpec((tm, cols), lambda i, *_: (i, 0))]
    nct = parts[0].shape[0] // tm
    return [pl.BlockSpec((tm, cols), lambda i, *_: (jnp.minimum(i, nct - 1), 0)),
            pl.BlockSpec((tm, cols), lambda i, *_: (jnp.maximum(i - nct, 0), 0))]


def _row_tile(refs, n_ctx_tiles):
    if len(refs) == 1:
        return refs[0][...]
    return jnp.where(pl.program_id(0) < n_ctx_tiles, refs[0][...], refs[1][...])


def _in1_kernel(x_ref, mod_ref, w_ref, cos_ref, sin_ref, q_out, k_out, v_out, g_out):
    hq = RET_HEADS * RET_QK
    x = x_ref[...]
    h = (x * (1.0 + mod_ref[1:2, :]) + mod_ref[0:1, :]).astype(BF16)
    cs, sn = cos_ref[...], sin_ref[...]
    for o_ref, off, scale in ((q_out, 0, 1.0), (k_out, hq, RET_QK ** -0.5)):
        y = _dot(h, w_ref[:, off:off + hq])
        for hh in range(RET_HEADS):
            hs = slice(hh * RET_QK, (hh + 1) * RET_QK)
            yh = y[:, hs] * scale
            o_ref[:, hs] = (yh * cs + pltpu.roll(yh, RET_QK // 2, axis=1) * sn).astype(o_ref.dtype)
    v_out[...] = _dot(h, w_ref[:, 2 * hq:2 * hq + MIX1]).astype(v_out.dtype)
    g_out[...] = _silu(_dot(h, w_ref[:, 2 * hq + MIX1:])).astype(g_out.dtype)


def _in1(x, mods, w, tables, *, tm, n_ctx, lat_len):
    nt = x.shape[0]
    nct = n_ctx // tm
    lat_tiles = lat_len // tm
    hq = RET_HEADS * RET_QK
    cmap = lambda i: (_cond_index(i * tm, n_ctx, lat_len), 0, 0)
    tmap = lambda i: (jnp.where(i < nct, 0, 1 + (i - nct) % lat_tiles), 0)
    widths = (hq, hq, MIX1, MIX1)
    return pl.pallas_call(
        _in1_kernel,
        grid=(nt // tm,),
        in_specs=[pl.BlockSpec((tm, D_MODEL), lambda i: (i, 0)), pl.BlockSpec((None, 8, D_MODEL), cmap),
                  _resident(w.shape), pl.BlockSpec((tm, RET_QK), tmap), pl.BlockSpec((tm, RET_QK), tmap)],
        out_specs=[pl.BlockSpec((tm, wd), lambda i: (i, 0)) for wd in widths],
        out_shape=[jax.ShapeDtypeStruct((nt, wd), BF16) for wd in widths],
        compiler_params=_cparams("parallel"),
        name="in1",
    )(x, mods, w, tables[0], tables[1])


def _cumsum_rows(tril_bf, x):
    hi = x.astype(BF16)
    r = x - hi.astype(F32)
    mid = r.astype(BF16)
    lo = (r - mid.astype(F32)).astype(BF16)
    return _dot(tril_bf, hi) + _dot(tril_bf, mid) + _dot(tril_bf, lo)


def _ssd_kernel(z_ref, xbc_ref, dt_ref, sf0_ref, sb0_ref, cw_ref, cb_ref, dtb_ref, alog_ref,
                dsk_ref, ng_ref, y_ref, sf_ref, sb_ref,
                yacc, xs_s, cm_s, bmt_s, xb_s, erb_s, *, seq_len):
    nc = seq_len // CHUNK
    hd, ns = SSD_HEAD_DIM, SSD_STATE
    gw = (SSD_HEADS // SSD_GROUPS) * hd
    ri = lax.broadcasted_iota(I32, (CHUNK, CHUNK), 0)
    ci = lax.broadcasted_iota(I32, (CHUNK, CHUNK), 1)
    lower = ri >= ci
    upper = ri <= ci
    tril_bf = jnp.where(lower, 1.0, 0.0).astype(BF16)
    rowid = lax.broadcasted_iota(I32, (CHUNK, 1), 0)
    lane = lax.broadcasted_iota(I32, (CHUNK, LANES), 1)
    src = lax.broadcasted_iota(I32, (LANES, SSD_INNER), 0)
    dst_head = lax.broadcasted_iota(I32, (LANES, SSD_INNER), 1) // hd
    spread_f = jnp.where(src == dst_head, 1.0, 0.0).astype(BF16)
    spread_b = jnp.where(src == dst_head + SSD_HEADS, 1.0, 0.0).astype(BF16)

    def per_head(v, spread, split=True):
        hi = v.astype(BF16)
        if not split:
            return _dot(hi, spread)
        lo = (v - hi.astype(F32)).astype(BF16)
        return _dot(hi, spread) + _dot(lo, spread)

    sf_ref[...] = sf0_ref[...]
    sb_ref[...] = sb0_ref[...]

    def fwd(c, carry):
        r0 = pl.multiple_of(c * CHUNK, CHUNK)
        rows = pl.ds(r0, CHUNK)
        cur = xbc_ref[rows, :]
        pstart = pl.multiple_of(jnp.maximum(r0 - 8, 0), 8)
        nstart = pl.multiple_of(jnp.minimum(r0 + CHUNK, seq_len - 8), 8)
        prev_row = xbc_ref[pl.ds(pstart, 8), :][7:8, :] * jnp.where(c > 0, 1.0, 0.0)
        next_row = xbc_ref[pl.ds(nstart, 8), :][0:1, :] * jnp.where(c < nc - 1, 1.0, 0.0)
        sh_prev = jnp.where(rowid == 0, prev_row, pltpu.roll(cur, 1, axis=0))
        sh_next = jnp.where(rowid == CHUNK - 1, next_row, pltpu.roll(cur, CHUNK - 1, axis=0))
        conv = cw_ref[0:1, :] * sh_prev + cw_ref[1:2, :] * cur + cw_ref[2:3, :] * sh_next + cb_ref[...]
        u = _silu(conv)
        xs = u[:, :SSD_INNER]
        bm = u[:, SSD_INNER:SSD_INNER + LANES]
        cm = u[:, SSD_INNER + LANES:]

        xr = dt_ref[rows, :] + dtb_ref[...]
        dt = jnp.maximum(xr, 0.0) + jnp.log1p(jnp.exp(-jnp.abs(xr)))
        la = -dt * jnp.exp(alog_ref[...])
        facs = _cumsum_rows(tril_bf, la)
        racs = facs[CHUNK - 1:CHUNK, :] - facs + la
        packed = jnp.where(lane < SSD_HEADS, facs,
                           jnp.where(lane < 2 * SSD_HEADS, racs, pltpu.roll(dt, 2 * SSD_HEADS, axis=1)))
        packed_t = packed.T
        e_f = jnp.exp(facs)
        e_r = jnp.exp(racs)
        w_f = dt * jnp.exp(facs[CHUNK - 1:CHUNK, :] - facs)
        w_b = dt * jnp.exp(racs[0:1, :] - racs)

        cm_bf = cm.astype(BF16)
        bm_bf = bm.astype(BF16)
        bmt_bf = bm.T.astype(BF16)
        xs_bf = xs.astype(BF16)
        ef_full = per_head(e_f, spread_f)
        xf = (xs * per_head(w_f, spread_f, split=False)).astype(BF16)
        xb_s[rows, :] = (xs * per_head(w_b, spread_b, split=False)).astype(BF16)
        yoffs, news, mixes = [], [], []
        for g in range(SSD_GROUPS):
            gl = slice(g * ns, (g + 1) * ns)
            s_g = lax.dot_general(cm_bf[:, gl], bm_bf[:, gl], _NT, preferred_element_type=F32)
            yoffs.append(_dot(cm_bf[:, gl], sf_ref[:, g * gw:(g + 1) * gw].astype(BF16)))
            news.append(_dot(bmt_bf[g * ns:(g + 1) * ns, :], xf[:, g * gw:(g + 1) * gw]))
            for hh in range(SSD_HEADS // SSD_GROUPS):
                h = g * (SSD_HEADS // SSD_GROUPS) + hh
                hb = SSD_HEADS + h
                seg_f = facs[:, h:h + 1] - packed_t[h:h + 1, :]
                seg_b = racs[:, hb:hb + 1] - packed_t[hb:hb + 1, :]
                d_f = jnp.exp(jnp.where(lower, seg_f, NEG_BIG))
                d_b = jnp.exp(jnp.where(upper, seg_b, NEG_BIG))
                dt_f_row = packed_t[2 * SSD_HEADS + h:2 * SSD_HEADS + h + 1, :]
                dt_b_row = packed_t[2 * SSD_HEADS + hb:2 * SSD_HEADS + hb + 1, :]
                mixes.append((s_g * (d_f * dt_f_row + d_b * dt_b_row)).astype(BF16))
        for p in range(SSD_HEADS // 2):
            ps = slice(p * LANES, (p + 1) * LANES)
            x_pair = xs_bf[:, ps]
            x_diag = jnp.concatenate([jnp.where(lane < hd, x_pair, jnp.zeros((), BF16)),
                                      jnp.where(lane >= hd, x_pair, jnp.zeros((), BF16))], 0)
            yacc[rows, ps] = _dot(jnp.concatenate([mixes[2 * p], mixes[2 * p + 1]], 1), x_diag)
        yacc[rows, :] += ef_full * jnp.concatenate(yoffs, -1)
        sf_ref[...] = ef_full[CHUNK - 1:CHUNK, :] * sf_ref[...] + jnp.concatenate(news, -1)
        xs_s[rows, :] = xs
        cm_s[rows, :] = cm_bf
        bmt_s[c] = bmt_bf
        erb_s[rows, :] = e_r
        return carry

    lax.fori_loop(0, nc, fwd, 0)

    def bwd(i, carry):
        c = nc - 1 - i
        r0 = pl.multiple_of(c * CHUNK, CHUNK)
        rows = pl.ds(r0, CHUNK)
        cm_bf = cm_s[rows, :]
        bmt_bf = bmt_s[c]
        er_full = per_head(erb_s[rows, :], spread_b)
        yoffs, news = [], []
        for g in range(SSD_GROUPS):
            gl = slice(g * ns, (g + 1) * ns)
            yoffs.append(_dot(cm_bf[:, gl], sb_ref[:, g * gw:(g + 1) * gw].astype(BF16)))
            news.append(_dot(bmt_bf[g * ns:(g + 1) * ns, :], xb_s[rows, g * gw:(g + 1) * gw]))
        sb_ref[...] = er_full[0:1, :] * sb_ref[...] + jnp.concatenate(news, -1)
        yv = yacc[rows, :] + er_full * jnp.concatenate(yoffs, -1) + dsk_ref[...] * xs_s[rows, :]
        gz = yv * z_ref[rows, :].astype(F32)
        ms = jnp.mean(gz * gz, -1, keepdims=True)
        y_ref[rows, :] = (gz * lax.rsqrt(ms + RMS_EPS) * ng_ref[...]).astype(y_ref.dtype)
        return carry

    lax.fori_loop(0, nc, bwd, 0)


def _ssd(z, xbc, dt, sf0, sb0, params, *, row0, n_seq, seq_len, name):
    cw, cb, dtb, alog, dsk, ng = params
    nc = seq_len // CHUNK
    blk0 = row0 // seq_len
    rmap = lambda b: (b + blk0, 0)
    smap = lambda b: (b, 0, 0)
    const = lambda b: (0, 0)
    st = jax.ShapeDtypeStruct((n_seq, SSD_STATE, SSD_INNER), F32)
    return pl.pallas_call(
        functools.partial(_ssd_kernel, seq_len=seq_len),
        grid=(n_seq,),
        in_specs=[pl.BlockSpec((seq_len, SSD_INNER), rmap),
                  pl.BlockSpec((seq_len, SSD_XBC), rmap),
                  pl.BlockSpec((seq_len, LANES), rmap),
                  pl.BlockSpec((None, SSD_STATE, SSD_INNER), smap),
                  pl.BlockSpec((None, SSD_STATE, SSD_INNER), smap),
                  pl.BlockSpec(cw.shape, const), pl.BlockSpec(cb.shape, const),
                  pl.BlockSpec(dtb.shape, const), pl.BlockSpec(alog.shape, const),
                  pl.BlockSpec(dsk.shape, const), pl.BlockSpec(ng.shape, const)],
        out_specs=[pl.BlockSpec((seq_len, SSD_INNER), lambda b: (b, 0)),
                   pl.BlockSpec((None, SSD_STATE, SSD_INNER), smap),
                   pl.BlockSpec((None, SSD_STATE, SSD_INNER), smap)],
        out_shape=[jax.ShapeDtypeStruct((n_seq * seq_len, SSD_INNER), BF16), st, st],
        scratch_shapes=[pltpu.VMEM((seq_len, SSD_INNER), F32),
                        pltpu.VMEM((seq_len, SSD_INNER), F32),
                        pltpu.VMEM((seq_len, LANES), BF16),
                        pltpu.VMEM((nc, LANES, CHUNK), BF16),
                        pltpu.VMEM((seq_len, SSD_INNER), BF16),
                        pltpu.VMEM((seq_len, LANES), F32)],
        compiler_params=_cparams("parallel"),
        name=name,
    )(z, xbc, dt, sf0, sb0, cw, cb, dtb, alog, dsk, ng)


def _rms(x, g):
    return x * lax.rsqrt(jnp.mean(x * x, -1, keepdims=True) + RMS_EPS) * g


def _mla_prep_kernel(*refs, do_q, do_norm, do_rope):
    it = iter(refs)
    qlat_ref = next(it) if do_q else None
    kv_ref = next(it)
    kr_ref = next(it)
    cos_ref = next(it) if do_rope else None
    sin_ref = next(it) if do_rope else None
    if do_q:
        gq_ref, wq_ref = next(it), next(it)
        wqr_ref = next(it) if do_rope else None
    gkv_ref = next(it) if do_norm else None
    wk_ref, wv_ref = next(it), next(it)
    q_out = next(it) if do_q else None
    k_out, v_out = next(it), next(it)
    ckv_out = next(it) if do_norm else None
    cs, sn = (cos_ref[...], sin_ref[...]) if do_rope else (None, None)
    _mla_project(qlat_ref[...] if do_q else None, kv_ref[...], kr_ref[...], cs, sn,
                 (gq_ref, wq_ref, wqr_ref) if do_q else None, gkv_ref, wk_ref, wv_ref,
                 q_out, k_out, v_out, ckv_out)


def _mla_project(qlat, kvlat, kr, cs, sn, q_w, gkv_ref, wk_ref, wv_ref, q_out, k_out, v_out, ckv_out):
    do_rope = cs is not None
    if q_w is not None:
        gq_ref, wq_ref, wqr_ref = q_w
        qn = _rms(qlat, gq_ref[...]).astype(BF16)
        qa = _dot(qn, wq_ref[...])
        if do_rope:
            qb = _dot(qn, wqr_ref[...])
        scale = (MLA_NOPE + MLA_ROPE) ** -0.5 * math.log2(math.e)
        for h in range(MLA_HEADS):
            hs = slice(h * LANES, (h + 1) * LANES)
            qh = qa[:, hs] * cs + qb[:, hs] * sn if do_rope else qa[:, hs]
            q_out[h] = (qh * scale).astype(BF16)
    ckv = kvlat
    if gkv_ref is not None:
        ckv = _rms(ckv, gkv_ref[...])
        ckv_out[...] = ckv
    ckv_bf = ckv.astype(BF16)
    krp = kr[:, :LANES] * cs + kr[:, LANES:] * sn if do_rope else kr[:, :LANES]
    kn = _dot(ckv_bf, wk_ref[...])
    for h in range(MLA_HEADS):
        k_out[h] = (kn[:, h * LANES:(h + 1) * LANES] + krp).astype(BF16)
    v_out[...] = _dot(ckv_bf, wv_ref[...]).astype(BF16)


def _mla_prep(qlat, kv, kr, tables, weights, *, row0, n_rows, do_q, do_norm, tm, name):
    do_rope = tables is not None
    gq, wq, wqr, gkv, wk, wv = weights
    b0 = row0 // tm
    rmap = lambda i: (i + b0, 0)
    omap = lambda i: (i, 0)
    hmap = lambda i: (0, i, 0)
    ins, specs = [], []

    def add(a, spec):
        ins.append(a)
        specs.append(spec)

    if do_q:
        add(qlat, pl.BlockSpec((tm, MLA_Q_RANK), rmap))
    add(kv, pl.BlockSpec((tm, MLA_KV_RANK), rmap))
    add(kr, pl.BlockSpec((tm, 2 * LANES), rmap))
    if do_rope:
        lat_tiles = tables[0].shape[0] // tm
        tmap = lambda i: (i % lat_tiles, 0)
        add(tables[0], pl.BlockSpec((tm, LANES), tmap))
        add(tables[1], pl.BlockSpec((tm, LANES), tmap))
    if do_q:
        add(gq, _resident(gq.shape))
        add(wq, _resident(wq.shape))
        if do_rope:
            add(wqr, _resident(wqr.shape))
    if do_norm:
        add(gkv, _resident(gkv.shape))
    add(wk, _resident(wk.shape))
    add(wv, _resident(wv.shape))
    out_shape, out_specs = [], []
    if do_q:
        out_shape.append(jax.ShapeDtypeStruct((MLA_HEADS, n_rows, LANES), BF16))
        out_specs.append(pl.BlockSpec((MLA_HEADS, tm, LANES), hmap))
    out_shape.append(jax.ShapeDtypeStruct((MLA_HEADS, n_rows, LANES), BF16))
    out_specs.append(pl.BlockSpec((MLA_HEADS, tm, LANES), hmap))
    out_shape.append(jax.ShapeDtypeStruct((n_rows, MLA_OUT), BF16))
    out_specs.append(pl.BlockSpec((tm, MLA_OUT), omap))
    if do_norm:
        out_shape.append(jax.ShapeDtypeStruct((n_rows, MLA_KV_RANK), F32))
        out_specs.append(pl.BlockSpec((tm, MLA_KV_RANK), omap))
    return pl.pallas_call(
        functools.partial(_mla_prep_kernel, do_q=do_q, do_norm=do_norm, do_rope=do_rope),
        grid=(n_rows // tm,),
        in_specs=specs, out_specs=out_specs, out_shape=out_shape,
        compiler_params=_cparams("parallel"),
        name=name,
    )(*ins)


_IN0_SEGS = (SSD_INNER, SSD_XBC, LANES, MLA_Q_RANK, MLA_KV_RANK, 2 * LANES)


def _in0_kernel(*refs, n_x, n_ctx_tiles):
    x_refs = refs[:n_x]
    (mod_ref, w_ref, cos_ref, sin_ref, gq_ref, wq_ref, wqr_ref, gkv_ref, wk_ref, wv_ref,
     z_out, xbc_out, dt_out, q_out, k_out, v_out, ckv_out, kr_out) = refs[n_x:]
    x = _row_tile(x_refs, n_ctx_tiles)
    h = (x * (1.0 + mod_ref[1:2, :]) + mod_ref[0:1, :]).astype(BF16)
    segs, acc = [], 0
    for wd in _IN0_SEGS:
        segs.append(_dot(h, w_ref[:, acc:acc + wd]))
        acc += wd
    z, xbc, dt, qlat, kvlat, kr = segs
    z_out[...] = _silu(z).astype(z_out.dtype)
    xbc_out[...] = xbc
    dt_out[...] = dt
    kr_out[...] = kr[:, :LANES]
    _mla_project(qlat, kvlat, kr, cos_ref[...], sin_ref[...], (gq_ref, wq_ref, wqr_ref), gkv_ref,
                 wk_ref, wv_ref, q_out, k_out, v_out, ckv_out)


def _in0(x_parts, mods, w, tables, mla_w, *, tm, n_ctx, lat_len):
    nt = sum(p.shape[0] for p in x_parts)
    nct = n_ctx // tm
    lat_tiles = lat_len // tm
    gq, wq, wqr, gkv, wk, wv = mla_w
    cmap = lambda i: (_cond_index(i * tm, n_ctx, lat_len), 0, 0)
    tmap = lambda i: (jnp.where(i < nct, 0, 1 + (i - nct) % lat_tiles), 0)
    row = lambda cols: pl.BlockSpec((tm, cols), lambda i: (i, 0))
    heads = pl.BlockSpec((MLA_HEADS, tm, LANES), lambda i: (0, i, 0))
    consts = [w, gq, wq, wqr, gkv, wk, wv]
    return pl.pallas_call(
        functools.partial(_in0_kernel, n_x=len(x_parts), n_ctx_tiles=nct),
        grid=(nt // tm,),
        in_specs=(_row_specs(x_parts, tm) + [pl.BlockSpec((None, 8, D_MODEL), cmap), _resident(w.shape),
                                             pl.BlockSpec((tm, LANES), tmap), pl.BlockSpec((tm, LANES), tmap)]
                  + [_resident(a.shape) for a in consts[1:]]),
        out_specs=[row(SSD_INNER), row(SSD_XBC), row(LANES), heads, heads, row(MLA_OUT),
                   row(MLA_KV_RANK), row(LANES)],
        out_shape=[jax.ShapeDtypeStruct((nt, SSD_INNER), BF16),
                   jax.ShapeDtypeStruct((nt, SSD_XBC), F32),
                   jax.ShapeDtypeStruct((nt, LANES), F32),
                   jax.ShapeDtypeStruct((MLA_HEADS, nt, LANES), BF16),
                   jax.ShapeDtypeStruct((MLA_HEADS, nt, LANES), BF16),
                   jax.ShapeDtypeStruct((nt, MLA_OUT), BF16),
                   jax.ShapeDtypeStruct((nt, MLA_KV_RANK), F32),
                   jax.ShapeDtypeStruct((nt, LANES), F32)],
        compiler_params=_cparams("parallel"),
        name="in0",
    )(*x_parts, mods, w, tables[0], tables[1], gq, wq, wqr, gkv, wk, wv)


def _attn_kernel(*refs, seq_len, cache_len, kblk, n_pairs):
    if cache_len:
        q_ref, k_ref, v_ref, kc_ref, vc_ref, o_ref = refs
    else:
        q_ref, k_ref, v_ref, o_ref = refs
    tq = q_ref.shape[1]
    blocks = [(k_ref, v_ref, i * kblk) for i in range(seq_len // kblk)]
    if cache_len:
        blocks += [(kc_ref, vc_ref, i * kblk) for i in range(cache_len // kblk)]
    vlane = lax.broadcasted_iota(I32, (kblk, LANES), 1)
    lane = lax.broadcasted_iota(I32, (tq, LANES), 1)
    for pp in range(n_pairs):
        ps = slice(pp * LANES, (pp + 1) * LANES)
        outs = []
        for hh in range(2):
            own = (vlane < MLA_V) if hh == 0 else (vlane >= MLA_V)
            q = q_ref[2 * pp + hh]
            m = None
            for kr, vr, off in blocks:
                s = lax.dot_general(q, kr[2 * pp + hh, off:off + kblk, :], _NT, preferred_element_type=F32)
                v_aug = jnp.where(own, vr[off:off + kblk, ps], jnp.ones((), BF16))
                bmax = jnp.max(s, -1, keepdims=True)
                if m is None:
                    m = bmax
                    acc = _dot(jnp.exp2(s - m).astype(BF16), v_aug)
                else:
                    m_new = jnp.maximum(m, bmax)
                    acc = jnp.exp2(m - m_new) * acc + _dot(jnp.exp2(s - m_new).astype(BF16), v_aug)
                    m = m_new
            outs.append(acc / pltpu.roll(acc, MLA_V, axis=1))
        o_ref[:, ps] = jnp.where(lane < MLA_V, outs[0], outs[1]).astype(o_ref.dtype)


def _attention(q, k, v, kc, vc, *, row0, n_seq, seq_len, cache_len, tq, n_pairs, name):
    nq = seq_len // tq
    kblk = min(512, seq_len)
    n = n_seq * seq_len
    q0, s0 = row0 // tq, row0 // seq_len
    hb, vw = 2 * n_pairs, n_pairs * LANES
    ins = [q, k, v]
    specs = [pl.BlockSpec((hb, tq, LANES), lambda b, hp, qi: (hp, q0 + b * nq + qi, 0)),
             pl.BlockSpec((hb, seq_len, LANES), lambda b, hp, qi: (hp, s0 + b, 0)),
             pl.BlockSpec((seq_len, vw), lambda b, hp, qi: (s0 + b, hp))]
    if cache_len:
        ins += [kc, vc]
        specs += [pl.BlockSpec((hb, cache_len, LANES), lambda b, hp, qi: (hp, b, 0)),
                  pl.BlockSpec((cache_len, vw), lambda b, hp, qi: (b, hp))]
    return pl.pallas_call(
        functools.partial(_attn_kernel, seq_len=seq_len, cache_len=cache_len, kblk=kblk, n_pairs=n_pairs),
        grid=(n_seq, MLA_HEADS // hb, nq),
        in_specs=specs,
        out_specs=pl.BlockSpec((tq, vw), lambda b, hp, qi: (b * nq + qi, hp)),
        out_shape=jax.ShapeDtypeStruct((n, MLA_OUT), BF16),
        compiler_params=_cparams("parallel", "parallel", "arbitrary"),
        name=name,
    )(*ins)


def _outproj_kernel(*refs, n_parts, n_ctx_tiles, gate_row, with_router):
    it = iter(refs)
    acc = None
    for n in n_parts[:-1]:
        a = _row_tile([next(it) for _ in range(n)], n_ctx_tiles)
        part = _dot(a, next(it)[...])
        acc = part if acc is None else acc + part
    x = _row_tile([next(it) for _ in range(n_parts[-1])], n_ctx_tiles)
    mod_ref, g_ref, b_ref = next(it), next(it), next(it)
    router_ref = next(it) if with_router else None
    o_ref = next(it)
    y = ALPHA * x + mod_ref[gate_row:gate_row + 1, :] * acc
    res = _layer_norm(y, g_ref[...], b_ref[...])
    o_ref[...] = res
    if with_router:
        _route(res, mod_ref, router_ref, *it)


def _outproj(acts, ws, x_parts, mods, g, b, *, gate_row, tm, n_ctx, lat_len, name, router_w=None):
    nt = sum(p.shape[0] for p in x_parts)
    cmap = lambda i: (_cond_index(i * tm, n_ctx, lat_len), 0, 0)
    ins, specs = [], []
    for parts, w in zip(acts, ws):
        ins += list(parts) + [w]
        specs += _row_specs(parts, tm) + [_resident(w.shape)]
    ins += list(x_parts) + [mods, g, b]
    specs += _row_specs(x_parts, tm) + [pl.BlockSpec((None, 8, D_MODEL), cmap),
                                        _resident(g.shape), _resident(b.shape)]
    out_specs = [pl.BlockSpec((tm, D_MODEL), lambda i: (i, 0))]
    out_shape = [jax.ShapeDtypeStruct((nt, D_MODEL), F32)]
    if router_w is not None:
        assert tm == MOE_TOK_CHUNK
        ins.append(router_w)
        specs.append(_resident(router_w.shape))
        out_specs += [pl.BlockSpec((tm, D_MODEL), lambda i: (i, 0)),
                      pl.BlockSpec((tm, LANES), lambda i: (i, 0)),
                      pl.BlockSpec((None, 8, tm), lambda i: (i, 0, 0)),
                      pl.BlockSpec((None, 8, LANES), lambda i: (i, 0, 0))]
        out_shape += [jax.ShapeDtypeStruct((nt, D_MODEL), BF16),
                      jax.ShapeDtypeStruct((nt, LANES), F32),
                      jax.ShapeDtypeStruct((nt // tm, 8, tm), F32),
                      jax.ShapeDtypeStruct((nt // tm, 8, LANES), F32)]
    n_parts = tuple(len(p) for p in acts) + (len(x_parts),)
    outs = pl.pallas_call(
        functools.partial(_outproj_kernel, n_parts=n_parts, n_ctx_tiles=n_ctx // tm, gate_row=gate_row,
                          with_router=router_w is not None),
        grid=(nt // tm,),
        in_specs=specs,
        out_specs=out_specs,
        out_shape=out_shape,
        compiler_params=_cparams("parallel"),
        name=name,
    )(*ins)
    return outs[0] if router_w is None else outs


def _ffn_kernel(*refs, n_parts, n_ctx_tiles, ff_chunks):
    it = iter(refs)
    mix = None
    for n in n_parts[:-1]:
        a = _row_tile([next(it) for _ in range(n)], n_ctx_tiles)
        part = _dot(a, next(it)[...])
        mix = part if mix is None else mix + part
    x0 = _row_tile([next(it) for _ in range(n_parts[-1])], n_ctx_tiles)
    mod_ref, g1_ref, b1_ref, wg_ref, wu_ref, wd_ref, g_ref, b_ref, o_ref = it
    x = _layer_norm(ALPHA * x0 + mod_ref[2:3, :] * mix, g1_ref[...], b1_ref[...])
    h = (x * (1.0 + mod_ref[4:5, :]) + mod_ref[3:4, :]).astype(BF16)
    acc = None
    for a, b in ff_chunks:
        gt = _dot(h, wg_ref[:, a:b])
        up = _dot(h, wu_ref[:, a:b])
        act = (_silu(gt) * up).astype(BF16)
        part = _dot(act, wd_ref[a:b, :])
        acc = part if acc is None else acc + part
    y = ALPHA * x + mod_ref[5:6, :] * acc
    o_ref[...] = _layer_norm(y, g_ref[...], b_ref[...])


def _ffn(acts, ws, x_parts, mods, g1, b1, wg, wu, wd, g, b, *, tm, n_ctx, lat_len):
    nt = sum(p.shape[0] for p in x_parts)
    ff = wg.shape[1]
    chunks, a = [], 0
    while a < ff:
        chunks.append((a, min(a + FFN_CHUNK, ff)))
        a += FFN_CHUNK
    cmap = lambda i: (_cond_index(i * tm, n_ctx, lat_len), 0, 0)
    ins, specs = [], []
    for parts, w in zip(acts, ws):
        ins += list(parts) + [w]
        specs += _row_specs(parts, tm) + [_resident(w.shape)]
    consts = [g1, b1, wg, wu, wd, g, b]
    ins += list(x_parts) + [mods] + consts
    specs += _row_specs(x_parts, tm) + [pl.BlockSpec((None, 8, D_MODEL), cmap)]
    specs += [_resident(a.shape) for a in consts]
    n_parts = tuple(len(p) for p in acts) + (len(x_parts),)
    return pl.pallas_call(
        functools.partial(_ffn_kernel, n_parts=n_parts, n_ctx_tiles=n_ctx // tm, ff_chunks=tuple(chunks)),
        grid=(nt // tm,),
        in_specs=specs,
        out_specs=pl.BlockSpec((tm, D_MODEL), lambda i: (i, 0)),
        out_shape=jax.ShapeDtypeStruct((nt, D_MODEL), F32),
        compiler_params=_cparams("parallel"),
        name="out0_ffn",
    )(*ins)


def _ret_kernel(*refs, seq_len, n_heads, from_zero):
    it = iter(refs)
    q_ref, k_ref, v_ref, g_ref, dec_ref = next(it), next(it), next(it), next(it), next(it)
    sf0_ref, sb0_ref = (None, None) if from_zero else (next(it), next(it))
    y_ref = next(it)
    sf_out, sb_out = (next(it), next(it)) if from_zero else (None, None)
    yacc, kb_s, dcomb_s, ev_s, wk_s, cd_s, sf_ref, sb_ref = it
    rc = RET_CHUNK
    nc = seq_len // rc
    unroll = min(2, nc)

    @pl.when(pl.program_id(1) == 0)
    def _():
        ri = lax.broadcasted_iota(I32, (rc, rc), 0)
        ci = lax.broadcasted_iota(I32, (rc, rc), 1)
        dij = (ri - ci).astype(F32)
        pos_k = lax.broadcasted_iota(I32, (rc, RET_QK), 0).astype(F32)
        pos_v = lax.broadcasted_iota(I32, (rc, RET_V), 0).astype(F32)
        for hh in range(n_heads):
            la_f = -jnp.exp(dec_ref[hh, 0:1, :])
            la_b = -jnp.exp(dec_ref[hh, 1:2, :])
            dcomb_s[hh] = (jnp.exp(jnp.where(ri >= ci, dij * la_f[:, :rc], NEG_BIG)) +
                           jnp.exp(jnp.where(ri <= ci, -dij * la_b[:, :rc], NEG_BIG)))
            ev_s[hh, 0] = jnp.exp((pos_v + 1.0) * la_f)
            ev_s[hh, 1] = jnp.exp((rc - pos_v) * la_b)
            wk_s[hh, 0] = jnp.exp((rc - 1.0 - pos_k) * la_f[:, :RET_QK])
            wk_s[hh, 1] = jnp.exp(pos_k * la_b[:, :RET_QK])
            cd_s[hh, 0:1, :] = jnp.exp(rc * la_f)
            cd_s[hh, 1:2, :] = jnp.exp(rc * la_b)

    if from_zero:
        sf_ref[...] = jnp.zeros_like(sf_ref)
        sb_ref[...] = jnp.zeros_like(sb_ref)
    else:
        for hh in range(n_heads):
            sf_ref[hh] = sf0_ref[hh].T
            sb_ref[hh] = sb0_ref[hh].T

    def fwd(c, carry):
        rows = pl.ds(pl.multiple_of(c * rc, rc), rc)
        for hh in range(n_heads):
            qs = slice(hh * RET_QK, (hh + 1) * RET_QK)
            vs = slice(hh * RET_V, (hh + 1) * RET_V)
            q_bf = q_ref[rows, qs]
            k_bf = k_ref[rows, qs]
            k = k_bf.astype(F32)
            v = v_ref[rows, vs]
            s = lax.dot_general(q_bf, k_bf, _NT, preferred_element_type=F32)
            y = _dot((s * dcomb_s[hh]).astype(BF16), v)
            y = y + _dot(q_bf, sf_ref[hh].astype(BF16)) * ev_s[hh, 0]
            yacc[rows, vs] = y
            upd = lax.dot_general((k * wk_s[hh, 0]).astype(BF16), v, _TN, preferred_element_type=F32)
            sf_ref[hh] = cd_s[hh, 0:1, :] * sf_ref[hh] + upd
            kb_s[rows, qs] = (k * wk_s[hh, 1]).astype(BF16)
        return carry

    lax.fori_loop(0, nc, fwd, 0, unroll=unroll)

    def bwd(i, carry):
        rows = pl.ds(pl.multiple_of((nc - 1 - i) * rc, rc), rc)
        for hh in range(n_heads):
            qs = slice(hh * RET_QK, (hh + 1) * RET_QK)
            vs = slice(hh * RET_V, (hh + 1) * RET_V)
            v = v_ref[rows, vs]
            y = yacc[rows, vs] + _dot(q_ref[rows, qs], sb_ref[hh].astype(BF16)) * ev_s[hh, 1]
            upd = lax.dot_general(kb_s[rows, qs], v, _TN, preferred_element_type=F32)
            sb_ref[hh] = cd_s[hh, 1:2, :] * sb_ref[hh] + upd
            mu = jnp.mean(y, -1, keepdims=True)
            d = y - mu
            var = jnp.mean(d * d, -1, keepdims=True)
            yn = d * lax.rsqrt(var + LN_EPS)
            y_ref[rows, vs] = (yn * g_ref[rows, vs].astype(F32)).astype(y_ref.dtype)
        return carry

    lax.fori_loop(0, nc, bwd, 0, unroll=unroll)

    if from_zero:
        for hh in range(n_heads):
            sf_out[hh] = sf_ref[hh].T
            sb_out[hh] = sb_ref[hh].T


def _retention(q, k, v, g, dec, states, *, row0, n_seq, seq_len, hps, name):
    from_zero = states is None
    blk0 = row0 // seq_len
    qmap = lambda h, b: (b + blk0, h)
    state_spec = pl.BlockSpec((None, hps, RET_V, RET_QK), lambda h, b: (b, h, 0, 0))
    ins = [q, k, v, g, dec]
    specs = [pl.BlockSpec((seq_len, hps * RET_QK), qmap), pl.BlockSpec((seq_len, hps * RET_QK), qmap),
             pl.BlockSpec((seq_len, hps * RET_V), qmap), pl.BlockSpec((seq_len, hps * RET_V), qmap),
             pl.BlockSpec((hps, 8, RET_V), lambda h, b: (h, 0, 0))]
    out_specs = [pl.BlockSpec((seq_len, hps * RET_V), lambda h, b: (b, h))]
    out_shape = [jax.ShapeDtypeStruct((n_seq * seq_len, MIX1), BF16)]
    if from_zero:
        st = jax.ShapeDtypeStruct((n_seq, RET_HEADS, RET_V, RET_QK), F32)
        out_specs += [state_spec, state_spec]
        out_shape += [st, st]
    else:
        ins += list(states)
        specs += [state_spec, state_spec]
    rc = RET_CHUNK
    return pl.pallas_call(
        functools.partial(_ret_kernel, seq_len=seq_len, n_heads=hps, from_zero=from_zero),
        grid=(RET_HEADS // hps, n_seq),
        in_specs=specs,
        out_specs=out_specs,
        out_shape=out_shape,
        scratch_shapes=[pltpu.VMEM((seq_len, hps * RET_V), F32),
                        pltpu.VMEM((seq_len, hps * RET_QK), BF16),
                        pltpu.VMEM((hps, rc, rc), F32),
                        pltpu.VMEM((hps, 2, rc, RET_V), F32),
                        pltpu.VMEM((hps, 2, rc, RET_QK), F32),
                        pltpu.VMEM((hps, 8, RET_V), F32),
                        pltpu.VMEM((hps, RET_QK, RET_V), F32),
                        pltpu.VMEM((hps, RET_QK, RET_V), F32)],
        compiler_params=_cparams("parallel", "arbitrary"),
        name=name,
    )(*ins)


def _route(x, mod_ref, w_ref, hb_ref, tok_ref, lpt_ref, cap_ref):
    tm = x.shape[0]
    h = x * (1.0 + mod_ref[4:5, :]) + mod_ref[3:4, :]
    h_hi = h.astype(BF16)
    hb_ref[...] = h_hi
    h_lo = (h - h_hi.astype(F32)).astype(BF16)
    w = w_ref[...]
    w_hi = w.astype(BF16)
    w_lo = (w - w_hi.astype(F32)).astype(BF16)
    w_both = w_hi + pltpu.roll(w_lo.astype(F32), N_EXPERTS, axis=1).astype(BF16)
    part = _dot(h_hi, w_both) + _dot(h_lo, w_both)
    logits = part + pltpu.roll(part, LANES - N_EXPERTS, axis=1)
    lt = logits.T[0:N_EXPERTS, :]
    row = lax.broadcasted_iota(I32, (N_EXPERTS, tm), 0)
    m1 = jnp.max(lt, 0, keepdims=True)
    i1 = jnp.min(jnp.where(lt == m1, row, N_EXPERTS), 0, keepdims=True)
    rest = jnp.where(row == i1, NEG_BIG, lt)
    m2 = jnp.max(rest, 0, keepdims=True)
    i2 = jnp.min(jnp.where(rest == m2, row, N_EXPERTS), 0, keepdims=True)
    e = jnp.exp(m2 - m1)
    g1 = 1.0 / (1.0 + e)
    g2 = e / (1.0 + e)
    sel1 = row == i1
    sel2 = row == i2
    onehot = jnp.where(sel1 | sel2, 1.0, 0.0)
    ri = lax.broadcasted_iota(I32, (tm, tm), 0)
    ci = lax.broadcasted_iota(I32, (tm, tm), 1)
    earlier = jnp.where(ri < ci, 1.0, 0.0).astype(BF16)
    prefix = _dot(onehot.astype(BF16), earlier)
    n_col = jnp.sum(onehot, 1, keepdims=True)
    cap_col = jnp.floor((n_col + (MOE_CELL_ALIGN - 1.0)) * (1.0 / MOE_CELL_ALIGN)) * MOE_CELL_ALIGN
    sub = lax.broadcasted_iota(I32, (N_EXPERTS, 1), 0)
    base_col = jnp.zeros((N_EXPERTS, 1), F32)
    for ex in range(N_EXPERTS - 1):
        base_col = base_col + jnp.where(sub > ex, cap_col[ex:ex + 1, :], 0.0)
    local = prefix + base_col
    lpos1 = jnp.sum(jnp.where(sel1, local, 0.0), 0, keepdims=True)
    lpos2 = jnp.sum(jnp.where(sel2, local, 0.0), 0, keepdims=True)
    lpt = jnp.where(row == 0, g1,
          jnp.where(row == 1, g2,
          jnp.where(row == 2, lpos1,
          jnp.where(row == 3, lpos2, 0.0))))
    lpt_ref[...] = lpt
    tok_ref[...] = jnp.concatenate([lpt, jnp.zeros((LANES - N_EXPERTS, tm), F32)], 0).T
    elane = lax.broadcasted_iota(I32, (N_EXPERTS, LANES), 1)
    cap_row = jnp.sum(jnp.where(elane == sub, cap_col, 0.0), 0, keepdims=True)
    cap_ref[...] = jnp.broadcast_to(cap_row, cap_ref.shape)


def _cell_copies(tabs, chunk, hbm_ref, buf_ref, slot, sem, *, to_local, wait):
    g_ref, a_ref, nbig_ref, nsmall_ref = tabs
    for e in range(N_EXPERTS):
        k = chunk * N_EXPERTS + e
        g0, a0, nbig, nsmall = g_ref[k], a_ref[k], nbig_ref[k], nsmall_ref[k]

        def piece(i, carry, rows, goff, aoff):
            g = pl.multiple_of(goff + i * rows, MOE_CELL_ALIGN)
            a = pl.multiple_of(aoff + i * rows, MOE_CELL_ALIGN)
            far = hbm_ref.at[pl.ds(g, rows), :]
            near = buf_ref.at[slot, pl.ds(a, rows), :]
            cp = (pltpu.make_async_copy(far, near, sem.at[slot]) if to_local
                  else pltpu.make_async_copy(near, far, sem.at[slot]))
            if wait:
                cp.wait()
            else:
                cp.start()
            return carry

        lax.fori_loop(0, nbig, functools.partial(piece, rows=MOE_BIG_PIECE, goff=g0, aoff=a0), 0)
        done = nbig * MOE_BIG_PIECE
        lax.fori_loop(0, nsmall, functools.partial(piece, rows=MOE_CELL_ALIGN, goff=g0 + done, aoff=a0 + done), 0)


def _dispatch_kernel(g_ref, a_ref, nbig_ref, nsmall_ref, hb_ref, lpt_ref, xs_hbm, ybuf, sem, *, n_fill):
    c = pl.program_id(0)
    nc = pl.num_programs(0)
    slot = c % 2
    tabs = (g_ref, a_ref, nbig_ref, nsmall_ref)

    @pl.when(c == 0)
    def _():
        ybuf[2] = jnp.zeros(ybuf.shape[1:], ybuf.dtype)
        for j in range(n_fill):
            _cell_copies(tabs, nc + j, xs_hbm, ybuf, 2, sem, to_local=False, wait=False)

    @pl.when(c == nc - 1)
    def _():
        for j in range(n_fill):
            _cell_copies(tabs, nc + j, xs_hbm, ybuf, 2, sem, to_local=False, wait=True)

    l1 = lpt_ref[2:3, :]
    l2 = lpt_ref[3:4, :]
    rb = MOE_LOCAL_ROWS // 3
    for r in range(3):
        rid = (lax.broadcasted_iota(I32, (rb, MOE_TOK_CHUNK), 0) + r * rb).astype(F32)
        onehot = jnp.where((l1 == rid) | (l2 == rid), 1.0, 0.0).astype(BF16)
        ybuf[slot, r * rb:(r + 1) * rb, :] = _dot(onehot, hb_ref[...]).astype(BF16)

    @pl.when(c > 0)
    def _():
        _cell_copies(tabs, c - 1, xs_hbm, ybuf, 1 - slot, sem, to_local=False, wait=True)

    _cell_copies(tabs, c, xs_hbm, ybuf, slot, sem, to_local=False, wait=False)

    @pl.when(c == nc - 1)
    def _():
        _cell_copies(tabs, c, xs_hbm, ybuf, slot, sem, to_local=False, wait=True)


def _dispatch(tabs, hb, lpt, *, n_slots, n_fill):
    tm = MOE_TOK_CHUNK
    grid_spec = pltpu.PrefetchScalarGridSpec(
        num_scalar_prefetch=4,
        grid=(hb.shape[0] // tm,),
        in_specs=[pl.BlockSpec((tm, D_MODEL), lambda c, *_: (c, 0)),
                  pl.BlockSpec((None, 8, tm), lambda c, *_: (c, 0, 0))],
        out_specs=pl.BlockSpec(memory_space=pl.ANY),
        scratch_shapes=[pltpu.VMEM((3, MOE_LOCAL_ROWS, D_MODEL), BF16),
                        pltpu.SemaphoreType.DMA((3,))],
    )
    return pl.pallas_call(
        functools.partial(_dispatch_kernel, n_fill=n_fill),
        grid_spec=grid_spec,
        out_shape=jax.ShapeDtypeStruct((n_slots, D_MODEL), BF16),
        compiler_params=_cparams("arbitrary"),
        name="moe_dispatch",
    )(*tabs, hb, lpt)


def _expert_kernel(te_ref, nu_ref, x_ref, wg_ref, wu_ref, wd_ref, o_ref, acc):
    t = pl.program_id(0)
    f = pl.program_id(1)
    nf = pl.num_programs(1)

    @pl.when(t < nu_ref[0])
    def _():
        x = x_ref[...]
        gt = _dot(x, wg_ref[...])
        up = _dot(x, wu_ref[...])
        part = _dot((_silu(gt) * up).astype(BF16), wd_ref[...])

        @pl.when(f == 0)
        def _():
            acc[...] = part

        @pl.when((f > 0) & (f < nf - 1))
        def _():
            acc[...] += part

        @pl.when(f == nf - 1)
        def _():
            o_ref[...] = (acc[...] + part).astype(o_ref.dtype)

    @pl.when((t >= nu_ref[0]) & (f == nf - 1))
    def _():
        o_ref[...] = jnp.zeros_like(o_ref)


def _experts(te, nu, xs, wg, wu, wd):
    bs = MOE_SLOT_TILE
    n_slots = xs.shape[0]
    nf = D_FF_EXPERT // MOE_FF_CHUNK

    def tt(t, nu):
        return jnp.minimum(t, nu[0] - 1)

    def ff(t, f, nu):
        return jnp.where(t < nu[0], f, nf - 1)

    grid_spec = pltpu.PrefetchScalarGridSpec(
        num_scalar_prefetch=2,
        grid=(n_slots // bs, nf),
        in_specs=[pl.BlockSpec((bs, D_MODEL), lambda t, f, te, nu: (tt(t, nu), 0)),
                  pl.BlockSpec((None, D_MODEL, MOE_FF_CHUNK), lambda t, f, te, nu: (te[tt(t, nu)], 0, ff(t, f, nu))),
                  pl.BlockSpec((None, D_MODEL, MOE_FF_CHUNK), lambda t, f, te, nu: (te[tt(t, nu)], 0, ff(t, f, nu))),
                  pl.BlockSpec((None, MOE_FF_CHUNK, D_MODEL), lambda t, f, te, nu: (te[tt(t, nu)], ff(t, f, nu), 0))],
        out_specs=pl.BlockSpec((bs, D_MODEL), lambda t, f, te, nu: (t, 0)),
        scratch_shapes=[pltpu.VMEM((bs, D_MODEL), F32)],
    )
    return pl.pallas_call(
        _expert_kernel,
        grid_spec=grid_spec,
        out_shape=jax.ShapeDtypeStruct((n_slots, D_MODEL), BF16),
        compiler_params=_cparams("arbitrary", "arbitrary"),
        name="moe_experts",
    )(te, nu, xs, wg, wu, wd)


def _combine_kernel(g_ref, a_ref, nbig_ref, nsmall_ref, tok_ref, lpt_ref, x_ref, mod_ref, lg_ref, lb_ref, ys_hbm,
                    oc_ref, ol_ref, ybuf, wbuf, sem, *, n_ctx_tiles):
    c = pl.program_id(0)
    nc = pl.num_programs(0)
    slot = c % 2
    tabs = (g_ref, a_ref, nbig_ref, nsmall_ref)
    tm = tok_ref.shape[0]

    @pl.when(c == 0)
    def _():
        ybuf[...] = jnp.zeros_like(ybuf)
        _cell_copies(tabs, 0, ys_hbm, ybuf, 0, sem, to_local=True, wait=False)

    @pl.when(c + 1 < nc)
    def _():
        _cell_copies(tabs, c + 1, ys_hbm, ybuf, 1 - slot, sem, to_local=True, wait=False)

    _cell_copies(tabs, c, ys_hbm, ybuf, slot, sem, to_local=True, wait=True)

    g1, g2, l1, l2 = lpt_ref[0:1, :], lpt_ref[1:2, :], lpt_ref[2:3, :], lpt_ref[3:4, :]
    rb = MOE_LOCAL_ROWS // 3
    for r in range(3):
        rid = (lax.broadcasted_iota(I32, (rb, tm), 0) + r * rb).astype(F32)
        gate = jnp.sum(jnp.where(l1 == rid, g1, 0.0) + jnp.where(l2 == rid, g2, 0.0), 1, keepdims=True)
        rs = slice(r * rb, (r + 1) * rb)
        wbuf[rs, :] = (ybuf[slot, rs, :].astype(F32) * gate).astype(BF16)
    tok = tok_ref[...]
    col = lax.broadcasted_iota(I32, (tm, MOE_LOCAL_ROWS), 1).astype(F32)
    pick = jnp.where((tok[:, 2:3] == col) | (tok[:, 3:4] == col), 1.0, 0.0).astype(BF16)
    f = _dot(pick, wbuf[...])
    y = ALPHA * x_ref[...] + mod_ref[5:6, :] * f
    res = _layer_norm(y, lg_ref[...], lb_ref[...])

    @pl.when(c < n_ctx_tiles)
    def _():
        oc_ref[...] = res

    @pl.when(c >= n_ctx_tiles)
    def _():
        ol_ref[...] = res


def _combine(tabs, tok, lpt, ys, x, mods, g, b, *, n_ctx, lat_len):
    nt = x.shape[0]
    tm = MOE_TOK_CHUNK
    nct = n_ctx // tm
    cmap = lambda c, *_: (_cond_index(c * tm, n_ctx, lat_len), 0, 0)
    grid_spec = pltpu.PrefetchScalarGridSpec(
        num_scalar_prefetch=4,
        grid=(nt // tm,),
        in_specs=[pl.BlockSpec((tm, LANES), lambda c, *_: (c, 0)),
                  pl.BlockSpec((None, 8, tm), lambda c, *_: (c, 0, 0)),
                  pl.BlockSpec((tm, D_MODEL), lambda c, *_: (c, 0)),
                  pl.BlockSpec((None, 8, D_MODEL), cmap),
                  pl.BlockSpec((1, D_MODEL), lambda c, *_: (0, 0)),
                  pl.BlockSpec((1, D_MODEL), lambda c, *_: (0, 0)),
                  pl.BlockSpec(memory_space=pl.ANY)],
        out_specs=[pl.BlockSpec((tm, D_MODEL), lambda c, *_: (jnp.minimum(c, nct - 1), 0)),
                   pl.BlockSpec((tm, D_MODEL), lambda c, *_: (jnp.maximum(c - nct, 0), 0))],
        scratch_shapes=[pltpu.VMEM((2, MOE_LOCAL_ROWS, D_MODEL), BF16),
                        pltpu.VMEM((MOE_LOCAL_ROWS, D_MODEL), BF16),
                        pltpu.SemaphoreType.DMA((2,))],
    )
    return pl.pallas_call(
        functools.partial(_combine_kernel, n_ctx_tiles=nct),
        grid_spec=grid_spec,
        out_shape=[jax.ShapeDtypeStruct((n_ctx, D_MODEL), F32),
                   jax.ShapeDtypeStruct((nt - n_ctx, D_MODEL), F32)],
        compiler_params=_cparams("arbitrary"),
        name="moe_combine",
    )(*tabs, tok, lpt, x, mods, g, b, ys)


def _moe(x, routed, mods, wg, wu, wd, ln_g, ln_b, *, n_ctx, lat_len):
    nt = x.shape[0]
    bs, tc = MOE_SLOT_TILE, MOE_TOK_CHUNK
    nchunk = nt // tc
    max_rows = 2 * nt + nchunk * N_EXPERTS * (MOE_CELL_ALIGN - 1)
    n_tiles = -(-max_rows // bs) + N_EXPERTS
    n_slots = n_tiles * bs

    hb, tok, lpt, capt = routed

    cap = capt[:, 0, :N_EXPERTS].astype(I32)
    a_loc = jnp.cumsum(cap, 1) - cap
    tot = jnp.sum(cap, 0)
    padded = ((tot + bs - 1) // bs) * bs
    gend = jnp.cumsum(padded)
    g_glob = (gend - padded)[None, :] + (jnp.cumsum(cap, 0) - cap)
    fill_cell = (MOE_LOCAL_ROWS // MOE_BIG_PIECE) * MOE_BIG_PIECE
    n_tail_cells = -(-(n_slots - 2 * nt) // fill_cell)
    n_fill = 1 + -(-n_tail_cells // N_EXPERTS)
    tail_k = jnp.arange((n_fill - 1) * N_EXPERTS, dtype=I32) * fill_cell
    tail_rows = jnp.clip(n_slots - gend[-1] - tail_k, 0, fill_cell)
    g_all = jnp.concatenate([g_glob.reshape(-1), gend - padded + tot, gend[-1] + tail_k])
    a_all = jnp.concatenate([a_loc.reshape(-1), jnp.zeros((n_fill * N_EXPERTS,), I32)])
    rows_all = jnp.concatenate([cap.reshape(-1), padded - tot, tail_rows])
    n_big = rows_all // MOE_BIG_PIECE
    n_small = (rows_all - n_big * MOE_BIG_PIECE) // MOE_CELL_ALIGN
    tabs = tuple(t.astype(I32) for t in (g_all, a_all, n_big, n_small))
    tile_start = jnp.arange(n_tiles, dtype=I32) * bs
    tile_expert = jnp.minimum(jnp.sum((gend[None, :] <= tile_start[:, None]).astype(I32), 1), N_EXPERTS - 1)
    n_used = (gend[-1] // bs).astype(I32).reshape(1)

    xs = _dispatch(tabs, hb, lpt, n_slots=n_slots, n_fill=n_fill)
    ys = _experts(tile_expert, n_used, xs, wg, wu, wd)
    return _combine(tabs, tok, lpt, ys, x, mods, ln_g, ln_b, n_ctx=n_ctx, lat_len=lat_len)


def _axial_angles(n_tok, dim):
    rows = n_tok // GRID_W
    row = jnp.repeat(jnp.arange(rows), GRID_W).astype(F32)
    col = jnp.tile(jnp.arange(GRID_W), rows).astype(F32)
    axis_dim = dim // 2
    inv = 1.0 / (ROPE_BASE ** (jnp.arange(0, axis_dim, 2, dtype=F32) / axis_dim))
    ang = jnp.concatenate([row[:, None] * inv, col[:, None] * inv], -1)
    return jnp.cos(ang), jnp.sin(ang)


def _mla_tables(n_lat):
    cos, sin = _axial_angles(n_lat, MLA_ROPE)
    one = jnp.ones((n_lat, MLA_NOPE), F32)
    zero = jnp.zeros((n_lat, MLA_NOPE), F32)
    pad1 = jnp.ones((n_lat, LANES - MLA_NOPE - MLA_ROPE), F32)
    pad0 = jnp.zeros((n_lat, LANES - MLA_NOPE - MLA_ROPE), F32)
    return (jnp.concatenate([one, cos, cos, pad1], -1), jnp.concatenate([zero, sin, sin, pad0], -1))


def _ret_tables(n_lat):
    cos, sin = _axial_angles(n_lat, RET_QK)
    return jnp.concatenate([cos, cos], -1), jnp.concatenate([-sin, sin], -1)


def _rot_cols(w):
    half = w.shape[1] // 2
    return jnp.concatenate([-w[:, half:], w[:, :half]], 1)


def _in0_weights(w_in0):
    z, xbc, dt, ql, kvl, kr = jnp.split(
        w_in0, [SSD_INNER, SSD_INNER + SSD_XBC, SSD_INNER + SSD_XBC + 2 * SSD_HEADS,
                SSD_INNER + SSD_XBC + 2 * SSD_HEADS + MLA_Q_RANK,
                SSD_INNER + SSD_XBC + 2 * SSD_HEADS + MLA_Q_RANK + MLA_KV_RANK], axis=1)
    dtp = jnp.pad(dt, ((0, 0), (0, LANES - 2 * SSD_HEADS)))
    lpad = ((0, 0), (MLA_NOPE, LANES - MLA_NOPE - MLA_ROPE))
    krp = jnp.concatenate([jnp.pad(kr, lpad), jnp.pad(_rot_cols(kr), lpad)], 1)
    return jnp.concatenate([z, xbc, dtp, ql, kvl, krp], 1).astype(BF16)


def _mla_weights(w_q_up, w_kv_up):
    d = w_q_up.shape[0]
    wq = w_q_up.reshape(d, MLA_HEADS, MLA_NOPE + MLA_ROPE)
    nope, rope = wq[..., :MLA_NOPE], wq[..., MLA_NOPE:]
    half = MLA_ROPE // 2
    rot = jnp.concatenate([-rope[..., half:], rope[..., :half]], -1)
    tail = jnp.zeros((d, MLA_HEADS, LANES - MLA_NOPE - MLA_ROPE), F32)
    wq_pad = jnp.concatenate([nope, rope, tail], -1).reshape(d, MLA_HEADS * LANES)
    wq_rot = jnp.concatenate([jnp.zeros_like(nope), rot, tail], -1).reshape(d, MLA_HEADS * LANES)
    r = w_kv_up.shape[0]
    wkv = w_kv_up.reshape(r, MLA_HEADS, MLA_NOPE + MLA_V)
    wk = jnp.concatenate([wkv[..., :MLA_NOPE], jnp.zeros((r, MLA_HEADS, LANES - MLA_NOPE), F32)], -1)
    wv = wkv[..., MLA_NOPE:]
    return (wq_pad.astype(BF16), wq_rot.astype(BF16),
            wk.reshape(r, MLA_HEADS * LANES).astype(BF16), wv.reshape(r, MLA_OUT).astype(BF16))


def _lane_row(v, width):
    return jnp.pad(v, (0, width - v.shape[0])).reshape(1, width)


def kernel(x_prompt, x_sample, cache_mla_ckv, cache_mla_krope, state_ssd_f, state_ssd_b, state_ret_f, state_ret_b, c, c_ctx, ada_w, ada_b, ln1_g, ln1_b, ln2_g, ln2_b, w_in0, ssd_conv_w, ssd_conv_b, ssd_a_log_f, ssd_a_log_b, ssd_dt_bias_f, ssd_dt_bias_b, ssd_d, ssd_norm_g, mla_q_norm_g, mla_w_q_up, mla_kv_norm_g, mla_w_kv_up, w_out0, ffn_w_gate, ffn_w_up, ffn_w_down, w_in1, ret_decay_f, ret_decay_b, w_out1, moe_router, moe_w_gate, moe_w_up, moe_w_down):
    bc, lc, _ = x_prompt.shape
    bl, ll, _ = x_sample.shape
    past = cache_mla_ckv.shape[2]
    n_ctx, n_lat = bc * lc, bl * ll
    geo = dict(n_ctx=n_ctx, lat_len=ll)

    x_parts = [x_prompt.reshape(n_ctx, D_MODEL), x_sample.reshape(n_lat, D_MODEL)]
    n_cond = 1 + bl
    cond = jnp.concatenate([c_ctx[None, :], c, jnp.zeros((-n_cond % 8, D_MODEL), F32)], 0)
    mods = _ada_vectors(cond, ada_w, ada_b)

    tm0 = 512
    wq_pad, wq_rot, wk_pad, wv = _mla_weights(mla_w_q_up[0], mla_w_kv_up[0])
    mla_w = (mla_q_norm_g[0].reshape(1, MLA_Q_RANK), wq_pad, wq_rot,
             mla_kv_norm_g[0].reshape(1, MLA_KV_RANK), wk_pad, wv)
    cos_l, sin_l = _mla_tables(ll)
    tables = (jnp.concatenate([jnp.ones((tm0, LANES), F32), cos_l], 0),
              jnp.concatenate([jnp.zeros((tm0, LANES), F32), sin_l], 0))
    z, xbc, dt, q_all, k_all, v_all, ckv_all, kr_all = _in0(
        x_parts, mods[0], _in0_weights(w_in0[0]), tables, mla_w, tm=tm0, **geo)

    cw = jnp.pad(ssd_conv_w[0], ((0, 8 - ssd_conv_w.shape[1]), (0, 0)))
    ssd_params = (cw, ssd_conv_b[0].reshape(1, SSD_XBC),
                  _lane_row(jnp.concatenate([ssd_dt_bias_f[0], ssd_dt_bias_b[0]]), LANES),
                  _lane_row(jnp.concatenate([ssd_a_log_f[0], ssd_a_log_b[0]]), LANES),
                  jnp.repeat(ssd_d[0], SSD_HEAD_DIM).reshape(1, SSD_INNER),
                  ssd_norm_g[0].reshape(1, SSD_INNER))

    def st_in(s):
        return jnp.transpose(s, (0, 3, 1, 2)).reshape(s.shape[0], SSD_STATE, SSD_INNER)

    def st_out(s):
        return jnp.transpose(s.reshape(s.shape[0], SSD_STATE, SSD_HEADS, SSD_HEAD_DIM), (0, 2, 3, 1))

    zero_ssd = jnp.zeros((bc, SSD_STATE, SSD_INNER), F32)
    y_ssd_c, ssd_f, ssd_b = _ssd(z, xbc, dt, zero_ssd, zero_ssd, ssd_params,
                                 row0=0, n_seq=bc, seq_len=lc, name="ssd_ctx")
    y_ssd_l, _, _ = _ssd(z, xbc, dt, st_in(state_ssd_f[:, 0]), st_in(state_ssd_b[:, 0]), ssd_params,
                         row0=n_ctx, n_seq=bl, seq_len=ll, name="ssd_lat")

    lpad = ((0, 0), (MLA_NOPE, 2 * LANES - MLA_NOPE - MLA_ROPE))
    k_p, v_p = _mla_prep(None, cache_mla_ckv[:, 0].reshape(bl * past, MLA_KV_RANK),
                         jnp.pad(cache_mla_krope[:, 0].reshape(bl * past, MLA_ROPE), lpad),
                         None, mla_w, row0=0, n_rows=bl * past,
                         do_q=False, do_norm=False, tm=256, name="mla_prep_cache")
    o_c = _attention(q_all, k_all, v_all, None, None, row0=0, n_seq=bc, seq_len=lc, cache_len=0,
                     tq=lc, n_pairs=MLA_HEADS // 2, name="attn_ctx")
    o_l = _attention(q_all, k_all, v_all, k_p, v_p, row0=n_ctx, n_seq=bl, seq_len=ll, cache_len=past,
                     tq=min(1024, ll), n_pairs=1, name="attn_lat")

    w_out0_bf = w_out0[0].astype(BF16)
    x = _ffn([[y_ssd_c, y_ssd_l], [o_c, o_l]], [w_out0_bf[:SSD_INNER], w_out0_bf[SSD_INNER:]], x_parts, mods[0],
             ln1_g[0].reshape(1, D_MODEL), ln1_b[0].reshape(1, D_MODEL),
             ffn_w_gate[0].astype(BF16), ffn_w_up[0].astype(BF16), ffn_w_down[0].astype(BF16),
             ln2_g[0].reshape(1, D_MODEL), ln2_b[0].reshape(1, D_MODEL), tm=512, **geo)

    tm1 = 256
    cos_r, sin_r = _ret_tables(ll)
    ret_tables = (jnp.concatenate([jnp.ones((tm1, RET_QK), F32), cos_r], 0),
                  jnp.concatenate([jnp.zeros((tm1, RET_QK), F32), sin_r], 0))
    q1, k1, v1, g1 = _in1(x, mods[1], w_in1[0].astype(BF16), ret_tables, tm=tm1, **geo)
    dec = jnp.stack([ret_decay_f[0], ret_decay_b[0]], 1)
    dec = jnp.broadcast_to(jnp.pad(dec, ((0, 0), (0, 6)))[:, :, None], (RET_HEADS, 8, RET_V))
    y_ret_c, ret_f, ret_b = _retention(q1, k1, v1, g1, dec, None,
                                       row0=0, n_seq=bc, seq_len=lc, hps=4, name="ret_ctx")
    y_ret_l, = _retention(q1, k1, v1, g1, dec, (state_ret_f[:, 0], state_ret_b[:, 0]),
                          row0=n_ctx, n_seq=bl, seq_len=ll, hps=2, name="ret_lat")
    router_w = jnp.pad(moe_router[0], ((0, 0), (0, LANES - N_EXPERTS)))
    x, *routed = _outproj([[y_ret_c, y_ret_l]], [w_out1[0].astype(BF16)], [x], mods[1],
                          ln1_g[1].reshape(1, D_MODEL), ln1_b[1].reshape(1, D_MODEL),
                          gate_row=2, tm=MOE_TOK_CHUNK, name="out1", router_w=router_w, **geo)
    y_c, y_l = _moe(x, routed, mods[1], moe_w_gate[0].astype(BF16), moe_w_up[0].astype(BF16),
                    moe_w_down[0].astype(BF16), ln2_g[1].reshape(1, D_MODEL), ln2_b[1].reshape(1, D_MODEL), **geo)

    y_prompt = y_c.reshape(bc, lc, D_MODEL)
    y_sample = y_l.reshape(bl, ll, D_MODEL)
    new_ckv = ckv_all[:n_ctx].reshape(bc, 1, lc, MLA_KV_RANK)
    new_krope = kr_all[:n_ctx, MLA_NOPE:MLA_NOPE + MLA_ROPE].reshape(bc, 1, lc, MLA_ROPE)
    return (y_prompt, y_sample, new_ckv, new_krope,
            st_out(ssd_f)[:, None], st_out(ssd_b)[:, None], ret_f[:, None], ret_b[:, None])
```

```python
import functools
import math

import jax
import jax.numpy as jnp
from jax import lax
from jax.experimental import pallas as pl
from jax.experimental.pallas import tpu as pltpu

F32 = jnp.float32
BF16 = jnp.bfloat16
I32 = jnp.int32

D_MODEL = 1024
DEPTH = 2
GRID_W = 64
CHUNK = 128
SSD_HEADS = 8
SSD_HEAD_DIM = 64
SSD_INNER = SSD_HEADS * SSD_HEAD_DIM
SSD_GROUPS = 2
SSD_STATE = 64
SSD_XBC = SSD_INNER + 2 * SSD_GROUPS * SSD_STATE
MLA_HEADS = 8
MLA_NOPE = 64
MLA_ROPE = 32
MLA_V = 64
MLA_Q_RANK = 384
MLA_KV_RANK = 256
MLA_OUT = MLA_HEADS * MLA_V
RET_HEADS = 8
RET_QK = 128
RET_V = 256
MIX1 = RET_HEADS * RET_V
D_FF = 2816
N_EXPERTS = 8
D_FF_EXPERT = 3584
ALPHA = (2 * DEPTH) ** 0.25
LN_EPS = 1e-5
RMS_EPS = 1e-6
ROPE_BASE = 10000.0

LANES = 128
VMEM_LIMIT = 56 * 1024 * 1024
NEG_BIG = -1e30

MOE_SLOT_TILE = 512
MOE_TOK_CHUNK = 512
MOE_FF_CHUNK = 1792
MOE_CELL_ALIGN = 16
MOE_BIG_PIECE = 64
MOE_LOCAL_ROWS = 2 * MOE_TOK_CHUNK + N_EXPERTS * MOE_CELL_ALIGN
FFN_CHUNK = 512
RET_CHUNK = 256

_NT = (((1,), (1,)), ((), ()))
_TN = (((0,), (0,)), ((), ()))


def _cparams(*sem):
    return pltpu.CompilerParams(dimension_semantics=sem, vmem_limit_bytes=VMEM_LIMIT)


def _resident(shape):
    nd = len(shape)
    return pl.BlockSpec(shape, lambda *_: (0,) * nd, pipeline_mode=pl.Buffered(1))


def _silu(x):
    return x * jax.nn.sigmoid(x)


def _dot(a, b):
    return jnp.dot(a, b, preferred_element_type=F32)


def _cond_index(row, n_ctx, lat_len):
    return jnp.where(row < n_ctx, 0, 1 + (row - n_ctx) // lat_len)


def _layer_norm(y, g, b):
    mu = jnp.mean(y, -1, keepdims=True)
    d = y - mu
    var = jnp.mean(d * d, -1, keepdims=True)
    return d * lax.rsqrt(var + LN_EPS) * g + b


def _ada_kernel(c_ref, w_ref, b_ref, o_ref):
    s = _silu(c_ref[...])
    o_ref[...] = jnp.dot(s, w_ref[...], precision=lax.Precision.HIGHEST,
                         preferred_element_type=F32) + b_ref[...]


def _ada_vectors(cond, ada_w, ada_b):
    r = cond.shape[0]
    tn = 2048
    out = pl.pallas_call(
        _ada_kernel,
        grid=(DEPTH, 6 * D_MODEL // tn),
        in_specs=[pl.BlockSpec((r, D_MODEL), lambda l, j: (0, 0)),
                  pl.BlockSpec((None, D_MODEL, tn), lambda l, j: (l, 0, j)),
                  pl.BlockSpec((None, 1, tn), lambda l, j: (l, 0, j))],
        out_specs=pl.BlockSpec((None, r, tn), lambda l, j: (l, 0, j)),
        out_shape=jax.ShapeDtypeStruct((DEPTH, r, 6 * D_MODEL), F32),
        compiler_params=_cparams("parallel", "parallel"),
        name="ada",
    )(cond, ada_w, ada_b.reshape(DEPTH, 1, 6 * D_MODEL))
    out = out.reshape(DEPTH, r, 6, D_MODEL)
    return jnp.pad(out, ((0, 0), (0, 0), (0, 2), (0, 0)))


def _row_specs(parts, tm):
    cols = parts[0].shape[1]
    if len(parts) == 1:
        return [pl.BlockSpec((tm, cols), lambda i, *_: (i, 0))]
    nct = parts[0].shape[0] // tm
    return [pl.BlockSpec((tm, cols), lambda i, *_: (jnp.minimum(i, nct - 1), 0)),
            pl.BlockSpec((tm, cols), lambda i, *_: (jnp.maximum(i - nct, 0), 0))]


def _row_tile(refs, n_ctx_tiles):
    if len(refs) == 1:
        return refs[0][...]
    return jnp.where(pl.program_id(0) < n_ctx_tiles, refs[0][...], refs[1][...])


def _in1_kernel(x_ref, mod_ref, w_ref, cos_ref, sin_ref, q_out, k_out, v_out, g_out):
    hq = RET_HEADS * RET_QK
    x = x_ref[...]
    h = (x * (1.0 + mod_ref[1:2, :]) + mod_ref[0:1, :]).astype(BF16)
    cs, sn = cos_ref[...], sin_ref[...]
    for o_ref, off, scale in ((q_out, 0, 1.0), (k_out, hq, RET_QK ** -0.5)):
        y = _dot(h, w_ref[:, off:off + hq])
        for hh in range(RET_HEADS):
            hs = slice(hh * RET_QK, (hh + 1) * RET_QK)
            yh = y[:, hs] * scale
            o_ref[:, hs] = (yh * cs + pltpu.roll(yh, RET_QK // 2, axis=1) * sn).astype(o_ref.dtype)
    v_out[...] = _dot(h, w_ref[:, 2 * hq:2 * hq + MIX1]).astype(v_out.dtype)
    g_out[...] = _silu(_dot(h, w_ref[:, 2 * hq + MIX1:])).astype(g_out.dtype)


def _in1(x, mods, w, tables, *, tm, n_ctx, lat_len):
    nt = x.shape[0]
    nct = n_ctx // tm
    lat_tiles = lat_len // tm
    hq = RET_HEADS * RET_QK
    cmap = lambda i: (_cond_index(i * tm, n_ctx, lat_len), 0, 0)
    tmap = lambda i: (jnp.where(i < nct, 0, 1 + (i - nct) % lat_tiles), 0)
    widths = (hq, hq, MIX1, MIX1)
    return pl.pallas_call(
        _in1_kernel,
        grid=(nt // tm,),
        in_specs=[pl.BlockSpec((tm, D_MODEL), lambda i: (i, 0)), pl.BlockSpec((None, 8, D_MODEL), cmap),
                  _resident(w.shape), pl.BlockSpec((tm, RET_QK), tmap), pl.BlockSpec((tm, RET_QK), tmap)],
        out_specs=[pl.BlockSpec((tm, wd), lambda i: (i, 0)) for wd in widths],
        out_shape=[jax.ShapeDtypeStruct((nt, wd), BF16) for wd in widths],
        compiler_params=_cparams("parallel"),
        name="in1",
    )(x, mods, w, tables[0], tables[1])


def _cumsum_rows(tril_bf, x):
    hi = x.astype(BF16)
    r = x - hi.astype(F32)
    mid = r.astype(BF16)
    lo = (r - mid.astype(F32)).astype(BF16)
    return _dot(tril_bf, hi) + _dot(tril_bf, mid) + _dot(tril_bf, lo)


def _ssd_kernel(z_ref, xbc_ref, dt_ref, sf0_ref, sb0_ref, cw_ref, cb_ref, dtb_ref, alog_ref,
                dsk_ref, ng_ref, y_ref, sf_ref, sb_ref,
                yacc, xs_s, cm_s, bmt_s, xb_s, erb_s, *, seq_len):
    nc = seq_len // CHUNK
    hd, ns = SSD_HEAD_DIM, SSD_STATE
    gw = (SSD_HEADS // SSD_GROUPS) * hd
    ri = lax.broadcasted_iota(I32, (CHUNK, CHUNK), 0)
    ci = lax.broadcasted_iota(I32, (CHUNK, CHUNK), 1)
    lower = ri >= ci
    upper = ri <= ci
    tril_bf = jnp.where(lower, 1.0, 0.0).astype(BF16)
    rowid = lax.broadcasted_iota(I32, (CHUNK, 1), 0)
    lane = lax.broadcasted_iota(I32, (CHUNK, LANES), 1)
    src = lax.broadcasted_iota(I32, (LANES, SSD_INNER), 0)
    dst_head = lax.broadcasted_iota(I32, (LANES, SSD_INNER), 1) // hd
    spread_f = jnp.where(src == dst_head, 1.0, 0.0).astype(BF16)
    spread_b = jnp.where(src == dst_head + SSD_HEADS, 1.0, 0.0).astype(BF16)

    def per_head(v, spread, split=True):
        hi = v.astype(BF16)
        if not split:
            return _dot(hi, spread)
        lo = (v - hi.astype(F32)).astype(BF16)
        return _dot(hi, spread) + _dot(lo, spread)

    sf_ref[...] = sf0_ref[...]
    sb_ref[...] = sb0_ref[...]

    def fwd(c, carry):
        r0 = pl.multiple_of(c * CHUNK, CHUNK)
        rows = pl.ds(r0, CHUNK)
        cur = xbc_ref[rows, :]
        pstart = pl.multiple_of(jnp.maximum(r0 - 8, 0), 8)
        nstart = pl.multiple_of(jnp.minimum(r0 + CHUNK, seq_len - 8), 8)
        prev_row = xbc_ref[pl.ds(pstart, 8), :][7:8, :] * jnp.where(c > 0, 1.0, 0.0)
        next_row = xbc_ref[pl.ds(nstart, 8), :][0:1, :] * jnp.where(c < nc - 1, 1.0, 0.0)
        sh_prev = jnp.where(rowid == 0, prev_row, pltpu.roll(cur, 1, axis=0))
        sh_next = jnp.where(rowid == CHUNK - 1, next_row, pltpu.roll(cur, CHUNK - 1, axis=0))
        conv = cw_ref[0:1, :] * sh_prev + cw_ref[1:2, :] * cur + cw_ref[2:3, :] * sh_next + cb_ref[...]
        u = _silu(conv)
        xs = u[:, :SSD_INNER]
        bm = u[:, SSD_INNER:SSD_INNER + LANES]
        cm = u[:, SSD_INNER + LANES:]

        xr = dt_ref[rows, :] + dtb_ref[...]
        dt = jnp.maximum(xr, 0.0) + jnp.log1p(jnp.exp(-jnp.abs(xr)))
        la = -dt * jnp.exp(alog_ref[...])
        facs = _cumsum_rows(tril_bf, la)
        racs = facs[CHUNK - 1:CHUNK, :] - facs + la
        packed = jnp.where(lane < SSD_HEADS, facs,
                           jnp.where(lane < 2 * SSD_HEADS, racs, pltpu.roll(dt, 2 * SSD_HEADS, axis=1)))
        packed_t = packed.T
        e_f = jnp.exp(facs)
        e_r = jnp.exp(racs)
        w_f = dt * jnp.exp(facs[CHUNK - 1:CHUNK, :] - facs)
        w_b = dt * jnp.exp(racs[0:1, :] - racs)

        cm_bf = cm.astype(BF16)
        bm_bf = bm.astype(BF16)
        bmt_bf = bm.T.astype(BF16)
        xs_bf = xs.astype(BF16)
        ef_full = per_head(e_f, spread_f)
        xf = (xs * per_head(w_f, spread_f, split=False)).astype(BF16)
        xb_s[rows, :] = (xs * per_head(w_b, spread_b, split=False)).astype(BF16)
        yoffs, news, mixes = [], [], []
        for g in range(SSD_GROUPS):
            gl = slice(g * ns, (g + 1) * ns)
            s_g = lax.dot_general(cm_bf[:, gl], bm_bf[:, gl], _NT, preferred_element_type=F32)
            yoffs.append(_dot(cm_bf[:, gl], sf_ref[:, g * gw:(g + 1) * gw].astype(BF16)))
            news.append(_dot(bmt_bf[g * ns:(g + 1) * ns, :], xf[:, g * gw:(g + 1) * gw]))
            for hh in range(SSD_HEADS // SSD_GROUPS):
                h = g * (SSD_HEADS // SSD_GROUPS) + hh
                hb = SSD_HEADS + h
                seg_f = facs[:, h:h + 1] - packed_t[h:h + 1, :]
                seg_b = racs[:, hb:hb + 1] - packed_t[hb:hb + 1, :]
                d_f = jnp.exp(jnp.where(lower, seg_f, NEG_BIG))
                d_b = jnp.exp(jnp.where(upper, seg_b, NEG_BIG))
                dt_f_row = packed_t[2 * SSD_HEADS + h:2 * SSD_HEADS + h + 1, :]
                dt_b_row = packed_t[2 * SSD_HEADS + hb:2 * SSD_HEADS + hb + 1, :]
                mixes.append((s_g * (d_f * dt_f_row + d_b * dt_b_row)).astype(BF16))
        for p in range(SSD_HEADS // 2):
            ps = slice(p * LANES, (p + 1) * LANES)
            x_pair = xs_bf[:, ps]
            x_diag = jnp.concatenate([jnp.where(lane < hd, x_pair, jnp.zeros((), BF16)),
                                      jnp.where(lane >= hd, x_pair, jnp.zeros((), BF16))], 0)
            yacc[rows, ps] = _dot(jnp.concatenate([mixes[2 * p], mixes[2 * p + 1]], 1), x_diag)
        yacc[rows, :] += ef_full * jnp.concatenate(yoffs, -1)
        sf_ref[...] = ef_full[CHUNK - 1:CHUNK, :] * sf_ref[...] + jnp.concatenate(news, -1)
        xs_s[rows, :] = xs
        cm_s[rows, :] = cm_bf
        bmt_s[c] = bmt_bf
        erb_s[rows, :] = e_r
        return carry

    lax.fori_loop(0, nc, fwd, 0)

    def bwd(i, carry):
        c = nc - 1 - i
        r0 = pl.multiple_of(c * CHUNK, CHUNK)
        rows = pl.ds(r0, CHUNK)
        cm_bf = cm_s[rows, :]
        bmt_bf = bmt_s[c]
        er_full = per_head(erb_s[rows, :], spread_b)
        yoffs, news = [], []
        for g in range(SSD_GROUPS):
            gl = slice(g * ns, (g + 1) * ns)
            yoffs.append(_dot(cm_bf[:, gl], sb_ref[:, g * gw:(g + 1) * gw].astype(BF16)))
            news.append(_dot(bmt_bf[g * ns:(g + 1) * ns, :], xb_s[rows, g * gw:(g + 1) * gw]))
        sb_ref[...] = er_full[0:1, :] * sb_ref[...] + jnp.concatenate(news, -1)
        yv = yacc[rows, :] + er_full * jnp.concatenate(yoffs, -1) + dsk_ref[...] * xs_s[rows, :]
        gz = yv * z_ref[rows, :].astype(F32)
        ms = jnp.mean(gz * gz, -1, keepdims=True)
        y_ref[rows, :] = (gz * lax.rsqrt(ms + RMS_EPS) * ng_ref[...]).astype(y_ref.dtype)
        return carry

    lax.fori_loop(0, nc, bwd, 0)


def _ssd(z, xbc, dt, sf0, sb0, params, *, row0, n_seq, seq_len, name):
    cw, cb, dtb, alog, dsk, ng = params
    nc = seq_len // CHUNK
    blk0 = row0 // seq_len
    rmap = lambda b: (b + blk0, 0)
    smap = lambda b: (b, 0, 0)
    const = lambda b: (0, 0)
    st = jax.ShapeDtypeStruct((n_seq, SSD_STATE, SSD_INNER), F32)
    return pl.pallas_call(
        functools.partial(_ssd_kernel, seq_len=seq_len),
        grid=(n_seq,),
        in_specs=[pl.BlockSpec((seq_len, SSD_INNER), rmap),
                  pl.BlockSpec((seq_len, SSD_XBC), rmap),
                  pl.BlockSpec((seq_len, LANES), rmap),
                  pl.BlockSpec((None, SSD_STATE, SSD_INNER), smap),
                  pl.BlockSpec((None, SSD_STATE, SSD_INNER), smap),
                  pl.BlockSpec(cw.shape, const), pl.BlockSpec(cb.shape, const),
                  pl.BlockSpec(dtb.shape, const), pl.BlockSpec(alog.shape, const),
                  pl.BlockSpec(dsk.shape, const), pl.BlockSpec(ng.shape, const)],
        out_specs=[pl.BlockSpec((seq_len, SSD_INNER), lambda b: (b, 0)),
                   pl.BlockSpec((None, SSD_STATE, SSD_INNER), smap),
                   pl.BlockSpec((None, SSD_STATE, SSD_INNER), smap)],
        out_shape=[jax.ShapeDtypeStruct((n_seq * seq_len, SSD_INNER), BF16), st, st],
        scratch_shapes=[pltpu.VMEM((seq_len, SSD_INNER), F32),
                        pltpu.VMEM((seq_len, SSD_INNER), F32),
                        pltpu.VMEM((seq_len, LANES), BF16),
                        pltpu.VMEM((nc, LANES, CHUNK), BF16),
                        pltpu.VMEM((seq_len, SSD_INNER), BF16),
                        pltpu.VMEM((seq_len, LANES), F32)],
        compiler_params=_cparams("parallel"),
        name=name,
    )(z, xbc, dt, sf0, sb0, cw, cb, dtb, alog, dsk, ng)


def _rms(x, g):
    return x * lax.rsqrt(jnp.mean(x * x, -1, keepdims=True) + RMS_EPS) * g


def _mla_prep_kernel(*refs, do_q, do_norm, do_rope):
    it = iter(refs)
    qlat_ref = next(it) if do_q else None
    kv_ref = next(it)
    kr_ref = next(it)
    cos_ref = next(it) if do_rope else None
    sin_ref = next(it) if do_rope else None
    if do_q:
        gq_ref, wq_ref = next(it), next(it)
        wqr_ref = next(it) if do_rope else None
    gkv_ref = next(it) if do_norm else None
    wk_ref, wv_ref = next(it), next(it)
    q_out = next(it) if do_q else None
    k_out, v_out = next(it), next(it)
    ckv_out = next(it) if do_norm else None
    cs, sn = (cos_ref[...], sin_ref[...]) if do_rope else (None, None)
    _mla_project(qlat_ref[...] if do_q else None, kv_ref[...], kr_ref[...], cs, sn,
                 (gq_ref, wq_ref, wqr_ref) if do_q else None, gkv_ref, wk_ref, wv_ref,
                 q_out, k_out, v_out, ckv_out)


def _mla_project(qlat, kvlat, kr, cs, sn, q_w, gkv_ref, wk_ref, wv_ref, q_out, k_out, v_out, ckv_out):
    do_rope = cs is not None
    if q_w is not None:
        gq_ref, wq_ref, wqr_ref = q_w
        qn = _rms(qlat, gq_ref[...]).astype(BF16)
        qa = _dot(qn, wq_ref[...])
        if do_rope:
            qb = _dot(qn, wqr_ref[...])
        scale = (MLA_NOPE + MLA_ROPE) ** -0.5 * math.log2(math.e)
        for h in range(MLA_HEADS):
            hs = slice(h * LANES, (h + 1) * LANES)
            qh = qa[:, hs] * cs + qb[:, hs] * sn if do_rope else qa[:, hs]
            q_out[h] = (qh * scale).astype(BF16)
    ckv = kvlat
    if gkv_ref is not None:
        ckv = _rms(ckv, gkv_ref[...])
        ckv_out[...] = ckv
    ckv_bf = ckv.astype(BF16)
    krp = kr[:, :LANES] * cs + kr[:, LANES:] * sn if do_rope else kr[:, :LANES]
    kn = _dot(ckv_bf, wk_ref[...])
    for h in range(MLA_HEADS):
        k_out[h] = (kn[:, h * LANES:(h + 1) * LANES] + krp).astype(BF16)
    v_out[...] = _dot(ckv_bf, wv_ref[...]).astype(BF16)


def _mla_prep(qlat, kv, kr, tables, weights, *, row0, n_rows, do_q, do_norm, tm, name):
    do_rope = tables is not None
    gq, wq, wqr, gkv, wk, wv = weights
    b0 = row0 // tm
    rmap = lambda i: (i + b0, 0)
    omap = lambda i: (i, 0)
    hmap = lambda i: (0, i, 0)
    ins, specs = [], []

    def add(a, spec):
        ins.append(a)
        specs.append(spec)

    if do_q:
        add(qlat, pl.BlockSpec((tm, MLA_Q_RANK), rmap))
    add(kv, pl.BlockSpec((tm, MLA_KV_RANK), rmap))
    add(kr, pl.BlockSpec((tm, 2 * LANES), rmap))
    if do_rope:
        lat_tiles = tables[0].shape[0] // tm
        tmap = lambda i: (i % lat_tiles, 0)
        add(tables[0], pl.BlockSpec((tm, LANES), tmap))
        add(tables[1], pl.BlockSpec((tm, LANES), tmap))
    if do_q:
        add(gq, _resident(gq.shape))
        add(wq, _resident(wq.shape))
        if do_rope:
            add(wqr, _resident(wqr.shape))
    if do_norm:
        add(gkv, _resident(gkv.shape))
    add(wk, _resident(wk.shape))
    add(wv, _resident(wv.shape))
    out_shape, out_specs = [], []
    if do_q:
        out_shape.append(jax.ShapeDtypeStruct((MLA_HEADS, n_rows, LANES), BF16))
        out_specs.append(pl.BlockSpec((MLA_HEADS, tm, LANES), hmap))
    out_shape.append(jax.ShapeDtypeStruct((MLA_HEADS, n_rows, LANES), BF16))
    out_specs.append(pl.BlockSpec((MLA_HEADS, tm, LANES), hmap))
    out_shape.append(jax.ShapeDtypeStruct((n_rows, MLA_OUT), BF16))
    out_specs.append(pl.BlockSpec((tm, MLA_OUT), omap))
    if do_norm:
        out_shape.append(jax.ShapeDtypeStruct((n_rows, MLA_KV_RANK), F32))
        out_specs.append(pl.BlockSpec((tm, MLA_KV_RANK), omap))
    return pl.pallas_call(
        functools.partial(_mla_prep_kernel, do_q=do_q, do_norm=do_norm, do_rope=do_rope),
        grid=(n_rows // tm,),
        in_specs=specs, out_specs=out_specs, out_shape=out_shape,
        compiler_params=_cparams("parallel"),
        name=name,
    )(*ins)


_IN0_SEGS = (SSD_INNER, SSD_XBC, LANES, MLA_Q_RANK, MLA_KV_RANK, 2 * LANES)


def _in0_kernel(*refs, n_x, n_ctx_tiles):
    x_refs = refs[:n_x]
    (mod_ref, w_ref, cos_ref, sin_ref, gq_ref, wq_ref, wqr_ref, gkv_ref, wk_ref, wv_ref,
     z_out, xbc_out, dt_out, q_out, k_out, v_out, ckv_out, kr_out) = refs[n_x:]
    x = _row_tile(x_refs, n_ctx_tiles)
    h = (x * (1.0 + mod_ref[1:2, :]) + mod_ref[0:1, :]).astype(BF16)
    segs, acc = [], 0
    for wd in _IN0_SEGS:
        segs.append(_dot(h, w_ref[:, acc:acc + wd]))
        acc += wd
    z, xbc, dt, qlat, kvlat, kr = segs
    z_out[...] = _silu(z).astype(z_out.dtype)
    xbc_out[...] = xbc
    dt_out[...] = dt
    kr_out[...] = kr[:, :LANES]
    _mla_project(qlat, kvlat, kr, cos_ref[...], sin_ref[...], (gq_ref, wq_ref, wqr_ref), gkv_ref,
                 wk_ref, wv_ref, q_out, k_out, v_out, ckv_out)


def _in0(x_parts, mods, w, tables, mla_w, *, tm, n_ctx, lat_len):
    nt = sum(p.shape[0] for p in x_parts)
    nct = n_ctx // tm
    lat_tiles = lat_len // tm
    gq, wq, wqr, gkv, wk, wv = mla_w
    cmap = lambda i: (_cond_index(i * tm, n_ctx, lat_len), 0, 0)
    tmap = lambda i: (jnp.where(i < nct, 0, 1 + (i - nct) % lat_tiles), 0)
    row = lambda cols: pl.BlockSpec((tm, cols), lambda i: (i, 0))
    heads = pl.BlockSpec((MLA_HEADS, tm, LANES), lambda i: (0, i, 0))
    consts = [w, gq, wq, wqr, gkv, wk, wv]
    return pl.pallas_call(
        functools.partial(_in0_kernel, n_x=len(x_parts), n_ctx_tiles=nct),
        grid=(nt // tm,),
        in_specs=(_row_specs(x_parts, tm) + [pl.BlockSpec((None, 8, D_MODEL), cmap), _resident(w.shape),
                                             pl.BlockSpec((tm, LANES), tmap), pl.BlockSpec((tm, LANES), tmap)]
                  + [_resident(a.shape) for a in consts[1:]]),
        out_specs=[row(SSD_INNER), row(SSD_XBC), row(LANES), heads, heads, row(MLA_OUT),
                   row(MLA_KV_RANK), row(LANES)],
        out_shape=[jax.ShapeDtypeStruct((nt, SSD_INNER), BF16),
                   jax.ShapeDtypeStruct((nt, SSD_XBC), F32),
                   jax.ShapeDtypeStruct((nt, LANES), F32),
                   jax.ShapeDtypeStruct((MLA_HEADS, nt, LANES), BF16),
                   jax.ShapeDtypeStruct((MLA_HEADS, nt, LANES), BF16),
                   jax.ShapeDtypeStruct((nt, MLA_OUT), BF16),
                   jax.ShapeDtypeStruct((nt, MLA_KV_RANK), F32),
                   jax.ShapeDtypeStruct((nt, LANES), F32)],
        compiler_params=_cparams("parallel"),
        name="in0",
    )(*x_parts, mods, w, tables[0], tables[1], gq, wq, wqr, gkv, wk, wv)


def _attn_kernel(*refs, seq_len, cache_len, kblk, n_pairs):
    if cache_len:
        q_ref, k_ref, v_ref, kc_ref, vc_ref, o_ref = refs
    else:
        q_ref, k_ref, v_ref, o_ref = refs
    tq = q_ref.shape[1]
    blocks = [(k_ref, v_ref, i * kblk) for i in range(seq_len // kblk)]
    if cache_len:
        blocks += [(kc_ref, vc_ref, i * kblk) for i in range(cache_len // kblk)]
    vlane = lax.broadcasted_iota(I32, (kblk, LANES), 1)
    lane = lax.broadcasted_iota(I32, (tq, LANES), 1)
    for pp in range(n_pairs):
        ps = slice(pp * LANES, (pp + 1) * LANES)
        outs = []
        for hh in range(2):
            own = (vlane < MLA_V) if hh == 0 else (vlane >= MLA_V)
            q = q_ref[2 * pp + hh]
            m = None
            for kr, vr, off in blocks:
                s = lax.dot_general(q, kr[2 * pp + hh, off:off + kblk, :], _NT, preferred_element_type=F32)
                v_aug = jnp.where(own, vr[off:off + kblk, ps], jnp.ones((), BF16))
                bmax = jnp.max(s, -1, keepdims=True)
                if m is None:
                    m = bmax
                    acc = _dot(jnp.exp2(s - m).astype(BF16), v_aug)
                else:
                    m_new = jnp.maximum(m, bmax)
                    acc = jnp.exp2(m - m_new) * acc + _dot(jnp.exp2(s - m_new).astype(BF16), v_aug)
                    m = m_new
            outs.append(acc / pltpu.roll(acc, MLA_V, axis=1))
        o_ref[:, ps] = jnp.where(lane < MLA_V, outs[0], outs[1]).astype(o_ref.dtype)


def _attention(q, k, v, kc, vc, *, row0, n_seq, seq_len, cache_len, tq, n_pairs, name):
    nq = seq_len // tq
    kblk = min(512, seq_len)
    n = n_seq * seq_len
    q0, s0 = row0 // tq, row0 // seq_len
    hb, vw = 2 * n_pairs, n_pairs * LANES
    ins = [q, k, v]
    specs = [pl.BlockSpec((hb, tq, LANES), lambda b, hp, qi: (hp, q0 + b * nq + qi, 0)),
             pl.BlockSpec((hb, seq_len, LANES), lambda b, hp, qi: (hp, s0 + b, 0)),
             pl.BlockSpec((seq_len, vw), lambda b, hp, qi: (s0 + b, hp))]
    if cache_len:
        ins += [kc, vc]
        specs += [pl.BlockSpec((hb, cache_len, LANES), lambda b, hp, qi: (hp, b, 0)),
                  pl.BlockSpec((cache_len, vw), lambda b, hp, qi: (b, hp))]
    return pl.pallas_call(
        functools.partial(_attn_kernel, seq_len=seq_len, cache_len=cache_len, kblk=kblk, n_pairs=n_pairs),
        grid=(n_seq, MLA_HEADS // hb, nq),
        in_specs=specs,
        out_specs=pl.BlockSpec((tq, vw), lambda b, hp, qi: (b * nq + qi, hp)),
        out_shape=jax.ShapeDtypeStruct((n, MLA_OUT), BF16),
        compiler_params=_cparams("parallel", "parallel", "arbitrary"),
        name=name,
    )(*ins)


def _outproj_kernel(*refs, n_parts, n_ctx_tiles, gate_row, with_router):
    it = iter(refs)
    acc = None
    for n in n_parts[:-1]:
        a = _row_tile([next(it) for _ in range(n)], n_ctx_tiles)
        part = _dot(a, next(it)[...])
        acc = part if acc is None else acc + part
    x = _row_tile([next(it) for _ in range(n_parts[-1])], n_ctx_tiles)
    mod_ref, g_ref, b_ref = next(it), next(it), next(it)
    router_ref = next(it) if with_router else None
    o_ref = next(it)
    y = ALPHA * x + mod_ref[gate_row:gate_row + 1, :] * acc
    res = _layer_norm(y, g_ref[...], b_ref[...])
    o_ref[...] = res
    if with_router:
        _route(res, mod_ref, router_ref, *it)


def _outproj(acts, ws, x_parts, mods, g, b, *, gate_row, tm, n_ctx, lat_len, name, router_w=None):
    nt = sum(p.shape[0] for p in x_parts)
    cmap = lambda i: (_cond_index(i * tm, n_ctx, lat_len), 0, 0)
    ins, specs = [], []
    for parts, w in zip(acts, ws):
        ins += list(parts) + [w]
        specs += _row_specs(parts, tm) + [_resident(w.shape)]
    ins += list(x_parts) + [mods, g, b]
    specs += _row_specs(x_parts, tm) + [pl.BlockSpec((None, 8, D_MODEL), cmap),
                                        _resident(g.shape), _resident(b.shape)]
    out_specs = [pl.BlockSpec((tm, D_MODEL), lambda i: (i, 0))]
    out_shape = [jax.ShapeDtypeStruct((nt, D_MODEL), F32)]
    if router_w is not None:
        assert tm == MOE_TOK_CHUNK
        ins.append(router_w)
        specs.append(_resident(router_w.shape))
        out_specs += [pl.BlockSpec((tm, D_MODEL), lambda i: (i, 0)),
                      pl.BlockSpec((tm, LANES), lambda i: (i, 0)),
                      pl.BlockSpec((None, 8, tm), lambda i: (i, 0, 0)),
                      pl.BlockSpec((None, 8, LANES), lambda i: (i, 0, 0))]
        out_shape += [jax.ShapeDtypeStruct((nt, D_MODEL), BF16),
                      jax.ShapeDtypeStruct((nt, LANES), F32),
                      jax.ShapeDtypeStruct((nt // tm, 8, tm), F32),
                      jax.ShapeDtypeStruct((nt // tm, 8, LANES), F32)]
    n_parts = tuple(len(p) for p in acts) + (len(x_parts),)
    outs = pl.pallas_call(
        functools.partial(_outproj_kernel, n_parts=n_parts, n_ctx_tiles=n_ctx // tm, gate_row=gate_row,
                          with_router=router_w is not None),
        grid=(nt // tm,),
        in_specs=specs,
        out_specs=out_specs,
        out_shape=out_shape,
        compiler_params=_cparams("parallel"),
        name=name,
    )(*ins)
    return outs[0] if router_w is None else outs


def _ffn_kernel(*refs, n_parts, n_ctx_tiles, ff_chunks):
    it = iter(refs)
    mix = None
    for n in n_parts[:-1]:
        a = _row_tile([next(it) for _ in range(n)], n_ctx_tiles)
        part = _dot(a, next(it)[...])
        mix = part if mix is None else mix + part
    x0 = _row_tile([next(it) for _ in range(n_parts[-1])], n_ctx_tiles)
    mod_ref, g1_ref, b1_ref, wg_ref, wu_ref, wd_ref, g_ref, b_ref, o_ref = it
    x = _layer_norm(ALPHA * x0 + mod_ref[2:3, :] * mix, g1_ref[...], b1_ref[...])
    h = (x * (1.0 + mod_ref[4:5, :]) + mod_ref[3:4, :]).astype(BF16)
    acc = None
    for a, b in ff_chunks:
        gt = _dot(h, wg_ref[:, a:b])
        up = _dot(h, wu_ref[:, a:b])
        act = (_silu(gt) * up).astype(BF16)
        part = _dot(act, wd_ref[a:b, :])
        acc = part if acc is None else acc + part
    y = ALPHA * x + mod_ref[5:6, :] * acc
    o_ref[...] = _layer_norm(y, g_ref[...], b_ref[...])


def _ffn(acts, ws, x_parts, mods, g1, b1, wg, wu, wd, g, b, *, tm, n_ctx, lat_len):
    nt = sum(p.shape[0] for p in x_parts)
    ff = wg.shape[1]
    chunks, a = [], 0
    while a < ff:
        chunks.append((a, min(a + FFN_CHUNK, ff)))
        a += FFN_CHUNK
    cmap = lambda i: (_cond_index(i * tm, n_ctx, lat_len), 0, 0)
    ins, specs = [], []
    for parts, w in zip(acts, ws):
        ins += list(parts) + [w]
        specs += _row_specs(parts, tm) + [_resident(w.shape)]
    consts = [g1, b1, wg, wu, wd, g, b]
    ins += list(x_parts) + [mods] + consts
    specs += _row_specs(x_parts, tm) + [pl.BlockSpec((None, 8, D_MODEL), cmap)]
    specs += [_resident(a.shape) for a in consts]
    n_parts = tuple(len(p) for p in acts) + (len(x_parts),)
    return pl.pallas_call(
        functools.partial(_ffn_kernel, n_parts=n_parts, n_ctx_tiles=n_ctx // tm, ff_chunks=tuple(chunks)),
        grid=(nt // tm,),
        in_specs=specs,
        out_specs=pl.BlockSpec((tm, D_MODEL), lambda i: (i, 0)),
        out_shape=jax.ShapeDtypeStruct((nt, D_MODEL), F32),
        compiler_params=_cparams("parallel"),
        name="out0_ffn",
    )(*ins)


def _ret_kernel(*refs, seq_len, n_heads, from_zero):
    it = iter(refs)
    q_ref, k_ref, v_ref, g_ref, dec_ref = next(it), next(it), next(it), next(it), next(it)
    sf0_ref, sb0_ref = (None, None) if from_zero else (next(it), next(it))
    y_ref = next(it)
    sf_out, sb_out = (next(it), next(it)) if from_zero else (None, None)
    yacc, kb_s, dcomb_s, ev_s, wk_s, cd_s, sf_ref, sb_ref = it
    rc = RET_CHUNK
    nc = seq_len // rc
    unroll = min(2, nc)

    @pl.when(pl.program_id(1) == 0)
    def _():
        ri = lax.broadcasted_iota(I32, (rc, rc), 0)
        ci = lax.broadcasted_iota(I32, (rc, rc), 1)
        dij = (ri - ci).astype(F32)
        pos_k = lax.broadcasted_iota(I32, (rc, RET_QK), 0).astype(F32)
        pos_v = lax.broadcasted_iota(I32, (rc, RET_V), 0).astype(F32)
        for hh in range(n_heads):
            la_f = -jnp.exp(dec_ref[hh, 0:1, :])
            la_b = -jnp.exp(dec_ref[hh, 1:2, :])
            dcomb_s[hh] = (jnp.exp(jnp.where(ri >= ci, dij * la_f[:, :rc], NEG_BIG)) +
                           jnp.exp(jnp.where(ri <= ci, -dij * la_b[:, :rc], NEG_BIG)))
            ev_s[hh, 0] = jnp.exp((pos_v + 1.0) * la_f)
            ev_s[hh, 1] = jnp.exp((rc - pos_v) * la_b)
            wk_s[hh, 0] = jnp.exp((rc - 1.0 - pos_k) * la_f[:, :RET_QK])
            wk_s[hh, 1] = jnp.exp(pos_k * la_b[:, :RET_QK])
            cd_s[hh, 0:1, :] = jnp.exp(rc * la_f)
            cd_s[hh, 1:2, :] = jnp.exp(rc * la_b)

    if from_zero:
        sf_ref[...] = jnp.zeros_like(sf_ref)
        sb_ref[...] = jnp.zeros_like(sb_ref)
    else:
        for hh in range(n_heads):
            sf_ref[hh] = sf0_ref[hh].T
            sb_ref[hh] = sb0_ref[hh].T

    def fwd(c, carry):
        rows = pl.ds(pl.multiple_of(c * rc, rc), rc)
        for hh in range(n_heads):
            qs = slice(hh * RET_QK, (hh + 1) * RET_QK)
            vs = slice(hh * RET_V, (hh + 1) * RET_V)
            q_bf = q_ref[rows, qs]
            k_bf = k_ref[rows, qs]
            k = k_bf.astype(F32)
            v = v_ref[rows, vs]
            s = lax.dot_general(q_bf, k_bf, _NT, preferred_element_type=F32)
            y = _dot((s * dcomb_s[hh]).astype(BF16), v)
            y = y + _dot(q_bf, sf_ref[hh].astype(BF16)) * ev_s[hh, 0]
            yacc[rows, vs] = y
            upd = lax.dot_general((k * wk_s[hh, 0]).astype(BF16), v, _TN, preferred_element_type=F32)
            sf_ref[hh] = cd_s[hh, 0:1, :] * sf_ref[hh] + upd
            kb_s[rows, qs] = (k * wk_s[hh, 1]).astype(BF16)
        return carry

    lax.fori_loop(0, nc, fwd, 0, unroll=unroll)

    def bwd(i, carry):
        rows = pl.ds(pl.multiple_of((nc - 1 - i) * rc, rc), rc)
        for hh in range(n_heads):
            qs = slice(hh * RET_QK, (hh + 1) * RET_QK)
            vs = slice(hh * RET_V, (hh + 1) * RET_V)
            v = v_ref[rows, vs]
            y = yacc[rows, vs] + _dot(q_ref[rows, qs], sb_ref[hh].astype(BF16)) * ev_s[hh, 1]
            upd = lax.dot_general(kb_s[rows, qs], v, _TN, preferred_element_type=F32)
            sb_ref[hh] = cd_s[hh, 1:2, :] * sb_ref[hh] + upd
            mu = jnp.mean(y, -1, keepdims=True)
            d = y - mu
            var = jnp.mean(d * d, -1, keepdims=True)
            yn = d * lax.rsqrt(var + LN_EPS)
            y_ref[rows, vs] = (yn * g_ref[rows, vs].astype(F32)).astype(y_ref.dtype)
        return carry

    lax.fori_loop(0, nc, bwd, 0, unroll=unroll)

    if from_zero:
        for hh in range(n_heads):
            sf_out[hh] = sf_ref[hh].T
            sb_out[hh] = sb_ref[hh].T


def _retention(q, k, v, g, dec, states, *, row0, n_seq, seq_len, hps, name):
    from_zero = states is None
    blk0 = row0 // seq_len
    qmap = lambda h, b: (b + blk0, h)
    state_spec = pl.BlockSpec((None, hps, RET_V, RET_QK), lambda h, b: (b, h, 0, 0))
    ins = [q, k, v, g, dec]
    specs = [pl.BlockSpec((seq_len, hps * RET_QK), qmap), pl.BlockSpec((seq_len, hps * RET_QK), qmap),
             pl.BlockSpec((seq_len, hps * RET_V), qmap), pl.BlockSpec((seq_len, hps * RET_V), qmap),
             pl.BlockSpec((hps, 8, RET_V), lambda h, b: (h, 0, 0))]
    out_specs = [pl.BlockSpec((seq_len, hps * RET_V), lambda h, b: (b, h))]
    out_shape = [jax.ShapeDtypeStruct((n_seq * seq_len, MIX1), BF16)]
    if from_zero:
        st = jax.ShapeDtypeStruct((n_seq, RET_HEADS, RET_V, RET_QK), F32)
        out_specs += [state_spec, state_spec]
        out_shape += [st, st]
    else:
        ins += list(states)
        specs += [state_spec, state_spec]
    rc = RET_CHUNK
    return pl.pallas_call(
        functools.partial(_ret_kernel, seq_len=seq_len, n_heads=hps, from_zero=from_zero),
        grid=(RET_HEADS // hps, n_seq),
        in_specs=specs,
        out_specs=out_specs,
        out_shape=out_shape,
        scratch_shapes=[pltpu.VMEM((seq_len, hps * RET_V), F32),
                        pltpu.VMEM((seq_len, hps * RET_QK), BF16),
                        pltpu.VMEM((hps, rc, rc), F32),
                        pltpu.VMEM((hps, 2, rc, RET_V), F32),
                        pltpu.VMEM((hps, 2, rc, RET_QK), F32),
                        pltpu.VMEM((hps, 8, RET_V), F32),
                        pltpu.VMEM((hps, RET_QK, RET_V), F32),
                        pltpu.VMEM((hps, RET_QK, RET_V), F32)],
        compiler_params=_cparams("parallel", "arbitrary"),
        name=name,
    )(*ins)


def _route(x, mod_ref, w_ref, hb_ref, tok_ref, lpt_ref, cap_ref):
    tm = x.shape[0]
    h = x * (1.0 + mod_ref[4:5, :]) + mod_ref[3:4, :]
    h_hi = h.astype(BF16)
    hb_ref[...] = h_hi
    h_lo = (h - h_hi.astype(F32)).astype(BF16)
    w = w_ref[...]
    w_hi = w.astype(BF16)
    w_lo = (w - w_hi.astype(F32)).astype(BF16)
    w_both = w_hi + pltpu.roll(w_lo.astype(F32), N_EXPERTS, axis=1).astype(BF16)
    part = _dot(h_hi, w_both) + _dot(h_lo, w_both)
    logits = part + pltpu.roll(part, LANES - N_EXPERTS, axis=1)
    lt = logits.T[0:N_EXPERTS, :]
    row = lax.broadcasted_iota(I32, (N_EXPERTS, tm), 0)
    m1 = jnp.max(lt, 0, keepdims=True)
    i1 = jnp.min(jnp.where(lt == m1, row, N_EXPERTS), 0, keepdims=True)
    rest = jnp.where(row == i1, NEG_BIG, lt)
    m2 = jnp.max(rest, 0, keepdims=True)
    i2 = jnp.min(jnp.where(rest == m2, row, N_EXPERTS), 0, keepdims=True)
    e = jnp.exp(m2 - m1)
    g1 = 1.0 / (1.0 + e)
    g2 = e / (1.0 + e)
    sel1 = row == i1
    sel2 = row == i2
    onehot = jnp.where(sel1 | sel2, 1.0, 0.0)
    ri = lax.broadcasted_iota(I32, (tm, tm), 0)
    ci = lax.broadcasted_iota(I32, (tm, tm), 1)
    earlier = jnp.where(ri < ci, 1.0, 0.0).astype(BF16)
    prefix = _dot(onehot.astype(BF16), earlier)
    n_col = jnp.sum(onehot, 1, keepdims=True)
    cap_col = jnp.floor((n_col + (MOE_CELL_ALIGN - 1.0)) * (1.0 / MOE_CELL_ALIGN)) * MOE_CELL_ALIGN
    sub = lax.broadcasted_iota(I32, (N_EXPERTS, 1), 0)
    base_col = jnp.zeros((N_EXPERTS, 1), F32)
    for ex in range(N_EXPERTS - 1):
        base_col = base_col + jnp.where(sub > ex, cap_col[ex:ex + 1, :], 0.0)
    local = prefix + base_col
    lpos1 = jnp.sum(jnp.where(sel1, local, 0.0), 0, keepdims=True)
    lpos2 = jnp.sum(jnp.where(sel2, local, 0.0), 0, keepdims=True)
    lpt = jnp.where(row == 0, g1,
          jnp.where(row == 1, g2,
          jnp.where(row == 2, lpos1,
          jnp.where(row == 3, lpos2, 0.0))))
    lpt_ref[...] = lpt
    tok_ref[...] = jnp.concatenate([lpt, jnp.zeros((LANES - N_EXPERTS, tm), F32)], 0).T
    elane = lax.broadcasted_iota(I32, (N_EXPERTS, LANES), 1)
    cap_row = jnp.sum(jnp.where(elane == sub, cap_col, 0.0), 0, keepdims=True)
    cap_ref[...] = jnp.broadcast_to(cap_row, cap_ref.shape)


def _cell_copies(tabs, chunk, hbm_ref, buf_ref, slot, sem, *, to_local, wait):
    g_ref, a_ref, nbig_ref, nsmall_ref = tabs
    for e in range(N_EXPERTS):
        k = chunk * N_EXPERTS + e
        g0, a0, nbig, nsmall = g_ref[k], a_ref[k], nbig_ref[k], nsmall_ref[k]

        def piece(i, carry, rows, goff, aoff):
            g = pl.multiple_of(goff + i * rows, MOE_CELL_ALIGN)
            a = pl.multiple_of(aoff + i * rows, MOE_CELL_ALIGN)
            far = hbm_ref.at[pl.ds(g, rows), :]
            near = buf_ref.at[slot, pl.ds(a, rows), :]
            cp = (pltpu.make_async_copy(far, near, sem.at[slot]) if to_local
                  else pltpu.make_async_copy(near, far, sem.at[slot]))
            if wait:
                cp.wait()
            else:
                cp.start()
            return carry

        lax.fori_loop(0, nbig, functools.partial(piece, rows=MOE_BIG_PIECE, goff=g0, aoff=a0), 0)
        done = nbig * MOE_BIG_PIECE
        lax.fori_loop(0, nsmall, functools.partial(piece, rows=MOE_CELL_ALIGN, goff=g0 + done, aoff=a0 + done), 0)


def _dispatch_kernel(g_ref, a_ref, nbig_ref, nsmall_ref, hb_ref, lpt_ref, xs_hbm, ybuf, sem, *, n_fill):
    c = pl.program_id(0)
    nc = pl.num_programs(0)
    slot = c % 2
    tabs = (g_ref, a_ref, nbig_ref, nsmall_ref)

    @pl.when(c == 0)
    def _():
        ybuf[2] = jnp.zeros(ybuf.shape[1:], ybuf.dtype)
        for j in range(n_fill):
            _cell_copies(tabs, nc + j, xs_hbm, ybuf, 2, sem, to_local=False, wait=False)

    @pl.when(c == nc - 1)
    def _():
        for j in range(n_fill):
            _cell_copies(tabs, nc + j, xs_hbm, ybuf, 2, sem, to_local=False, wait=True)

    l1 = lpt_ref[2:3, :]
    l2 = lpt_ref[3:4, :]
    rb = MOE_LOCAL_ROWS // 3
    for r in range(3):
        rid = (lax.broadcasted_iota(I32, (rb, MOE_TOK_CHUNK), 0) + r * rb).astype(F32)
        onehot = jnp.where((l1 == rid) | (l2 == rid), 1.0, 0.0).astype(BF16)
        ybuf[slot, r * rb:(r + 1) * rb, :] = _dot(onehot, hb_ref[...]).astype(BF16)

    @pl.when(c > 0)
    def _():
        _cell_copies(tabs, c - 1, xs_hbm, ybuf, 1 - slot, sem, to_local=False, wait=True)

    _cell_copies(tabs, c, xs_hbm, ybuf, slot, sem, to_local=False, wait=False)

    @pl.when(c == nc - 1)
    def _():
        _cell_copies(tabs, c, xs_hbm, ybuf, slot, sem, to_local=False, wait=True)


def _dispatch(tabs, hb, lpt, *, n_slots, n_fill):
    tm = MOE_TOK_CHUNK
    grid_spec = pltpu.PrefetchScalarGridSpec(
        num_scalar_prefetch=4,
        grid=(hb.shape[0] // tm,),
        in_specs=[pl.BlockSpec((tm, D_MODEL), lambda c, *_: (c, 0)),
                  pl.BlockSpec((None, 8, tm), lambda c, *_: (c, 0, 0))],
        out_specs=pl.BlockSpec(memory_space=pl.ANY),
        scratch_shapes=[pltpu.VMEM((3, MOE_LOCAL_ROWS, D_MODEL), BF16),
                        pltpu.SemaphoreType.DMA((3,))],
    )
    return pl.pallas_call(
        functools.partial(_dispatch_kernel, n_fill=n_fill),
        grid_spec=grid_spec,
        out_shape=jax.ShapeDtypeStruct((n_slots, D_MODEL), BF16),
        compiler_params=_cparams("arbitrary"),
        name="moe_dispatch",
    )(*tabs, hb, lpt)


def _expert_kernel(te_ref, nu_ref, x_ref, wg_ref, wu_ref, wd_ref, o_ref, acc):
    t = pl.program_id(0)
    f = pl.program_id(1)
    nf = pl.num_programs(1)

    @pl.when(t < nu_ref[0])
    def _():
        x = x_ref[...]
        gt = _dot(x, wg_ref[...])
        up = _dot(x, wu_ref[...])
        part = _dot((_silu(gt) * up).astype(BF16), wd_ref[...])

        @pl.when(f == 0)
        def _():
            acc[...] = part

        @pl.when((f > 0) & (f < nf - 1))
        def _():
            acc[...] += part

        @pl.when(f == nf - 1)
        def _():
            o_ref[...] = (acc[...] + part).astype(o_ref.dtype)

    @pl.when((t >= nu_ref[0]) & (f == nf - 1))
    def _():
        o_ref[...] = jnp.zeros_like(o_ref)


def _experts(te, nu, xs, wg, wu, wd):
    bs = MOE_SLOT_TILE
    n_slots = xs.shape[0]
    nf = D_FF_EXPERT // MOE_FF_CHUNK

    def tt(t, nu):
        return jnp.minimum(t, nu[0] - 1)

    def ff(t, f, nu):
        return jnp.where(t < nu[0], f, nf - 1)

    grid_spec = pltpu.PrefetchScalarGridSpec(
        num_scalar_prefetch=2,
        grid=(n_slots // bs, nf),
        in_specs=[pl.BlockSpec((bs, D_MODEL), lambda t, f, te, nu: (tt(t, nu), 0)),
                  pl.BlockSpec((None, D_MODEL, MOE_FF_CHUNK), lambda t, f, te, nu: (te[tt(t, nu)], 0, ff(t, f, nu))),
                  pl.BlockSpec((None, D_MODEL, MOE_FF_CHUNK), lambda t, f, te, nu: (te[tt(t, nu)], 0, ff(t, f, nu))),
                  pl.BlockSpec((None, MOE_FF_CHUNK, D_MODEL), lambda t, f, te, nu: (te[tt(t, nu)], ff(t, f, nu), 0))],
        out_specs=pl.BlockSpec((bs, D_MODEL), lambda t, f, te, nu: (t, 0)),
        scratch_shapes=[pltpu.VMEM((bs, D_MODEL), F32)],
    )
    return pl.pallas_call(
        _expert_kernel,
        grid_spec=grid_spec,
        out_shape=jax.ShapeDtypeStruct((n_slots, D_MODEL), BF16),
        compiler_params=_cparams("arbitrary", "arbitrary"),
        name="moe_experts",
    )(te, nu, xs, wg, wu, wd)


def _combine_kernel(g_ref, a_ref, nbig_ref, nsmall_ref, tok_ref, lpt_ref, x_ref, mod_ref, lg_ref, lb_ref, ys_hbm,
                    oc_ref, ol_ref, ybuf, wbuf, sem, *, n_ctx_tiles):
    c = pl.program_id(0)
    nc = pl.num_programs(0)
    slot = c % 2
    tabs = (g_ref, a_ref, nbig_ref, nsmall_ref)
    tm = tok_ref.shape[0]

    @pl.when(c == 0)
    def _():
        ybuf[...] = jnp.zeros_like(ybuf)
        _cell_copies(tabs, 0, ys_hbm, ybuf, 0, sem, to_local=True, wait=False)

    @pl.when(c + 1 < nc)
    def _():
        _cell_copies(tabs, c + 1, ys_hbm, ybuf, 1 - slot, sem, to_local=True, wait=False)

    _cell_copies(tabs, c, ys_hbm, ybuf, slot, sem, to_local=True, wait=True)

    g1, g2, l1, l2 = lpt_ref[0:1, :], lpt_ref[1:2, :], lpt_ref[2:3, :], lpt_ref[3:4, :]
    rb = MOE_LOCAL_ROWS // 3
    for r in range(3):
        rid = (lax.broadcasted_iota(I32, (rb, tm), 0) + r * rb).astype(F32)
        gate = jnp.sum(jnp.where(l1 == rid, g1, 0.0) + jnp.where(l2 == rid, g2, 0.0), 1, keepdims=True)
        rs = slice(r * rb, (r + 1) * rb)
        wbuf[rs, :] = (ybuf[slot, rs, :].astype(F32) * gate).astype(BF16)
    tok = tok_ref[...]
    col = lax.broadcasted_iota(I32, (tm, MOE_LOCAL_ROWS), 1).astype(F32)
    pick = jnp.where((tok[:, 2:3] == col) | (tok[:, 3:4] == col), 1.0, 0.0).astype(BF16)
    f = _dot(pick, wbuf[...])
    y = ALPHA * x_ref[...] + mod_ref[5:6, :] * f
    res = _layer_norm(y, lg_ref[...], lb_ref[...])

    @pl.when(c < n_ctx_tiles)
    def _():
        oc_ref[...] = res

    @pl.when(c >= n_ctx_tiles)
    def _():
        ol_ref[...] = res


def _combine(tabs, tok, lpt, ys, x, mods, g, b, *, n_ctx, lat_len):
    nt = x.shape[0]
    tm = MOE_TOK_CHUNK
    nct = n_ctx // tm
    cmap = lambda c, *_: (_cond_index(c * tm, n_ctx, lat_len), 0, 0)
    grid_spec = pltpu.PrefetchScalarGridSpec(
        num_scalar_prefetch=4,
        grid=(nt // tm,),
        in_specs=[pl.BlockSpec((tm, LANES), lambda c, *_: (c, 0)),
                  pl.BlockSpec((None, 8, tm), lambda c, *_: (c, 0, 0)),
                  pl.BlockSpec((tm, D_MODEL), lambda c, *_: (c, 0)),
                  pl.BlockSpec((None, 8, D_MODEL), cmap),
                  pl.BlockSpec((1, D_MODEL), lambda c, *_: (0, 0)),
                  pl.BlockSpec((1, D_MODEL), lambda c, *_: (0, 0)),
                  pl.BlockSpec(memory_space=pl.ANY)],
        out_specs=[pl.BlockSpec((tm, D_MODEL), lambda c, *_: (jnp.minimum(c, nct - 1), 0)),
                   pl.BlockSpec((tm, D_MODEL), lambda c, *_: (jnp.maximum(c - nct, 0), 0))],
        scratch_shapes=[pltpu.VMEM((2, MOE_LOCAL_ROWS, D_MODEL), BF16),
                        pltpu.VMEM((MOE_LOCAL_ROWS, D_MODEL), BF16),
                        pltpu.SemaphoreType.DMA((2,))],
    )
    return pl.pallas_call(
        functools.partial(_combine_kernel, n_ctx_tiles=nct),
        grid_spec=grid_spec,
        out_shape=[jax.ShapeDtypeStruct((n_ctx, D_MODEL), F32),
                   jax.ShapeDtypeStruct((nt - n_ctx, D_MODEL), F32)],
        compiler_params=_cparams("arbitrary"),
        name="moe_combine",
    )(*tabs, tok, lpt, x, mods, g, b, ys)


def _moe(x, routed, mods, wg, wu, wd, ln_g, ln_b, *, n_ctx, lat_len):
    nt = x.shape[0]
    bs, tc = MOE_SLOT_TILE, MOE_TOK_CHUNK
    nchunk = nt // tc
    max_rows = 2 * nt + nchunk * N_EXPERTS * (MOE_CELL_ALIGN - 1)
    n_tiles = -(-max_rows // bs) + N_EXPERTS
    n_slots = n_tiles * bs

    hb, tok, lpt, capt = routed

    cap = capt[:, 0, :N_EXPERTS].astype(I32)
    a_loc = jnp.cumsum(cap, 1) - cap
    tot = jnp.sum(cap, 0)
    padded = ((tot + bs - 1) // bs) * bs
    gend = jnp.cumsum(padded)
    g_glob = (gend - padded)[None, :] + (jnp.cumsum(cap, 0) - cap)
    fill_cell = (MOE_LOCAL_ROWS // MOE_BIG_PIECE) * MOE_BIG_PIECE
    n_tail_cells = -(-(n_slots - 2 * nt) // fill_cell)
    n_fill = 1 + -(-n_tail_cells // N_EXPERTS)
    tail_k = jnp.arange((n_fill - 1) * N_EXPERTS, dtype=I32) * fill_cell
    tail_rows = jnp.clip(n_slots - gend[-1] - tail_k, 0, fill_cell)
    g_all = jnp.concatenate([g_glob.reshape(-1), gend - padded + tot, gend[-1] + tail_k])
    a_all = jnp.concatenate([a_loc.reshape(-1), jnp.zeros((n_fill * N_EXPERTS,), I32)])
    rows_all = jnp.concatenate([cap.reshape(-1), padded - tot, tail_rows])
    n_big = rows_all // MOE_BIG_PIECE
    n_small = (rows_all - n_big * MOE_BIG_PIECE) // MOE_CELL_ALIGN
    tabs = tuple(t.astype(I32) for t in (g_all, a_all, n_big, n_small))
    tile_start = jnp.arange(n_tiles, dtype=I32) * bs
    tile_expert = jnp.minimum(jnp.sum((gend[None, :] <= tile_start[:, None]).astype(I32), 1), N_EXPERTS - 1)
    n_used = (gend[-1] // bs).astype(I32).reshape(1)

    xs = _dispatch(tabs, hb, lpt, n_slots=n_slots, n_fill=n_fill)
    ys = _experts(tile_expert, n_used, xs, wg, wu, wd)
    return _combine(tabs, tok, lpt, ys, x, mods, ln_g, ln_b, n_ctx=n_ctx, lat_len=lat_len)


def _axial_angles(n_tok, dim):
    rows = n_tok // GRID_W
    row = jnp.repeat(jnp.arange(rows), GRID_W).astype(F32)
    col = jnp.tile(jnp.arange(GRID_W), rows).astype(F32)
    axis_dim = dim // 2
    inv = 1.0 / (ROPE_BASE ** (jnp.arange(0, axis_dim, 2, dtype=F32) / axis_dim))
    ang = jnp.concatenate([row[:, None] * inv, col[:, None] * inv], -1)
    return jnp.cos(ang), jnp.sin(ang)


def _mla_tables(n_lat):
    cos, sin = _axial_angles(n_lat, MLA_ROPE)
    one = jnp.ones((n_lat, MLA_NOPE), F32)
    zero = jnp.zeros((n_lat, MLA_NOPE), F32)
    pad1 = jnp.ones((n_lat, LANES - MLA_NOPE - MLA_ROPE), F32)
    pad0 = jnp.zeros((n_lat, LANES - MLA_NOPE - MLA_ROPE), F32)
    return (jnp.concatenate([one, cos, cos, pad1], -1), jnp.concatenate([zero, sin, sin, pad0], -1))


def _ret_tables(n_lat):
    cos, sin = _axial_angles(n_lat, RET_QK)
    return jnp.concatenate([cos, cos], -1), jnp.concatenate([-sin, sin], -1)


def _rot_cols(w):
    half = w.shape[1] // 2
    return jnp.concatenate([-w[:, half:], w[:, :half]], 1)


def _in0_weights(w_in0):
    z, xbc, dt, ql, kvl, kr = jnp.split(
        w_in0, [SSD_INNER, SSD_INNER + SSD_XBC, SSD_INNER + SSD_XBC + 2 * SSD_HEADS,
                SSD_INNER + SSD_XBC + 2 * SSD_HEADS + MLA_Q_RANK,
                SSD_INNER + SSD_XBC + 2 * SSD_HEADS + MLA_Q_RANK + MLA_KV_RANK], axis=1)
    dtp = jnp.pad(dt, ((0, 0), (0, LANES - 2 * SSD_HEADS)))
    lpad = ((0, 0), (MLA_NOPE, LANES - MLA_NOPE - MLA_ROPE))
    krp = jnp.concatenate([jnp.pad(kr, lpad), jnp.pad(_rot_cols(kr), lpad)], 1)
    return jnp.concatenate([z, xbc, dtp, ql, kvl, krp], 1).astype(BF16)


def _mla_weights(w_q_up, w_kv_up):
    d = w_q_up.shape[0]
    wq = w_q_up.reshape(d, MLA_HEADS, MLA_NOPE + MLA_ROPE)
    nope, rope = wq[..., :MLA_NOPE], wq[..., MLA_NOPE:]
    half = MLA_ROPE // 2
    rot = jnp.concatenate([-rope[..., half:], rope[..., :half]], -1)
    tail = jnp.zeros((d, MLA_HEADS, LANES - MLA_NOPE - MLA_ROPE), F32)
    wq_pad = jnp.concatenate([nope, rope, tail], -1).reshape(d, MLA_HEADS * LANES)
    wq_rot = jnp.concatenate([jnp.zeros_like(nope), rot, tail], -1).reshape(d, MLA_HEADS * LANES)
    r = w_kv_up.shape[0]
    wkv = w_kv_up.reshape(r, MLA_HEADS, MLA_NOPE + MLA_V)
    wk = jnp.concatenate([wkv[..., :MLA_NOPE], jnp.zeros((r, MLA_HEADS, LANES - MLA_NOPE), F32)], -1)
    wv = wkv[..., MLA_NOPE:]
    return (wq_pad.astype(BF16), wq_rot.astype(BF16),
            wk.reshape(r, MLA_HEADS * LANES).astype(BF16), wv.reshape(r, MLA_OUT).astype(BF16))


def _lane_row(v, width):
    return jnp.pad(v, (0, width - v.shape[0])).reshape(1, width)


def kernel(x_prompt, x_sample, cache_mla_ckv, cache_mla_krope, state_ssd_f, state_ssd_b, state_ret_f, state_ret_b, c, c_ctx, ada_w, ada_b, ln1_g, ln1_b, ln2_g, ln2_b, w_in0, ssd_conv_w, ssd_conv_b, ssd_a_log_f, ssd_a_log_b, ssd_dt_bias_f, ssd_dt_bias_b, ssd_d, ssd_norm_g, mla_q_norm_g, mla_w_q_up, mla_kv_norm_g, mla_w_kv_up, w_out0, ffn_w_gate, ffn_w_up, ffn_w_down, w_in1, ret_decay_f, ret_decay_b, w_out1, moe_router, moe_w_gate, moe_w_up, moe_w_down):
    bc, lc, _ = x_prompt.shape
    bl, ll, _ = x_sample.shape
    past = cache_mla_ckv.shape[2]
    n_ctx, n_lat = bc * lc, bl * ll
    geo = dict(n_ctx=n_ctx, lat_len=ll)

    x_parts = [x_prompt.reshape(n_ctx, D_MODEL), x_sample.reshape(n_lat, D_MODEL)]
    n_cond = 1 + bl
    cond = jnp.concatenate([c_ctx[None, :], c, jnp.zeros((-n_cond % 8, D_MODEL), F32)], 0)
    mods = _ada_vectors(cond, ada_w, ada_b)

    tm0 = 512
    wq_pad, wq_rot, wk_pad, wv = _mla_weights(mla_w_q_up[0], mla_w_kv_up[0])
    mla_w = (mla_q_norm_g[0].reshape(1, MLA_Q_RANK), wq_pad, wq_rot,
             mla_kv_norm_g[0].reshape(1, MLA_KV_RANK), wk_pad, wv)
    cos_l, sin_l = _mla_tables(ll)
    tables = (jnp.concatenate([jnp.ones((tm0, LANES), F32), cos_l], 0),
              jnp.concatenate([jnp.zeros((tm0, LANES), F32), sin_l], 0))
    z, xbc, dt, q_all, k_all, v_all, ckv_all, kr_all = _in0(
        x_parts, mods[0], _in0_weights(w_in0[0]), tables, mla_w, tm=tm0, **geo)

    cw = jnp.pad(ssd_conv_w[0], ((0, 8 - ssd_conv_w.shape[1]), (0, 0)))
    ssd_params = (cw, ssd_conv_b[0].reshape(1, SSD_XBC),
                  _lane_row(jnp.concatenate([ssd_dt_bias_f[0], ssd_dt_bias_b[0]]), LANES),
                  _lane_row(jnp.concatenate([ssd_a_log_f[0], ssd_a_log_b[0]]), LANES),
                  jnp.repeat(ssd_d[0], SSD_HEAD_DIM).reshape(1, SSD_INNER),
                  ssd_norm_g[0].reshape(1, SSD_INNER))

    def st_in(s):
        return jnp.transpose(s, (0, 3, 1, 2)).reshape(s.shape[0], SSD_STATE, SSD_INNER)

    def st_out(s):
        return jnp.transpose(s.reshape(s.shape[0], SSD_STATE, SSD_HEADS, SSD_HEAD_DIM), (0, 2, 3, 1))

    zero_ssd = jnp.zeros((bc, SSD_STATE, SSD_INNER), F32)
    y_ssd_c, ssd_f, ssd_b = _ssd(z, xbc, dt, zero_ssd, zero_ssd, ssd_params,
                                 row0=0, n_seq=bc, seq_len=lc, name="ssd_ctx")
    y_ssd_l, _, _ = _ssd(z, xbc, dt, st_in(state_ssd_f[:, 0]), st_in(state_ssd_b[:, 0]), ssd_params,
                         row0=n_ctx, n_seq=bl, seq_len=ll, name="ssd_lat")

    lpad = ((0, 0), (MLA_NOPE, 2 * LANES - MLA_NOPE - MLA_ROPE))
    k_p, v_p = _mla_prep(None, cache_mla_ckv[:, 0].reshape(bl * past, MLA_KV_RANK),
                         jnp.pad(cache_mla_krope[:, 0].reshape(bl * past, MLA_ROPE), lpad),
                         None, mla_w, row0=0, n_rows=bl * past,
                         do_q=False, do_norm=False, tm=256, name="mla_prep_cache")
    o_c = _attention(q_all, k_all, v_all, None, None, row0=0, n_seq=bc, seq_len=lc, cache_len=0,
                     tq=lc, n_pairs=MLA_HEADS // 2, name="attn_ctx")
    o_l = _attention(q_all, k_all, v_all, k_p, v_p, row0=n_ctx, n_seq=bl, seq_len=ll, cache_len=past,
                     tq=min(1024, ll), n_pairs=1, name="attn_lat")

    w_out0_bf = w_out0[0].astype(BF16)
    x = _ffn([[y_ssd_c, y_ssd_l], [o_c, o_l]], [w_out0_bf[:SSD_INNER], w_out0_bf[SSD_INNER:]], x_parts, mods[0],
             ln1_g[0].reshape(1, D_MODEL), ln1_b[0].reshape(1, D_MODEL),
             ffn_w_gate[0].astype(BF16), ffn_w_up[0].astype(BF16), ffn_w_down[0].astype(BF16),
             ln2_g[0].reshape(1, D_MODEL), ln2_b[0].reshape(1, D_MODEL), tm=512, **geo)

    tm1 = 256
    cos_r, sin_r = _ret_tables(ll)
    ret_tables = (jnp.concatenate([jnp.ones((tm1, RET_QK), F32), cos_r], 0),
                  jnp.concatenate([jnp.zeros((tm1, RET_QK), F32), sin_r], 0))
    q1, k1, v1, g1 = _in1(x, mods[1], w_in1[0].astype(BF16), ret_tables, tm=tm1, **geo)
    dec = jnp.stack([ret_decay_f[0], ret_decay_b[0]], 1)
    dec = jnp.broadcast_to(jnp.pad(dec, ((0, 0), (0, 6)))[:, :, None], (RET_HEADS, 8, RET_V))
    y_ret_c, ret_f, ret_b = _retention(q1, k1, v1, g1, dec, None,
                                       row0=0, n_seq=bc, seq_len=lc, hps=8, name="ret_ctx")
    y_ret_l, = _retention(q1, k1, v1, g1, dec, (state_ret_f[:, 0], state_ret_b[:, 0]),
                          row0=n_ctx, n_seq=bl, seq_len=ll, hps=2, name="ret_lat")
    router_w = jnp.pad(moe_router[0], ((0, 0), (0, LANES - N_EXPERTS)))
    x, *routed = _outproj([[y_ret_c, y_ret_l]], [w_out1[0].astype(BF16)], [x], mods[1],
                          ln1_g[1].reshape(1, D_MODEL), ln1_b[1].reshape(1, D_MODEL),
                          gate_row=2, tm=MOE_TOK_CHUNK, name="out1", router_w=router_w, **geo)
    y_c, y_l = _moe(x, routed, mods[1], moe_w_gate[0].astype(BF16), moe_w_up[0].astype(BF16),
                    moe_w_down[0].astype(BF16), ln2_g[1].reshape(1, D_MODEL), ln2_b[1].reshape(1, D_MODEL), **geo)

    y_prompt = y_c.reshape(bc, lc, D_MODEL)
    y_sample = y_l.reshape(bl, ll, D_MODEL)
    new_ckv = ckv_all[:n_ctx].reshape(bc, 1, lc, MLA_KV_RANK)
    new_krope = kr_all[:n_ctx, MLA_NOPE:MLA_NOPE + MLA_ROPE].reshape(bc, 1, lc, MLA_ROPE)
    return (y_prompt, y_sample, new_ckv, new_krope,
            st_out(ssd_f)[:, None], st_out(ssd_b)[:, None], ret_f[:, None], ret_b[:, None])
```

```python
import functools
import math

import jax
import jax.numpy as jnp
from jax import lax
from jax.experimental import pallas as pl
from jax.experimental.pallas import tpu as pltpu

F32 = jnp.float32
BF16 = jnp.bfloat16
I32 = jnp.int32

D_MODEL = 1024
DEPTH = 2
GRID_W = 64
CHUNK = 128
SSD_HEADS = 8
SSD_HEAD_DIM = 64
SSD_INNER = SSD_HEADS * SSD_HEAD_DIM
SSD_GROUPS = 2
SSD_STATE = 64
SSD_XBC = SSD_INNER + 2 * SSD_GROUPS * SSD_STATE
MLA_HEADS = 8
MLA_NOPE = 64
MLA_ROPE = 32
MLA_V = 64
MLA_Q_RANK = 384
MLA_KV_RANK = 256
MLA_OUT = MLA_HEADS * MLA_V
RET_HEADS = 8
RET_QK = 128
RET_V = 256
MIX1 = RET_HEADS * RET_V
D_FF = 2816
N_EXPERTS = 8
D_FF_EXPERT = 3584
ALPHA = (2 * DEPTH) ** 0.25
LN_EPS = 1e-5
RMS_EPS = 1e-6
ROPE_BASE = 10000.0

LANES = 128
VMEM_LIMIT = 56 * 1024 * 1024
NEG_BIG = -1e30

MOE_SLOT_TILE = 512
MOE_TOK_CHUNK = 512
MOE_FF_CHUNK = 1792
MOE_CELL_ALIGN = 16
MOE_BIG_PIECE = 64
MOE_LOCAL_ROWS = 2 * MOE_TOK_CHUNK + N_EXPERTS * MOE_CELL_ALIGN
FFN_CHUNK = 512
RET_CHUNK = 256

_NT = (((1,), (1,)), ((), ()))
_TN = (((0,), (0,)), ((), ()))


def _cparams(*sem):
    return pltpu.CompilerParams(dimension_semantics=sem, vmem_limit_bytes=VMEM_LIMIT)


def _resident(shape):
    nd = len(shape)
    return pl.BlockSpec(shape, lambda *_: (0,) * nd, pipeline_mode=pl.Buffered(1))


def _silu(x):
    return x * jax.nn.sigmoid(x)


def _dot(a, b):
    return jnp.dot(a, b, preferred_element_type=F32)


def _cond_index(row, n_ctx, lat_len):
    return jnp.where(row < n_ctx, 0, 1 + (row - n_ctx) // lat_len)


def _layer_norm(y, g, b):
    mu = jnp.mean(y, -1, keepdims=True)
    d = y - mu
    var = jnp.mean(d * d, -1, keepdims=True)
    return d * lax.rsqrt(var + LN_EPS) * g + b


def _ada_kernel(c_ref, w_ref, b_ref, o_ref):
    s = _silu(c_ref[...])
    o_ref[...] = jnp.dot(s, w_ref[...], precision=lax.Precision.HIGHEST,
                         preferred_element_type=F32) + b_ref[...]


def _ada_vectors(cond, ada_w, ada_b):
    r = cond.shape[0]
    tn = 2048
    out = pl.pallas_call(
        _ada_kernel,
        grid=(DEPTH, 6 * D_MODEL // tn),
        in_specs=[pl.BlockSpec((r, D_MODEL), lambda l, j: (0, 0)),
                  pl.BlockSpec((None, D_MODEL, tn), lambda l, j: (l, 0, j)),
                  pl.BlockSpec((None, 1, tn), lambda l, j: (l, 0, j))],
        out_specs=pl.BlockSpec((None, r, tn), lambda l, j: (l, 0, j)),
        out_shape=jax.ShapeDtypeStruct((DEPTH, r, 6 * D_MODEL), F32),
        compiler_params=_cparams("parallel", "parallel"),
        name="ada",
    )(cond, ada_w, ada_b.reshape(DEPTH, 1, 6 * D_MODEL))
    out = out.reshape(DEPTH, r, 6, D_MODEL)
    return jnp.pad(out, ((0, 0), (0, 0), (0, 2), (0, 0)))


def _row_specs(parts, tm):
    cols = parts[0].shape[1]
    if len(parts) == 1:
        return [pl.BlockSpec((tm, cols), lambda i, *_: (i, 0))]
    nct = parts[0].shape[0] // tm
    return [pl.BlockSpec((tm, cols), lambda i, *_: (jnp.minimum(i, nct - 1), 0)),
            pl.BlockSpec((tm, cols), lambda i, *_: (jnp.maximum(i - nct, 0), 0))]


def _row_tile(refs, n_ctx_tiles):
    if len(refs) == 1:
        return refs[0][...]
    return jnp.where(pl.program_id(0) < n_ctx_tiles, refs[0][...], refs[1][...])


def _in1_kernel(x_ref, mod_ref, w_ref, cos_ref, sin_ref, q_out, k_out, v_out, g_out):
    hq = RET_HEADS * RET_QK
    x = x_ref[...]
    h = (x * (1.0 + mod_ref[1:2, :]) + mod_ref[0:1, :]).astype(BF16)
    cs, sn = cos_ref[...], sin_ref[...]
    for o_ref, off, scale in ((q_out, 0, 1.0), (k_out, hq, RET_QK ** -0.5)):
        y = _dot(h, w_ref[:, off:off + hq])
        for hh in range(RET_HEADS):
            hs = slice(hh * RET_QK, (hh + 1) * RET_QK)
            yh = y[:, hs] * scale
            o_ref[:, hs] = (yh * cs + pltpu.roll(yh, RET_QK // 2, axis=1) * sn).astype(o_ref.dtype)
    v_out[...] = _dot(h, w_ref[:, 2 * hq:2 * hq + MIX1]).astype(v_out.dtype)
    g_out[...] = _silu(_dot(h, w_ref[:, 2 * hq + MIX1:])).astype(g_out.dtype)


def _in1(x, mods, w, tables, *, tm, n_ctx, lat_len):
    nt = x.shape[0]
    nct = n_ctx // tm
    lat_tiles = lat_len // tm
    hq = RET_HEADS * RET_QK
    cmap = lambda i: (_cond_index(i * tm, n_ctx, lat_len), 0, 0)
    tmap = lambda i: (jnp.where(i < nct, 0, 1 + (i - nct) % lat_tiles), 0)
    widths = (hq, hq, MIX1, MIX1)
    return pl.pallas_call(
        _in1_kernel,
        grid=(nt // tm,),
        in_specs=[pl.BlockSpec((tm, D_MODEL), lambda i: (i, 0)), pl.BlockSpec((None, 8, D_MODEL), cmap),
                  _resident(w.shape), pl.BlockSpec((tm, RET_QK), tmap), pl.BlockSpec((tm, RET_QK), tmap)],
        out_specs=[pl.BlockSpec((tm, wd), lambda i: (i, 0)) for wd in widths],
        out_shape=[jax.ShapeDtypeStruct((nt, wd), BF16) for wd in widths],
        compiler_params=_cparams("parallel"),
        name="in1",
    )(x, mods, w, tables[0], tables[1])


def _cumsum_rows(tril_bf, x):
    hi = x.astype(BF16)
    r = x - hi.astype(F32)
    mid = r.astype(BF16)
    lo = (r - mid.astype(F32)).astype(BF16)
    return _dot(tril_bf, hi) + _dot(tril_bf, mid) + _dot(tril_bf, lo)


def _ssd_kernel(z_ref, xbc_ref, dt_ref, sf0_ref, sb0_ref, cw_ref, cb_ref, dtb_ref, alog_ref,
                dsk_ref, ng_ref, y_ref, sf_ref, sb_ref,
                yacc, xs_s, cm_s, bmt_s, xb_s, erb_s, *, seq_len):
    nc = seq_len // CHUNK
    hd, ns = SSD_HEAD_DIM, SSD_STATE
    gw = (SSD_HEADS // SSD_GROUPS) * hd
    ri = lax.broadcasted_iota(I32, (CHUNK, CHUNK), 0)
    ci = lax.broadcasted_iota(I32, (CHUNK, CHUNK), 1)
    lower = ri >= ci
    upper = ri <= ci
    tril_bf = jnp.where(lower, 1.0, 0.0).astype(BF16)
    rowid = lax.broadcasted_iota(I32, (CHUNK, 1), 0)
    lane = lax.broadcasted_iota(I32, (CHUNK, LANES), 1)
    src = lax.broadcasted_iota(I32, (LANES, SSD_INNER), 0)
    dst_head = lax.broadcasted_iota(I32, (LANES, SSD_INNER), 1) // hd
    spread_f = jnp.where(src == dst_head, 1.0, 0.0).astype(BF16)
    spread_b = jnp.where(src == dst_head + SSD_HEADS, 1.0, 0.0).astype(BF16)

    def per_head(v, spread, split=True):
        hi = v.astype(BF16)
        if not split:
            return _dot(hi, spread)
        lo = (v - hi.astype(F32)).astype(BF16)
        return _dot(hi, spread) + _dot(lo, spread)

    sf_ref[...] = sf0_ref[...]
    sb_ref[...] = sb0_ref[...]

    def fwd(c, carry):
        r0 = pl.multiple_of(c * CHUNK, CHUNK)
        rows = pl.ds(r0, CHUNK)
        cur = xbc_ref[rows, :]
        pstart = pl.multiple_of(jnp.maximum(r0 - 8, 0), 8)
        nstart = pl.multiple_of(jnp.minimum(r0 + CHUNK, seq_len - 8), 8)
        prev_row = xbc_ref[pl.ds(pstart, 8), :][7:8, :] * jnp.where(c > 0, 1.0, 0.0)
        next_row = xbc_ref[pl.ds(nstart, 8), :][0:1, :] * jnp.where(c < nc - 1, 1.0, 0.0)
        sh_prev = jnp.where(rowid == 0, prev_row, pltpu.roll(cur, 1, axis=0))
        sh_next = jnp.where(rowid == CHUNK - 1, next_row, pltpu.roll(cur, CHUNK - 1, axis=0))
        conv = cw_ref[0:1, :] * sh_prev + cw_ref[1:2, :] * cur + cw_ref[2:3, :] * sh_next + cb_ref[...]
        u = _silu(conv)
        xs = u[:, :SSD_INNER]
        bm = u[:, SSD_INNER:SSD_INNER + LANES]
        cm = u[:, SSD_INNER + LANES:]

        xr = dt_ref[rows, :] + dtb_ref[...]
        dt = jnp.maximum(xr, 0.0) + jnp.log1p(jnp.exp(-jnp.abs(xr)))
        la = -dt * jnp.exp(alog_ref[...])
        facs = _cumsum_rows(tril_bf, la)
        racs = facs[CHUNK - 1:CHUNK, :] - facs + la
        packed = jnp.where(lane < SSD_HEADS, facs,
                           jnp.where(lane < 2 * SSD_HEADS, racs, pltpu.roll(dt, 2 * SSD_HEADS, axis=1)))
        packed_t = packed.T
        e_f = jnp.exp(facs)
        e_r = jnp.exp(racs)
        w_f = dt * jnp.exp(facs[CHUNK - 1:CHUNK, :] - facs)
        w_b = dt * jnp.exp(racs[0:1, :] - racs)

        cm_bf = cm.astype(BF16)
        bm_bf = bm.astype(BF16)
        bmt_bf = bm.T.astype(BF16)
        xs_bf = xs.astype(BF16)
        ef_full = per_head(e_f, spread_f)
        xf = (xs * per_head(w_f, spread_f, split=False)).astype(BF16)
        xb_s[rows, :] = (xs * per_head(w_b, spread_b, split=False)).astype(BF16)
        yoffs, news, mixes = [], [], []
        for g in range(SSD_GROUPS):
            gl = slice(g * ns, (g + 1) * ns)
            s_g = lax.dot_general(cm_bf[:, gl], bm_bf[:, gl], _NT, preferred_element_type=F32)
            yoffs.append(_dot(cm_bf[:, gl], sf_ref[:, g * gw:(g + 1) * gw].astype(BF16)))
            news.append(_dot(bmt_bf[g * ns:(g + 1) * ns, :], xf[:, g * gw:(g + 1) * gw]))
            for hh in range(SSD_HEADS // SSD_GROUPS):
                h = g * (SSD_HEADS // SSD_GROUPS) + hh
                hb = SSD_HEADS + h
                seg_f = facs[:, h:h + 1] - packed_t[h:h + 1, :]
                seg_b = racs[:, hb:hb + 1] - packed_t[hb:hb + 1, :]
                d_f = jnp.exp(jnp.where(lower, seg_f, NEG_BIG))
                d_b = jnp.exp(jnp.where(upper, seg_b, NEG_BIG))
                dt_f_row = packed_t[2 * SSD_HEADS + h:2 * SSD_HEADS + h + 1, :]
                dt_b_row = packed_t[2 * SSD_HEADS + hb:2 * SSD_HEADS + hb + 1, :]
                mixes.append((s_g * (d_f * dt_f_row + d_b * dt_b_row)).astype(BF16))
        for p in range(SSD_HEADS // 2):
            ps = slice(p * LANES, (p + 1) * LANES)
            x_pair = xs_bf[:, ps]
            x_diag = jnp.concatenate([jnp.where(lane < hd, x_pair, jnp.zeros((), BF16)),
                                      jnp.where(lane >= hd, x_pair, jnp.zeros((), BF16))], 0)
            yacc[rows, ps] = _dot(jnp.concatenate([mixes[2 * p], mixes[2 * p + 1]], 1), x_diag)
        yacc[rows, :] += ef_full * jnp.concatenate(yoffs, -1)
        sf_ref[...] = ef_full[CHUNK - 1:CHUNK, :] * sf_ref[...] + jnp.concatenate(news, -1)
        xs_s[rows, :] = xs
        cm_s[rows, :] = cm_bf
        bmt_s[c] = bmt_bf
        erb_s[rows, :] = e_r
        return carry

    lax.fori_loop(0, nc, fwd, 0, unroll=2)

    def bwd(i, carry):
        c = nc - 1 - i
        r0 = pl.multiple_of(c * CHUNK, CHUNK)
        rows = pl.ds(r0, CHUNK)
        cm_bf = cm_s[rows, :]
        bmt_bf = bmt_s[c]
        er_full = per_head(erb_s[rows, :], spread_b)
        yoffs, news = [], []
        for g in range(SSD_GROUPS):
            gl = slice(g * ns, (g + 1) * ns)
            yoffs.append(_dot(cm_bf[:, gl], sb_ref[:, g * gw:(g + 1) * gw].astype(BF16)))
            news.append(_dot(bmt_bf[g * ns:(g + 1) * ns, :], xb_s[rows, g * gw:(g + 1) * gw]))
        sb_ref[...] = er_full[0:1, :] * sb_ref[...] + jnp.concatenate(news, -1)
        yv = yacc[rows, :] + er_full * jnp.concatenate(yoffs, -1) + dsk_ref[...] * xs_s[rows, :]
        gz = yv * z_ref[rows, :].astype(F32)
        ms = jnp.mean(gz * gz, -1, keepdims=True)
        y_ref[rows, :] = (gz * lax.rsqrt(ms + RMS_EPS) * ng_ref[...]).astype(y_ref.dtype)
        return carry

    lax.fori_loop(0, nc, bwd, 0, unroll=2)


def _ssd(z, xbc, dt, sf0, sb0, params, *, row0, n_seq, seq_len, name):
    cw, cb, dtb, alog, dsk, ng = params
    nc = seq_len // CHUNK
    blk0 = row0 // seq_len
    rmap = lambda b: (b + blk0, 0)
    smap = lambda b: (b, 0, 0)
    const = lambda b: (0, 0)
    st = jax.ShapeDtypeStruct((n_seq, SSD_STATE, SSD_INNER), F32)
    return pl.pallas_call(
        functools.partial(_ssd_kernel, seq_len=seq_len),
        grid=(n_seq,),
        in_specs=[pl.BlockSpec((seq_len, SSD_INNER), rmap),
                  pl.BlockSpec((seq_len, SSD_XBC), rmap),
                  pl.BlockSpec((seq_len, LANES), rmap),
                  pl.BlockSpec((None, SSD_STATE, SSD_INNER), smap),
                  pl.BlockSpec((None, SSD_STATE, SSD_INNER), smap),
                  pl.BlockSpec(cw.shape, const), pl.BlockSpec(cb.shape, const),
                  pl.BlockSpec(dtb.shape, const), pl.BlockSpec(alog.shape, const),
                  pl.BlockSpec(dsk.shape, const), pl.BlockSpec(ng.shape, const)],
        out_specs=[pl.BlockSpec((seq_len, SSD_INNER), lambda b: (b, 0)),
                   pl.BlockSpec((None, SSD_STATE, SSD_INNER), smap),
                   pl.BlockSpec((None, SSD_STATE, SSD_INNER), smap)],
        out_shape=[jax.ShapeDtypeStruct((n_seq * seq_len, SSD_INNER), BF16), st, st],
        scratch_shapes=[pltpu.VMEM((seq_len, SSD_INNER), F32),
                        pltpu.VMEM((seq_len, SSD_INNER), F32),
                        pltpu.VMEM((seq_len, LANES), BF16),
                        pltpu.VMEM((nc, LANES, CHUNK), BF16),
                        pltpu.VMEM((seq_len, SSD_INNER), BF16),
                        pltpu.VMEM((seq_len, LANES), F32)],
        compiler_params=_cparams("parallel"),
        name=name,
    )(z, xbc, dt, sf0, sb0, cw, cb, dtb, alog, dsk, ng)


def _rms(x, g):
    return x * lax.rsqrt(jnp.mean(x * x, -1, keepdims=True) + RMS_EPS) * g


def _mla_prep_kernel(*refs, do_q, do_norm, do_rope):
    it = iter(refs)
    qlat_ref = next(it) if do_q else None
    kv_ref = next(it)
    kr_ref = next(it)
    cos_ref = next(it) if do_rope else None
    sin_ref = next(it) if do_rope else None
    if do_q:
        gq_ref, wq_ref = next(it), next(it)
        wqr_ref = next(it) if do_rope else None
    gkv_ref = next(it) if do_norm else None
    wk_ref, wv_ref = next(it), next(it)
    q_out = next(it) if do_q else None
    k_out, v_out = next(it), next(it)
    ckv_out = next(it) if do_norm else None
    cs, sn = (cos_ref[...], sin_ref[...]) if do_rope else (None, None)
    _mla_project(qlat_ref[...] if do_q else None, kv_ref[...], kr_ref[...], cs, sn,
                 (gq_ref, wq_ref, wqr_ref) if do_q else None, gkv_ref, wk_ref, wv_ref,
                 q_out, k_out, v_out, ckv_out)


def _mla_project(qlat, kvlat, kr, cs, sn, q_w, gkv_ref, wk_ref, wv_ref, q_out, k_out, v_out, ckv_out):
    do_rope = cs is not None
    if q_w is not None:
        gq_ref, wq_ref, wqr_ref = q_w
        qn = _rms(qlat, gq_ref[...]).astype(BF16)
        qa = _dot(qn, wq_ref[...])
        if do_rope:
            qb = _dot(qn, wqr_ref[...])
        scale = (MLA_NOPE + MLA_ROPE) ** -0.5 * math.log2(math.e)
        for h in range(MLA_HEADS):
            hs = slice(h * LANES, (h + 1) * LANES)
            qh = qa[:, hs] * cs + qb[:, hs] * sn if do_rope else qa[:, hs]
            q_out[h] = (qh * scale).astype(BF16)
    ckv = kvlat
    if gkv_ref is not None:
        ckv = _rms(ckv, gkv_ref[...])
        ckv_out[...] = ckv
    ckv_bf = ckv.astype(BF16)
    krp = kr[:, :LANES] * cs + kr[:, LANES:] * sn if do_rope else kr[:, :LANES]
    kn = _dot(ckv_bf, wk_ref[...])
    for h in range(MLA_HEADS):
        k_out[h] = (kn[:, h * LANES:(h + 1) * LANES] + krp).astype(BF16)
    v_out[...] = _dot(ckv_bf, wv_ref[...]).astype(BF16)


def _mla_prep(qlat, kv, kr, tables, weights, *, row0, n_rows, do_q, do_norm, tm, name):
    do_rope = tables is not None
    gq, wq, wqr, gkv, wk, wv = weights
    b0 = row0 // tm
    rmap = lambda i: (i + b0, 0)
    omap = lambda i: (i, 0)
    hmap = lambda i: (0, i, 0)
    ins, specs = [], []

    def add(a, spec):
        ins.append(a)
        specs.append(spec)

    if do_q:
        add(qlat, pl.BlockSpec((tm, MLA_Q_RANK), rmap))
    add(kv, pl.BlockSpec((tm, MLA_KV_RANK), rmap))
    add(kr, pl.BlockSpec((tm, 2 * LANES), rmap))
    if do_rope:
        lat_tiles = tables[0].shape[0] // tm
        tmap = lambda i: (i % lat_tiles, 0)
        add(tables[0], pl.BlockSpec((tm, LANES), tmap))
        add(tables[1], pl.BlockSpec((tm, LANES), tmap))
    if do_q:
        add(gq, _resident(gq.shape))
        add(wq, _resident(wq.shape))
        if do_rope:
            add(wqr, _resident(wqr.shape))
    if do_norm:
        add(gkv, _resident(gkv.shape))
    add(wk, _resident(wk.shape))
    add(wv, _resident(wv.shape))
    out_shape, out_specs = [], []
    if do_q:
        out_shape.append(jax.ShapeDtypeStruct((MLA_HEADS, n_rows, LANES), BF16))
        out_specs.append(pl.BlockSpec((MLA_HEADS, tm, LANES), hmap))
    out_shape.append(jax.ShapeDtypeStruct((MLA_HEADS, n_rows, LANES), BF16))
    out_specs.append(pl.BlockSpec((MLA_HEADS, tm, LANES), hmap))
    out_shape.append(jax.ShapeDtypeStruct((n_rows, MLA_OUT), BF16))
    out_specs.append(pl.BlockSpec((tm, MLA_OUT), omap))
    if do_norm:
        out_shape.append(jax.ShapeDtypeStruct((n_rows, MLA_KV_RANK), F32))
        out_specs.append(pl.BlockSpec((tm, MLA_KV_RANK), omap))
    return pl.pallas_call(
        functools.partial(_mla_prep_kernel, do_q=do_q, do_norm=do_norm, do_rope=do_rope),
        grid=(n_rows // tm,),
        in_specs=specs, out_specs=out_specs, out_shape=out_shape,
        compiler_params=_cparams("parallel"),
        name=name,
    )(*ins)


_IN0_SEGS = (SSD_INNER, SSD_XBC, LANES, MLA_Q_RANK, MLA_KV_RANK, 2 * LANES)


def _in0_kernel(*refs, n_x, n_ctx_tiles):
    x_refs = refs[:n_x]
    (mod_ref, w_ref, cos_ref, sin_ref, gq_ref, wq_ref, wqr_ref, gkv_ref, wk_ref, wv_ref,
     z_out, xbc_out, dt_out, q_out, k_out, v_out, ckv_out, kr_out) = refs[n_x:]
    x = _row_tile(x_refs, n_ctx_tiles)
    h = (x * (1.0 + mod_ref[1:2, :]) + mod_ref[0:1, :]).astype(BF16)
    segs, acc = [], 0
    for wd in _IN0_SEGS:
        segs.append(_dot(h, w_ref[:, acc:acc + wd]))
        acc += wd
    z, xbc, dt, qlat, kvlat, kr = segs
    z_out[...] = _silu(z).astype(z_out.dtype)
    xbc_out[...] = xbc
    dt_out[...] = dt
    kr_out[...] = kr[:, :LANES]
    _mla_project(qlat, kvlat, kr, cos_ref[...], sin_ref[...], (gq_ref, wq_ref, wqr_ref), gkv_ref,
                 wk_ref, wv_ref, q_out, k_out, v_out, ckv_out)


def _in0(x_parts, mods, w, tables, mla_w, *, tm, n_ctx, lat_len):
    nt = sum(p.shape[0] for p in x_parts)
    nct = n_ctx // tm
    lat_tiles = lat_len // tm
    gq, wq, wqr, gkv, wk, wv = mla_w
    cmap = lambda i: (_cond_index(i * tm, n_ctx, lat_len), 0, 0)
    tmap = lambda i: (jnp.where(i < nct, 0, 1 + (i - nct) % lat_tiles), 0)
    row = lambda cols: pl.BlockSpec((tm, cols), lambda i: (i, 0))
    heads = pl.BlockSpec((MLA_HEADS, tm, LANES), lambda i: (0, i, 0))
    consts = [w, gq, wq, wqr, gkv, wk, wv]
    return pl.pallas_call(
        functools.partial(_in0_kernel, n_x=len(x_parts), n_ctx_tiles=nct),
        grid=(nt // tm,),
        in_specs=(_row_specs(x_parts, tm) + [pl.BlockSpec((None, 8, D_MODEL), cmap), _resident(w.shape),
                                             pl.BlockSpec((tm, LANES), tmap), pl.BlockSpec((tm, LANES), tmap)]
                  + [_resident(a.shape) for a in consts[1:]]),
        out_specs=[row(SSD_INNER), row(SSD_XBC), row(LANES), heads, heads, row(MLA_OUT),
                   row(MLA_KV_RANK), row(LANES)],
        out_shape=[jax.ShapeDtypeStruct((nt, SSD_INNER), BF16),
                   jax.ShapeDtypeStruct((nt, SSD_XBC), F32),
                   jax.ShapeDtypeStruct((nt, LANES), F32),
                   jax.ShapeDtypeStruct((MLA_HEADS, nt, LANES), BF16),
                   jax.ShapeDtypeStruct((MLA_HEADS, nt, LANES), BF16),
                   jax.ShapeDtypeStruct((nt, MLA_OUT), BF16),
                   jax.ShapeDtypeStruct((nt, MLA_KV_RANK), F32),
                   jax.ShapeDtypeStruct((nt, LANES), F32)],
        compiler_params=_cparams("parallel"),
        name="in0",
    )(*x_parts, mods, w, tables[0], tables[1], gq, wq, wqr, gkv, wk, wv)


def _attn_kernel(*refs, seq_len, cache_len, kblk, n_pairs):
    if cache_len:
        q_ref, k_ref, v_ref, kc_ref, vc_ref, o_ref = refs
    else:
        q_ref, k_ref, v_ref, o_ref = refs
    tq = q_ref.shape[1]
    blocks = [(k_ref, v_ref, i * kblk) for i in range(seq_len // kblk)]
    if cache_len:
        blocks += [(kc_ref, vc_ref, i * kblk) for i in range(cache_len // kblk)]
    vlane = lax.broadcasted_iota(I32, (kblk, LANES), 1)
    lane = lax.broadcasted_iota(I32, (tq, LANES), 1)
    for pp in range(n_pairs):
        ps = slice(pp * LANES, (pp + 1) * LANES)
        outs = []
        for hh in range(2):
            own = (vlane < MLA_V) if hh == 0 else (vlane >= MLA_V)
            q = q_ref[2 * pp + hh]
            m = None
            for kr, vr, off in blocks:
                s = lax.dot_general(q, kr[2 * pp + hh, off:off + kblk, :], _NT, preferred_element_type=F32)
                v_aug = jnp.where(own, vr[off:off + kblk, ps], jnp.ones((), BF16))
                bmax = jnp.max(s, -1, keepdims=True)
                if m is None:
                    m = bmax
                    acc = _dot(jnp.exp2(s - m).astype(BF16), v_aug)
                else:
                    m_new = jnp.maximum(m, bmax)
                    acc = jnp.exp2(m - m_new) * acc + _dot(jnp.exp2(s - m_new).astype(BF16), v_aug)
                    m = m_new
            outs.append(acc / pltpu.roll(acc, MLA_V, axis=1))
        o_ref[:, ps] = jnp.where(lane < MLA_V, outs[0], outs[1]).astype(o_ref.dtype)


def _attention(q, k, v, kc, vc, *, row0, n_seq, seq_len, cache_len, tq, n_pairs, name):
    nq = seq_len // tq
    kblk = min(512, seq_len)
    n = n_seq * seq_len
    q0, s0 = row0 // tq, row0 // seq_len
    hb, vw = 2 * n_pairs, n_pairs * LANES
    ins = [q, k, v]
    specs = [pl.BlockSpec((hb, tq, LANES), lambda b, hp, qi: (hp, q0 + b * nq + qi, 0)),
             pl.BlockSpec((hb, seq_len, LANES), lambda b, hp, qi: (hp, s0 + b, 0)),
             pl.BlockSpec((seq_len, vw), lambda b, hp, qi: (s0 + b, hp))]
    if cache_len:
        ins += [kc, vc]
        specs += [pl.BlockSpec((hb, cache_len, LANES), lambda b, hp, qi: (hp, b, 0)),
                  pl.BlockSpec((cache_len, vw), lambda b, hp, qi: (b, hp))]
    return pl.pallas_call(
        functools.partial(_attn_kernel, seq_len=seq_len, cache_len=cache_len, kblk=kblk, n_pairs=n_pairs),
        grid=(n_seq, MLA_HEADS // hb, nq),
        in_specs=specs,
        out_specs=pl.BlockSpec((tq, vw), lambda b, hp, qi: (b * nq + qi, hp)),
        out_shape=jax.ShapeDtypeStruct((n, MLA_OUT), BF16),
        compiler_params=_cparams("parallel", "parallel", "arbitrary"),
        name=name,
    )(*ins)


def _outproj_kernel(*refs, n_parts, n_ctx_tiles, gate_row, with_router):
    it = iter(refs)
    acc = None
    for n in n_parts[:-1]:
        a = _row_tile([next(it) for _ in range(n)], n_ctx_tiles)
        part = _dot(a, next(it)[...])
        acc = part if acc is None else acc + part
    x = _row_tile([next(it) for _ in range(n_parts[-1])], n_ctx_tiles)
    mod_ref, g_ref, b_ref = next(it), next(it), next(it)
    router_ref = next(it) if with_router else None
    o_ref = next(it)
    y = ALPHA * x + mod_ref[gate_row:gate_row + 1, :] * acc
    res = _layer_norm(y, g_ref[...], b_ref[...])
    o_ref[...] = res
    if with_router:
        _route(res, mod_ref, router_ref, *it)


def _outproj(acts, ws, x_parts, mods, g, b, *, gate_row, tm, n_ctx, lat_len, name, router_w=None):
    nt = sum(p.shape[0] for p in x_parts)
    cmap = lambda i: (_cond_index(i * tm, n_ctx, lat_len), 0, 0)
    ins, specs = [], []
    for parts, w in zip(acts, ws):
        ins += list(parts) + [w]
        specs += _row_specs(parts, tm) + [_resident(w.shape)]
    ins += list(x_parts) + [mods, g, b]
    specs += _row_specs(x_parts, tm) + [pl.BlockSpec((None, 8, D_MODEL), cmap),
                                        _resident(g.shape), _resident(b.shape)]
    out_specs = [pl.BlockSpec((tm, D_MODEL), lambda i: (i, 0))]
    out_shape = [jax.ShapeDtypeStruct((nt, D_MODEL), F32)]
    if router_w is not None:
        assert tm == MOE_TOK_CHUNK
        ins.append(router_w)
        specs.append(_resident(router_w.shape))
        out_specs += [pl.BlockSpec((tm, D_MODEL), lambda i: (i, 0)),
                      pl.BlockSpec((tm, LANES), lambda i: (i, 0)),
                      pl.BlockSpec((None, 8, tm), lambda i: (i, 0, 0)),
                      pl.BlockSpec((None, 8, LANES), lambda i: (i, 0, 0))]
        out_shape += [jax.ShapeDtypeStruct((nt, D_MODEL), BF16),
                      jax.ShapeDtypeStruct((nt, LANES), F32),
                      jax.ShapeDtypeStruct((nt // tm, 8, tm), F32),
                      jax.ShapeDtypeStruct((nt // tm, 8, LANES), F32)]
    n_parts = tuple(len(p) for p in acts) + (len(x_parts),)
    outs = pl.pallas_call(
        functools.partial(_outproj_kernel, n_parts=n_parts, n_ctx_tiles=n_ctx // tm, gate_row=gate_row,
                          with_router=router_w is not None),
        grid=(nt // tm,),
        in_specs=specs,
        out_specs=out_specs,
        out_shape=out_shape,
        compiler_params=_cparams("parallel"),
        name=name,
    )(*ins)
    return outs[0] if router_w is None else outs


def _ffn_kernel(*refs, n_parts, n_ctx_tiles, ff_chunks):
    it = iter(refs)
    mix = None
    for n in n_parts[:-1]:
        a = _row_tile([next(it) for _ in range(n)], n_ctx_tiles)
        part = _dot(a, next(it)[...])
        mix = part if mix is None else mix + part
    x0 = _row_tile([next(it) for _ in range(n_parts[-1])], n_ctx_tiles)
    mod_ref, g1_ref, b1_ref, wg_ref, wu_ref, wd_ref, g_ref, b_ref, o_ref = it
    x = _layer_norm(ALPHA * x0 + mod_ref[2:3, :] * mix, g1_ref[...], b1_ref[...])
    h = (x * (1.0 + mod_ref[4:5, :]) + mod_ref[3:4, :]).astype(BF16)
    acc = None
    for a, b in ff_chunks:
        gt = _dot(h, wg_ref[:, a:b])
        up = _dot(h, wu_ref[:, a:b])
        act = (_silu(gt) * up).astype(BF16)
        part = _dot(act, wd_ref[a:b, :])
        acc = part if acc is None else acc + part
    y = ALPHA * x + mod_ref[5:6, :] * acc
    o_ref[...] = _layer_norm(y, g_ref[...], b_ref[...])


def _ffn(acts, ws, x_parts, mods, g1, b1, wg, wu, wd, g, b, *, tm, n_ctx, lat_len):
    nt = sum(p.shape[0] for p in x_parts)
    ff = wg.shape[1]
    chunks, a = [], 0
    while a < ff:
        chunks.append((a, min(a + FFN_CHUNK, ff)))
        a += FFN_CHUNK
    cmap = lambda i: (_cond_index(i * tm, n_ctx, lat_len), 0, 0)
    ins, specs = [], []
    for parts, w in zip(acts, ws):
        ins += list(parts) + [w]
        specs += _row_specs(parts, tm) + [_resident(w.shape)]
    consts = [g1, b1, wg, wu, wd, g, b]
    ins += list(x_parts) + [mods] + consts
    specs += _row_specs(x_parts, tm) + [pl.BlockSpec((None, 8, D_MODEL), cmap)]
    specs += [_resident(a.shape) for a in consts]
    n_parts = tuple(len(p) for p in acts) + (len(x_parts),)
    return pl.pallas_call(
        functools.partial(_ffn_kernel, n_parts=n_parts, n_ctx_tiles=n_ctx // tm, ff_chunks=tuple(chunks)),
        grid=(nt // tm,),
        in_specs=specs,
        out_specs=pl.BlockSpec((tm, D_MODEL), lambda i: (i, 0)),
        out_shape=jax.ShapeDtypeStruct((nt, D_MODEL), F32),
        compiler_params=_cparams("parallel"),
        name="out0_ffn",
    )(*ins)


def _ret_kernel(*refs, seq_len, n_heads, from_zero):
    it = iter(refs)
    q_ref, k_ref, v_ref, g_ref, dec_ref = next(it), next(it), next(it), next(it), next(it)
    sf0_ref, sb0_ref = (None, None) if from_zero else (next(it), next(it))
    y_ref = next(it)
    sf_out, sb_out = (next(it), next(it)) if from_zero else (None, None)
    yacc, kb_s, dcomb_s, ev_s, wk_s, cd_s, sf_ref, sb_ref = it
    rc = RET_CHUNK
    nc = seq_len // rc
    unroll = min(2, nc)

    @pl.when(pl.program_id(1) == 0)
    def _():
        ri = lax.broadcasted_iota(I32, (rc, rc), 0)
        ci = lax.broadcasted_iota(I32, (rc, rc), 1)
        dij = (ri - ci).astype(F32)
        pos_k = lax.broadcasted_iota(I32, (rc, RET_QK), 0).astype(F32)
        pos_v = lax.broadcasted_iota(I32, (rc, RET_V), 0).astype(F32)
        for hh in range(n_heads):
            la_f = -jnp.exp(dec_ref[hh, 0:1, :])
            la_b = -jnp.exp(dec_ref[hh, 1:2, :])
            dcomb_s[hh] = (jnp.exp(jnp.where(ri >= ci, dij * la_f[:, :rc], NEG_BIG)) +
                           jnp.exp(jnp.where(ri <= ci, -dij * la_b[:, :rc], NEG_BIG)))
            ev_s[hh, 0] = jnp.exp((pos_v + 1.0) * la_f)
            ev_s[hh, 1] = jnp.exp((rc - pos_v) * la_b)
            wk_s[hh, 0] = jnp.exp((rc - 1.0 - pos_k) * la_f[:, :RET_QK])
            wk_s[hh, 1] = jnp.exp(pos_k * la_b[:, :RET_QK])
            cd_s[hh, 0:1, :] = jnp.exp(rc * la_f)
            cd_s[hh, 1:2, :] = jnp.exp(rc * la_b)

    if from_zero:
        sf_ref[...] = jnp.zeros_like(sf_ref)
        sb_ref[...] = jnp.zeros_like(sb_ref)
    else:
        for hh in range(n_heads):
            sf_ref[hh] = sf0_ref[hh].T
            sb_ref[hh] = sb0_ref[hh].T

    def fwd(c, carry):
        rows = pl.ds(pl.multiple_of(c * rc, rc), rc)
        for hh in range(n_heads):
            qs = slice(hh * RET_QK, (hh + 1) * RET_QK)
            vs = slice(hh * RET_V, (hh + 1) * RET_V)
            q_bf = q_ref[rows, qs]
            k_bf = k_ref[rows, qs]
            k = k_bf.astype(F32)
            v = v_ref[rows, vs]
            s = lax.dot_general(q_bf, k_bf, _NT, preferred_element_type=F32)
            y = _dot((s * dcomb_s[hh]).astype(BF16), v)
            y = y + _dot(q_bf, sf_ref[hh].astype(BF16)) * ev_s[hh, 0]
            yacc[rows, vs] = y
            upd = lax.dot_general((k * wk_s[hh, 0]).astype(BF16), v, _TN, preferred_element_type=F32)
            sf_ref[hh] = cd_s[hh, 0:1, :] * sf_ref[hh] + upd
            kb_s[rows, qs] = (k * wk_s[hh, 1]).astype(BF16)
        return carry

    lax.fori_loop(0, nc, fwd, 0, unroll=unroll)

    def bwd(i, carry):
        rows = pl.ds(pl.multiple_of((nc - 1 - i) * rc, rc), rc)
        for hh in range(n_heads):
            qs = slice(hh * RET_QK, (hh + 1) * RET_QK)
            vs = slice(hh * RET_V, (hh + 1) * RET_V)
            v = v_ref[rows, vs]
            y = yacc[rows, vs] + _dot(q_ref[rows, qs], sb_ref[hh].astype(BF16)) * ev_s[hh, 1]
            upd = lax.dot_general(kb_s[rows, qs], v, _TN, preferred_element_type=F32)
            sb_ref[hh] = cd_s[hh, 1:2, :] * sb_ref[hh] + upd
            mu = jnp.mean(y, -1, keepdims=True)
            d = y - mu
            var = jnp.mean(d * d, -1, keepdims=True)
            yn = d * lax.rsqrt(var + LN_EPS)
            y_ref[rows, vs] = (yn * g_ref[rows, vs].astype(F32)).astype(y_ref.dtype)
        return carry

    lax.fori_loop(0, nc, bwd, 0, unroll=unroll)

    if from_zero:
        for hh in range(n_heads):
            sf_out[hh] = sf_ref[hh].T
            sb_out[hh] = sb_ref[hh].T


def _retention(q, k, v, g, dec, states, *, row0, n_seq, seq_len, hps, name):
    from_zero = states is None
    blk0 = row0 // seq_len
    qmap = lambda h, b: (b + blk0, h)
    state_spec = pl.BlockSpec((None, hps, RET_V, RET_QK), lambda h, b: (b, h, 0, 0))
    ins = [q, k, v, g, dec]
    specs = [pl.BlockSpec((seq_len, hps * RET_QK), qmap), pl.BlockSpec((seq_len, hps * RET_QK), qmap),
             pl.BlockSpec((seq_len, hps * RET_V), qmap), pl.BlockSpec((seq_len, hps * RET_V), qmap),
             pl.BlockSpec((hps, 8, RET_V), lambda h, b: (h, 0, 0))]
    out_specs = [pl.BlockSpec((seq_len, hps * RET_V), lambda h, b: (b, h))]
    out_shape = [jax.ShapeDtypeStruct((n_seq * seq_len, MIX1), BF16)]
    if from_zero:
        st = jax.ShapeDtypeStruct((n_seq, RET_HEADS, RET_V, RET_QK), F32)
        out_specs += [state_spec, state_spec]
        out_shape += [st, st]
    else:
        ins += list(states)
        specs += [state_spec, state_spec]
    rc = RET_CHUNK
    return pl.pallas_call(
        functools.partial(_ret_kernel, seq_len=seq_len, n_heads=hps, from_zero=from_zero),
        grid=(RET_HEADS // hps, n_seq),
        in_specs=specs,
        out_specs=out_specs,
        out_shape=out_shape,
        scratch_shapes=[pltpu.VMEM((seq_len, hps * RET_V), F32),
                        pltpu.VMEM((seq_len, hps * RET_QK), BF16),
                        pltpu.VMEM((hps, rc, rc), F32),
                        pltpu.VMEM((hps, 2, rc, RET_V), F32),
                        pltpu.VMEM((hps, 2, rc, RET_QK), F32),
                        pltpu.VMEM((hps, 8, RET_V), F32),
                        pltpu.VMEM((hps, RET_QK, RET_V), F32),
                        pltpu.VMEM((hps, RET_QK, RET_V), F32)],
        compiler_params=_cparams("parallel", "arbitrary"),
        name=name,
    )(*ins)


def _route(x, mod_ref, w_ref, hb_ref, tok_ref, lpt_ref, cap_ref):
    tm = x.shape[0]
    h = x * (1.0 + mod_ref[4:5, :]) + mod_ref[3:4, :]
    h_hi = h.astype(BF16)
    hb_ref[...] = h_hi
    h_lo = (h - h_hi.astype(F32)).astype(BF16)
    w = w_ref[...]
    w_hi = w.astype(BF16)
    w_lo = (w - w_hi.astype(F32)).astype(BF16)
    w_both = w_hi + pltpu.roll(w_lo.astype(F32), N_EXPERTS, axis=1).astype(BF16)
    part = _dot(h_hi, w_both) + _dot(h_lo, w_both)
    logits = part + pltpu.roll(part, LANES - N_EXPERTS, axis=1)
    lt = logits.T[0:N_EXPERTS, :]
    row = lax.broadcasted_iota(I32, (N_EXPERTS, tm), 0)
    m1 = jnp.max(lt, 0, keepdims=True)
    i1 = jnp.min(jnp.where(lt == m1, row, N_EXPERTS), 0, keepdims=True)
    rest = jnp.where(row == i1, NEG_BIG, lt)
    m2 = jnp.max(rest, 0, keepdims=True)
    i2 = jnp.min(jnp.where(rest == m2, row, N_EXPERTS), 0, keepdims=True)
    e = jnp.exp(m2 - m1)
    g1 = 1.0 / (1.0 + e)
    g2 = e / (1.0 + e)
    sel1 = row == i1
    sel2 = row == i2
    onehot = jnp.where(sel1 | sel2, 1.0, 0.0)
    ri = lax.broadcasted_iota(I32, (tm, tm), 0)
    ci = lax.broadcasted_iota(I32, (tm, tm), 1)
    earlier = jnp.where(ri < ci, 1.0, 0.0).astype(BF16)
    prefix = _dot(onehot.astype(BF16), earlier)
    n_col = jnp.sum(onehot, 1, keepdims=True)
    cap_col = jnp.floor((n_col + (MOE_CELL_ALIGN - 1.0)) * (1.0 / MOE_CELL_ALIGN)) * MOE_CELL_ALIGN
    sub = lax.broadcasted_iota(I32, (N_EXPERTS, 1), 0)
    base_col = jnp.zeros((N_EXPERTS, 1), F32)
    for ex in range(N_EXPERTS - 1):
        base_col = base_col + jnp.where(sub > ex, cap_col[ex:ex + 1, :], 0.0)
    local = prefix + base_col
    lpos1 = jnp.sum(jnp.where(sel1, local, 0.0), 0, keepdims=True)
    lpos2 = jnp.sum(jnp.where(sel2, local, 0.0), 0, keepdims=True)
    lpt = jnp.where(row == 0, g1,
          jnp.where(row == 1, g2,
          jnp.where(row == 2, lpos1,
          jnp.where(row == 3, lpos2, 0.0))))
    lpt_ref[...] = lpt
    tok_ref[...] = jnp.concatenate([lpt, jnp.zeros((LANES - N_EXPERTS, tm), F32)], 0).T
    elane = lax.broadcasted_iota(I32, (N_EXPERTS, LANES), 1)
    cap_row = jnp.sum(jnp.where(elane == sub, cap_col, 0.0), 0, keepdims=True)
    cap_ref[...] = jnp.broadcast_to(cap_row, cap_ref.shape)


def _cell_copies(tabs, chunk, hbm_ref, buf_ref, slot, sem, *, to_local, wait):
    g_ref, a_ref, nbig_ref, nsmall_ref = tabs
    for e in range(N_EXPERTS):
        k = chunk * N_EXPERTS + e
        g0, a0, nbig, nsmall = g_ref[k], a_ref[k], nbig_ref[k], nsmall_ref[k]

        def piece(i, carry, rows, goff, aoff):
            g = pl.multiple_of(goff + i * rows, MOE_CELL_ALIGN)
            a = pl.multiple_of(aoff + i * rows, MOE_CELL_ALIGN)
            far = hbm_ref.at[pl.ds(g, rows), :]
            near = buf_ref.at[slot, pl.ds(a, rows), :]
            cp = (pltpu.make_async_copy(far, near, sem.at[slot]) if to_local
                  else pltpu.make_async_copy(near, far, sem.at[slot]))
            if wait:
                cp.wait()
            else:
                cp.start()
            return carry

        lax.fori_loop(0, nbig, functools.partial(piece, rows=MOE_BIG_PIECE, goff=g0, aoff=a0), 0)
        done = nbig * MOE_BIG_PIECE
        lax.fori_loop(0, nsmall, functools.partial(piece, rows=MOE_CELL_ALIGN, goff=g0 + done, aoff=a0 + done), 0)


def _dispatch_kernel(g_ref, a_ref, nbig_ref, nsmall_ref, hb_ref, lpt_ref, xs_hbm, ybuf, sem, *, n_fill):
    c = pl.program_id(0)
    nc = pl.num_programs(0)
    slot = c % 2
    tabs = (g_ref, a_ref, nbig_ref, nsmall_ref)

    @pl.when(c == 0)
    def _():
        ybuf[2] = jnp.zeros(ybuf.shape[1:], ybuf.dtype)
        for j in range(n_fill):
            _cell_copies(tabs, nc + j, xs_hbm, ybuf, 2, sem, to_local=False, wait=False)

    @pl.when(c == nc - 1)
    def _():
        for j in range(n_fill):
            _cell_copies(tabs, nc + j, xs_hbm, ybuf, 2, sem, to_local=False, wait=True)

    l1 = lpt_ref[2:3, :]
    l2 = lpt_ref[3:4, :]
    rb = MOE_LOCAL_ROWS // 3
    for r in range(3):
        rid = (lax.broadcasted_iota(I32, (rb, MOE_TOK_CHUNK), 0) + r * rb).astype(F32)
        onehot = jnp.where((l1 == rid) | (l2 == rid), 1.0, 0.0).astype(BF16)
        ybuf[slot, r * rb:(r + 1) * rb, :] = _dot(onehot, hb_ref[...]).astype(BF16)

    @pl.when(c > 0)
    def _():
        _cell_copies(tabs, c - 1, xs_hbm, ybuf, 1 - slot, sem, to_local=False, wait=True)

    _cell_copies(tabs, c, xs_hbm, ybuf, slot, sem, to_local=False, wait=False)

    @pl.when(c == nc - 1)
    def _():
        _cell_copies(tabs, c, xs_hbm, ybuf, slot, sem, to_local=False, wait=True)


def _dispatch(tabs, hb, lpt, *, n_slots, n_fill):
    tm = MOE_TOK_CHUNK
    grid_spec = pltpu.PrefetchScalarGridSpec(
        num_scalar_prefetch=4,
        grid=(hb.shape[0] // tm,),
        in_specs=[pl.BlockSpec((tm, D_MODEL), lambda c, *_: (c, 0)),
                  pl.BlockSpec((None, 8, tm), lambda c, *_: (c, 0, 0))],
        out_specs=pl.BlockSpec(memory_space=pl.ANY),
        scratch_shapes=[pltpu.VMEM((3, MOE_LOCAL_ROWS, D_MODEL), BF16),
                        pltpu.SemaphoreType.DMA((3,))],
    )
    return pl.pallas_call(
        functools.partial(_dispatch_kernel, n_fill=n_fill),
        grid_spec=grid_spec,
        out_shape=jax.ShapeDtypeStruct((n_slots, D_MODEL), BF16),
        compiler_params=_cparams("arbitrary"),
        name="moe_dispatch",
    )(*tabs, hb, lpt)


def _expert_kernel(te_ref, nu_ref, x_ref, wg_ref, wu_ref, wd_ref, o_ref, acc):
    t = pl.program_id(0)
    f = pl.program_id(1)
    nf = pl.num_programs(1)

    @pl.when(t < nu_ref[0])
    def _():
        x = x_ref[...]
        gt = _dot(x, wg_ref[...])
        up = _dot(x, wu_ref[...])
        part = _dot((_silu(gt) * up).astype(BF16), wd_ref[...])

        @pl.when(f == 0)
        def _():
            acc[...] = part

        @pl.when((f > 0) & (f < nf - 1))
        def _():
            acc[...] += part

        @pl.when(f == nf - 1)
        def _():
            o_ref[...] = (acc[...] + part).astype(o_ref.dtype)

    @pl.when((t >= nu_ref[0]) & (f == nf - 1))
    def _():
        o_ref[...] = jnp.zeros_like(o_ref)


def _experts(te, nu, xs, wg, wu, wd):
    bs = MOE_SLOT_TILE
    n_slots = xs.shape[0]
    nf = D_FF_EXPERT // MOE_FF_CHUNK

    def tt(t, nu):
        return jnp.minimum(t, nu[0] - 1)

    def ff(t, f, nu):
        return jnp.where(t < nu[0], f, nf - 1)

    grid_spec = pltpu.PrefetchScalarGridSpec(
        num_scalar_prefetch=2,
        grid=(n_slots // bs, nf),
        in_specs=[pl.BlockSpec((bs, D_MODEL), lambda t, f, te, nu: (tt(t, nu), 0)),
                  pl.BlockSpec((None, D_MODEL, MOE_FF_CHUNK), lambda t, f, te, nu: (te[tt(t, nu)], 0, ff(t, f, nu))),
                  pl.BlockSpec((None, D_MODEL, MOE_FF_CHUNK), lambda t, f, te, nu: (te[tt(t, nu)], 0, ff(t, f, nu))),
                  pl.BlockSpec((None, MOE_FF_CHUNK, D_MODEL), lambda t, f, te, nu: (te[tt(t, nu)], ff(t, f, nu), 0))],
        out_specs=pl.BlockSpec((bs, D_MODEL), lambda t, f, te, nu: (t, 0)),
        scratch_shapes=[pltpu.VMEM((bs, D_MODEL), F32)],
    )
    return pl.pallas_call(
        _expert_kernel,
        grid_spec=grid_spec,
        out_shape=jax.ShapeDtypeStruct((n_slots, D_MODEL), BF16),
        compiler_params=_cparams("arbitrary", "arbitrary"),
        name="moe_experts",
    )(te, nu, xs, wg, wu, wd)


def _combine_kernel(g_ref, a_ref, nbig_ref, nsmall_ref, tok_ref, lpt_ref, x_ref, mod_ref, lg_ref, lb_ref, ys_hbm,
                    oc_ref, ol_ref, ybuf, wbuf, sem, *, n_ctx_tiles):
    c = pl.program_id(0)
    nc = pl.num_programs(0)
    slot = c % 2
    tabs = (g_ref, a_ref, nbig_ref, nsmall_ref)
    tm = tok_ref.shape[0]

    @pl.when(c == 0)
    def _():
        ybuf[...] = jnp.zeros_like(ybuf)
        _cell_copies(tabs, 0, ys_hbm, ybuf, 0, sem, to_local=True, wait=False)

    @pl.when(c + 1 < nc)
    def _():
        _cell_copies(tabs, c + 1, ys_hbm, ybuf, 1 - slot, sem, to_local=True, wait=False)

    _cell_copies(tabs, c, ys_hbm, ybuf, slot, sem, to_local=True, wait=True)

    g1, g2, l1, l2 = lpt_ref[0:1, :], lpt_ref[1:2, :], lpt_ref[2:3, :], lpt_ref[3:4, :]
    rb = MOE_LOCAL_ROWS // 3
    for r in range(3):
        rid = (lax.broadcasted_iota(I32, (rb, tm), 0) + r * rb).astype(F32)
        gate = jnp.sum(jnp.where(l1 == rid, g1, 0.0) + jnp.where(l2 == rid, g2, 0.0), 1, keepdims=True)
        rs = slice(r * rb, (r + 1) * rb)
        wbuf[rs, :] = (ybuf[slot, rs, :].astype(F32) * gate).astype(BF16)
    tok = tok_ref[...]
    col = lax.broadcasted_iota(I32, (tm, MOE_LOCAL_ROWS), 1).astype(F32)
    pick = jnp.where((tok[:, 2:3] == col) | (tok[:, 3:4] == col), 1.0, 0.0).astype(BF16)
    f = _dot(pick, wbuf[...])
    y = ALPHA * x_ref[...] + mod_ref[5:6, :] * f
    res = _layer_norm(y, lg_ref[...], lb_ref[...])

    @pl.when(c < n_ctx_tiles)
    def _():
        oc_ref[...] = res

    @pl.when(c >= n_ctx_tiles)
    def _():
        ol_ref[...] = res


def _combine(tabs, tok, lpt, ys, x, mods, g, b, *, n_ctx, lat_len):
    nt = x.shape[0]
    tm = MOE_TOK_CHUNK
    nct = n_ctx // tm
    cmap = lambda c, *_: (_cond_index(c * tm, n_ctx, lat_len), 0, 0)
    grid_spec = pltpu.PrefetchScalarGridSpec(
        num_scalar_prefetch=4,
        grid=(nt // tm,),
        in_specs=[pl.BlockSpec((tm, LANES), lambda c, *_: (c, 0)),
                  pl.BlockSpec((None, 8, tm), lambda c, *_: (c, 0, 0)),
                  pl.BlockSpec((tm, D_MODEL), lambda c, *_: (c, 0)),
                  pl.BlockSpec((None, 8, D_MODEL), cmap),
                  pl.BlockSpec((1, D_MODEL), lambda c, *_: (0, 0)),
                  pl.BlockSpec((1, D_MODEL), lambda c, *_: (0, 0)),
                  pl.BlockSpec(memory_space=pl.ANY)],
        out_specs=[pl.BlockSpec((tm, D_MODEL), lambda c, *_: (jnp.minimum(c, nct - 1), 0)),
                   pl.BlockSpec((tm, D_MODEL), lambda c, *_: (jnp.maximum(c - nct, 0), 0))],
        scratch_shapes=[pltpu.VMEM((2, MOE_LOCAL_ROWS, D_MODEL), BF16),
                        pltpu.VMEM((MOE_LOCAL_ROWS, D_MODEL), BF16),
                        pltpu.SemaphoreType.DMA((2,))],
    )
    return pl.pallas_call(
        functools.partial(_combine_kernel, n_ctx_tiles=nct),
        grid_spec=grid_spec,
        out_shape=[jax.ShapeDtypeStruct((n_ctx, D_MODEL), F32),
                   jax.ShapeDtypeStruct((nt - n_ctx, D_MODEL), F32)],
        compiler_params=_cparams("arbitrary"),
        name="moe_combine",
    )(*tabs, tok, lpt, x, mods, g, b, ys)


def _moe(x, routed, mods, wg, wu, wd, ln_g, ln_b, *, n_ctx, lat_len):
    nt = x.shape[0]
    bs, tc = MOE_SLOT_TILE, MOE_TOK_CHUNK
    nchunk = nt // tc
    max_rows = 2 * nt + nchunk * N_EXPERTS * (MOE_CELL_ALIGN - 1)
    n_tiles = -(-max_rows // bs) + N_EXPERTS
    n_slots = n_tiles * bs

    hb, tok, lpt, capt = routed

    cap = capt[:, 0, :N_EXPERTS].astype(I32)
    a_loc = jnp.cumsum(cap, 1) - cap
    tot = jnp.sum(cap, 0)
    padded = ((tot + bs - 1) // bs) * bs
    gend = jnp.cumsum(padded)
    g_glob = (gend - padded)[None, :] + (jnp.cumsum(cap, 0) - cap)
    fill_cell = (MOE_LOCAL_ROWS // MOE_BIG_PIECE) * MOE_BIG_PIECE
    n_tail_cells = -(-(n_slots - 2 * nt) // fill_cell)
    n_fill = 1 + -(-n_tail_cells // N_EXPERTS)
    tail_k = jnp.arange((n_fill - 1) * N_EXPERTS, dtype=I32) * fill_cell
    tail_rows = jnp.clip(n_slots - gend[-1] - tail_k, 0, fill_cell)
    g_all = jnp.concatenate([g_glob.reshape(-1), gend - padded + tot, gend[-1] + tail_k])
    a_all = jnp.concatenate([a_loc.reshape(-1), jnp.zeros((n_fill * N_EXPERTS,), I32)])
    rows_all = jnp.concatenate([cap.reshape(-1), padded - tot, tail_rows])
    n_big = rows_all // MOE_BIG_PIECE
    n_small = (rows_all - n_big * MOE_BIG_PIECE) // MOE_CELL_ALIGN
    tabs = tuple(t.astype(I32) for t in (g_all, a_all, n_big, n_small))
    tile_start = jnp.arange(n_tiles, dtype=I32) * bs
    tile_expert = jnp.minimum(jnp.sum((gend[None, :] <= tile_start[:, None]).astype(I32), 1), N_EXPERTS - 1)
    n_used = (gend[-1] // bs).astype(I32).reshape(1)

    xs = _dispatch(tabs, hb, lpt, n_slots=n_slots, n_fill=n_fill)
    ys = _experts(tile_expert, n_used, xs, wg, wu, wd)
    return _combine(tabs, tok, lpt, ys, x, mods, ln_g, ln_b, n_ctx=n_ctx, lat_len=lat_len)


def _axial_angles(n_tok, dim):
    rows = n_tok // GRID_W
    row = jnp.repeat(jnp.arange(rows), GRID_W).astype(F32)
    col = jnp.tile(jnp.arange(GRID_W), rows).astype(F32)
    axis_dim = dim // 2
    inv = 1.0 / (ROPE_BASE ** (jnp.arange(0, axis_dim, 2, dtype=F32) / axis_dim))
    ang = jnp.concatenate([row[:, None] * inv, col[:, None] * inv], -1)
    return jnp.cos(ang), jnp.sin(ang)


def _mla_tables(n_lat):
    cos, sin = _axial_angles(n_lat, MLA_ROPE)
    one = jnp.ones((n_lat, MLA_NOPE), F32)
    zero = jnp.zeros((n_lat, MLA_NOPE), F32)
    pad1 = jnp.ones((n_lat, LANES - MLA_NOPE - MLA_ROPE), F32)
    pad0 = jnp.zeros((n_lat, LANES - MLA_NOPE - MLA_ROPE), F32)
    return (jnp.concatenate([one, cos, cos, pad1], -1), jnp.concatenate([zero, sin, sin, pad0], -1))


def _ret_tables(n_lat):
    cos, sin = _axial_angles(n_lat, RET_QK)
    return jnp.concatenate([cos, cos], -1), jnp.concatenate([-sin, sin], -1)


def _rot_cols(w):
    half = w.shape[1] // 2
    return jnp.concatenate([-w[:, half:], w[:, :half]], 1)


def _in0_weights(w_in0):
    z, xbc, dt, ql, kvl, kr = jnp.split(
        w_in0, [SSD_INNER, SSD_INNER + SSD_XBC, SSD_INNER + SSD_XBC + 2 * SSD_HEADS,
                SSD_INNER + SSD_XBC + 2 * SSD_HEADS + MLA_Q_RANK,
                SSD_INNER + SSD_XBC + 2 * SSD_HEADS + MLA_Q_RANK + MLA_KV_RANK], axis=1)
    dtp = jnp.pad(dt, ((0, 0), (0, LANES - 2 * SSD_HEADS)))
    lpad = ((0, 0), (MLA_NOPE, LANES - MLA_NOPE - MLA_ROPE))
    krp = jnp.concatenate([jnp.pad(kr, lpad), jnp.pad(_rot_cols(kr), lpad)], 1)
    return jnp.concatenate([z, xbc, dtp, ql, kvl, krp], 1).astype(BF16)


def _mla_weights(w_q_up, w_kv_up):
    d = w_q_up.shape[0]
    wq = w_q_up.reshape(d, MLA_HEADS, MLA_NOPE + MLA_ROPE)
    nope, rope = wq[..., :MLA_NOPE], wq[..., MLA_NOPE:]
    half = MLA_ROPE // 2
    rot = jnp.concatenate([-rope[..., half:], rope[..., :half]], -1)
    tail = jnp.zeros((d, MLA_HEADS, LANES - MLA_NOPE - MLA_ROPE), F32)
    wq_pad = jnp.concatenate([nope, rope, tail], -1).reshape(d, MLA_HEADS * LANES)
    wq_rot = jnp.concatenate([jnp.zeros_like(nope), rot, tail], -1).reshape(d, MLA_HEADS * LANES)
    r = w_kv_up.shape[0]
    wkv = w_kv_up.reshape(r, MLA_HEADS, MLA_NOPE + MLA_V)
    wk = jnp.concatenate([wkv[..., :MLA_NOPE], jnp.zeros((r, MLA_HEADS, LANES - MLA_NOPE), F32)], -1)
    wv = wkv[..., MLA_NOPE:]
    return (wq_pad.astype(BF16), wq_rot.astype(BF16),
            wk.reshape(r, MLA_HEADS * LANES).astype(BF16), wv.reshape(r, MLA_OUT).astype(BF16))


def _lane_row(v, width):
    return jnp.pad(v, (0, width - v.shape[0])).reshape(1, width)


def kernel(x_prompt, x_sample, cache_mla_ckv, cache_mla_krope, state_ssd_f, state_ssd_b, state_ret_f, state_ret_b, c, c_ctx, ada_w, ada_b, ln1_g, ln1_b, ln2_g, ln2_b, w_in0, ssd_conv_w, ssd_conv_b, ssd_a_log_f, ssd_a_log_b, ssd_dt_bias_f, ssd_dt_bias_b, ssd_d, ssd_norm_g, mla_q_norm_g, mla_w_q_up, mla_kv_norm_g, mla_w_kv_up, w_out0, ffn_w_gate, ffn_w_up, ffn_w_down, w_in1, ret_decay_f, ret_decay_b, w_out1, moe_router, moe_w_gate, moe_w_up, moe_w_down):
    bc, lc, _ = x_prompt.shape
    bl, ll, _ = x_sample.shape
    past = cache_mla_ckv.shape[2]
    n_ctx, n_lat = bc * lc, bl * ll
    geo = dict(n_ctx=n_ctx, lat_len=ll)

    x_parts = [x_prompt.reshape(n_ctx, D_MODEL), x_sample.reshape(n_lat, D_MODEL)]
    n_cond = 1 + bl
    cond = jnp.concatenate([c_ctx[None, :], c, jnp.zeros((-n_cond % 8, D_MODEL), F32)], 0)
    mods = _ada_vectors(cond, ada_w, ada_b)

    tm0 = 512
    wq_pad, wq_rot, wk_pad, wv = _mla_weights(mla_w_q_up[0], mla_w_kv_up[0])
    mla_w = (mla_q_norm_g[0].reshape(1, MLA_Q_RANK), wq_pad, wq_rot,
             mla_kv_norm_g[0].reshape(1, MLA_KV_RANK), wk_pad, wv)
    cos_l, sin_l = _mla_tables(ll)
    tables = (jnp.concatenate([jnp.ones((tm0, LANES), F32), cos_l], 0),
              jnp.concatenate([jnp.zeros((tm0, LANES), F32), sin_l], 0))
    z, xbc, dt, q_all, k_all, v_all, ckv_all, kr_all = _in0(
        x_parts, mods[0], _in0_weights(w_in0[0]), tables, mla_w, tm=tm0, **geo)

    cw = jnp.pad(ssd_conv_w[0], ((0, 8 - ssd_conv_w.shape[1]), (0, 0)))
    ssd_params = (cw, ssd_conv_b[0].reshape(1, SSD_XBC),
                  _lane_row(jnp.concatenate([ssd_dt_bias_f[0], ssd_dt_bias_b[0]]), LANES),
                  _lane_row(jnp.concatenate([ssd_a_log_f[0], ssd_a_log_b[0]]), LANES),
                  jnp.repeat(ssd_d[0], SSD_HEAD_DIM).reshape(1, SSD_INNER),
                  ssd_norm_g[0].reshape(1, SSD_INNER))

    def st_in(s):
        return jnp.transpose(s, (0, 3, 1, 2)).reshape(s.shape[0], SSD_STATE, SSD_INNER)

    def st_out(s):
        return jnp.transpose(s.reshape(s.shape[0], SSD_STATE, SSD_HEADS, SSD_HEAD_DIM), (0, 2, 3, 1))

    zero_ssd = jnp.zeros((bc, SSD_STATE, SSD_INNER), F32)
    y_ssd_c, ssd_f, ssd_b = _ssd(z, xbc, dt, zero_ssd, zero_ssd, ssd_params,
                                 row0=0, n_seq=bc, seq_len=lc, name="ssd_ctx")
    y_ssd_l, _, _ = _ssd(z, xbc, dt, st_in(state_ssd_f[:, 0]), st_in(state_ssd_b[:, 0]), ssd_params,
                         row0=n_ctx, n_seq=bl, seq_len=ll, name="ssd_lat")

    lpad = ((0, 0), (MLA_NOPE, 2 * LANES - MLA_NOPE - MLA_ROPE))
    k_p, v_p = _mla_prep(None, cache_mla_ckv[:, 0].reshape(bl * past, MLA_KV_RANK),
                         jnp.pad(cache_mla_krope[:, 0].reshape(bl * past, MLA_ROPE), lpad),
                         None, mla_w, row0=0, n_rows=bl * past,
                         do_q=False, do_norm=False, tm=256, name="mla_prep_cache")
    o_c = _attention(q_all, k_all, v_all, None, None, row0=0, n_seq=bc, seq_len=lc, cache_len=0,
                     tq=lc, n_pairs=MLA_HEADS // 2, name="attn_ctx")
    o_l = _attention(q_all, k_all, v_all, k_p, v_p, row0=n_ctx, n_seq=bl, seq_len=ll, cache_len=past,
                     tq=min(1024, ll), n_pairs=2, name="attn_lat")

    w_out0_bf = w_out0[0].astype(BF16)
    x = _ffn([[y_ssd_c, y_ssd_l], [o_c, o_l]], [w_out0_bf[:SSD_INNER], w_out0_bf[SSD_INNER:]], x_parts, mods[0],
             ln1_g[0].reshape(1, D_MODEL), ln1_b[0].reshape(1, D_MODEL),
             ffn_w_gate[0].astype(BF16), ffn_w_up[0].astype(BF16), ffn_w_down[0].astype(BF16),
             ln2_g[0].reshape(1, D_MODEL), ln2_b[0].reshape(1, D_MODEL), tm=512, **geo)

    tm1 = 256
    cos_r, sin_r = _ret_tables(ll)
    ret_tables = (jnp.concatenate([jnp.ones((tm1, RET_QK), F32), cos_r], 0),
                  jnp.concatenate([jnp.zeros((tm1, RET_QK), F32), sin_r], 0))
    q1, k1, v1, g1 = _in1(x, mods[1], w_in1[0].astype(BF16), ret_tables, tm=tm1, **geo)
    dec = jnp.stack([ret_decay_f[0], ret_decay_b[0]], 1)
    dec = jnp.broadcast_to(jnp.pad(dec, ((0, 0), (0, 6)))[:, :, None], (RET_HEADS, 8, RET_V))
    y_ret_c, ret_f, ret_b = _retention(q1, k1, v1, g1, dec, None,
                                       row0=0, n_seq=bc, seq_len=lc, hps=8, name="ret_ctx")
    y_ret_l, = _retention(q1, k1, v1, g1, dec, (state_ret_f[:, 0], state_ret_b[:, 0]),
                          row0=n_ctx, n_seq=bl, seq_len=ll, hps=2, name="ret_lat")
    router_w = jnp.pad(moe_router[0], ((0, 0), (0, LANES - N_EXPERTS)))
    x, *routed = _outproj([[y_ret_c, y_ret_l]], [w_out1[0].astype(BF16)], [x], mods[1],
                          ln1_g[1].reshape(1, D_MODEL), ln1_b[1].reshape(1, D_MODEL),
                          gate_row=2, tm=MOE_TOK_CHUNK, name="out1", router_w=router_w, **geo)
    y_c, y_l = _moe(x, routed, mods[1], moe_w_gate[0].astype(BF16), moe_w_up[0].astype(BF16),
                    moe_w_down[0].astype(BF16), ln2_g[1].reshape(1, D_MODEL), ln2_b[1].reshape(1, D_MODEL), **geo)

    y_prompt = y_c.reshape(bc, lc, D_MODEL)
    y_sample = y_l.reshape(bl, ll, D_MODEL)
    new_ckv = ckv_all[:n_ctx].reshape(bc, 1, lc, MLA_KV_RANK)
    new_krope = kr_all[:n_ctx, MLA_NOPE:MLA_NOPE + MLA_ROPE].reshape(bc, 1, lc, MLA_ROPE)
    return (y_prompt, y_sample, new_ckv, new_krope,
            st_out(ssd_f)[:, None], st_out(ssd_b)[:, None], ret_f[:, None], ret_b[:, None])
```
